```python
import math
import jax, jax.numpy as jnp
from jax import lax
import numpy as np


D_MODEL = 1024
BATCH = 2
SEQ = 8192
DEPTH = 4

CHUNK = 64
N_MIXERS = 3
Q_BLOCK = 128
EPS = 1e-6
ROPE_THETA = 10000.0
NEG = -1e30

H_A = 16
DH_A = 64
H_IDX = 8
D_IDX = 64
D_IDX_ROPE = 32
TOPK_MAX = 256
T5_BUCKETS = 32
T5_MAX_DIST = 128
A_IN = 3 * H_A * DH_A + H_IDX * D_IDX + D_IDX + H_IDX

H_B = 16
Q_LORA = 256
KV_LORA = 128
QK_NOPE = 64
QK_ROPE = 32
V_DIM = 64
B_DOWN = Q_LORA + KV_LORA + QK_ROPE

H_C = 16
DH_C = 64
LEFT_CHUNKS = 8
REL_MAX_PAST = 128
REL_SPAN = REL_MAX_PAST + CHUNK

D_FF = 2816
CONV_W = 3

N_A = (DEPTH + 2) // N_MIXERS
N_B = (DEPTH + 1) // N_MIXERS
N_C = DEPTH // N_MIXERS

kernel_name = 'hybrid_chunk_causal_dsa_mla_band_convffn'


def rms_norm(x, g):
    xf = x.astype(jnp.float32)
    y = xf * lax.rsqrt(jnp.mean(xf * xf, axis=-1, keepdims=True) + EPS)
    return (y * g.astype(jnp.float32)).astype(x.dtype)


def rope_tables(seq, dim, dtype):
    inv = 1.0 / (ROPE_THETA ** (jnp.arange(0, dim, 2, dtype=jnp.float32) / dim))
    ang = jnp.arange(seq, dtype=jnp.float32)[:, None] * inv[None, :]
    return jnp.cos(ang).astype(dtype), jnp.sin(ang).astype(dtype)


def rope_tail(x, n_rot, cos, sin):
    xp, xr = x[..., :-n_rot], x[..., -n_rot:]
    x1, x2 = jnp.split(xr, 2, axis=-1)
    rot = jnp.concatenate([x1 * cos - x2 * sin, x1 * sin + x2 * cos], axis=-1)
    return jnp.concatenate([xp, rot], axis=-1)


def t5_bucket(rel):
    nb = T5_BUCKETS // 2
    max_exact = nb // 2
    n = jnp.abs(rel)
    large = max_exact + (jnp.log(jnp.maximum(n, 1).astype(jnp.float32) / max_exact)
                         / math.log(T5_MAX_DIST / max_exact) * (nb - max_exact)).astype(jnp.int32)
    large = jnp.minimum(large, nb - 1)
    return jnp.where(rel > 0, nb, 0) + jnp.where(n < max_exact, n, large)


def chunk_limit(pos):
    return (pos // CHUNK + 1) * CHUNK


def mixer_a(h, w_in, q_g, k_g, kidx_g, w_out, t5_bias):
    B, S, _ = h.shape
    topk = min(TOPK_MAX, S // 4)
    hd = H_A * DH_A
    q, k, v, qi, ki, wi = jnp.split(
        h @ w_in, [hd, 2 * hd, 3 * hd, 3 * hd + H_IDX * D_IDX, 3 * hd + H_IDX * D_IDX + D_IDX], axis=-1)
    q = rms_norm(q.reshape(B, S, H_A, DH_A), q_g)
    k = rms_norm(k.reshape(B, S, H_A, DH_A), k_g)
    v = v.reshape(B, S, H_A, DH_A)
    cos, sin = rope_tables(S, D_IDX_ROPE, h.dtype)
    qi = rope_tail(qi.reshape(B, S, H_IDX, D_IDX), D_IDX_ROPE, cos[:, None, :], sin[:, None, :])
    ki = rope_tail(rms_norm(ki, kidx_g), D_IDX_ROPE, cos, sin)
    pos = jnp.arange(S, dtype=jnp.int32)
    gather = jax.vmap(lambda arr, ib: arr[ib])

    def block(i):
        t0 = i * Q_BLOCK
        qb = lax.dynamic_slice_in_dim(q, t0, Q_BLOCK, axis=1)
        qib = lax.dynamic_slice_in_dim(qi, t0, Q_BLOCK, axis=1)
        wib = lax.dynamic_slice_in_dim(wi, t0, Q_BLOCK, axis=1)
        tq = t0 + jnp.arange(Q_BLOCK, dtype=jnp.int32)
        lim = chunk_limit(tq)
        sc = jnp.einsum('bqhd,bsd->bqhs', qib, ki).astype(jnp.float32) * (D_IDX ** -0.5)
        score = jnp.einsum('bqhs,bqh->bqs', jax.nn.relu(sc), wib.astype(jnp.float32)) * (H_IDX ** -0.5)
        score = jnp.where((pos[None, :] < lim[:, None])[None], score, -jnp.inf)
        _, idx = lax.top_k(score, topk)
        valid = idx < lim[None, :, None]
        ks = gather(k, idx)
        vs = gather(v, idx)
        logits = jnp.einsum('bqhd,bqkhd->bhqk', qb, ks).astype(jnp.float32) * (DH_A ** -0.5)
        bias = t5_bias[t5_bucket(idx - tq[None, :, None])]
        logits = logits + jnp.transpose(bias, (0, 3, 1, 2)).astype(jnp.float32)
        logits = jnp.where(valid[:, None], logits, NEG)
        p = jax.nn.softmax(logits, axis=-1).astype(vs.dtype)
        return jnp.einsum('bhqk,bqkhd->bqhd', p, vs)

    out = lax.map(block, jnp.arange(S // Q_BLOCK))
    out = jnp.transpose(out, (1, 0, 2, 3, 4)).reshape(B, S, hd)
    return out @ w_out


def mixer_b(h, w_down, qa_g, kva_g, w_uq, w_ukv, q_g, k_g, w_out):
    B, S, _ = h.shape
    cq, ckv, kr = jnp.split(h @ w_down, [Q_LORA, Q_LORA + KV_LORA], axis=-1)
    q = (rms_norm(cq, qa_g) @ w_uq).reshape(B, S, H_B, QK_NOPE + QK_ROPE)
    kv = (rms_norm(ckv, kva_g) @ w_ukv).reshape(B, S, H_B, QK_NOPE + V_DIM)
    k_nope, v = kv[..., :QK_NOPE], kv[..., QK_NOPE:]
    k = jnp.concatenate([k_nope, jnp.broadcast_to(kr[:, :, None, :], (B, S, H_B, QK_ROPE))], axis=-1)
    cos, sin = rope_tables(S, QK_ROPE, h.dtype)
    q = rope_tail(rms_norm(q, q_g), QK_ROPE, cos[:, None, :], sin[:, None, :])
    k = rope_tail(rms_norm(k, k_g), QK_ROPE, cos[:, None, :], sin[:, None, :])
    pos = jnp.arange(S, dtype=jnp.int32)
    scale = (QK_NOPE + QK_ROPE) ** -0.5

    def block(i):
        t0 = i * Q_BLOCK
        qb = lax.dynamic_slice_in_dim(q, t0, Q_BLOCK, axis=1)
        lim = chunk_limit(t0 + jnp.arange(Q_BLOCK, dtype=jnp.int32))
        logits = jnp.einsum('bqhd,bshd->bhqs', qb, k).astype(jnp.float32) * scale
        logits = jnp.where((pos[None, :] < lim[:, None])[None, None], logits, NEG)
        p = jax.nn.softmax(logits, axis=-1).astype(v.dtype)
        return jnp.einsum('bhqs,bshd->bqhd', p, v)

    out = lax.map(block, jnp.arange(S // Q_BLOCK))
    out = jnp.transpose(out, (1, 0, 2, 3, 4)).reshape(B, S, H_B * V_DIM)
    return out @ w_out


def mixer_c(h, w_in, q_g, k_g, rel_bias, w_out):
    B, S, _ = h.shape
    q, k, v = jnp.split(h @ w_in, 3, axis=-1)
    q = rms_norm(q.reshape(B, S, H_C, DH_C), q_g)
    k = rms_norm(k.reshape(B, S, H_C, DH_C), k_g)
    v = v.reshape(B, S, H_C, DH_C)
    pad = LEFT_CHUNKS * CHUNK
    band = pad + CHUNK
    kp = jnp.pad(k, ((0, 0), (pad, 0), (0, 0), (0, 0)))
    vp = jnp.pad(v, ((0, 0), (pad, 0), (0, 0), (0, 0)))
    qi = jnp.arange(CHUNK, dtype=jnp.int32)
    kj = jnp.arange(band, dtype=jnp.int32)
    dist = qi[:, None] - kj[None, :] + pad
    bias = rel_bias[:, jnp.clip(dist, -(CHUNK - 1), REL_MAX_PAST) + CHUNK - 1].astype(jnp.float32)
    scale = DH_C ** -0.5

    def chunk(c):
        t0 = c * CHUNK
        qc = lax.dynamic_slice_in_dim(q, t0, CHUNK, axis=1)
        kc = lax.dynamic_slice_in_dim(kp, t0, band, axis=1)
        vc = lax.dynamic_slice_in_dim(vp, t0, band, axis=1)
        key_ok = (t0 - pad + kj) >= 0
        logits = jnp.einsum('bqhd,bkhd->bhqk', qc, kc).astype(jnp.float32) * scale + bias
        logits = jnp.where(key_ok[None, None, None, :], logits, NEG)
        p = jax.nn.softmax(logits, axis=-1).astype(vc.dtype)
        return jnp.einsum('bhqk,bkhd->bqhd', p, vc)

    out = lax.map(chunk, jnp.arange(S // CHUNK))
    out = jnp.transpose(out, (1, 0, 2, 3, 4)).reshape(B, S, H_C * DH_C)
    return out @ w_out


def conv_ffn(h, w_up, conv_w, conv_b, w_down):
    S = h.shape[1]
    u = h @ w_up
    up = jnp.pad(u, ((0, 0), (CONV_W - 1, 0), (0, 0)))
    u = conv_b + sum(conv_w[j] * up[:, j:j + S] for j in range(CONV_W))
    gate, val = jnp.split(u, 2, axis=-1)
    return (jax.nn.silu(gate) * val) @ w_down


def setup_inputs(seed: int = 0) -> dict:
    key = jax.random.key(seed)
    ks = iter(jax.random.split(key, 32))
    f32 = jnp.float32

    def dense(shape, fan_in):
        return jax.random.normal(next(ks), shape, f32) * fan_in ** -0.5

    def gain(shape):
        return 1.0 + 0.05 * jax.random.normal(next(ks), shape, f32)

    def small(shape, s):
        return s * jax.random.normal(next(ks), shape, f32)

    return {
        'x': jax.random.normal(next(ks), (BATCH, SEQ, D_MODEL), f32),
        'norm_mix': gain((DEPTH, D_MODEL)),
        'norm_ffn': gain((DEPTH, D_MODEL)),
        't5_bias': small((T5_BUCKETS, H_A), 0.5),
        'a_w_in': dense((N_A, D_MODEL, A_IN), D_MODEL),
        'a_q_norm': gain((N_A, DH_A)),
        'a_k_norm': gain((N_A, DH_A)),
        'a_kidx_norm': gain((N_A, D_IDX)),
        'a_w_out': dense((N_A, H_A * DH_A, D_MODEL), H_A * DH_A),
        'b_w_down': dense((N_B, D_MODEL, B_DOWN), D_MODEL),
        'b_q_a_norm': gain((N_B, Q_LORA)),
        'b_kv_a_norm': gain((N_B, KV_LORA)),
        'b_w_uq': dense((N_B, Q_LORA, H_B * (QK_NOPE + QK_ROPE)), Q_LORA),
        'b_w_ukv': dense((N_B, KV_LORA, H_B * (QK_NOPE + V_DIM)), KV_LORA),
        'b_q_norm': gain((N_B, QK_NOPE + QK_ROPE)),
        'b_k_norm': gain((N_B, QK_NOPE + QK_ROPE)),
        'b_w_out': dense((N_B, H_B * V_DIM, D_MODEL), H_B * V_DIM),
        'c_w_in': dense((N_C, D_MODEL, 3 * H_C * DH_C), D_MODEL),
        'c_q_norm': gain((N_C, DH_C)),
        'c_k_norm': gain((N_C, DH_C)),
        'c_rel_bias': small((N_C, H_C, REL_SPAN), 0.5),
        'c_w_out': dense((N_C, H_C * DH_C, D_MODEL), H_C * DH_C),
        'f_w_up': dense((DEPTH, D_MODEL, 2 * D_FF), D_MODEL),
        'f_conv_w': dense((DEPTH, CONV_W, 2 * D_FF), CONV_W),
        'f_conv_b': small((DEPTH, 2 * D_FF), 0.01),
        'f_w_down': dense((DEPTH, D_FF, D_MODEL), D_FF),
    }


def reference(x, norm_mix, norm_ffn, t5_bias,
              a_w_in, a_q_norm, a_k_norm, a_kidx_norm, a_w_out,
              b_w_down, b_q_a_norm, b_kv_a_norm, b_w_uq, b_w_ukv, b_q_norm, b_k_norm, b_w_out,
              c_w_in, c_q_norm, c_k_norm, c_rel_bias, c_w_out,
              f_w_up, f_conv_w, f_conv_b, f_w_down):
    for layer in range(DEPTH):
        h = rms_norm(x, norm_mix[layer])
        kind = layer % N_MIXERS
        j = layer // N_MIXERS
        if kind == 0:
            m = mixer_a(h, a_w_in[j], a_q_norm[j], a_k_norm[j], a_kidx_norm[j], a_w_out[j], t5_bias)
        elif kind == 1:
            m = mixer_b(h, b_w_down[j], b_q_a_norm[j], b_kv_a_norm[j], b_w_uq[j], b_w_ukv[j],
                        b_q_norm[j], b_k_norm[j], b_w_out[j])
        else:
            m = mixer_c(h, c_w_in[j], c_q_norm[j], c_k_norm[j], c_rel_bias[j], c_w_out[j])
        x = x + m
        x = x + conv_ffn(rms_norm(x, norm_ffn[layer]), f_w_up[layer], f_conv_w[layer],
                         f_conv_b[layer], f_w_down[layer])
    return x
```

```python
import functools
import math

import numpy as np
import jax
import jax.numpy as jnp
from jax import lax
from jax.experimental import pallas as pl
from jax.experimental.pallas import tpu as pltpu

F32 = jnp.float32
BF16 = jnp.bfloat16

D_MODEL = 1024
CHUNK = 64
EPS = 1e-6
ROPE_THETA = 10000.0
NEG = -1e30
NEG_BITS = int(np.float32(NEG).view(np.int32))
M_INIT = -1e38
INT_MIN = -(2 ** 31)

N_MIXERS = 3
N_HEADS = 16
HEAD_DIM = 64
N_PAIRS = N_HEADS // 2
LANES = 128

H_IDX = 8
D_IDX = 64
ROPE_DIM = 32
TOPK_MAX = 256
T5_BUCKETS = 32
T5_MAX_DIST = 128
A_QKV = 3 * N_HEADS * HEAD_DIM
Q_LORA = 256
KV_LORA = 128
QK_NOPE = 64
QK_ROPE = 32
V_DIM = 64
B_QK = QK_NOPE + QK_ROPE
LEFT_CHUNKS = 8
REL_MAX_PAST = 128
D_FF = 2816
CONV_W = 3
FF_CHUNK = 256
HALO = 8

VMEM_LIMIT = 56 * 1024 * 1024


def _cparams(sem):
    return pltpu.CompilerParams(dimension_semantics=sem, vmem_limit_bytes=VMEM_LIMIT)


def _rms(x, g):
    return x * lax.rsqrt(jnp.mean(x * x, axis=-1, keepdims=True) + EPS) * g


def _dot(a, b):
    return jnp.dot(a, b, preferred_element_type=F32)


def _dot_t(a, b):
    return lax.dot_general(a, b, (((1,), (1,)), ((), ())), preferred_element_type=F32)


def _split_bf16(x):
    hi = x.astype(BF16)
    lo = (x - hi.astype(F32)).astype(BF16)
    return hi, lo


def _group_sumsq(y, bd):
    hi, lo = _split_bf16(y * y)
    return _dot(hi, bd) + _dot(lo, bd)


def _rope(y, ct, s1, s2):
    return y * ct + pltpu.roll(y, LANES - 16, 1) * s1 + pltpu.roll(y, 16, 1) * s2


def _rope_wide(y, ct, s1, s2):
    w = y.shape[-1]
    parts = [_rope(y[:, c:c + LANES], ct, s1, s2) for c in range(0, w, LANES)]
    return parts[0] if len(parts) == 1 else jnp.concatenate(parts, axis=-1)


def _qkv_proj_kernel(x_ref, g_ref, w_ref, gq_ref, gk_ref, bd_ref, q_ref, k_ref, v_ref):
    xb = _rms(x_ref[...], g_ref[...]).astype(BF16)
    bd = bd_ref[...]
    hd = N_HEADS * HEAD_DIM
    for which, (o_ref, gain_ref) in enumerate(((q_ref, gq_ref), (k_ref, gk_ref), (v_ref, None))):
        for c in range(hd // 256):
            col = which * hd + c * 256
            y = _dot(xb, w_ref[:, col:col + 256])
            if gain_ref is not None:
                y = y * lax.rsqrt(_group_sumsq(y, bd) * (1.0 / HEAD_DIM) + EPS) * gain_ref[...]
            yb = y.astype(BF16)
            o_ref[2 * c] = yb[:, :LANES]
            o_ref[2 * c + 1] = yb[:, LANES:]


def _qkv_proj(x, g, w, gq, gk, tm=512):
    B, S, D = x.shape
    hd = N_HEADS * HEAD_DIM
    bd = jnp.kron(jnp.eye(256 // HEAD_DIM, dtype=F32), jnp.ones((HEAD_DIM, HEAD_DIM), F32)).astype(BF16)
    out = jax.ShapeDtypeStruct((B, N_PAIRS, S, LANES), BF16)
    const = lambda b, i: (0, 0)
    ospec = pl.BlockSpec((None, N_PAIRS, tm, LANES), lambda b, i: (b, 0, i, 0))
    return pl.pallas_call(
        _qkv_proj_kernel,
        grid=(B, S // tm),
        in_specs=[
            pl.BlockSpec((None, tm, D), lambda b, i: (b, i, 0)),
            pl.BlockSpec((1, D), const),
            pl.BlockSpec((D, 3 * hd), const),
            pl.BlockSpec((1, 256), const),
            pl.BlockSpec((1, 256), const),
            pl.BlockSpec((256, 256), const),
        ],
        out_specs=[ospec, ospec, ospec],
        out_shape=[out, out, out],
        compiler_params=_cparams(("parallel", "parallel")),
        name="qkv_proj",
    )(x, g.reshape(1, D), w.astype(BF16), jnp.tile(gq, 4).reshape(1, 256), jnp.tile(gk, 4).reshape(1, 256), bd)


def _idx_proj_kernel(x_ref, g_ref, wh_ref, wl_ref, gki_ref, ct_ref, s1_ref, s2_ref, qp_ref, kp_ref, wi_ref,
                     *, w_scale):
    xn = _rms(x_ref[...], g_ref[...])
    xh, xl = _split_bf16(xn)

    def mm(c0, c1):
        wh = wh_ref[:, c0:c1]
        return _dot(xh, wh) + (_dot(xl, wh) + _dot(xh, wl_ref[:, c0:c1]))

    ct, s1, s2 = ct_ref[...], s1_ref[...], s2_ref[...]
    tm = xn.shape[0]
    low = lax.broadcasted_iota(jnp.int32, (tm, LANES), 1) < D_IDX
    zero = jnp.zeros((tm, LANES), F32)

    for c in range(H_IDX // 2):
        y = _rope(mm(c * LANES, (c + 1) * LANES), ct, s1, s2)
        hi = y.astype(BF16).astype(F32)
        lo = y - hi
        rhi = pltpu.roll(hi, D_IDX, 1)
        rlo = pltpu.roll(lo, D_IDX, 1)
        qp_ref[2 * c, :, :LANES] = jnp.where(low, hi, rlo).astype(BF16)
        qp_ref[2 * c, :, LANES:] = jnp.where(low, hi, zero).astype(BF16)
        qp_ref[2 * c + 1, :, :LANES] = jnp.where(low, rhi, lo).astype(BF16)
        qp_ref[2 * c + 1, :, LANES:] = jnp.where(low, rhi, zero).astype(BF16)

    c0 = H_IDX * D_IDX
    y = mm(c0, c0 + LANES)
    y = y * lax.rsqrt(jnp.sum(y * y, axis=-1, keepdims=True) * (1.0 / D_IDX) + EPS) * gki_ref[...]
    y = _rope(y, ct, s1, s2)
    hi = y.astype(BF16).astype(F32)
    lo = y - hi
    kp_ref[:, :LANES] = jnp.where(low, hi, pltpu.roll(hi, D_IDX, 1)).astype(BF16)
    kp_ref[:, LANES:] = jnp.where(low, lo, zero).astype(BF16)

    wi_ref[...] = mm(c0 + LANES, c0 + 2 * LANES) * w_scale


def _idx_proj(x, g, w_idx, gki, tabs, tm=512):
    B, S, D = x.shape
    nq = H_IDX * D_IDX
    wq, wk, ww = w_idx[:, :nq], w_idx[:, nq:nq + D_IDX], w_idx[:, nq + D_IDX:]
    wpad = jnp.concatenate([
        wq, jnp.pad(wk, ((0, 0), (0, LANES - D_IDX))), jnp.pad(ww, ((0, 0), (0, LANES - H_IDX)))], axis=1)
    wh = wpad.astype(BF16)
    wl = (wpad - wh.astype(F32)).astype(BF16)
    ncol = nq + 2 * LANES
    gpad = jnp.pad(gki, (0, LANES - D_IDX)).reshape(1, LANES)
    const = lambda b, i: (0, 0)
    tspec = pl.BlockSpec((tm, LANES), lambda b, i: (i, 0))
    return pl.pallas_call(
        functools.partial(_idx_proj_kernel, w_scale=(D_IDX ** -0.5) * (H_IDX ** -0.5)),
        grid=(B, S // tm),
        in_specs=[
            pl.BlockSpec((None, tm, D), lambda b, i: (b, i, 0)),
            pl.BlockSpec((1, D), const),
            pl.BlockSpec((D, ncol), const),
            pl.BlockSpec((D, ncol), const),
            pl.BlockSpec((1, LANES), const),
            tspec, tspec, tspec,
        ],
        out_specs=[
            pl.BlockSpec((None, H_IDX, tm, 256), lambda b, i: (b, 0, i, 0)),
            pl.BlockSpec((None, tm, 256), lambda b, i: (b, i, 0)),
            pl.BlockSpec((None, tm, LANES), lambda b, i: (b, i, 0)),
        ],
        out_shape=[
            jax.ShapeDtypeStruct((B, H_IDX, S, 256), BF16),
            jax.ShapeDtypeStruct((B, S, 256), BF16),
            jax.ShapeDtypeStruct((B, S, LANES), F32),
        ],
        compiler_params=_cparams(("parallel", "parallel")),
        name="idx_proj",
    )(x, g.reshape(1, D), wh, wl, gpad, *tabs)


def _mla_proj_kernel(x_ref, g_ref, wd_ref, gqa_ref, gkva_ref, wuq_ref, wuk_ref, wuv_ref, gq_ref, gk_ref, bd_ref,
                     ct_ref, s1_ref, s2_ref, q_ref, k_ref, v_ref):
    xb = _rms(x_ref[...], g_ref[...]).astype(BF16)
    d = _dot(xb, wd_ref[...])
    cq = _rms(d[:, :Q_LORA], gqa_ref[...]).astype(BF16)
    ckv = _rms(d[:, Q_LORA:Q_LORA + KV_LORA], gkva_ref[...]).astype(BF16)
    krz = pltpu.roll(d[:, Q_LORA + KV_LORA:], QK_NOPE, 1)
    krz2 = jnp.concatenate([krz, krz], axis=-1)
    bd = bd_ref[...]
    ct, s1, s2 = ct_ref[...], s1_ref[...], s2_ref[...]
    inv = 1.0 / B_QK
    for p in range(N_PAIRS):
        cols = slice(p * 256, (p + 1) * 256)
        y = _dot(cq, wuq_ref[:, cols])
        y = y * lax.rsqrt(_group_sumsq(y, bd) * inv + EPS) * gq_ref[...]
        q_ref[p] = _rope_wide(y, ct, s1, s2).astype(BF16)
        y = _dot(ckv, wuk_ref[:, cols]) + krz2
        y = y * lax.rsqrt(_group_sumsq(y, bd) * inv + EPS) * gk_ref[...]
        k_ref[p] = _rope_wide(y, ct, s1, s2).astype(BF16)
        v_ref[p] = _dot(ckv, wuv_ref[:, p * LANES:(p + 1) * LANES]).astype(BF16)


def _mla_proj(x, g, w_down, gqa, gkva, w_uq, w_ukv, gq, gk, tabs, tm=512):
    B, S, D = x.shape
    nd = Q_LORA + KV_LORA + QK_ROPE
    wd = jnp.pad(w_down, ((0, 0), (0, 512 - nd))).astype(BF16)
    padh = LANES - B_QK
    wuq = jnp.pad(w_uq.reshape(Q_LORA, N_HEADS, B_QK), ((0, 0), (0, 0), (0, padh))).reshape(Q_LORA, N_HEADS * LANES)
    wukv = w_ukv.reshape(KV_LORA, N_HEADS, QK_NOPE + V_DIM)
    wuk = jnp.pad(wukv[:, :, :QK_NOPE], ((0, 0), (0, 0), (0, LANES - QK_NOPE))).reshape(KV_LORA, N_HEADS * LANES)
    wuv = wukv[:, :, QK_NOPE:].reshape(KV_LORA, N_HEADS * V_DIM)
    gq2 = jnp.tile(jnp.pad(gq * (B_QK ** -0.5), (0, padh)), 2).reshape(1, 256)
    gk2 = jnp.tile(jnp.pad(gk, (0, padh)), 2).reshape(1, 256)
    bd = jnp.kron(jnp.eye(2, dtype=F32), jnp.ones((LANES, LANES), F32)).astype(BF16)
    const = lambda b, i: (0, 0)
    tspec = pl.BlockSpec((tm, LANES), lambda b, i: (i, 0))
    qk_shape = jax.ShapeDtypeStruct((B, N_PAIRS, S, 256), BF16)
    qk_spec = pl.BlockSpec((None, N_PAIRS, tm, 256), lambda b, i: (b, 0, i, 0))
    return pl.pallas_call(
        _mla_proj_kernel,
        grid=(B, S // tm),
        in_specs=[
            pl.BlockSpec((None, tm, D), lambda b, i: (b, i, 0)),
            pl.BlockSpec((1, D), const),
            pl.BlockSpec((D, 512), const),
            pl.BlockSpec((1, Q_LORA), const),
            pl.BlockSpec((1, KV_LORA), const),
            pl.BlockSpec((Q_LORA, N_HEADS * LANES), const),
            pl.BlockSpec((KV_LORA, N_HEADS * LANES), const),
            pl.BlockSpec((KV_LORA, N_HEADS * V_DIM), const),
            pl.BlockSpec((1, 256), const),
            pl.BlockSpec((1, 256), const),
            pl.BlockSpec((256, 256), const),
            tspec, tspec, tspec,
        ],
        out_specs=[qk_spec, qk_spec, pl.BlockSpec((None, N_PAIRS, tm, LANES), lambda b, i: (b, 0, i, 0))],
        out_shape=[qk_shape, qk_shape, jax.ShapeDtypeStruct((B, N_PAIRS, S, LANES), BF16)],
        compiler_params=_cparams(("parallel", "parallel")),
        name="mla_proj",
    )(x, g.reshape(1, D), wd, gqa.reshape(1, Q_LORA), gkva.reshape(1, KV_LORA), wuq.astype(BF16),
      wuk.astype(BF16), wuv.astype(BF16), gq2, gk2, bd, *tabs)


def _stack_heads(q2, half):
    qf = q2.astype(F32)
    low = lax.broadcasted_iota(jnp.int32, qf.shape, 1) < half
    zero = jnp.zeros_like(qf)
    return jnp.concatenate([jnp.where(low, qf, zero), jnp.where(low, zero, qf)], axis=0).astype(BF16)


def _flash_init(m_ref, l_ref, acc_ref):
    m_ref[...] = jnp.full(m_ref.shape, M_INIT, F32)
    l_ref[...] = jnp.zeros(l_ref.shape, F32)
    acc_ref[...] = jnp.zeros(acc_ref.shape, F32)


def _flash_step(qs, k, v, add, m_ref, l_ref, acc_ref):
    s = _dot_t(qs, k)
    if add is not None:
        s = s + add
    m_old = m_ref[...]
    m_new = jnp.maximum(m_old, jnp.max(s, axis=-1, keepdims=True))
    alpha = jnp.exp(m_old - m_new)
    p = jnp.exp(s - m_new)
    l_ref[...] = alpha * l_ref[...] + jnp.sum(p, axis=-1, keepdims=True)
    acc_ref[...] = alpha * acc_ref[...] + _dot(p.astype(BF16), v)
    m_ref[...] = m_new


def _flash_out(l_ref, acc_ref, tq):
    o = acc_ref[...] / l_ref[...]
    low = lax.broadcasted_iota(jnp.int32, (tq, LANES), 1) < HEAD_DIM
    return jnp.where(low, o[:tq], o[tq:]).astype(BF16)


def _band_attn_kernel(q_ref, k_ref, v_ref, bias_ref, o_ref, m_ref, l_ref, acc_ref, *, tq, nblk):
    i = pl.program_id(2)
    qs = _stack_heads(q_ref[...], HEAD_DIM)
    _flash_init(m_ref, l_ref, acc_ref)
    for j in range(nblk):
        kb = i - (nblk - 1) + j

        @pl.when(kb >= 0)
        def _():
            ks = pl.multiple_of(kb * tq, tq)
            add = jnp.concatenate([bias_ref[0, :, j * tq:(j + 1) * tq], bias_ref[1, :, j * tq:(j + 1) * tq]], axis=0)
            _flash_step(qs, k_ref[pl.ds(ks, tq), :], v_ref[pl.ds(ks, tq), :], add, m_ref, l_ref, acc_ref)

    o_ref[...] = _flash_out(l_ref, acc_ref, tq)


def _band_attn(q, k, v, bias, tq=256):
    B, _, S, _ = q.shape
    nblk = bias.shape[-1] // tq
    kv_spec = pl.BlockSpec((None, None, S, LANES), lambda b, p, i: (b, p, 0, 0))
    return pl.pallas_call(
        functools.partial(_band_attn_kernel, tq=tq, nblk=nblk),
        grid=(B, N_PAIRS, S // tq),
        in_specs=[
            pl.BlockSpec((None, None, tq, LANES), lambda b, p, i: (b, p, i, 0)),
            kv_spec, kv_spec,
            pl.BlockSpec((None, 2, tq, nblk * tq), lambda b, p, i: (p, 0, 0, 0)),
        ],
        out_specs=pl.BlockSpec((None, tq, LANES), lambda b, p, i: (b, i, p)),
        out_shape=jax.ShapeDtypeStruct((B, S, N_HEADS * HEAD_DIM), BF16),
        scratch_shapes=[pltpu.VMEM((2 * tq, 1), F32), pltpu.VMEM((2 * tq, 1), F32), pltpu.VMEM((2 * tq, LANES), F32)],
        compiler_params=_cparams(("parallel", "parallel", "arbitrary")),
        name="band_attn",
    )(q, k, v, bias)


def _mla_attn_kernel(q_ref, k_ref, v_ref, o_ref, m_ref, l_ref, acc_ref, *, tq, tk):
    i = pl.program_id(2)
    qs = _stack_heads(q_ref[...], LANES)
    _flash_init(m_ref, l_ref, acc_ref)
    t0 = i * tq
    n_full = t0 // tk
    n_blk = (t0 + tq + tk - 1) // tk

    def full_body(kb, carry):
        ks = pl.multiple_of(kb * tk, tk)
        _flash_step(qs, k_ref[pl.ds(ks, tk), :], v_ref[pl.ds(ks, tk), :], None, m_ref, l_ref, acc_ref)
        return carry

    lax.fori_loop(0, n_full, full_body, 0)

    lim = ((t0 + lax.broadcasted_iota(jnp.int32, (tq, 1), 0)) // CHUNK + 1) * CHUNK

    def edge_body(kb, carry):
        ks = pl.multiple_of(kb * tk, tk)
        col = ks + lax.broadcasted_iota(jnp.int32, (tq, tk), 1)
        add = jnp.where(col < lim, 0.0, NEG).astype(F32)
        _flash_step(qs, k_ref[pl.ds(ks, tk), :], v_ref[pl.ds(ks, tk), :], jnp.concatenate([add, add], axis=0),
                    m_ref, l_ref, acc_ref)
        return carry

    lax.fori_loop(n_full, n_blk, edge_body, 0)
    o_ref[...] = _flash_out(l_ref, acc_ref, tq)


def _mla_attn(q, k, v, tq=256, tk=512):
    B, _, S, _ = q.shape
    tk = min(tk, S)
    return pl.pallas_call(
        functools.partial(_mla_attn_kernel, tq=tq, tk=tk),
        grid=(B, N_PAIRS, S // tq),
        in_specs=[
            pl.BlockSpec((None, None, tq, 256), lambda b, p, i: (b, p, i, 0)),
            pl.BlockSpec((None, None, S, 256), lambda b, p, i: (b, p, 0, 0)),
            pl.BlockSpec((None, None, S, LANES), lambda b, p, i: (b, p, 0, 0)),
        ],
        out_specs=pl.BlockSpec((None, tq, LANES), lambda b, p, i: (b, i, p)),
        out_shape=jax.ShapeDtypeStruct((B, S, N_HEADS * V_DIM), BF16),
        scratch_shapes=[pltpu.VMEM((2 * tq, 1), F32), pltpu.VMEM((2 * tq, 1), F32), pltpu.VMEM((2 * tq, LANES), F32)],
        compiler_params=_cparams(("parallel", "parallel", "arbitrary")),
        name="mla_attn",
    )(q, k, v)


def _dsa_attn_kernel(q_ref, qp_ref, wi_ref, bias_ref, k_ref, v_ref, kp_ref, o_ref,
                     sc_ref, m_ref, l_ref, acc_ref, stage_ref, *, tq, tk, topk):
    i = pl.program_id(1)
    t0 = i * tq
    n_blk = (t0 + tq + tk - 1) // tk
    n_far = jnp.maximum((t0 // LANES - 1) // (tk // LANES), 0)
    lim = ((t0 + lax.broadcasted_iota(jnp.int32, (tq, 1), 0)) // CHUNK + 1) * CHUNK

    wv = wi_ref[...]

    def score_body(kb, carry):
        ks = pl.multiple_of(kb * tk, tk)
        kp = kp_ref[pl.ds(ks, tk), :]
        acc = jnp.zeros((tq, tk), F32)
        for h in range(H_IDX):
            acc = acc + jnp.maximum(_dot_t(qp_ref[h], kp), 0.0) * wv[:, h:h + 1]
        col = ks + lax.broadcasted_iota(jnp.int32, (tq, tk), 1)
        acc = jnp.where(col < lim, acc, -jnp.inf)
        bits = pltpu.bitcast(acc, jnp.int32)
        sc_ref[kb] = bits ^ ((bits >> 31) & 0x7FFFFFFF)
        return carry

    lax.fori_loop(0, n_blk, score_body, 0)

    def bit_body(t, ans):
        cand = ans + jnp.left_shift(jnp.int32(1), 31 - t)

        def count_body(kb, part):
            ge = jnp.where(sc_ref[kb] >= cand, 1, 0)
            for c in range(tk // LANES):
                part = part + ge[:, c * LANES:(c + 1) * LANES]
            return part

        part = lax.fori_loop(0, n_blk, count_body, jnp.zeros((tq, LANES), jnp.int32))
        cnt = jnp.sum(part, axis=-1, keepdims=True)
        return jnp.where(cnt >= topk, cand, ans)

    thr = lax.fori_loop(0, 32, bit_body, jnp.full((tq, 1), INT_MIN, jnp.int32))

    def mask_body(kb, carry):
        sc_ref[kb] = jnp.where(sc_ref[kb] >= thr, 0, NEG_BITS)
        return carry

    lax.fori_loop(0, n_blk, mask_body, 0)

    def pair_body(p, carry):
        qs = _stack_heads(q_ref[p], HEAD_DIM)
        _flash_init(m_ref, l_ref, acc_ref)

        def far_body(kb, c2):
            ks = pl.multiple_of(kb * tk, tk)
            am = pltpu.bitcast(sc_ref[kb], F32)
            _flash_step(qs, k_ref[p, pl.ds(ks, tk), :], v_ref[p, pl.ds(ks, tk), :],
                        jnp.concatenate([am, am], axis=0), m_ref, l_ref, acc_ref)
            return c2

        lax.fori_loop(0, n_far, far_body, 0)

        def near_body(kb, c2):
            ks = pl.multiple_of(kb * tk, tk)
            am = pltpu.bitcast(sc_ref[kb], F32)
            adds = []
            for hh in range(2):
                t_same = bias_ref[p, hh, 0]
                t_prev = bias_ref[p, hh, 1]
                rows = []
                for a in range(tq // LANES):
                    tiles = []
                    for c in range(tk // LANES):
                        d = (kb * (tk // LANES) + c) - (i * (tq // LANES) + a)
                        tile = jnp.where(d >= 1, NEG, jnp.where(d == 0, t_same, jnp.where(d == -1, t_prev, 0.0)))
                        tiles.append(tile)
                    rows.append(jnp.concatenate(tiles, axis=1))
                adds.append(jnp.concatenate(rows, axis=0) + am)
            _flash_step(qs, k_ref[p, pl.ds(ks, tk), :], v_ref[p, pl.ds(ks, tk), :],
                        jnp.concatenate(adds, axis=0), m_ref, l_ref, acc_ref)
            return c2

        lax.fori_loop(n_far, n_blk, near_body, 0)
        stage_ref[p] = _flash_out(l_ref, acc_ref, tq)
        return carry

    lax.fori_loop(0, N_PAIRS, pair_body, 0)
    for p in range(N_PAIRS):
        o_ref[:, p * LANES:(p + 1) * LANES] = stage_ref[p]


def _dsa_attn(q, k, v, qp, kp, wi, bias, tq=256, tk=256):
    B, _, S, _ = q.shape
    topk = min(TOPK_MAX, S // 4)
    once = pl.Buffered(1)
    kv_spec = pl.BlockSpec((None, N_PAIRS, S, LANES), lambda b, i: (b, 0, 0, 0), pipeline_mode=once)
    return pl.pallas_call(
        functools.partial(_dsa_attn_kernel, tq=tq, tk=tk, topk=topk),
        grid=(B, S // tq),
        in_specs=[
            pl.BlockSpec((None, N_PAIRS, tq, LANES), lambda b, i: (b, 0, i, 0)),
            pl.BlockSpec((None, H_IDX, tq, 256), lambda b, i: (b, 0, i, 0)),
            pl.BlockSpec((None, tq, LANES), lambda b, i: (b, i, 0)),
            pl.BlockSpec((N_PAIRS, 2, 2, LANES, LANES), lambda b, i: (0, 0, 0, 0, 0), pipeline_mode=once),
            kv_spec, kv_spec,
            pl.BlockSpec((None, S, 256), lambda b, i: (b, 0, 0), pipeline_mode=once),
        ],
        out_specs=pl.BlockSpec((None, tq, N_HEADS * HEAD_DIM), lambda b, i: (b, i, 0)),
        out_shape=jax.ShapeDtypeStruct((B, S, N_HEADS * HEAD_DIM), BF16),
        scratch_shapes=[
            pltpu.VMEM((S // tk, tq, tk), jnp.int32),
            pltpu.VMEM((2 * tq, 1), F32), pltpu.VMEM((2 * tq, 1), F32), pltpu.VMEM((2 * tq, LANES), F32),
            pltpu.VMEM((N_PAIRS, tq, LANES), BF16),
        ],
        compiler_params=_cparams(("parallel", "arbitrary")),
        name="dsa_attn",
    )(q, qp, wi, bias, k, v, kp)


def _out_proj_kernel(x_ref, a_ref, w_ref, o_ref):
    o_ref[...] = x_ref[...] + _dot(a_ref[...], w_ref[...])


def _out_proj(x, a, w, tm=512):
    B, S, D = x.shape
    return pl.pallas_call(
        _out_proj_kernel,
        grid=(B, S // tm),
        in_specs=[
            pl.BlockSpec((None, tm, D), lambda b, i: (b, i, 0)),
            pl.BlockSpec((None, tm, a.shape[-1]), lambda b, i: (b, i, 0)),
            pl.BlockSpec(w.shape, lambda b, i: (0, 0)),
        ],
        out_specs=pl.BlockSpec((None, tm, D), lambda b, i: (b, i, 0)),
        out_shape=jax.ShapeDtypeStruct((B, S, D), F32),
        compiler_params=_cparams(("parallel", "parallel")),
        name="out_proj",
    )(x, a, w.astype(BF16))


def _ffn_kernel(x_ref, xp_ref, g_ref, wg_ref, wv_ref, cw_ref, cb_ref, wd_ref, o_ref, acc_ref, *, tm, n_chunks):
    i = pl.program_id(1)
    x = x_ref[...]
    g = g_ref[...]
    hp = _rms(xp_ref[...], g) * jnp.where(i > 0, 1.0, 0.0)
    he = jnp.concatenate([hp, _rms(x, g)], axis=0).astype(BF16)
    acc_ref[...] = jnp.zeros(acc_ref.shape, F32)

    def conv(u, w, b):
        return (b + w[0:1] * u[HALO - 2:HALO - 2 + tm] + w[1:2] * u[HALO - 1:HALO - 1 + tm]
                + w[2:3] * u[HALO:HALO + tm])

    def chunk_body(c, carry):
        cw = cw_ref[c]
        cb = cb_ref[c]
        gate = conv(_dot(he, wg_ref[c]), cw[0], cb[0])
        val = conv(_dot(he, wv_ref[c]), cw[1], cb[1])
        act = gate * jax.nn.sigmoid(gate) * val
        acc_ref[...] += _dot(act.astype(BF16), wd_ref[c])
        return carry

    lax.fori_loop(0, n_chunks, chunk_body, 0)
    o_ref[...] = x + acc_ref[...]


def _ffn(x, g, w_up, conv_w, conv_b, w_down, tm=512):
    B, S, D = x.shape
    nc = D_FF // FF_CHUNK
    wup = w_up.astype(BF16).reshape(D, 2, nc, FF_CHUNK).transpose(1, 2, 0, 3)
    cw = conv_w.reshape(CONV_W, 2, nc, FF_CHUNK).transpose(2, 1, 0, 3)
    cb = conv_b.reshape(2, nc, 1, FF_CHUNK).transpose(1, 0, 2, 3)
    wd = w_down.astype(BF16).reshape(nc, FF_CHUNK, D)
    c3 = lambda b, i: (0, 0, 0)
    c4 = lambda b, i: (0, 0, 0, 0)
    return pl.pallas_call(
        functools.partial(_ffn_kernel, tm=tm, n_chunks=nc),
        grid=(B, S // tm),
        in_specs=[
            pl.BlockSpec((None, tm, D), lambda b, i: (b, i, 0)),
            pl.BlockSpec((None, HALO, D), lambda b, i: (b, jnp.maximum(i * (tm // HALO) - 1, 0), 0)),
            pl.BlockSpec((1, D), lambda b, i: (0, 0)),
            pl.BlockSpec((nc, D, FF_CHUNK), c3),
            pl.BlockSpec((nc, D, FF_CHUNK), c3),
            pl.BlockSpec((nc, 2, CONV_W, FF_CHUNK), c4),
            pl.BlockSpec((nc, 2, 1, FF_CHUNK), c4),
            pl.BlockSpec((nc, FF_CHUNK, D), c3),
        ],
        out_specs=pl.BlockSpec((None, tm, D), lambda b, i: (b, i, 0)),
        out_shape=jax.ShapeDtypeStruct((B, S, D), F32),
        scratch_shapes=[pltpu.VMEM((tm, D), F32)],
        compiler_params=_cparams(("parallel", "parallel")),
        name="conv_ffn",
    )(x, x, g.reshape(1, D), wup[0], wup[1], cw, cb, wd)


def _rope_tables(S, period, off):
    inv = 1.0 / (ROPE_THETA ** (jnp.arange(0, ROPE_DIM, 2, dtype=F32) / ROPE_DIM))
    ang = jnp.arange(S, dtype=F32)[:, None] * inv[None, :]
    cos, sin = jnp.cos(ang), jnp.sin(ang)
    half = ROPE_DIM // 2
    lane = np.arange(LANES) % period
    is1 = (lane >= off) & (lane < off + half)
    is2 = (lane >= off + half) & (lane < off + ROPE_DIM)
    idx = np.where(is1, lane - off, np.where(is2, lane - off - half, 0))
    ct = jnp.where(is1 | is2, cos[:, idx], 1.0)
    s1 = jnp.where(is1, -sin[:, idx], 0.0)
    s2 = jnp.where(is2, sin[:, idx], 0.0)
    return ct, s1, s2


def _t5_bucket(rel):
    nb = T5_BUCKETS // 2
    max_exact = nb // 2
    n = jnp.abs(rel)
    large = max_exact + (jnp.log(jnp.maximum(n, 1).astype(F32) / max_exact)
                         / math.log(T5_MAX_DIST / max_exact) * (nb - max_exact)).astype(jnp.int32)
    large = jnp.minimum(large, nb - 1)
    return jnp.where(rel > 0, nb, 0) + jnp.where(n < max_exact, n, large)


def _dsa_bias_tiles(t5_bias):
    qi = jnp.arange(LANES, dtype=jnp.int32)[:, None]
    kj = jnp.arange(LANES, dtype=jnp.int32)[None, :]
    far = t5_bias[_t5_bucket(jnp.int32(-4 * T5_MAX_DIST))]
    same = jnp.transpose(t5_bias[_t5_bucket(kj - qi)], (2, 0, 1)) - far[:, None, None]
    same = jnp.where((kj < (qi // CHUNK + 1) * CHUNK)[None], same, NEG)
    prev = jnp.transpose(t5_bias[_t5_bucket(kj - LANES - qi)], (2, 0, 1)) - far[:, None, None]
    return jnp.stack([same, prev], axis=1).reshape(N_PAIRS, 2, 2, LANES, LANES)


def _band_bias(rel_bias, tq):
    pad = LEFT_CHUNKS * CHUNK
    nblk = -(-pad // tq) + 1
    lead = (nblk - 1) * tq
    i = jnp.arange(tq, dtype=jnp.int32)[:, None]
    kk = jnp.arange(nblk * tq, dtype=jnp.int32)[None, :]
    dist = i - (kk - lead)
    first = (i // CHUNK) * CHUNK - pad
    ok = (kk - lead >= first) & (kk - lead < (i // CHUNK + 1) * CHUNK)
    b = rel_bias[:, jnp.clip(dist, -(CHUNK - 1), REL_MAX_PAST) + CHUNK - 1]
    return jnp.where(ok[None], b, NEG).reshape(N_PAIRS, 2, tq, nblk * tq)


def kernel(x, norm_mix, norm_ffn, t5_bias, a_w_in, a_q_norm, a_k_norm, a_kidx_norm, a_w_out, b_w_down, b_q_a_norm, b_kv_a_norm, b_w_uq, b_w_ukv, b_q_norm, b_k_norm, b_w_out, c_w_in, c_q_norm, c_k_norm, c_rel_bias, c_w_out, f_w_up, f_conv_w, f_conv_b, f_w_down):
    B, S, D = x.shape
    depth = norm_mix.shape[0]
    tq_c = 256
    tabs_idx = _rope_tables(S, D_IDX, D_IDX - ROPE_DIM)
    tabs_mla = _rope_tables(S, LANES, QK_NOPE)
    scale = HEAD_DIM ** -0.5
    for layer in range(depth):
        kind, j = layer % N_MIXERS, layer // N_MIXERS
        g = norm_mix[layer]
        if kind == 0:
            q, k, v = _qkv_proj(x, g, a_w_in[j][:, :A_QKV], a_q_norm[j] * scale, a_k_norm[j])
            qp, kp, wi = _idx_proj(x, g, a_w_in[j][:, A_QKV:], a_kidx_norm[j], tabs_idx)
            att = _dsa_attn(q, k, v, qp, kp, wi, _dsa_bias_tiles(t5_bias))
            w_out = a_w_out[j]
        elif kind == 1:
            q, k, v = _mla_proj(x, g, b_w_down[j], b_q_a_norm[j], b_kv_a_norm[j], b_w_uq[j], b_w_ukv[j],
                                b_q_norm[j], b_k_norm[j], tabs_mla)
            att = _mla_attn(q, k, v)
            w_out = b_w_out[j]
        else:
            q, k, v = _qkv_proj(x, g, c_w_in[j], c_q_norm[j] * scale, c_k_norm[j])
            att = _band_attn(q, k, v, _band_bias(c_rel_bias[j], tq_c), tq=tq_c)
            w_out = c_w_out[j]
        x = _out_proj(x, att, w_out)
        x = _ffn(x, norm_ffn[layer], f_w_up[layer], f_conv_w[layer], f_conv_b[layer], f_w_down[layer])
    return x
```

```python
import functools
import math

import numpy as np
import jax
import jax.numpy as jnp
from jax import lax
from jax.experimental import pallas as pl
from jax.experimental.pallas import tpu as pltpu

F32 = jnp.float32
BF16 = jnp.bfloat16

D_MODEL = 1024
CHUNK = 64
EPS = 1e-6
ROPE_THETA = 10000.0
NEG = -1e30
NEG_BITS = int(np.float32(NEG).view(np.int32))
M_INIT = -1e38
INT_MIN = -(2 ** 31)

N_MIXERS = 3
N_HEADS = 16
HEAD_DIM = 64
N_PAIRS = N_HEADS // 2
LANES = 128
VT_BLK = 256

H_IDX = 8
D_IDX = 64
ROPE_DIM = 32
TOPK_MAX = 256
T5_BUCKETS = 32
T5_MAX_DIST = 128
A_QKV = 3 * N_HEADS * HEAD_DIM
Q_LORA = 256
KV_LORA = 128
QK_NOPE = 64
QK_ROPE = 32
V_DIM = 64
B_QK = QK_NOPE + QK_ROPE
LEFT_CHUNKS = 8
REL_MAX_PAST = 128
D_FF = 2816
CONV_W = 3
FF_CHUNK = 256
HALO = 8

VMEM_LIMIT = 56 * 1024 * 1024


def _cparams(sem):
    return pltpu.CompilerParams(dimension_semantics=sem, vmem_limit_bytes=VMEM_LIMIT)


def _rms(x, g):
    return x * lax.rsqrt(jnp.mean(x * x, axis=-1, keepdims=True) + EPS) * g


def _dot(a, b):
    return jnp.dot(a, b, preferred_element_type=F32)


def _dot_t(a, b):
    return lax.dot_general(a, b, (((1,), (1,)), ((), ())), preferred_element_type=F32)


def _split_bf16(x):
    hi = x.astype(BF16)
    lo = (x - hi.astype(F32)).astype(BF16)
    return hi, lo


def _group_sumsq(y, bd):
    hi, lo = _split_bf16(y * y)
    return _dot(hi, bd) + _dot(lo, bd)


def _rope(y, ct, s1, s2):
    return y * ct + pltpu.roll(y, LANES - 16, 1) * s1 + pltpu.roll(y, 16, 1) * s2


def _rope_wide(y, ct, s1, s2):
    w = y.shape[-1]
    parts = [_rope(y[:, c:c + LANES], ct, s1, s2) for c in range(0, w, LANES)]
    return parts[0] if len(parts) == 1 else jnp.concatenate(parts, axis=-1)


def _store_vt(vt_ref, p, y):
    yt = y.T.astype(BF16)
    for r in range(y.shape[0] // VT_BLK):
        vt_ref[p, r] = yt[:, r * VT_BLK:(r + 1) * VT_BLK]


def _qkv_proj_kernel(x_ref, g_ref, w_ref, gq_ref, gk_ref, bd_ref, q_ref, k_ref, vt_ref):
    xb = _rms(x_ref[...], g_ref[...]).astype(BF16)
    bd = bd_ref[...]
    hd = N_HEADS * HEAD_DIM
    for which, (o_ref, gain_ref) in enumerate(((q_ref, gq_ref), (k_ref, gk_ref), (vt_ref, None))):
        for c in range(hd // 256):
            col = which * hd + c * 256
            y = _dot(xb, w_ref[:, col:col + 256])
            if gain_ref is None:
                _store_vt(o_ref, 2 * c, y[:, :LANES])
                _store_vt(o_ref, 2 * c + 1, y[:, LANES:])
            else:
                y = y * lax.rsqrt(_group_sumsq(y, bd) * (1.0 / HEAD_DIM) + EPS) * gain_ref[...]
                yb = y.astype(BF16)
                o_ref[2 * c] = yb[:, :LANES]
                o_ref[2 * c + 1] = yb[:, LANES:]


def _vt_spec(tm):
    return pl.BlockSpec((None, N_PAIRS, tm // VT_BLK, LANES, VT_BLK), lambda b, i: (b, 0, i, 0, 0))


def _vt_shape(B, S):
    return jax.ShapeDtypeStruct((B, N_PAIRS, S // VT_BLK, LANES, VT_BLK), BF16)


def _qkv_proj(x, g, w, gq, gk, tm=512):
    B, S, D = x.shape
    hd = N_HEADS * HEAD_DIM
    bd = jnp.kron(jnp.eye(256 // HEAD_DIM, dtype=F32), jnp.ones((HEAD_DIM, HEAD_DIM), F32)).astype(BF16)
    out = jax.ShapeDtypeStruct((B, N_PAIRS, S, LANES), BF16)
    const = lambda b, i: (0, 0)
    ospec = pl.BlockSpec((None, N_PAIRS, tm, LANES), lambda b, i: (b, 0, i, 0))
    return pl.pallas_call(
        _qkv_proj_kernel,
        grid=(B, S // tm),
        in_specs=[
            pl.BlockSpec((None, tm, D), lambda b, i: (b, i, 0)),
            pl.BlockSpec((1, D), const),
            pl.BlockSpec((D, 3 * hd), const),
            pl.BlockSpec((1, 256), const),
            pl.BlockSpec((1, 256), const),
            pl.BlockSpec((256, 256), const),
        ],
        out_specs=[ospec, ospec, _vt_spec(tm)],
        out_shape=[out, out, _vt_shape(B, S)],
        compiler_params=_cparams(("parallel", "parallel")),
        name="qkv_proj",
    )(x, g.reshape(1, D), w.astype(BF16), jnp.tile(gq, 4).reshape(1, 256), jnp.tile(gk, 4).reshape(1, 256), bd)


def _idx_proj_kernel(x_ref, g_ref, wh_ref, wl_ref, gki_ref, ct_ref, s1_ref, s2_ref, qp_ref, kp_ref, wi_ref,
                     *, w_scale):
    xn = _rms(x_ref[...], g_ref[...])
    xh, xl = _split_bf16(xn)

    def mm(c0, c1):
        wh = wh_ref[:, c0:c1]
        return _dot(xh, wh) + (_dot(xl, wh) + _dot(xh, wl_ref[:, c0:c1]))

    ct, s1, s2 = ct_ref[...], s1_ref[...], s2_ref[...]
    tm = xn.shape[0]
    low = lax.broadcasted_iota(jnp.int32, (tm, LANES), 1) < D_IDX
    zero = jnp.zeros((tm, LANES), F32)

    for c in range(H_IDX // 2):
        y = _rope(mm(c * LANES, (c + 1) * LANES), ct, s1, s2)
        hi = y.astype(BF16).astype(F32)
        lo = y - hi
        rhi = pltpu.roll(hi, D_IDX, 1)
        rlo = pltpu.roll(lo, D_IDX, 1)
        qp_ref[2 * c, :, :LANES] = jnp.where(low, hi, rlo).astype(BF16)
        qp_ref[2 * c, :, LANES:] = jnp.where(low, hi, zero).astype(BF16)
        qp_ref[2 * c + 1, :, :LANES] = jnp.where(low, rhi, lo).astype(BF16)
        qp_ref[2 * c + 1, :, LANES:] = jnp.where(low, rhi, zero).astype(BF16)

    c0 = H_IDX * D_IDX
    y = mm(c0, c0 + LANES)
    y = y * lax.rsqrt(jnp.sum(y * y, axis=-1, keepdims=True) * (1.0 / D_IDX) + EPS) * gki_ref[...]
    y = _rope(y, ct, s1, s2)
    hi = y.astype(BF16).astype(F32)
    lo = y - hi
    kp_ref[:, :LANES] = jnp.where(low, hi, pltpu.roll(hi, D_IDX, 1)).astype(BF16)
    kp_ref[:, LANES:] = jnp.where(low, lo, zero).astype(BF16)

    wi_ref[...] = (mm(c0 + LANES, c0 + 2 * LANES) * w_scale).T[:H_IDX]


def _idx_proj(x, g, w_idx, gki, tabs, tm=512):
    B, S, D = x.shape
    nq = H_IDX * D_IDX
    wq, wk, ww = w_idx[:, :nq], w_idx[:, nq:nq + D_IDX], w_idx[:, nq + D_IDX:]
    wpad = jnp.concatenate([
        wq, jnp.pad(wk, ((0, 0), (0, LANES - D_IDX))), jnp.pad(ww, ((0, 0), (0, LANES - H_IDX)))], axis=1)
    wh = wpad.astype(BF16)
    wl = (wpad - wh.astype(F32)).astype(BF16)
    ncol = nq + 2 * LANES
    gpad = jnp.pad(gki, (0, LANES - D_IDX)).reshape(1, LANES)
    const = lambda b, i: (0, 0)
    tspec = pl.BlockSpec((tm, LANES), lambda b, i: (i, 0))
    return pl.pallas_call(
        functools.partial(_idx_proj_kernel, w_scale=(D_IDX ** -0.5) * (H_IDX ** -0.5)),
        grid=(B, S // tm),
        in_specs=[
            pl.BlockSpec((None, tm, D), lambda b, i: (b, i, 0)),
            pl.BlockSpec((1, D), const),
            pl.BlockSpec((D, ncol), const),
            pl.BlockSpec((D, ncol), const),
            pl.BlockSpec((1, LANES), const),
            tspec, tspec, tspec,
        ],
        out_specs=[
            pl.BlockSpec((None, H_IDX, tm, 256), lambda b, i: (b, 0, i, 0)),
            pl.BlockSpec((None, tm, 256), lambda b, i: (b, i, 0)),
            pl.BlockSpec((None, H_IDX, tm), lambda b, i: (b, 0, i)),
        ],
        out_shape=[
            jax.ShapeDtypeStruct((B, H_IDX, S, 256), BF16),
            jax.ShapeDtypeStruct((B, S, 256), BF16),
            jax.ShapeDtypeStruct((B, H_IDX, S), F32),
        ],
        compiler_params=_cparams(("parallel", "parallel")),
        name="idx_proj",
    )(x, g.reshape(1, D), wh, wl, gpad, *tabs)


def _mla_proj_kernel(x_ref, g_ref, wd_ref, gqa_ref, gkva_ref, wuq_ref, wuk_ref, wuv_ref, gq_ref, gk_ref, bd_ref,
                     ct_ref, s1_ref, s2_ref, q_ref, k_ref, vt_ref):
    xb = _rms(x_ref[...], g_ref[...]).astype(BF16)
    d = _dot(xb, wd_ref[...])
    cq = _rms(d[:, :Q_LORA], gqa_ref[...]).astype(BF16)
    ckv = _rms(d[:, Q_LORA:Q_LORA + KV_LORA], gkva_ref[...]).astype(BF16)
    krz = pltpu.roll(d[:, Q_LORA + KV_LORA:], QK_NOPE, 1)
    krz2 = jnp.concatenate([krz, krz], axis=-1)
    bd = bd_ref[...]
    ct, s1, s2 = ct_ref[...], s1_ref[...], s2_ref[...]
    inv = 1.0 / B_QK
    for p in range(N_PAIRS):
        cols = slice(p * 256, (p + 1) * 256)
        y = _dot(cq, wuq_ref[:, cols])
        y = y * lax.rsqrt(_group_sumsq(y, bd) * inv + EPS) * gq_ref[...]
        q_ref[p] = _rope_wide(y, ct, s1, s2).astype(BF16)
        y = _dot(ckv, wuk_ref[:, cols]) + krz2
        y = y * lax.rsqrt(_group_sumsq(y, bd) * inv + EPS) * gk_ref[...]
        k_ref[p] = _rope_wide(y, ct, s1, s2).astype(BF16)
        _store_vt(vt_ref, p, _dot(ckv, wuv_ref[:, p * LANES:(p + 1) * LANES]))


def _mla_proj(x, g, w_down, gqa, gkva, w_uq, w_ukv, gq, gk, tabs, tm=512):
    B, S, D = x.shape
    nd = Q_LORA + KV_LORA + QK_ROPE
    wd = jnp.pad(w_down, ((0, 0), (0, 512 - nd))).astype(BF16)
    padh = LANES - B_QK
    wuq = jnp.pad(w_uq.reshape(Q_LORA, N_HEADS, B_QK), ((0, 0), (0, 0), (0, padh))).reshape(Q_LORA, N_HEADS * LANES)
    wukv = w_ukv.reshape(KV_LORA, N_HEADS, QK_NOPE + V_DIM)
    wuk = jnp.pad(wukv[:, :, :QK_NOPE], ((0, 0), (0, 0), (0, LANES - QK_NOPE))).reshape(KV_LORA, N_HEADS * LANES)
    wuv = wukv[:, :, QK_NOPE:].reshape(KV_LORA, N_HEADS * V_DIM)
    gq2 = jnp.tile(jnp.pad(gq * (B_QK ** -0.5), (0, padh)), 2).reshape(1, 256)
    gk2 = jnp.tile(jnp.pad(gk, (0, padh)), 2).reshape(1, 256)
    bd = jnp.kron(jnp.eye(2, dtype=F32), jnp.ones((LANES, LANES), F32)).astype(BF16)
    const = lambda b, i: (0, 0)
    tspec = pl.BlockSpec((tm, LANES), lambda b, i: (i, 0))
    qk_shape = jax.ShapeDtypeStruct((B, N_PAIRS, S, 256), BF16)
    qk_spec = pl.BlockSpec((None, N_PAIRS, tm, 256), lambda b, i: (b, 0, i, 0))
    return pl.pallas_call(
        _mla_proj_kernel,
        grid=(B, S // tm),
        in_specs=[
            pl.BlockSpec((None, tm, D), lambda b, i: (b, i, 0)),
            pl.BlockSpec((1, D), const),
            pl.BlockSpec((D, 512), const),
            pl.BlockSpec((1, Q_LORA), const),
            pl.BlockSpec((1, KV_LORA), const),
            pl.BlockSpec((Q_LORA, N_HEADS * LANES), const),
            pl.BlockSpec((KV_LORA, N_HEADS * LANES), const),
            pl.BlockSpec((KV_LORA, N_HEADS * V_DIM), const),
            pl.BlockSpec((1, 256), const),
            pl.BlockSpec((1, 256), const),
            pl.BlockSpec((256, 256), const),
            tspec, tspec, tspec,
        ],
        out_specs=[qk_spec, qk_spec, _vt_spec(tm)],
        out_shape=[qk_shape, qk_shape, _vt_shape(B, S)],
        compiler_params=_cparams(("parallel", "parallel")),
        name="mla_proj",
    )(x, g.reshape(1, D), wd, gqa.reshape(1, Q_LORA), gkva.reshape(1, KV_LORA), wuq.astype(BF16),
      wuk.astype(BF16), wuv.astype(BF16), gq2, gk2, bd, *tabs)


def _stack_heads(q2, half):
    qf = q2.astype(F32)
    low = lax.broadcasted_iota(jnp.int32, qf.shape, 1) < half
    zero = jnp.zeros_like(qf)
    return jnp.concatenate([jnp.where(low, qf, zero), jnp.where(low, zero, qf)], axis=0).astype(BF16)


def _flash_init(acc_ref):
    r = acc_ref.shape[1]
    acc_ref[...] = jnp.zeros(acc_ref.shape, F32)
    return jnp.full((1, r), M_INIT, F32), jnp.zeros((1, r), F32)


def _flash_step(qs, k, vt, add, m, l, acc_ref):
    s = _dot_t(k, qs)
    if add is not None:
        s = s + add
    m_new = jnp.maximum(m, jnp.max(s, axis=0, keepdims=True))
    alpha = jnp.exp(m - m_new)
    p = jnp.exp(s - m_new)
    l_new = alpha * l + jnp.sum(p, axis=0, keepdims=True)
    acc_ref[...] = alpha * acc_ref[...] + _dot(vt, p.astype(BF16))
    return m_new, l_new


def _flash_out(l, acc_ref, tq):
    o = acc_ref[...] / l
    ot = jnp.concatenate([o[:HEAD_DIM, :tq], o[HEAD_DIM:, tq:]], axis=0)
    return ot.T.astype(BF16)


def _load_vt(vt_ref, kb, tk):
    n = tk // VT_BLK
    parts = [vt_ref[kb * n + r] for r in range(n)]
    return parts[0] if n == 1 else jnp.concatenate(parts, axis=1)


def _band_attn_kernel(q_ref, k_ref, vt_ref, bias_ref, o_ref, acc_ref, *, tq, nblk):
    i = pl.program_id(2)
    qs = _stack_heads(q_ref[...], HEAD_DIM)
    m, l = _flash_init(acc_ref)
    for j in range(nblk):
        kb = i - (nblk - 1) + j

        def step(m, l, kb=kb, j=j):
            kc = jnp.maximum(kb, 0)
            ks = pl.multiple_of(kc * tq, tq)
            return _flash_step(qs, k_ref[pl.ds(ks, tq), :], _load_vt(vt_ref, kc, tq), bias_ref[j], m, l, acc_ref)

        if j == nblk - 1:
            m, l = step(m, l)
        else:
            m, l = lax.cond(kb >= 0, step, lambda m, l: (m, l), m, l)
    o_ref[...] = _flash_out(l, acc_ref, tq)


def _band_attn(q, k, vt, bias, tq=256):
    B, _, S, _ = q.shape
    nblk = bias.shape[1]
    return pl.pallas_call(
        functools.partial(_band_attn_kernel, tq=tq, nblk=nblk),
        grid=(B, N_PAIRS, S // tq),
        in_specs=[
            pl.BlockSpec((None, None, tq, LANES), lambda b, p, i: (b, p, i, 0)),
            pl.BlockSpec((None, None, S, LANES), lambda b, p, i: (b, p, 0, 0)),
            pl.BlockSpec((None, None, S // VT_BLK, LANES, VT_BLK), lambda b, p, i: (b, p, 0, 0, 0)),
            pl.BlockSpec((None, nblk, tq, 2 * tq), lambda b, p, i: (p, 0, 0, 0)),
        ],
        out_specs=pl.BlockSpec((None, tq, LANES), lambda b, p, i: (b, i, p)),
        out_shape=jax.ShapeDtypeStruct((B, S, N_HEADS * HEAD_DIM), BF16),
        scratch_shapes=[pltpu.VMEM((LANES, 2 * tq), F32)],
        compiler_params=_cparams(("parallel", "parallel", "arbitrary")),
        name="band_attn",
    )(q, k, vt, bias)


def _mla_attn_kernel(q_ref, k_ref, vt_ref, o_ref, acc_ref, *, tq, tk):
    i = pl.program_id(2)
    qs = _stack_heads(q_ref[...], LANES)
    t0 = i * tq
    n_full = t0 // tk
    n_blk = (t0 + tq + tk - 1) // tk

    def full_body(kb, ml):
        ks = pl.multiple_of(kb * tk, tk)
        return _flash_step(qs, k_ref[pl.ds(ks, tk), :], _load_vt(vt_ref, kb, tk), None, *ml, acc_ref)

    ml = lax.fori_loop(0, n_full, full_body, _flash_init(acc_ref))

    lim = ((t0 + lax.broadcasted_iota(jnp.int32, (1, tq), 1)) // CHUNK + 1) * CHUNK

    def edge_body(kb, ml):
        ks = pl.multiple_of(kb * tk, tk)
        row = ks + lax.broadcasted_iota(jnp.int32, (tk, tq), 0)
        add = jnp.where(row < lim, 0.0, NEG).astype(F32)
        return _flash_step(qs, k_ref[pl.ds(ks, tk), :], _load_vt(vt_ref, kb, tk),
                           jnp.concatenate([add, add], axis=1), *ml, acc_ref)

    _, l = lax.fori_loop(n_full, n_blk, edge_body, ml)
    o_ref[...] = _flash_out(l, acc_ref, tq)


def _mla_attn(q, k, vt, tq=256, tk=512):
    B, _, S, _ = q.shape
    tk = min(tk, S)
    return pl.pallas_call(
        functools.partial(_mla_attn_kernel, tq=tq, tk=tk),
        grid=(B, N_PAIRS, S // tq),
        in_specs=[
            pl.BlockSpec((None, None, tq, 256), lambda b, p, i: (b, p, i, 0)),
            pl.BlockSpec((None, None, S, 256), lambda b, p, i: (b, p, 0, 0)),
            pl.BlockSpec((None, None, S // VT_BLK, LANES, VT_BLK), lambda b, p, i: (b, p, 0, 0, 0)),
        ],
        out_specs=pl.BlockSpec((None, tq, LANES), lambda b, p, i: (b, i, p)),
        out_shape=jax.ShapeDtypeStruct((B, S, N_HEADS * V_DIM), BF16),
        scratch_shapes=[pltpu.VMEM((LANES, 2 * tq), F32)],
        compiler_params=_cparams(("parallel", "parallel", "arbitrary")),
        name="mla_attn",
    )(q, k, vt)


def _dsa_attn_kernel(q_ref, qp_ref, wi_ref, bias_ref, k_ref, vt_ref, kp_ref, o_ref,
                     sc_ref, acc_ref, stage_ref, *, tq, tk, topk):
    i = pl.program_id(1)
    t0 = i * tq
    n_blk = (t0 + tq + tk - 1) // tk
    n_far = jnp.maximum((t0 // LANES - 1) // (tk // LANES), 0)
    lim = ((t0 + lax.broadcasted_iota(jnp.int32, (1, tq), 1)) // CHUNK + 1) * CHUNK

    wv = wi_ref[...]

    def score_body(kb, carry):
        ks = pl.multiple_of(kb * tk, tk)
        kp = kp_ref[pl.ds(ks, tk), :]
        acc = jnp.zeros((tk, tq), F32)
        for h in range(H_IDX):
            acc = acc + jnp.maximum(_dot_t(kp, qp_ref[h]), 0.0) * wv[h:h + 1, :]
        row = ks + lax.broadcasted_iota(jnp.int32, (tk, tq), 0)
        acc = jnp.where(row < lim, acc, -jnp.inf)
        bits = pltpu.bitcast(acc, jnp.int32)
        sc_ref[kb] = bits ^ ((bits >> 31) & 0x7FFFFFFF)
        return carry

    lax.fori_loop(0, n_blk, score_body, 0)

    def bit_body(t, ans):
        cand = ans + jnp.left_shift(jnp.int32(1), 31 - t)

        def count_body(kb, part):
            ge = jnp.where(sc_ref[kb] >= cand, 1, 0)
            return part + jnp.sum(ge.reshape(tk // 8, 8, tq), axis=0)

        part = lax.fori_loop(0, n_blk, count_body, jnp.zeros((8, tq), jnp.int32))
        cnt = jnp.sum(part, axis=0, keepdims=True)
        return jnp.where(cnt >= topk, cand, ans)

    thr = lax.fori_loop(0, 32, bit_body, jnp.full((1, tq), INT_MIN, jnp.int32))

    def mask_body(kb, carry):
        sc_ref[kb] = jnp.where(sc_ref[kb] >= thr, 0, NEG_BITS)
        return carry

    lax.fori_loop(0, n_blk, mask_body, 0)

    def pair_body(p, carry):
        qs = _stack_heads(q_ref[p], HEAD_DIM)

        def far_body(kb, ml):
            ks = pl.multiple_of(kb * tk, tk)
            am = pltpu.bitcast(sc_ref[kb], F32)
            return _flash_step(qs, k_ref[p, pl.ds(ks, tk), :], _load_vt(vt_ref.at[p], kb, tk),
                               jnp.concatenate([am, am], axis=1), *ml, acc_ref)

        ml = lax.fori_loop(0, n_far, far_body, _flash_init(acc_ref))

        def near_body(kb, ml):
            ks = pl.multiple_of(kb * tk, tk)
            am = pltpu.bitcast(sc_ref[kb], F32)
            cols = []
            for hh in range(2):
                t_same = bias_ref[p, hh, 0]
                t_prev = bias_ref[p, hh, 1]
                for a in range(tq // LANES):
                    tiles = []
                    for c in range(tk // LANES):
                        d = (kb * (tk // LANES) + c) - (i * (tq // LANES) + a)
                        tiles.append(jnp.where(d >= 1, NEG,
                                               jnp.where(d == 0, t_same, jnp.where(d == -1, t_prev, 0.0))))
                    cols.append(jnp.concatenate(tiles, axis=0))
            add = jnp.concatenate(cols, axis=1) + jnp.concatenate([am, am], axis=1)
            return _flash_step(qs, k_ref[p, pl.ds(ks, tk), :], _load_vt(vt_ref.at[p], kb, tk), add, *ml, acc_ref)

        _, l = lax.fori_loop(n_far, n_blk, near_body, ml)
        stage_ref[p] = _flash_out(l, acc_ref, tq)
        return carry

    lax.fori_loop(0, N_PAIRS, pair_body, 0)
    for p in range(N_PAIRS):
        o_ref[:, p * LANES:(p + 1) * LANES] = stage_ref[p]


def _dsa_attn(q, k, vt, qp, kp, wi, bias, tq=256, tk=256):
    B, _, S, _ = q.shape
    topk = min(TOPK_MAX, S // 4)
    once = pl.Buffered(1)
    return pl.pallas_call(
        functools.partial(_dsa_attn_kernel, tq=tq, tk=tk, topk=topk),
        grid=(B, S // tq),
        in_specs=[
            pl.BlockSpec((None, N_PAIRS, tq, LANES), lambda b, i: (b, 0, i, 0)),
            pl.BlockSpec((None, H_IDX, tq, 256), lambda b, i: (b, 0, i, 0)),
            pl.BlockSpec((None, H_IDX, tq), lambda b, i: (b, 0, i)),
            pl.BlockSpec((N_PAIRS, 2, 2, LANES, LANES), lambda b, i: (0, 0, 0, 0, 0), pipeline_mode=once),
            pl.BlockSpec((None, N_PAIRS, S, LANES), lambda b, i: (b, 0, 0, 0), pipeline_mode=once),
            pl.BlockSpec((None, N_PAIRS, S // VT_BLK, LANES, VT_BLK), lambda b, i: (b, 0, 0, 0, 0),
                         pipeline_mode=once),
            pl.BlockSpec((None, S, 256), lambda b, i: (b, 0, 0), pipeline_mode=once),
        ],
        out_specs=pl.BlockSpec((None, tq, N_HEADS * HEAD_DIM), lambda b, i: (b, i, 0)),
        out_shape=jax.ShapeDtypeStruct((B, S, N_HEADS * HEAD_DIM), BF16),
        scratch_shapes=[
            pltpu.VMEM((S // tk, tk, tq), jnp.int32),
            pltpu.VMEM((LANES, 2 * tq), F32),
            pltpu.VMEM((N_PAIRS, tq, LANES), BF16),
        ],
        compiler_params=_cparams(("parallel", "arbitrary")),
        name="dsa_attn",
    )(q, qp, wi, bias, k, vt, kp)


def _out_proj_kernel(x_ref, a_ref, w_ref, o_ref):
    o_ref[...] = x_ref[...] + _dot(a_ref[...], w_ref[...])


def _out_proj(x, a, w, tm=512):
    B, S, D = x.shape
    return pl.pallas_call(
        _out_proj_kernel,
        grid=(B, S // tm),
        in_specs=[
            pl.BlockSpec((None, tm, D), lambda b, i: (b, i, 0)),
            pl.BlockSpec((None, tm, a.shape[-1]), lambda b, i: (b, i, 0)),
            pl.BlockSpec(w.shape, lambda b, i: (0, 0)),
        ],
        out_specs=pl.BlockSpec((None, tm, D), lambda b, i: (b, i, 0)),
        out_shape=jax.ShapeDtypeStruct((B, S, D), F32),
        compiler_params=_cparams(("parallel", "parallel")),
        name="out_proj",
    )(x, a, w.astype(BF16))


def _ffn_kernel(x_ref, xp_ref, g_ref, wg_ref, wv_ref, cw_ref, cb_ref, wd_ref, o_ref, acc_ref, *, tm, n_chunks):
    i = pl.program_id(1)
    x = x_ref[...]
    g = g_ref[...]
    hp = _rms(xp_ref[...], g) * jnp.where(i > 0, 1.0, 0.0)
    he = jnp.concatenate([hp, _rms(x, g)], axis=0).astype(BF16)
    acc_ref[...] = jnp.zeros(acc_ref.shape, F32)

    def conv(u, w, b):
        return (b + w[0:1] * u[HALO - 2:HALO - 2 + tm] + w[1:2] * u[HALO - 1:HALO - 1 + tm]
                + w[2:3] * u[HALO:HALO + tm])

    def chunk_body(c, carry):
        cw = cw_ref[c]
        cb = cb_ref[c]
        gate = conv(_dot(he, wg_ref[c]), cw[0], cb[0])
        val = conv(_dot(he, wv_ref[c]), cw[1], cb[1])
        act = gate * jax.nn.sigmoid(gate) * val
        acc_ref[...] += _dot(act.astype(BF16), wd_ref[c])
        return carry

    lax.fori_loop(0, n_chunks, chunk_body, 0)
    o_ref[...] = x + acc_ref[...]


def _ffn(x, g, w_up, conv_w, conv_b, w_down, tm=512):
    B, S, D = x.shape
    nc = D_FF // FF_CHUNK
    wup = w_up.astype(BF16).reshape(D, 2, nc, FF_CHUNK).transpose(1, 2, 0, 3)
    cw = conv_w.reshape(CONV_W, 2, nc, FF_CHUNK).transpose(2, 1, 0, 3)
    cb = conv_b.reshape(2, nc, 1, FF_CHUNK).transpose(1, 0, 2, 3)
    wd = w_down.astype(BF16).reshape(nc, FF_CHUNK, D)
    c3 = lambda b, i: (0, 0, 0)
    c4 = lambda b, i: (0, 0, 0, 0)
    return pl.pallas_call(
        functools.partial(_ffn_kernel, tm=tm, n_chunks=nc),
        grid=(B, S // tm),
        in_specs=[
            pl.BlockSpec((None, tm, D), lambda b, i: (b, i, 0)),
            pl.BlockSpec((None, HALO, D), lambda b, i: (b, jnp.maximum(i * (tm // HALO) - 1, 0), 0)),
            pl.BlockSpec((1, D), lambda b, i: (0, 0)),
            pl.BlockSpec((nc, D, FF_CHUNK), c3),
            pl.BlockSpec((nc, D, FF_CHUNK), c3),
            pl.BlockSpec((nc, 2, CONV_W, FF_CHUNK), c4),
            pl.BlockSpec((nc, 2, 1, FF_CHUNK), c4),
            pl.BlockSpec((nc, FF_CHUNK, D), c3),
        ],
        out_specs=pl.BlockSpec((None, tm, D), lambda b, i: (b, i, 0)),
        out_shape=jax.ShapeDtypeStruct((B, S, D), F32),
        scratch_shapes=[pltpu.VMEM((tm, D), F32)],
        compiler_params=_cparams(("parallel", "parallel")),
        name="conv_ffn",
    )(x, x, g.reshape(1, D), wup[0], wup[1], cw, cb, wd)


def _rope_tables(S, period, off):
    inv = 1.0 / (ROPE_THETA ** (jnp.arange(0, ROPE_DIM, 2, dtype=F32) / ROPE_DIM))
    ang = jnp.arange(S, dtype=F32)[:, None] * inv[None, :]
    cos, sin = jnp.cos(ang), jnp.sin(ang)
    half = ROPE_DIM // 2
    lane = np.arange(LANES) % period
    is1 = (lane >= off) & (lane < off + half)
    is2 = (lane >= off + half) & (lane < off + ROPE_DIM)
    idx = np.where(is1, lane - off, np.where(is2, lane - off - half, 0))
    ct = jnp.where(is1 | is2, cos[:, idx], 1.0)
    s1 = jnp.where(is1, -sin[:, idx], 0.0)
    s2 = jnp.where(is2, sin[:, idx], 0.0)
    return ct, s1, s2


def _t5_bucket(rel):
    nb = T5_BUCKETS // 2
    max_exact = nb // 2
    n = jnp.abs(rel)
    large = max_exact + (jnp.log(jnp.maximum(n, 1).astype(F32) / max_exact)
                         / math.log(T5_MAX_DIST / max_exact) * (nb - max_exact)).astype(jnp.int32)
    large = jnp.minimum(large, nb - 1)
    return jnp.where(rel > 0, nb, 0) + jnp.where(n < max_exact, n, large)


def _dsa_bias_tiles(t5_bias):
    kj = jnp.arange(LANES, dtype=jnp.int32)[:, None]
    qi = jnp.arange(LANES, dtype=jnp.int32)[None, :]
    far = t5_bias[_t5_bucket(jnp.int32(-4 * T5_MAX_DIST))]
    same = jnp.transpose(t5_bias[_t5_bucket(kj - qi)], (2, 0, 1)) - far[:, None, None]
    same = jnp.where((kj < (qi // CHUNK + 1) * CHUNK)[None], same, NEG)
    prev = jnp.transpose(t5_bias[_t5_bucket(kj - LANES - qi)], (2, 0, 1)) - far[:, None, None]
    return jnp.stack([same, prev], axis=1).reshape(N_PAIRS, 2, 2, LANES, LANES)


def _band_bias(rel_bias, tq):
    pad = LEFT_CHUNKS * CHUNK
    nblk = -(-pad // tq) + 1
    lead = (nblk - 1) * tq
    kk = jnp.arange(nblk * tq, dtype=jnp.int32)[:, None] - lead
    i = jnp.arange(tq, dtype=jnp.int32)[None, :]
    first = (i // CHUNK) * CHUNK - pad
    ok = (kk >= first) & (kk < (i // CHUNK + 1) * CHUNK)
    b = rel_bias[:, jnp.clip(i - kk, -(CHUNK - 1), REL_MAX_PAST) + CHUNK - 1]
    b = jnp.where(ok[None], b, NEG).reshape(N_PAIRS, 2, nblk, tq, tq)
    return jnp.transpose(b, (0, 2, 3, 1, 4)).reshape(N_PAIRS, nblk, tq, 2 * tq)


def kernel(x, norm_mix, norm_ffn, t5_bias, a_w_in, a_q_norm, a_k_norm, a_kidx_norm, a_w_out, b_w_down, b_q_a_norm, b_kv_a_norm, b_w_uq, b_w_ukv, b_q_norm, b_k_norm, b_w_out, c_w_in, c_q_norm, c_k_norm, c_rel_bias, c_w_out, f_w_up, f_conv_w, f_conv_b, f_w_down):
    B, S, D = x.shape
    depth = norm_mix.shape[0]
    tq_c = 256
    tabs_idx = _rope_tables(S, D_IDX, D_IDX - ROPE_DIM)
    tabs_mla = _rope_tables(S, LANES, QK_NOPE)
    scale = HEAD_DIM ** -0.5
    for layer in range(depth):
        kind, j = layer % N_MIXERS, layer // N_MIXERS
        g = norm_mix[layer]
        if kind == 0:
            q, k, vt = _qkv_proj(x, g, a_w_in[j][:, :A_QKV], a_q_norm[j] * scale, a_k_norm[j])
            qp, kp, wi = _idx_proj(x, g, a_w_in[j][:, A_QKV:], a_kidx_norm[j], tabs_idx)
            att = _dsa_attn(q, k, vt, qp, kp, wi, _dsa_bias_tiles(t5_bias))
            w_out = a_w_out[j]
        elif kind == 1:
            q, k, vt = _mla_proj(x, g, b_w_down[j], b_q_a_norm[j], b_kv_a_norm[j], b_w_uq[j], b_w_ukv[j],
                                 b_q_norm[j], b_k_norm[j], tabs_mla)
            att = _mla_attn(q, k, vt)
            w_out = b_w_out[j]
        else:
            q, k, vt = _qkv_proj(x, g, c_w_in[j], c_q_norm[j] * scale, c_k_norm[j])
            att = _band_attn(q, k, vt, _band_bias(c_rel_bias[j], tq_c), tq=tq_c)
            w_out = c_w_out[j]
        x = _out_proj(x, att, w_out)
        x = _ffn(x, norm_ffn[layer], f_w_up[layer], f_conv_w[layer], f_conv_b[layer], f_w_down[layer])
    return x
```

```python
import functools
import math

import numpy as np
import jax
import jax.numpy as jnp
from jax import lax
from jax.experimental import pallas as pl
from jax.experimental.pallas import tpu as pltpu

F32 = jnp.float32
BF16 = jnp.bfloat16

D_MODEL = 1024
CHUNK = 64
EPS = 1e-6
ROPE_THETA = 10000.0
NEG = -1e30
NEG_BITS = int(np.float32(NEG).view(np.int32))
M_INIT = -1e38
INT_MIN = -(2 ** 31)
LOG2E = math.log2(math.e)

N_MIXERS = 3
N_HEADS = 16
HEAD_DIM = 64
N_PAIRS = N_HEADS // 2
LANES = 128
VT_BLK = 256

H_IDX = 8
D_IDX = 64
ROPE_DIM = 32
TOPK_MAX = 256
T5_BUCKETS = 32
T5_MAX_DIST = 128
A_QKV = 3 * N_HEADS * HEAD_DIM
Q_LORA = 256
KV_LORA = 128
QK_NOPE = 64
QK_ROPE = 32
V_DIM = 64
B_QK = QK_NOPE + QK_ROPE
LEFT_CHUNKS = 8
REL_MAX_PAST = 128
D_FF = 2816
CONV_W = 3
FF_CHUNK = 256
HALO = 8

VMEM_LIMIT = 56 * 1024 * 1024


def _cparams(sem):
    return pltpu.CompilerParams(dimension_semantics=sem, vmem_limit_bytes=VMEM_LIMIT)


def _rms(x, g):
    return x * lax.rsqrt(jnp.mean(x * x, axis=-1, keepdims=True) + EPS) * g


def _dot(a, b):
    return jnp.dot(a, b, preferred_element_type=F32)


def _dot_t(a, b):
    return lax.dot_general(a, b, (((1,), (1,)), ((), ())), preferred_element_type=F32)


def _split_bf16(x):
    hi = x.astype(BF16)
    lo = (x - hi.astype(F32)).astype(BF16)
    return hi, lo


def _group_sumsq(y, bd):
    hi, lo = _split_bf16(y * y)
    return _dot(hi, bd) + _dot(lo, bd)


def _rope(y, ct, s1, s2):
    return y * ct + pltpu.roll(y, LANES - 16, 1) * s1 + pltpu.roll(y, 16, 1) * s2


def _rope_wide(y, ct, s1, s2):
    w = y.shape[-1]
    parts = [_rope(y[:, c:c + LANES], ct, s1, s2) for c in range(0, w, LANES)]
    return parts[0] if len(parts) == 1 else jnp.concatenate(parts, axis=-1)


def _store_vt(vt_ref, p, y):
    yt = y.T.astype(BF16)
    for r in range(y.shape[0] // VT_BLK):
        vt_ref[p, r] = yt[:, r * VT_BLK:(r + 1) * VT_BLK]


def _qkv_proj_kernel(x_ref, g_ref, w_ref, gq_ref, gk_ref, bd_ref, q_ref, k_ref, vt_ref):
    xb = _rms(x_ref[...], g_ref[...]).astype(BF16)
    bd = bd_ref[...]
    hd = N_HEADS * HEAD_DIM
    for which, (o_ref, gain_ref) in enumerate(((q_ref, gq_ref), (k_ref, gk_ref), (vt_ref, None))):
        for c in range(hd // 256):
            col = which * hd + c * 256
            y = _dot(xb, w_ref[:, col:col + 256])
            if gain_ref is None:
                _store_vt(o_ref, 2 * c, y[:, :LANES])
                _store_vt(o_ref, 2 * c + 1, y[:, LANES:])
            else:
                y = y * lax.rsqrt(_group_sumsq(y, bd) * (1.0 / HEAD_DIM) + EPS) * gain_ref[...]
                yb = y.astype(BF16)
                o_ref[2 * c] = yb[:, :LANES]
                o_ref[2 * c + 1] = yb[:, LANES:]


def _vt_spec(tm):
    return pl.BlockSpec((None, N_PAIRS, tm // VT_BLK, LANES, VT_BLK), lambda b, i: (b, 0, i, 0, 0))


def _vt_shape(B, S):
    return jax.ShapeDtypeStruct((B, N_PAIRS, S // VT_BLK, LANES, VT_BLK), BF16)


def _qkv_proj(x, g, w, gq, gk, tm=512):
    B, S, D = x.shape
    hd = N_HEADS * HEAD_DIM
    bd = jnp.kron(jnp.eye(256 // HEAD_DIM, dtype=F32), jnp.ones((HEAD_DIM, HEAD_DIM), F32)).astype(BF16)
    out = jax.ShapeDtypeStruct((B, N_PAIRS, S, LANES), BF16)
    const = lambda b, i: (0, 0)
    ospec = pl.BlockSpec((None, N_PAIRS, tm, LANES), lambda b, i: (b, 0, i, 0))
    return pl.pallas_call(
        _qkv_proj_kernel,
        grid=(B, S // tm),
        in_specs=[
            pl.BlockSpec((None, tm, D), lambda b, i: (b, i, 0)),
            pl.BlockSpec((1, D), const),
            pl.BlockSpec((D, 3 * hd), const),
            pl.BlockSpec((1, 256), const),
            pl.BlockSpec((1, 256), const),
            pl.BlockSpec((256, 256), const),
        ],
        out_specs=[ospec, ospec, _vt_spec(tm)],
        out_shape=[out, out, _vt_shape(B, S)],
        compiler_params=_cparams(("parallel", "parallel")),
        name="qkv_proj",
    )(x, g.reshape(1, D), w.astype(BF16), jnp.tile(gq, 4).reshape(1, 256), jnp.tile(gk, 4).reshape(1, 256), bd)


def _idx_proj_kernel(x_ref, g_ref, wh_ref, wl_ref, gki_ref, ct_ref, s1_ref, s2_ref, qp_ref, kp_ref, wi_ref,
                     *, w_scale):
    xn = _rms(x_ref[...], g_ref[...])
    xh, xl = _split_bf16(xn)

    def mm(c0, c1):
        wh = wh_ref[:, c0:c1]
        return _dot(xh, wh) + (_dot(xl, wh) + _dot(xh, wl_ref[:, c0:c1]))

    ct, s1, s2 = ct_ref[...], s1_ref[...], s2_ref[...]
    tm = xn.shape[0]
    low = lax.broadcasted_iota(jnp.int32, (tm, LANES), 1) < D_IDX
    zero = jnp.zeros((tm, LANES), F32)

    for c in range(H_IDX // 2):
        y = _rope(mm(c * LANES, (c + 1) * LANES), ct, s1, s2)
        hi = y.astype(BF16).astype(F32)
        lo = y - hi
        rhi = pltpu.roll(hi, D_IDX, 1)
        rlo = pltpu.roll(lo, D_IDX, 1)
        qp_ref[2 * c, :, :LANES] = jnp.where(low, hi, rlo).astype(BF16)
        qp_ref[2 * c, :, LANES:] = jnp.where(low, hi, zero).astype(BF16)
        qp_ref[2 * c + 1, :, :LANES] = jnp.where(low, rhi, lo).astype(BF16)
        qp_ref[2 * c + 1, :, LANES:] = jnp.where(low, rhi, zero).astype(BF16)

    c0 = H_IDX * D_IDX
    y = mm(c0, c0 + LANES)
    y = y * lax.rsqrt(jnp.sum(y * y, axis=-1, keepdims=True) * (1.0 / D_IDX) + EPS) * gki_ref[...]
    y = _rope(y, ct, s1, s2)
    hi = y.astype(BF16).astype(F32)
    lo = y - hi
    kp_ref[:, :LANES] = jnp.where(low, hi, pltpu.roll(hi, D_IDX, 1)).astype(BF16)
    kp_ref[:, LANES:] = jnp.where(low, lo, zero).astype(BF16)

    wi_ref[...] = (mm(c0 + LANES, c0 + 2 * LANES) * w_scale).T[:H_IDX]


def _idx_proj(x, g, w_idx, gki, tabs, tm=512):
    B, S, D = x.shape
    nq = H_IDX * D_IDX
    wq, wk, ww = w_idx[:, :nq], w_idx[:, nq:nq + D_IDX], w_idx[:, nq + D_IDX:]
    wpad = jnp.concatenate([
        wq, jnp.pad(wk, ((0, 0), (0, LANES - D_IDX))), jnp.pad(ww, ((0, 0), (0, LANES - H_IDX)))], axis=1)
    wh = wpad.astype(BF16)
    wl = (wpad - wh.astype(F32)).astype(BF16)
    ncol = nq + 2 * LANES
    gpad = jnp.pad(gki, (0, LANES - D_IDX)).reshape(1, LANES)
    const = lambda b, i: (0, 0)
    tspec = pl.BlockSpec((tm, LANES), lambda b, i: (i, 0))
    return pl.pallas_call(
        functools.partial(_idx_proj_kernel, w_scale=(D_IDX ** -0.5) * (H_IDX ** -0.5)),
        grid=(B, S // tm),
        in_specs=[
            pl.BlockSpec((None, tm, D), lambda b, i: (b, i, 0)),
            pl.BlockSpec((1, D), const),
            pl.BlockSpec((D, ncol), const),
            pl.BlockSpec((D, ncol), const),
            pl.BlockSpec((1, LANES), const),
            tspec, tspec, tspec,
        ],
        out_specs=[
            pl.BlockSpec((None, H_IDX, tm, 256), lambda b, i: (b, 0, i, 0)),
            pl.BlockSpec((None, tm, 256), lambda b, i: (b, i, 0)),
            pl.BlockSpec((None, H_IDX, tm), lambda b, i: (b, 0, i)),
        ],
        out_shape=[
            jax.ShapeDtypeStruct((B, H_IDX, S, 256), BF16),
            jax.ShapeDtypeStruct((B, S, 256), BF16),
            jax.ShapeDtypeStruct((B, H_IDX, S), F32),
        ],
        compiler_params=_cparams(("parallel", "parallel")),
        name="idx_proj",
    )(x, g.reshape(1, D), wh, wl, gpad, *tabs)


def _mla_proj_kernel(x_ref, g_ref, wd_ref, gqa_ref, gkva_ref, wuq_ref, wuk_ref, wuv_ref, gq_ref, gk_ref, bd_ref,
                     ct_ref, s1_ref, s2_ref, q_ref, k_ref, vt_ref):
    xb = _rms(x_ref[...], g_ref[...]).astype(BF16)
    d = _dot(xb, wd_ref[...])
    cq = _rms(d[:, :Q_LORA], gqa_ref[...]).astype(BF16)
    ckv = _rms(d[:, Q_LORA:Q_LORA + KV_LORA], gkva_ref[...]).astype(BF16)
    krz = pltpu.roll(d[:, Q_LORA + KV_LORA:], QK_NOPE, 1)
    krz2 = jnp.concatenate([krz, krz], axis=-1)
    bd = bd_ref[...]
    ct, s1, s2 = ct_ref[...], s1_ref[...], s2_ref[...]
    inv = 1.0 / B_QK
    for p in range(N_PAIRS):
        cols = slice(p * 256, (p + 1) * 256)
        y = _dot(cq, wuq_ref[:, cols])
        y = y * lax.rsqrt(_group_sumsq(y, bd) * inv + EPS) * gq_ref[...]
        q_ref[p] = _rope_wide(y, ct, s1, s2).astype(BF16)
        y = _dot(ckv, wuk_ref[:, cols]) + krz2
        y = y * lax.rsqrt(_group_sumsq(y, bd) * inv + EPS) * gk_ref[...]
        k_ref[p] = _rope_wide(y, ct, s1, s2).astype(BF16)
        _store_vt(vt_ref, p, _dot(ckv, wuv_ref[:, p * LANES:(p + 1) * LANES]))


def _mla_proj(x, g, w_down, gqa, gkva, w_uq, w_ukv, gq, gk, tabs, tm=512):
    B, S, D = x.shape
    nd = Q_LORA + KV_LORA + QK_ROPE
    wd = jnp.pad(w_down, ((0, 0), (0, 512 - nd))).astype(BF16)
    padh = LANES - B_QK
    wuq = jnp.pad(w_uq.reshape(Q_LORA, N_HEADS, B_QK), ((0, 0), (0, 0), (0, padh))).reshape(Q_LORA, N_HEADS * LANES)
    wukv = w_ukv.reshape(KV_LORA, N_HEADS, QK_NOPE + V_DIM)
    wuk = jnp.pad(wukv[:, :, :QK_NOPE], ((0, 0), (0, 0), (0, LANES - QK_NOPE))).reshape(KV_LORA, N_HEADS * LANES)
    wuv = wukv[:, :, QK_NOPE:].reshape(KV_LORA, N_HEADS * V_DIM)
    gq2 = jnp.tile(jnp.pad(gq * (B_QK ** -0.5 * LOG2E), (0, padh)), 2).reshape(1, 256)
    gk2 = jnp.tile(jnp.pad(gk, (0, padh)), 2).reshape(1, 256)
    bd = jnp.kron(jnp.eye(2, dtype=F32), jnp.ones((LANES, LANES), F32)).astype(BF16)
    const = lambda b, i: (0, 0)
    tspec = pl.BlockSpec((tm, LANES), lambda b, i: (i, 0))
    qk_shape = jax.ShapeDtypeStruct((B, N_PAIRS, S, 256), BF16)
    qk_spec = pl.BlockSpec((None, N_PAIRS, tm, 256), lambda b, i: (b, 0, i, 0))
    return pl.pallas_call(
        _mla_proj_kernel,
        grid=(B, S // tm),
        in_specs=[
            pl.BlockSpec((None, tm, D), lambda b, i: (b, i, 0)),
            pl.BlockSpec((1, D), const),
            pl.BlockSpec((D, 512), const),
            pl.BlockSpec((1, Q_LORA), const),
            pl.BlockSpec((1, KV_LORA), const),
            pl.BlockSpec((Q_LORA, N_HEADS * LANES), const),
            pl.BlockSpec((KV_LORA, N_HEADS * LANES), const),
            pl.BlockSpec((KV_LORA, N_HEADS * V_DIM), const),
            pl.BlockSpec((1, 256), const),
            pl.BlockSpec((1, 256), const),
            pl.BlockSpec((256, 256), const),
            tspec, tspec, tspec,
        ],
        out_specs=[qk_spec, qk_spec, _vt_spec(tm)],
        out_shape=[qk_shape, qk_shape, _vt_shape(B, S)],
        compiler_params=_cparams(("parallel", "parallel")),
        name="mla_proj",
    )(x, g.reshape(1, D), wd, gqa.reshape(1, Q_LORA), gkva.reshape(1, KV_LORA), wuq.astype(BF16),
      wuk.astype(BF16), wuv.astype(BF16), gq2, gk2, bd, *tabs)


def _stack_heads(q2, half):
    qf = q2.astype(F32)
    low = lax.broadcasted_iota(jnp.int32, qf.shape, 1) < half
    zero = jnp.zeros_like(qf)
    return jnp.concatenate([jnp.where(low, qf, zero), jnp.where(low, zero, qf)], axis=0).astype(BF16)


def _flash_init(acc_ref):
    r = acc_ref.shape[1]
    acc_ref[...] = jnp.zeros(acc_ref.shape, F32)
    return jnp.full((1, r), M_INIT, F32), jnp.zeros((1, r), F32)


def _consume(s, vt, m, l, acc_ref):
    m_new = jnp.maximum(m, jnp.max(s, axis=0, keepdims=True))
    alpha = jnp.exp2(m - m_new)
    p = jnp.exp2(s - m_new)
    l_new = alpha * l + jnp.sum(p, axis=0, keepdims=True)
    acc_ref[...] = alpha * acc_ref[...] + _dot(vt, p.astype(BF16))
    return m_new, l_new


def _sweep(first, n, s_ref, scores, values, ml, acc_ref):
    def one(j, s, ml):
        return _consume(s, values(j), *ml, acc_ref)

    odd = n % 2
    ml = lax.cond(odd == 1, lambda ml: one(first, scores(first), ml), lambda ml: ml, ml)
    j0 = first + odd
    last = first + n - 1

    @pl.when(n >= 2)
    def _():
        s_ref[0] = scores(j0)

    def body(t, ml):
        j = j0 + 2 * t
        s1 = scores(j + 1)
        ml = one(j, s_ref[0], ml)
        s_ref[1] = s1
        s0 = scores(jnp.minimum(j + 2, last))
        ml = one(j + 1, s_ref[1], ml)
        s_ref[0] = s0
        return ml

    return lax.fori_loop(0, n // 2, body, ml)


def _flash_out(l, acc_ref, tq):
    o = acc_ref[...] / l
    ot = jnp.concatenate([o[:HEAD_DIM, :tq], o[HEAD_DIM:, tq:]], axis=0)
    return ot.T.astype(BF16)


def _load_vt(vt_ref, kb, tk):
    n = tk // VT_BLK
    parts = [vt_ref[kb * n + r] for r in range(n)]
    return parts[0] if n == 1 else jnp.concatenate(parts, axis=1)


def _band_attn_kernel(q_ref, k_ref, vt_ref, bias_ref, o_ref, acc_ref, s_ref, *, tq, nblk):
    i = pl.program_id(2)
    qs = _stack_heads(q_ref[...], HEAD_DIM)
    lead = i - (nblk - 1)

    def scores(kb):
        ks = pl.multiple_of(kb * tq, tq)
        return _dot_t(k_ref[pl.ds(ks, tq), :], qs) + bias_ref[kb - lead]

    _, l = _sweep(jnp.maximum(lead, 0), jnp.minimum(i + 1, nblk), s_ref, scores,
                  lambda kb: _load_vt(vt_ref, kb, tq), _flash_init(acc_ref), acc_ref)
    o_ref[...] = _flash_out(l, acc_ref, tq)


def _band_attn(q, k, vt, bias, tq=256):
    B, _, S, _ = q.shape
    nblk = bias.shape[1]
    return pl.pallas_call(
        functools.partial(_band_attn_kernel, tq=tq, nblk=nblk),
        grid=(B, N_PAIRS, S // tq),
        in_specs=[
            pl.BlockSpec((None, None, tq, LANES), lambda b, p, i: (b, p, i, 0)),
            pl.BlockSpec((None, None, S, LANES), lambda b, p, i: (b, p, 0, 0)),
            pl.BlockSpec((None, None, S // VT_BLK, LANES, VT_BLK), lambda b, p, i: (b, p, 0, 0, 0)),
            pl.BlockSpec((None, nblk, tq, 2 * tq), lambda b, p, i: (p, 0, 0, 0)),
        ],
        out_specs=pl.BlockSpec((None, tq, LANES), lambda b, p, i: (b, i, p)),
        out_shape=jax.ShapeDtypeStruct((B, S, N_HEADS * HEAD_DIM), BF16),
        scratch_shapes=[pltpu.VMEM((LANES, 2 * tq), F32), pltpu.VMEM((2, tq, 2 * tq), F32)],
        compiler_params=_cparams(("parallel", "parallel", "arbitrary")),
        name="band_attn",
    )(q, k, vt, bias)


def _mla_attn_kernel(q_ref, k_ref, vt_ref, o_ref, acc_ref, s_ref, *, tq, tk):
    i = pl.program_id(2)
    qs = _stack_heads(q_ref[...], LANES)
    t0 = i * tq
    n_full = t0 // tk
    n_blk = (t0 + tq + tk - 1) // tk
    lim = ((t0 + lax.broadcasted_iota(jnp.int32, (1, tq), 1)) // CHUNK + 1) * CHUNK

    def scores(kb):
        ks = pl.multiple_of(kb * tk, tk)
        return _dot_t(k_ref[pl.ds(ks, tk), :], qs)

    def edge_scores(kb):
        row = kb * tk + lax.broadcasted_iota(jnp.int32, (tk, tq), 0)
        add = jnp.where(row < lim, 0.0, NEG).astype(F32)
        return scores(kb) + jnp.concatenate([add, add], axis=1)

    values = lambda kb: _load_vt(vt_ref, kb, tk)
    ml = _sweep(0, n_full, s_ref, scores, values, _flash_init(acc_ref), acc_ref)
    _, l = _sweep(n_full, n_blk - n_full, s_ref, edge_scores, values, ml, acc_ref)
    o_ref[...] = _flash_out(l, acc_ref, tq)


def _mla_attn(q, k, vt, tq=256, tk=512):
    B, _, S, _ = q.shape
    tk = min(tk, S)
    return pl.pallas_call(
        functools.partial(_mla_attn_kernel, tq=tq, tk=tk),
        grid=(B, N_PAIRS, S // tq),
        in_specs=[
            pl.BlockSpec((None, None, tq, 256), lambda b, p, i: (b, p, i, 0)),
            pl.BlockSpec((None, None, S, 256), lambda b, p, i: (b, p, 0, 0)),
            pl.BlockSpec((None, None, S // VT_BLK, LANES, VT_BLK), lambda b, p, i: (b, p, 0, 0, 0)),
        ],
        out_specs=pl.BlockSpec((None, tq, LANES), lambda b, p, i: (b, i, p)),
        out_shape=jax.ShapeDtypeStruct((B, S, N_HEADS * V_DIM), BF16),
        scratch_shapes=[pltpu.VMEM((LANES, 2 * tq), F32), pltpu.VMEM((2, tk, 2 * tq), F32)],
        compiler_params=_cparams(("parallel", "parallel", "arbitrary")),
        name="mla_attn",
    )(q, k, vt)


def _dsa_attn_kernel(q_ref, qp_ref, wi_ref, bias_ref, k_ref, vt_ref, kp_ref, o_ref,
                     sc_ref, acc_ref, stage_ref, s_ref, *, tq, tk, topk):
    i = pl.program_id(1)
    t0 = i * tq
    n_blk = (t0 + tq + tk - 1) // tk
    n_far = jnp.maximum((t0 // LANES - 1) // (tk // LANES), 0)
    lim = ((t0 + lax.broadcasted_iota(jnp.int32, (1, tq), 1)) // CHUNK + 1) * CHUNK

    wv = wi_ref[...]

    def score_body(kb, carry):
        ks = pl.multiple_of(kb * tk, tk)
        kp = kp_ref[pl.ds(ks, tk), :]
        acc = jnp.zeros((tk, tq), F32)
        for h in range(H_IDX):
            acc = acc + jnp.maximum(_dot_t(kp, qp_ref[h]), 0.0) * wv[h:h + 1, :]
        row = ks + lax.broadcasted_iota(jnp.int32, (tk, tq), 0)
        acc = jnp.where(row < lim, acc, -jnp.inf)
        bits = pltpu.bitcast(acc, jnp.int32)
        sc_ref[kb] = bits ^ ((bits >> 31) & 0x7FFFFFFF)
        return carry

    lax.fori_loop(0, n_blk, score_body, 0)

    def bit_body(t, ans):
        cand = ans + jnp.left_shift(jnp.int32(1), 31 - t)

        def count_body(kb, part):
            ge = jnp.where(sc_ref[kb] >= cand, 1, 0)
            return part + jnp.sum(ge.reshape(tk // 8, 8, tq), axis=0)

        part = lax.fori_loop(0, n_blk, count_body, jnp.zeros((8, tq), jnp.int32))
        cnt = jnp.sum(part, axis=0, keepdims=True)
        return jnp.where(cnt >= topk, cand, ans)

    thr = lax.fori_loop(0, 32, bit_body, jnp.full((1, tq), INT_MIN, jnp.int32))

    def mask_body(kb, carry):
        sc_ref[kb] = jnp.where(sc_ref[kb] >= thr, 0, NEG_BITS)
        return carry

    lax.fori_loop(0, n_blk, mask_body, 0)

    def pair_body(p, carry):
        qs = _stack_heads(q_ref[p], HEAD_DIM)

        def far_scores(kb):
            ks = pl.multiple_of(kb * tk, tk)
            am = pltpu.bitcast(sc_ref[kb], F32)
            return _dot_t(k_ref[p, pl.ds(ks, tk), :], qs) + jnp.concatenate([am, am], axis=1)

        def near_scores(kb):
            cols = []
            for hh in range(2):
                t_same = bias_ref[p, hh, 0]
                t_prev = bias_ref[p, hh, 1]
                for a in range(tq // LANES):
                    tiles = []
                    for c in range(tk // LANES):
                        d = (kb * (tk // LANES) + c) - (i * (tq // LANES) + a)
                        tiles.append(jnp.where(d >= 1, NEG,
                                               jnp.where(d == 0, t_same, jnp.where(d == -1, t_prev, 0.0))))
                    cols.append(jnp.concatenate(tiles, axis=0))
            return far_scores(kb) + jnp.concatenate(cols, axis=1)

        values = lambda kb: _load_vt(vt_ref.at[p], kb, tk)
        ml = _sweep(0, n_far, s_ref, far_scores, values, _flash_init(acc_ref), acc_ref)
        _, l = _sweep(n_far, n_blk - n_far, s_ref, near_scores, values, ml, acc_ref)
        stage_ref[p] = _flash_out(l, acc_ref, tq)
        return carry

    lax.fori_loop(0, N_PAIRS, pair_body, 0)
    for p in range(N_PAIRS):
        o_ref[:, p * LANES:(p + 1) * LANES] = stage_ref[p]


def _dsa_attn(q, k, vt, qp, kp, wi, bias, tq=256, tk=256):
    B, _, S, _ = q.shape
    topk = min(TOPK_MAX, S // 4)
    once = pl.Buffered(1)
    return pl.pallas_call(
        functools.partial(_dsa_attn_kernel, tq=tq, tk=tk, topk=topk),
        grid=(B, S // tq),
        in_specs=[
            pl.BlockSpec((None, N_PAIRS, tq, LANES), lambda b, i: (b, 0, i, 0)),
            pl.BlockSpec((None, H_IDX, tq, 256), lambda b, i: (b, 0, i, 0)),
            pl.BlockSpec((None, H_IDX, tq), lambda b, i: (b, 0, i)),
            pl.BlockSpec((N_PAIRS, 2, 2, LANES, LANES), lambda b, i: (0, 0, 0, 0, 0), pipeline_mode=once),
            pl.BlockSpec((None, N_PAIRS, S, LANES), lambda b, i: (b, 0, 0, 0), pipeline_mode=once),
            pl.BlockSpec((None, N_PAIRS, S // VT_BLK, LANES, VT_BLK), lambda b, i: (b, 0, 0, 0, 0),
                         pipeline_mode=once),
            pl.BlockSpec((None, S, 256), lambda b, i: (b, 0, 0), pipeline_mode=once),
        ],
        out_specs=pl.BlockSpec((None, tq, N_HEADS * HEAD_DIM), lambda b, i: (b, i, 0)),
        out_shape=jax.ShapeDtypeStruct((B, S, N_HEADS * HEAD_DIM), BF16),
        scratch_shapes=[
            pltpu.VMEM((S // tk, tk, tq), jnp.int32),
            pltpu.VMEM((LANES, 2 * tq), F32),
            pltpu.VMEM((N_PAIRS, tq, LANES), BF16),
            pltpu.VMEM((2, tk, 2 * tq), F32),
        ],
        compiler_params=_cparams(("parallel", "arbitrary")),
        name="dsa_attn",
    )(q, qp, wi, bias, k, vt, kp)


def _out_proj_kernel(x_ref, a_ref, w_ref, o_ref):
    o_ref[...] = x_ref[...] + _dot(a_ref[...], w_ref[...])


def _out_proj(x, a, w, tm=512):
    B, S, D = x.shape
    return pl.pallas_call(
        _out_proj_kernel,
        grid=(B, S // tm),
        in_specs=[
            pl.BlockSpec((None, tm, D), lambda b, i: (b, i, 0)),
            pl.BlockSpec((None, tm, a.shape[-1]), lambda b, i: (b, i, 0)),
            pl.BlockSpec(w.shape, lambda b, i: (0, 0)),
        ],
        out_specs=pl.BlockSpec((None, tm, D), lambda b, i: (b, i, 0)),
        out_shape=jax.ShapeDtypeStruct((B, S, D), F32),
        compiler_params=_cparams(("parallel", "parallel")),
        name="out_proj",
    )(x, a, w.astype(BF16))


def _ffn_kernel(x_ref, xp_ref, g_ref, wg_ref, wv_ref, cw_ref, cb_ref, wd_ref, o_ref, acc_ref, *, tm, n_chunks):
    i = pl.program_id(1)
    x = x_ref[...]
    g = g_ref[...]
    hp = _rms(xp_ref[...], g) * jnp.where(i > 0, 1.0, 0.0)
    he = jnp.concatenate([hp, _rms(x, g)], axis=0).astype(BF16)
    acc_ref[...] = jnp.zeros(acc_ref.shape, F32)

    def conv(u, w, b):
        return (b + w[0:1] * u[HALO - 2:HALO - 2 + tm] + w[1:2] * u[HALO - 1:HALO - 1 + tm]
                + w[2:3] * u[HALO:HALO + tm])

    def chunk_body(c, carry):
        cw = cw_ref[c]
        cb = cb_ref[c]
        gate = conv(_dot(he, wg_ref[c]), cw[0], cb[0])
        val = conv(_dot(he, wv_ref[c]), cw[1], cb[1])
        act = gate * jax.nn.sigmoid(gate) * val
        acc_ref[...] += _dot(act.astype(BF16), wd_ref[c])
        return carry

    lax.fori_loop(0, n_chunks, chunk_body, 0)
    o_ref[...] = x + acc_ref[...]


def _ffn(x, g, w_up, conv_w, conv_b, w_down, tm=512):
    B, S, D = x.shape
    nc = D_FF // FF_CHUNK
    wup = w_up.astype(BF16).reshape(D, 2, nc, FF_CHUNK).transpose(1, 2, 0, 3)
    cw = conv_w.reshape(CONV_W, 2, nc, FF_CHUNK).transpose(2, 1, 0, 3)
    cb = conv_b.reshape(2, nc, 1, FF_CHUNK).transpose(1, 0, 2, 3)
    wd = w_down.astype(BF16).reshape(nc, FF_CHUNK, D)
    c3 = lambda b, i: (0, 0, 0)
    c4 = lambda b, i: (0, 0, 0, 0)
    return pl.pallas_call(
        functools.partial(_ffn_kernel, tm=tm, n_chunks=nc),
        grid=(B, S // tm),
        in_specs=[
            pl.BlockSpec((None, tm, D), lambda b, i: (b, i, 0)),
            pl.BlockSpec((None, HALO, D), lambda b, i: (b, jnp.maximum(i * (tm // HALO) - 1, 0), 0)),
            pl.BlockSpec((1, D), lambda b, i: (0, 0)),
            pl.BlockSpec((nc, D, FF_CHUNK), c3),
            pl.BlockSpec((nc, D, FF_CHUNK), c3),
            pl.BlockSpec((nc, 2, CONV_W, FF_CHUNK), c4),
            pl.BlockSpec((nc, 2, 1, FF_CHUNK), c4),
            pl.BlockSpec((nc, FF_CHUNK, D), c3),
        ],
        out_specs=pl.BlockSpec((None, tm, D), lambda b, i: (b, i, 0)),
        out_shape=jax.ShapeDtypeStruct((B, S, D), F32),
        scratch_shapes=[pltpu.VMEM((tm, D), F32)],
        compiler_params=_cparams(("parallel", "parallel")),
        name="conv_ffn",
    )(x, x, g.reshape(1, D), wup[0], wup[1], cw, cb, wd)


def _rope_tables(S, period, off):
    inv = 1.0 / (ROPE_THETA ** (jnp.arange(0, ROPE_DIM, 2, dtype=F32) / ROPE_DIM))
    ang = jnp.arange(S, dtype=F32)[:, None] * inv[None, :]
    cos, sin = jnp.cos(ang), jnp.sin(ang)
    half = ROPE_DIM // 2
    lane = np.arange(LANES) % period
    is1 = (lane >= off) & (lane < off + half)
    is2 = (lane >= off + half) & (lane < off + ROPE_DIM)
    idx = np.where(is1, lane - off, np.where(is2, lane - off - half, 0))
    ct = jnp.where(is1 | is2, cos[:, idx], 1.0)
    s1 = jnp.where(is1, -sin[:, idx], 0.0)
    s2 = jnp.where(is2, sin[:, idx], 0.0)
    return ct, s1, s2


def _t5_bucket(rel):
    nb = T5_BUCKETS // 2
    max_exact = nb // 2
    n = jnp.abs(rel)
    large = max_exact + (jnp.log(jnp.maximum(n, 1).astype(F32) / max_exact)
                         / math.log(T5_MAX_DIST / max_exact) * (nb - max_exact)).astype(jnp.int32)
    large = jnp.minimum(large, nb - 1)
    return jnp.where(rel > 0, nb, 0) + jnp.where(n < max_exact, n, large)


def _dsa_bias_tiles(t5_bias):
    kj = jnp.arange(LANES, dtype=jnp.int32)[:, None]
    qi = jnp.arange(LANES, dtype=jnp.int32)[None, :]
    far = t5_bias[_t5_bucket(jnp.int32(-4 * T5_MAX_DIST))]
    same = (jnp.transpose(t5_bias[_t5_bucket(kj - qi)], (2, 0, 1)) - far[:, None, None]) * LOG2E
    same = jnp.where((kj < (qi // CHUNK + 1) * CHUNK)[None], same, NEG)
    prev = (jnp.transpose(t5_bias[_t5_bucket(kj - LANES - qi)], (2, 0, 1)) - far[:, None, None]) * LOG2E
    return jnp.stack([same, prev], axis=1).reshape(N_PAIRS, 2, 2, LANES, LANES)


def _band_bias(rel_bias, tq):
    pad = LEFT_CHUNKS * CHUNK
    nblk = -(-pad // tq) + 1
    lead = (nblk - 1) * tq
    kk = jnp.arange(nblk * tq, dtype=jnp.int32)[:, None] - lead
    i = jnp.arange(tq, dtype=jnp.int32)[None, :]
    first = (i // CHUNK) * CHUNK - pad
    ok = (kk >= first) & (kk < (i // CHUNK + 1) * CHUNK)
    b = rel_bias[:, jnp.clip(i - kk, -(CHUNK - 1), REL_MAX_PAST) + CHUNK - 1]
    b = jnp.where(ok[None], b * LOG2E, NEG).reshape(N_PAIRS, 2, nblk, tq, tq)
    return jnp.transpose(b, (0, 2, 3, 1, 4)).reshape(N_PAIRS, nblk, tq, 2 * tq)


def kernel(x, norm_mix, norm_ffn, t5_bias, a_w_in, a_q_norm, a_k_norm, a_kidx_norm, a_w_out, b_w_down, b_q_a_norm, b_kv_a_norm, b_w_uq, b_w_ukv, b_q_norm, b_k_norm, b_w_out, c_w_in, c_q_norm, c_k_norm, c_rel_bias, c_w_out, f_w_up, f_conv_w, f_conv_b, f_w_down):
    B, S, D = x.shape
    depth = norm_mix.shape[0]
    tq_c = 256
    tabs_idx = _rope_tables(S, D_IDX, D_IDX - ROPE_DIM)
    tabs_mla = _rope_tables(S, LANES, QK_NOPE)
    scale = HEAD_DIM ** -0.5 * LOG2E
    for layer in range(depth):
        kind, j = layer % N_MIXERS, layer // N_MIXERS
        g = norm_mix[layer]
        if kind == 0:
            q, k, vt = _qkv_proj(x, g, a_w_in[j][:, :A_QKV], a_q_norm[j] * scale, a_k_norm[j])
            qp, kp, wi = _idx_proj(x, g, a_w_in[j][:, A_QKV:], a_kidx_norm[j], tabs_idx)
            att = _dsa_attn(q, k, vt, qp, kp, wi, _dsa_bias_tiles(t5_bias))
            w_out = a_w_out[j]
        elif kind == 1:
            q, k, vt = _mla_proj(x, g, b_w_down[j], b_q_a_norm[j], b_kv_a_norm[j], b_w_uq[j], b_w_ukv[j],
                                 b_q_norm[j], b_k_norm[j], tabs_mla)
            att = _mla_attn(q, k, vt)
            w_out = b_w_out[j]
        else:
            q, k, vt = _qkv_proj(x, g, c_w_in[j], c_q_norm[j] * scale, c_k_norm[j])
            att = _band_attn(q, k, vt, _band_bias(c_rel_bias[j], tq_c), tq=tq_c)
            w_out = c_w_out[j]
        x = _out_proj(x, att, w_out)
        x = _ffn(x, norm_ffn[layer], f_w_up[layer], f_conv_w[layer], f_conv_b[layer], f_w_down[layer])
    return x
```

```python
import functools
import math

import numpy as np
import jax
import jax.numpy as jnp
from jax import lax
from jax.experimental import pallas as pl
from jax.experimental.pallas import tpu as pltpu

F32 = jnp.float32
BF16 = jnp.bfloat16

D_MODEL = 1024
CHUNK = 64
EPS = 1e-6
ROPE_THETA = 10000.0
NEG = -1e30
NEG_BITS = int(np.float32(NEG).view(np.int32))
M_INIT = -1e38
INT_MIN = -(2 ** 31)
LOG2E = math.log2(math.e)

N_MIXERS = 3
N_HEADS = 16
HEAD_DIM = 64
N_PAIRS = N_HEADS // 2
LANES = 128
VT_BLK = 256

H_IDX = 8
D_IDX = 64
ROPE_DIM = 32
TOPK_MAX = 256
T5_BUCKETS = 32
T5_MAX_DIST = 128
A_QKV = 3 * N_HEADS * HEAD_DIM
Q_LORA = 256
KV_LORA = 128
QK_NOPE = 64
QK_ROPE = 32
V_DIM = 64
B_QK = QK_NOPE + QK_ROPE
LEFT_CHUNKS = 8
REL_MAX_PAST = 128
D_FF = 2816
CONV_W = 3
FF_CHUNK = 256
HALO = 8

VMEM_LIMIT = 56 * 1024 * 1024


def _cparams(sem):
    return pltpu.CompilerParams(dimension_semantics=sem, vmem_limit_bytes=VMEM_LIMIT)


def _rms(x, g):
    return x * lax.rsqrt(jnp.mean(x * x, axis=-1, keepdims=True) + EPS) * g


def _dot(a, b):
    return jnp.dot(a, b, preferred_element_type=F32)


def _split_bf16(x):
    hi = x.astype(BF16)
    lo = (x - hi.astype(F32)).astype(BF16)
    return hi, lo


def _group_sumsq(y, bd):
    hi, lo = _split_bf16(y * y)
    return _dot(hi, bd) + _dot(lo, bd)


def _rope(y, ct, s1, s2):
    return y * ct + pltpu.roll(y, LANES - 16, 1) * s1 + pltpu.roll(y, 16, 1) * s2


def _rope_wide(y, ct, s1, s2):
    w = y.shape[-1]
    parts = [_rope(y[:, c:c + LANES], ct, s1, s2) for c in range(0, w, LANES)]
    return parts[0] if len(parts) == 1 else jnp.concatenate(parts, axis=-1)


def _store_vt(vt_ref, p, y):
    yt = y.T.astype(BF16)
    for r in range(y.shape[0] // VT_BLK):
        vt_ref[p, r] = yt[:, r * VT_BLK:(r + 1) * VT_BLK]


def _qkv_proj_kernel(x_ref, g_ref, w_ref, gq_ref, gk_ref, bd_ref, q_ref, k_ref, vt_ref):
    xb = _rms(x_ref[...], g_ref[...]).astype(BF16)
    bd = bd_ref[...]
    hd = N_HEADS * HEAD_DIM
    for which, (o_ref, gain_ref) in enumerate(((q_ref, gq_ref), (k_ref, gk_ref), (vt_ref, None))):
        for c in range(hd // 256):
            col = which * hd + c * 256
            y = _dot(xb, w_ref[:, col:col + 256])
            if gain_ref is None:
                _store_vt(o_ref, 2 * c, y[:, :LANES])
                _store_vt(o_ref, 2 * c + 1, y[:, LANES:])
            else:
                y = y * lax.rsqrt(_group_sumsq(y, bd) * (1.0 / HEAD_DIM) + EPS) * gain_ref[...]
                yb = y.astype(BF16)
                o_ref[2 * c] = yb[:, :LANES]
                o_ref[2 * c + 1] = yb[:, LANES:]


def _vt_spec(tm):
    return pl.BlockSpec((None, N_PAIRS, tm // VT_BLK, LANES, VT_BLK), lambda b, i: (b, 0, i, 0, 0))


def _vt_shape(B, S):
    return jax.ShapeDtypeStruct((B, N_PAIRS, S // VT_BLK, LANES, VT_BLK), BF16)


def _qkv_proj(x, g, w, gq, gk, tm=512):
    B, S, D = x.shape
    hd = N_HEADS * HEAD_DIM
    bd = jnp.kron(jnp.eye(256 // HEAD_DIM, dtype=F32), jnp.ones((HEAD_DIM, HEAD_DIM), F32)).astype(BF16)
    out = jax.ShapeDtypeStruct((B, N_PAIRS, S, LANES), BF16)
    const = lambda b, i: (0, 0)
    ospec = pl.BlockSpec((None, N_PAIRS, tm, LANES), lambda b, i: (b, 0, i, 0))
    return pl.pallas_call(
        _qkv_proj_kernel,
        grid=(B, S // tm),
        in_specs=[
            pl.BlockSpec((None, tm, D), lambda b, i: (b, i, 0)),
            pl.BlockSpec((1, D), const),
            pl.BlockSpec((D, 3 * hd), const),
            pl.BlockSpec((1, 256), const),
            pl.BlockSpec((1, 256), const),
            pl.BlockSpec((256, 256), const),
        ],
        out_specs=[ospec, ospec, _vt_spec(tm)],
        out_shape=[out, out, _vt_shape(B, S)],
        compiler_params=_cparams(("parallel", "parallel")),
        name="qkv_proj",
    )(x, g.reshape(1, D), w.astype(BF16), jnp.tile(gq, 4).reshape(1, 256), jnp.tile(gk, 4).reshape(1, 256), bd)


def _idx_proj_kernel(x_ref, g_ref, wh_ref, wl_ref, gki_ref, ct_ref, s1_ref, s2_ref, qp_ref, kp_ref, wi_ref,
                     *, w_scale):
    xn = _rms(x_ref[...], g_ref[...])
    xh, xl = _split_bf16(xn)

    def mm(c0, c1):
        wh = wh_ref[:, c0:c1]
        return _dot(xh, wh) + (_dot(xl, wh) + _dot(xh, wl_ref[:, c0:c1]))

    ct, s1, s2 = ct_ref[...], s1_ref[...], s2_ref[...]
    tm = xn.shape[0]
    low = lax.broadcasted_iota(jnp.int32, (tm, LANES), 1) < D_IDX
    zero = jnp.zeros((tm, LANES), F32)

    for c in range(H_IDX // 2):
        y = _rope(mm(c * LANES, (c + 1) * LANES), ct, s1, s2)
        hi = y.astype(BF16).astype(F32)
        lo = y - hi
        rhi = pltpu.roll(hi, D_IDX, 1)
        rlo = pltpu.roll(lo, D_IDX, 1)
        qp_ref[2 * c, :LANES, :] = jnp.where(low, hi, rlo).T.astype(BF16)
        qp_ref[2 * c, LANES:, :] = jnp.where(low, hi, zero).T.astype(BF16)
        qp_ref[2 * c + 1, :LANES, :] = jnp.where(low, rhi, lo).T.astype(BF16)
        qp_ref[2 * c + 1, LANES:, :] = jnp.where(low, rhi, zero).T.astype(BF16)

    c0 = H_IDX * D_IDX
    y = mm(c0, c0 + LANES)
    y = y * lax.rsqrt(jnp.sum(y * y, axis=-1, keepdims=True) * (1.0 / D_IDX) + EPS) * gki_ref[...]
    y = _rope(y, ct, s1, s2)
    hi = y.astype(BF16).astype(F32)
    lo = y - hi
    kp_ref[:, :LANES] = jnp.where(low, hi, pltpu.roll(hi, D_IDX, 1)).astype(BF16)
    kp_ref[:, LANES:] = jnp.where(low, lo, zero).astype(BF16)

    wi_ref[...] = (mm(c0 + LANES, c0 + 2 * LANES) * w_scale).T[:H_IDX]


def _idx_proj(x, g, w_idx, gki, tabs, tm=512):
    B, S, D = x.shape
    nq = H_IDX * D_IDX
    wq, wk, ww = w_idx[:, :nq], w_idx[:, nq:nq + D_IDX], w_idx[:, nq + D_IDX:]
    wpad = jnp.concatenate([
        wq, jnp.pad(wk, ((0, 0), (0, LANES - D_IDX))), jnp.pad(ww, ((0, 0), (0, LANES - H_IDX)))], axis=1)
    wh = wpad.astype(BF16)
    wl = (wpad - wh.astype(F32)).astype(BF16)
    ncol = nq + 2 * LANES
    gpad = jnp.pad(gki, (0, LANES - D_IDX)).reshape(1, LANES)
    const = lambda b, i: (0, 0)
    tspec = pl.BlockSpec((tm, LANES), lambda b, i: (i, 0))
    return pl.pallas_call(
        functools.partial(_idx_proj_kernel, w_scale=(D_IDX ** -0.5) * (H_IDX ** -0.5)),
        grid=(B, S // tm),
        in_specs=[
            pl.BlockSpec((None, tm, D), lambda b, i: (b, i, 0)),
            pl.BlockSpec((1, D), const),
            pl.BlockSpec((D, ncol), const),
            pl.BlockSpec((D, ncol), const),
            pl.BlockSpec((1, LANES), const),
            tspec, tspec, tspec,
        ],
        out_specs=[
            pl.BlockSpec((None, H_IDX, 256, tm), lambda b, i: (b, 0, 0, i)),
            pl.BlockSpec((None, tm, 256), lambda b, i: (b, i, 0)),
            pl.BlockSpec((None, H_IDX, tm), lambda b, i: (b, 0, i)),
        ],
        out_shape=[
            jax.ShapeDtypeStruct((B, H_IDX, 256, S), BF16),
            jax.ShapeDtypeStruct((B, S, 256), BF16),
            jax.ShapeDtypeStruct((B, H_IDX, S), F32),
        ],
        compiler_params=_cparams(("parallel", "parallel")),
        name="idx_proj",
    )(x, g.reshape(1, D), wh, wl, gpad, *tabs)


def _mla_proj_kernel(x_ref, g_ref, wd_ref, gqa_ref, gkva_ref, wuq_ref, wuk_ref, wuv_ref, gq_ref, gk_ref, bd_ref,
                     ct_ref, s1_ref, s2_ref, q_ref, k_ref, vt_ref):
    xb = _rms(x_ref[...], g_ref[...]).astype(BF16)
    d = _dot(xb, wd_ref[...])
    cq = _rms(d[:, :Q_LORA], gqa_ref[...]).astype(BF16)
    ckv = _rms(d[:, Q_LORA:Q_LORA + KV_LORA], gkva_ref[...]).astype(BF16)
    krz = pltpu.roll(d[:, Q_LORA + KV_LORA:], QK_NOPE, 1)
    krz2 = jnp.concatenate([krz, krz], axis=-1)
    bd = bd_ref[...]
    ct, s1, s2 = ct_ref[...], s1_ref[...], s2_ref[...]
    inv = 1.0 / B_QK
    for p in range(N_PAIRS):
        cols = slice(p * 256, (p + 1) * 256)
        y = _dot(cq, wuq_ref[:, cols])
        y = y * lax.rsqrt(_group_sumsq(y, bd) * inv + EPS) * gq_ref[...]
        q_ref[p] = _rope_wide(y, ct, s1, s2).astype(BF16)
        y = _dot(ckv, wuk_ref[:, cols]) + krz2
        y = y * lax.rsqrt(_group_sumsq(y, bd) * inv + EPS) * gk_ref[...]
        k_ref[p] = _rope_wide(y, ct, s1, s2).astype(BF16)
        _store_vt(vt_ref, p, _dot(ckv, wuv_ref[:, p * LANES:(p + 1) * LANES]))


def _mla_proj(x, g, w_down, gqa, gkva, w_uq, w_ukv, gq, gk, tabs, tm=512):
    B, S, D = x.shape
    nd = Q_LORA + KV_LORA + QK_ROPE
    wd = jnp.pad(w_down, ((0, 0), (0, 512 - nd))).astype(BF16)
    padh = LANES - B_QK
    wuq = jnp.pad(w_uq.reshape(Q_LORA, N_HEADS, B_QK), ((0, 0), (0, 0), (0, padh))).reshape(Q_LORA, N_HEADS * LANES)
    wukv = w_ukv.reshape(KV_LORA, N_HEADS, QK_NOPE + V_DIM)
    wuk = jnp.pad(wukv[:, :, :QK_NOPE], ((0, 0), (0, 0), (0, LANES - QK_NOPE))).reshape(KV_LORA, N_HEADS * LANES)
    wuv = wukv[:, :, QK_NOPE:].reshape(KV_LORA, N_HEADS * V_DIM)
    gq2 = jnp.tile(jnp.pad(gq * (B_QK ** -0.5 * LOG2E), (0, padh)), 2).reshape(1, 256)
    gk2 = jnp.tile(jnp.pad(gk, (0, padh)), 2).reshape(1, 256)
    bd = jnp.kron(jnp.eye(2, dtype=F32), jnp.ones((LANES, LANES), F32)).astype(BF16)
    const = lambda b, i: (0, 0)
    tspec = pl.BlockSpec((tm, LANES), lambda b, i: (i, 0))
    qk_shape = jax.ShapeDtypeStruct((B, N_PAIRS, S, 256), BF16)
    qk_spec = pl.BlockSpec((None, N_PAIRS, tm, 256), lambda b, i: (b, 0, i, 0))
    return pl.pallas_call(
        _mla_proj_kernel,
        grid=(B, S // tm),
        in_specs=[
            pl.BlockSpec((None, tm, D), lambda b, i: (b, i, 0)),
            pl.BlockSpec((1, D), const),
            pl.BlockSpec((D, 512), const),
            pl.BlockSpec((1, Q_LORA), const),
            pl.BlockSpec((1, KV_LORA), const),
            pl.BlockSpec((Q_LORA, N_HEADS * LANES), const),
            pl.BlockSpec((KV_LORA, N_HEADS * LANES), const),
            pl.BlockSpec((KV_LORA, N_HEADS * V_DIM), const),
            pl.BlockSpec((1, 256), const),
            pl.BlockSpec((1, 256), const),
            pl.BlockSpec((256, 256), const),
            tspec, tspec, tspec,
        ],
        out_specs=[qk_spec, qk_spec, _vt_spec(tm)],
        out_shape=[qk_shape, qk_shape, _vt_shape(B, S)],
        compiler_params=_cparams(("parallel", "parallel")),
        name="mla_proj",
    )(x, g.reshape(1, D), wd, gqa.reshape(1, Q_LORA), gkva.reshape(1, KV_LORA), wuq.astype(BF16),
      wuk.astype(BF16), wuv.astype(BF16), gq2, gk2, bd, *tabs)


def _stack_heads(q2, half):
    qf = q2.astype(F32)
    low = lax.broadcasted_iota(jnp.int32, qf.shape, 1) < half
    zero = jnp.zeros_like(qf)
    return jnp.concatenate([jnp.where(low, qf, zero).T, jnp.where(low, zero, qf).T], axis=1).astype(BF16)


def _flash_init(acc_ref):
    r = acc_ref.shape[1]
    acc_ref[...] = jnp.zeros(acc_ref.shape, F32)
    return jnp.full((1, r), M_INIT, F32), jnp.zeros((1, r), F32)


def _consume(s, vt, m, l, acc_ref):
    m_new = jnp.maximum(m, jnp.max(s, axis=0, keepdims=True))
    alpha = jnp.exp2(m - m_new)
    p = jnp.exp2(s - m_new)
    l_new = alpha * l + jnp.sum(p, axis=0, keepdims=True)
    acc_ref[...] = alpha * acc_ref[...] + _dot(vt, p.astype(BF16))
    return m_new, l_new


def _sweep(first, n, s_ref, scores, values, ml, acc_ref):
    def one(j, s, ml):
        return _consume(s, values(j), *ml, acc_ref)

    odd = n % 2
    ml = lax.cond(odd == 1, lambda ml: one(first, scores(first), ml), lambda ml: ml, ml)
    j0 = first + odd
    last = first + n - 1

    @pl.when(n >= 2)
    def _():
        s_ref[0] = scores(j0)

    def body(t, ml):
        j = j0 + 2 * t
        s1 = scores(j + 1)
        ml = one(j, s_ref[0], ml)
        s_ref[1] = s1
        s0 = scores(jnp.minimum(j + 2, last))
        ml = one(j + 1, s_ref[1], ml)
        s_ref[0] = s0
        return ml

    return lax.fori_loop(0, n // 2, body, ml)


def _flash_out(l, acc_ref, tq):
    o = acc_ref[...] / l
    ot = jnp.concatenate([o[:HEAD_DIM, :tq], o[HEAD_DIM:, tq:]], axis=0)
    return ot.T.astype(BF16)


def _load_vt(vt_ref, kb, tk):
    n = tk // VT_BLK
    parts = [vt_ref[kb * n + r] for r in range(n)]
    return parts[0] if n == 1 else jnp.concatenate(parts, axis=1)


def _band_attn_kernel(q_ref, k_ref, vt_ref, bias_ref, o_ref, acc_ref, s_ref, *, tq, nblk):
    i = pl.program_id(2)
    qs = _stack_heads(q_ref[...], HEAD_DIM)
    lead = i - (nblk - 1)

    def scores(kb):
        ks = pl.multiple_of(kb * tq, tq)
        return _dot(k_ref[pl.ds(ks, tq), :], qs) + bias_ref[kb - lead]

    _, l = _sweep(jnp.maximum(lead, 0), jnp.minimum(i + 1, nblk), s_ref, scores,
                  lambda kb: _load_vt(vt_ref, kb, tq), _flash_init(acc_ref), acc_ref)
    o_ref[...] = _flash_out(l, acc_ref, tq)


def _band_attn(q, k, vt, bias, tq=256):
    B, _, S, _ = q.shape
    nblk = bias.shape[1]
    return pl.pallas_call(
        functools.partial(_band_attn_kernel, tq=tq, nblk=nblk),
        grid=(B, N_PAIRS, S // tq),
        in_specs=[
            pl.BlockSpec((None, None, tq, LANES), lambda b, p, i: (b, p, i, 0)),
            pl.BlockSpec((None, None, S, LANES), lambda b, p, i: (b, p, 0, 0)),
            pl.BlockSpec((None, None, S // VT_BLK, LANES, VT_BLK), lambda b, p, i: (b, p, 0, 0, 0)),
            pl.BlockSpec((None, nblk, tq, 2 * tq), lambda b, p, i: (p, 0, 0, 0)),
        ],
        out_specs=pl.BlockSpec((None, tq, LANES), lambda b, p, i: (b, i, p)),
        out_shape=jax.ShapeDtypeStruct((B, S, N_HEADS * HEAD_DIM), BF16),
        scratch_shapes=[pltpu.VMEM((LANES, 2 * tq), F32), pltpu.VMEM((2, tq, 2 * tq), F32)],
        compiler_params=_cparams(("parallel", "parallel", "arbitrary")),
        name="band_attn",
    )(q, k, vt, bias)


def _mla_attn_kernel(q_ref, k_ref, vt_ref, o_ref, acc_ref, s_ref, *, tq, tk):
    i = pl.program_id(2)
    qs = _stack_heads(q_ref[...], LANES)
    t0 = i * tq
    n_full = t0 // tk
    n_blk = (t0 + tq + tk - 1) // tk
    lim = ((t0 + lax.broadcasted_iota(jnp.int32, (1, tq), 1)) // CHUNK + 1) * CHUNK

    def scores(kb):
        ks = pl.multiple_of(kb * tk, tk)
        return _dot(k_ref[pl.ds(ks, tk), :], qs)

    def edge_scores(kb):
        row = kb * tk + lax.broadcasted_iota(jnp.int32, (tk, tq), 0)
        add = jnp.where(row < lim, 0.0, NEG).astype(F32)
        return scores(kb) + jnp.concatenate([add, add], axis=1)

    values = lambda kb: _load_vt(vt_ref, kb, tk)
    ml = _sweep(0, n_full, s_ref, scores, values, _flash_init(acc_ref), acc_ref)
    _, l = _sweep(n_full, n_blk - n_full, s_ref, edge_scores, values, ml, acc_ref)
    o_ref[...] = _flash_out(l, acc_ref, tq)


def _mla_attn(q, k, vt, tq=256, tk=512):
    B, _, S, _ = q.shape
    tk = min(tk, S)
    return pl.pallas_call(
        functools.partial(_mla_attn_kernel, tq=tq, tk=tk),
        grid=(B, N_PAIRS, S // tq),
        in_specs=[
            pl.BlockSpec((None, None, tq, 256), lambda b, p, i: (b, p, i, 0)),
            pl.BlockSpec((None, None, S, 256), lambda b, p, i: (b, p, 0, 0)),
            pl.BlockSpec((None, None, S // VT_BLK, LANES, VT_BLK), lambda b, p, i: (b, p, 0, 0, 0)),
        ],
        out_specs=pl.BlockSpec((None, tq, LANES), lambda b, p, i: (b, i, p)),
        out_shape=jax.ShapeDtypeStruct((B, S, N_HEADS * V_DIM), BF16),
        scratch_shapes=[pltpu.VMEM((LANES, 2 * tq), F32), pltpu.VMEM((2, tk, 2 * tq), F32)],
        compiler_params=_cparams(("parallel", "parallel", "arbitrary")),
        name="mla_attn",
    )(q, k, vt)


def _dsa_attn_kernel(q_ref, qp_ref, wi_ref, bias_ref, k_ref, vt_ref, kp_ref, o_ref,
                     sc_ref, acc_ref, stage_ref, s_ref, *, tq, tk, topk):
    i = pl.program_id(1)
    t0 = i * tq
    n_blk = (t0 + tq + tk - 1) // tk
    n_far = jnp.maximum((t0 // LANES - 1) // (tk // LANES), 0)
    lim = ((t0 + lax.broadcasted_iota(jnp.int32, (1, tq), 1)) // CHUNK + 1) * CHUNK

    wv = wi_ref[...]

    def score_body(kb, carry):
        ks = pl.multiple_of(kb * tk, tk)
        kp = kp_ref[pl.ds(ks, tk), :]
        acc = jnp.zeros((tk, tq), F32)
        for h in range(H_IDX):
            acc = acc + jnp.maximum(_dot(kp, qp_ref[h]), 0.0) * wv[h:h + 1, :]
        row = ks + lax.broadcasted_iota(jnp.int32, (tk, tq), 0)
        acc = jnp.where(row < lim, acc, -jnp.inf)
        bits = pltpu.bitcast(acc, jnp.int32)
        sc_ref[kb] = bits ^ ((bits >> 31) & 0x7FFFFFFF)
        return carry

    lax.fori_loop(0, n_blk, score_body, 0)

    def bit_body(t, ans):
        cand = ans + jnp.left_shift(jnp.int32(1), 31 - t)

        def count_body(kb, part):
            ge = jnp.where(sc_ref[kb] >= cand, 1, 0)
            return part + jnp.sum(ge.reshape(tk // 8, 8, tq), axis=0)

        part = lax.fori_loop(0, n_blk, count_body, jnp.zeros((8, tq), jnp.int32))
        cnt = jnp.sum(part, axis=0, keepdims=True)
        return jnp.where(cnt >= topk, cand, ans)

    thr = lax.fori_loop(0, 32, bit_body, jnp.full((1, tq), INT_MIN, jnp.int32))

    def mask_body(kb, carry):
        sc_ref[kb] = jnp.where(sc_ref[kb] >= thr, 0, NEG_BITS)
        return carry

    lax.fori_loop(0, n_blk, mask_body, 0)

    def pair_body(p, carry):
        qs = _stack_heads(q_ref[p], HEAD_DIM)

        def far_scores(kb):
            ks = pl.multiple_of(kb * tk, tk)
            am = pltpu.bitcast(sc_ref[kb], F32)
            return _dot(k_ref[p, pl.ds(ks, tk), :], qs) + jnp.concatenate([am, am], axis=1)

        def near_scores(kb):
            cols = []
            for hh in range(2):
                t_same = bias_ref[p, hh, 0]
                t_prev = bias_ref[p, hh, 1]
                for a in range(tq // LANES):
                    tiles = []
                    for c in range(tk // LANES):
                        d = (kb * (tk // LANES) + c) - (i * (tq // LANES) + a)
                        tiles.append(jnp.where(d >= 1, NEG,
                                               jnp.where(d == 0, t_same, jnp.where(d == -1, t_prev, 0.0))))
                    cols.append(jnp.concatenate(tiles, axis=0))
            return far_scores(kb) + jnp.concatenate(cols, axis=1)

        values = lambda kb: _load_vt(vt_ref.at[p], kb, tk)
        ml = _sweep(0, n_far, s_ref, far_scores, values, _flash_init(acc_ref), acc_ref)
        _, l = _sweep(n_far, n_blk - n_far, s_ref, near_scores, values, ml, acc_ref)
        stage_ref[p] = _flash_out(l, acc_ref, tq)
        return carry

    lax.fori_loop(0, N_PAIRS, pair_body, 0)
    for p in range(N_PAIRS):
        o_ref[:, p * LANES:(p + 1) * LANES] = stage_ref[p]


def _dsa_attn(q, k, vt, qp, kp, wi, bias, tq=256, tk=256):
    B, _, S, _ = q.shape
    topk = min(TOPK_MAX, S // 4)
    once = pl.Buffered(1)
    return pl.pallas_call(
        functools.partial(_dsa_attn_kernel, tq=tq, tk=tk, topk=topk),
        grid=(B, S // tq),
        in_specs=[
            pl.BlockSpec((None, N_PAIRS, tq, LANES), lambda b, i: (b, 0, i, 0)),
            pl.BlockSpec((None, H_IDX, 256, tq), lambda b, i: (b, 0, 0, i)),
            pl.BlockSpec((None, H_IDX, tq), lambda b, i: (b, 0, i)),
            pl.BlockSpec((N_PAIRS, 2, 2, LANES, LANES), lambda b, i: (0, 0, 0, 0, 0), pipeline_mode=once),
            pl.BlockSpec((None, N_PAIRS, S, LANES), lambda b, i: (b, 0, 0, 0), pipeline_mode=once),
            pl.BlockSpec((None, N_PAIRS, S // VT_BLK, LANES, VT_BLK), lambda b, i: (b, 0, 0, 0, 0),
                         pipeline_mode=once),
            pl.BlockSpec((None, S, 256), lambda b, i: (b, 0, 0), pipeline_mode=once),
        ],
        out_specs=pl.BlockSpec((None, tq, N_HEADS * HEAD_DIM), lambda b, i: (b, i, 0)),
        out_shape=jax.ShapeDtypeStruct((B, S, N_HEADS * HEAD_DIM), BF16),
        scratch_shapes=[
            pltpu.VMEM((S // tk, tk, tq), jnp.int32),
            pltpu.VMEM((LANES, 2 * tq), F32),
            pltpu.VMEM((N_PAIRS, tq, LANES), BF16),
            pltpu.VMEM((2, tk, 2 * tq), F32),
        ],
        compiler_params=_cparams(("parallel", "arbitrary")),
        name="dsa_attn",
    )(q, qp, wi, bias, k, vt, kp)


def _out_proj_kernel(x_ref, a_ref, w_ref, o_ref):
    o_ref[...] = x_ref[...] + _dot(a_ref[...], w_ref[...])


def _out_proj(x, a, w, tm=512):
    B, S, D = x.shape
    return pl.pallas_call(
        _out_proj_kernel,
        grid=(B, S // tm),
        in_specs=[
            pl.BlockSpec((None, tm, D), lambda b, i: (b, i, 0)),
            pl.BlockSpec((None, tm, a.shape[-1]), lambda b, i: (b, i, 0)),
            pl.BlockSpec(w.shape, lambda b, i: (0, 0)),
        ],
        out_specs=pl.BlockSpec((None, tm, D), lambda b, i: (b, i, 0)),
        out_shape=jax.ShapeDtypeStruct((B, S, D), F32),
        compiler_params=_cparams(("parallel", "parallel")),
        name="out_proj",
    )(x, a, w.astype(BF16))


def _ffn_kernel(x_ref, xp_ref, g_ref, wg_ref, wv_ref, cw_ref, cb_ref, wd_ref, o_ref, acc_ref, he_ref, u_ref,
                *, tm, n_chunks):
    i = pl.program_id(1)
    g = g_ref[...]
    he_ref[:HALO] = (_rms(xp_ref[...], g) * jnp.where(i > 0, 1.0, 0.0)).astype(BF16)
    he_ref[HALO:] = _rms(x_ref[...], g).astype(BF16)
    acc_ref[...] = jnp.zeros(acc_ref.shape, F32)

    def up(c, slot):
        he = he_ref[...]
        u_ref[slot, 0] = _dot(he, wg_ref[c])
        u_ref[slot, 1] = _dot(he, wv_ref[c])

    def conv(u, w, b):
        return (b + w[0:1] * u[HALO - 2:HALO - 2 + tm] + w[1:2] * u[HALO - 1:HALO - 1 + tm]
                + w[2:3] * u[HALO:HALO + tm])

    def down(c, slot):
        cw = cw_ref[c]
        cb = cb_ref[c]
        gate = conv(u_ref[slot, 0], cw[0], cb[0])
        val = conv(u_ref[slot, 1], cw[1], cb[1])
        act = gate * jax.nn.sigmoid(gate) * val
        acc_ref[...] += _dot(act.astype(BF16), wd_ref[c])

    up(0, 0)

    def pair_body(t, carry):
        c = 2 * t
        up(c + 1, 1)
        down(c, 0)
        up(c + 2, 0)
        down(c + 1, 1)
        return carry

    assert n_chunks % 2 == 1
    lax.fori_loop(0, n_chunks // 2, pair_body, 0)
    down(n_chunks - 1, 0)
    o_ref[...] = x_ref[...] + acc_ref[...]


def _ffn(x, g, w_up, conv_w, conv_b, w_down, tm=512):
    B, S, D = x.shape
    nc = D_FF // FF_CHUNK
    wup = w_up.astype(BF16).reshape(D, 2, nc, FF_CHUNK).transpose(1, 2, 0, 3)
    cw = conv_w.reshape(CONV_W, 2, nc, FF_CHUNK).transpose(2, 1, 0, 3)
    cb = conv_b.reshape(2, nc, 1, FF_CHUNK).transpose(1, 0, 2, 3)
    wd = w_down.astype(BF16).reshape(nc, FF_CHUNK, D)
    c3 = lambda b, i: (0, 0, 0)
    c4 = lambda b, i: (0, 0, 0, 0)
    return pl.pallas_call(
        functools.partial(_ffn_kernel, tm=tm, n_chunks=nc),
        grid=(B, S // tm),
        in_specs=[
            pl.BlockSpec((None, tm, D), lambda b, i: (b, i, 0)),
            pl.BlockSpec((None, HALO, D), lambda b, i: (b, jnp.maximum(i * (tm // HALO) - 1, 0), 0)),
            pl.BlockSpec((1, D), lambda b, i: (0, 0)),
            pl.BlockSpec((nc, D, FF_CHUNK), c3),
            pl.BlockSpec((nc, D, FF_CHUNK), c3),
            pl.BlockSpec((nc, 2, CONV_W, FF_CHUNK), c4),
            pl.BlockSpec((nc, 2, 1, FF_CHUNK), c4),
            pl.BlockSpec((nc, FF_CHUNK, D), c3),
        ],
        out_specs=pl.BlockSpec((None, tm, D), lambda b, i: (b, i, 0)),
        out_shape=jax.ShapeDtypeStruct((B, S, D), F32),
        scratch_shapes=[pltpu.VMEM((tm, D), F32), pltpu.VMEM((HALO + tm, D), BF16),
                        pltpu.VMEM((2, 2, HALO + tm, FF_CHUNK), F32)],
        compiler_params=_cparams(("parallel", "parallel")),
        name="conv_ffn",
    )(x, x, g.reshape(1, D), wup[0], wup[1], cw, cb, wd)


def _rope_tables(S, period, off):
    inv = 1.0 / (ROPE_THETA ** (jnp.arange(0, ROPE_DIM, 2, dtype=F32) / ROPE_DIM))
    ang = jnp.arange(S, dtype=F32)[:, None] * inv[None, :]
    cos, sin = jnp.cos(ang), jnp.sin(ang)
    half = ROPE_DIM // 2
    lane = np.arange(LANES) % period
    is1 = (lane >= off) & (lane < off + half)
    is2 = (lane >= off + half) & (lane < off + ROPE_DIM)
    idx = np.where(is1, lane - off, np.where(is2, lane - off - half, 0))
    ct = jnp.where(is1 | is2, cos[:, idx], 1.0)
    s1 = jnp.where(is1, -sin[:, idx], 0.0)
    s2 = jnp.where(is2, sin[:, idx], 0.0)
    return ct, s1, s2


def _t5_bucket(rel):
    nb = T5_BUCKETS // 2
    max_exact = nb // 2
    n = jnp.abs(rel)
    large = max_exact + (jnp.log(jnp.maximum(n, 1).astype(F32) / max_exact)
                         / math.log(T5_MAX_DIST / max_exact) * (nb - max_exact)).astype(jnp.int32)
    large = jnp.minimum(large, nb - 1)
    return jnp.where(rel > 0, nb, 0) + jnp.where(n < max_exact, n, large)


def _dsa_bias_tiles(t5_bias):
    kj = jnp.arange(LANES, dtype=jnp.int32)[:, None]
    qi = jnp.arange(LANES, dtype=jnp.int32)[None, :]
    far = t5_bias[_t5_bucket(jnp.int32(-4 * T5_MAX_DIST))]
    same = (jnp.transpose(t5_bias[_t5_bucket(kj - qi)], (2, 0, 1)) - far[:, None, None]) * LOG2E
    same = jnp.where((kj < (qi // CHUNK + 1) * CHUNK)[None], same, NEG)
    prev = (jnp.transpose(t5_bias[_t5_bucket(kj - LANES - qi)], (2, 0, 1)) - far[:, None, None]) * LOG2E
    return jnp.stack([same, prev], axis=1).reshape(N_PAIRS, 2, 2, LANES, LANES)


def _band_bias(rel_bias, tq):
    pad = LEFT_CHUNKS * CHUNK
    nblk = -(-pad // tq) + 1
    lead = (nblk - 1) * tq
    nk = nblk * tq
    kk = jnp.arange(nk, dtype=jnp.int32)[:, None] - lead
    i = jnp.arange(tq, dtype=jnp.int32)[None, :]
    first = (i // CHUNK) * CHUNK - pad
    ok = (kk >= first) & (kk < (i // CHUNK + 1) * CHUNK)
    dist = jnp.arange(lead - nk + 1, lead + tq, dtype=jnp.int32)
    v = rel_bias[:, jnp.clip(dist, -(CHUNK - 1), REL_MAX_PAST) + CHUNK - 1]
    L = nk + tq
    v = jnp.pad(v, ((0, 0), (0, L - v.shape[1])))
    rot = jnp.tile(v, (1, nk + 1))[:, :nk * (L + 1)].reshape(v.shape[0], nk, L + 1)
    b = rot[:, ::-1, :tq]
    b = jnp.where(ok[None], b * LOG2E, NEG).reshape(N_PAIRS, 2, nblk, tq, tq)
    return jnp.transpose(b, (0, 2, 3, 1, 4)).reshape(N_PAIRS, nblk, tq, 2 * tq)


def kernel(x, norm_mix, norm_ffn, t5_bias, a_w_in, a_q_norm, a_k_norm, a_kidx_norm, a_w_out, b_w_down, b_q_a_norm, b_kv_a_norm, b_w_uq, b_w_ukv, b_q_norm, b_k_norm, b_w_out, c_w_in, c_q_norm, c_k_norm, c_rel_bias, c_w_out, f_w_up, f_conv_w, f_conv_b, f_w_down):
    B, S, D = x.shape
    depth = norm_mix.shape[0]
    tq_c = 256
    tabs_idx = _rope_tables(S, D_IDX, D_IDX - ROPE_DIM)
    tabs_mla = _rope_tables(S, LANES, QK_NOPE)
    scale = HEAD_DIM ** -0.5 * LOG2E
    for layer in range(depth):
        kind, j = layer % N_MIXERS, layer // N_MIXERS
        g = norm_mix[layer]
        if kind == 0:
            q, k, vt = _qkv_proj(x, g, a_w_in[j][:, :A_QKV], a_q_norm[j] * scale, a_k_norm[j])
            qp, kp, wi = _idx_proj(x, g, a_w_in[j][:, A_QKV:], a_kidx_norm[j], tabs_idx)
            att = _dsa_attn(q, k, vt, qp, kp, wi, _dsa_bias_tiles(t5_bias))
            w_out = a_w_out[j]
        elif kind == 1:
            q, k, vt = _mla_proj(x, g, b_w_down[j], b_q_a_norm[j], b_kv_a_norm[j], b_w_uq[j], b_w_ukv[j],
                                 b_q_norm[j], b_k_norm[j], tabs_mla)
            att = _mla_attn(q, k, vt)
            w_out = b_w_out[j]
        else:
            q, k, vt = _qkv_proj(x, g, c_w_in[j], c_q_norm[j] * scale, c_k_norm[j])
            att = _band_attn(q, k, vt, _band_bias(c_rel_bias[j], tq_c), tq=tq_c)
            w_out = c_w_out[j]
        x = _out_proj(x, att, w_out)
        x = _ffn(x, norm_ffn[layer], f_w_up[layer], f_conv_w[layer], f_conv_b[layer], f_w_down[layer])
    return x
```

```python
import functools
import math

import numpy as np
import jax
import jax.numpy as jnp
from jax import lax
from jax.experimental import pallas as pl
from jax.experimental.pallas import tpu as pltpu

F32 = jnp.float32
BF16 = jnp.bfloat16

D_MODEL = 1024
CHUNK = 64
EPS = 1e-6
ROPE_THETA = 10000.0
NEG = -1e30
NEG_BITS = int(np.float32(NEG).view(np.int32))
M_INIT = -1e38
INT_MIN = -(2 ** 31)
LOG2E = math.log2(math.e)

N_MIXERS = 3
N_HEADS = 16
HEAD_DIM = 64
N_PAIRS = N_HEADS // 2
LANES = 128
VT_BLK = 256

H_IDX = 8
D_IDX = 64
ROPE_DIM = 32
TOPK_MAX = 256
T5_BUCKETS = 32
T5_MAX_DIST = 128
A_QKV = 3 * N_HEADS * HEAD_DIM
Q_LORA = 256
KV_LORA = 128
QK_NOPE = 64
QK_ROPE = 32
V_DIM = 64
B_QK = QK_NOPE + QK_ROPE
LEFT_CHUNKS = 8
REL_MAX_PAST = 128
D_FF = 2816
CONV_W = 3
FF_CHUNK = 256
HALO = 8

VMEM_LIMIT = 56 * 1024 * 1024
VMEM_LIMIT_DSA = 60 * 1024 * 1024


def _cparams(sem, vmem_limit=VMEM_LIMIT):
    return pltpu.CompilerParams(dimension_semantics=sem, vmem_limit_bytes=vmem_limit)


def _rms(x, g):
    return x * lax.rsqrt(jnp.mean(x * x, axis=-1, keepdims=True) + EPS) * g


def _dot(a, b):
    return jnp.dot(a, b, preferred_element_type=F32)


def _split_bf16(x):
    hi = x.astype(BF16)
    lo = (x - hi.astype(F32)).astype(BF16)
    return hi, lo


def _group_sumsq(y, bd):
    hi, lo = _split_bf16(y * y)
    return _dot(hi, bd) + _dot(lo, bd)


def _rope(y, ct, s1, s2):
    return y * ct + pltpu.roll(y, LANES - 16, 1) * s1 + pltpu.roll(y, 16, 1) * s2


def _rope_wide(y, ct, s1, s2):
    w = y.shape[-1]
    parts = [_rope(y[:, c:c + LANES], ct, s1, s2) for c in range(0, w, LANES)]
    return parts[0] if len(parts) == 1 else jnp.concatenate(parts, axis=-1)


def _store_vt(vt_ref, p, y):
    yt = y.T.astype(BF16)
    for r in range(y.shape[0] // VT_BLK):
        vt_ref[p, r] = yt[:, r * VT_BLK:(r + 1) * VT_BLK]


def _qkv_proj_kernel(x_ref, g_ref, w_ref, gq_ref, gk_ref, bd_ref, q_ref, k_ref, vt_ref):
    xb = _rms(x_ref[...], g_ref[...]).astype(BF16)
    bd = bd_ref[...]
    hd = N_HEADS * HEAD_DIM
    for which, (o_ref, gain_ref) in enumerate(((q_ref, gq_ref), (k_ref, gk_ref), (vt_ref, None))):
        for c in range(hd // 256):
            col = which * hd + c * 256
            y = _dot(xb, w_ref[:, col:col + 256])
            if gain_ref is None:
                _store_vt(o_ref, 2 * c, y[:, :LANES])
                _store_vt(o_ref, 2 * c + 1, y[:, LANES:])
            else:
                y = y * lax.rsqrt(_group_sumsq(y, bd) * (1.0 / HEAD_DIM) + EPS) * gain_ref[...]
                yb = y.astype(BF16)
                o_ref[2 * c] = yb[:, :LANES]
                o_ref[2 * c + 1] = yb[:, LANES:]


def _vt_spec(tm):
    return pl.BlockSpec((None, N_PAIRS, tm // VT_BLK, LANES, VT_BLK), lambda b, i: (b, 0, i, 0, 0))


def _vt_shape(B, S):
    return jax.ShapeDtypeStruct((B, N_PAIRS, S // VT_BLK, LANES, VT_BLK), BF16)


def _qkv_proj(x, g, w, gq, gk, tm=512):
    B, S, D = x.shape
    hd = N_HEADS * HEAD_DIM
    bd = jnp.kron(jnp.eye(256 // HEAD_DIM, dtype=F32), jnp.ones((HEAD_DIM, HEAD_DIM), F32)).astype(BF16)
    out = jax.ShapeDtypeStruct((B, N_PAIRS, S, LANES), BF16)
    const = lambda b, i: (0, 0)
    ospec = pl.BlockSpec((None, N_PAIRS, tm, LANES), lambda b, i: (b, 0, i, 0))
    return pl.pallas_call(
        _qkv_proj_kernel,
        grid=(B, S // tm),
        in_specs=[
            pl.BlockSpec((None, tm, D), lambda b, i: (b, i, 0)),
            pl.BlockSpec((1, D), const),
            pl.BlockSpec((D, 3 * hd), const),
            pl.BlockSpec((1, 256), const),
            pl.BlockSpec((1, 256), const),
            pl.BlockSpec((256, 256), const),
        ],
        out_specs=[ospec, ospec, _vt_spec(tm)],
        out_shape=[out, out, _vt_shape(B, S)],
        compiler_params=_cparams(("parallel", "parallel")),
        name="qkv_proj",
    )(x, g.reshape(1, D), w.astype(BF16), jnp.tile(gq, 4).reshape(1, 256), jnp.tile(gk, 4).reshape(1, 256), bd)


def _idx_proj_kernel(x_ref, g_ref, wh_ref, wl_ref, gki_ref, ct_ref, s1_ref, s2_ref, qp_ref, kp_ref, wi_ref,
                     *, w_scale):
    xn = _rms(x_ref[...], g_ref[...])
    xh, xl = _split_bf16(xn)

    def mm(c0, c1):
        wh = wh_ref[:, c0:c1]
        return _dot(xh, wh) + (_dot(xl, wh) + _dot(xh, wl_ref[:, c0:c1]))

    ct, s1, s2 = ct_ref[...], s1_ref[...], s2_ref[...]
    tm = xn.shape[0]
    low = lax.broadcasted_iota(jnp.int32, (tm, LANES), 1) < D_IDX
    zero = jnp.zeros((tm, LANES), F32)

    for c in range(H_IDX // 2):
        y = _rope(mm(c * LANES, (c + 1) * LANES), ct, s1, s2)
        hi = y.astype(BF16).astype(F32)
        lo = y - hi
        rhi = pltpu.roll(hi, D_IDX, 1)
        rlo = pltpu.roll(lo, D_IDX, 1)
        qp_ref[2 * c, :LANES, :] = jnp.where(low, hi, rlo).T.astype(BF16)
        qp_ref[2 * c, LANES:, :] = jnp.where(low, hi, zero).T.astype(BF16)
        qp_ref[2 * c + 1, :LANES, :] = jnp.where(low, rhi, lo).T.astype(BF16)
        qp_ref[2 * c + 1, LANES:, :] = jnp.where(low, rhi, zero).T.astype(BF16)

    c0 = H_IDX * D_IDX
    y = mm(c0, c0 + LANES)
    y = y * lax.rsqrt(jnp.sum(y * y, axis=-1, keepdims=True) * (1.0 / D_IDX) + EPS) * gki_ref[...]
    y = _rope(y, ct, s1, s2)
    hi = y.astype(BF16).astype(F32)
    lo = y - hi
    kp_ref[:, :LANES] = jnp.where(low, hi, pltpu.roll(hi, D_IDX, 1)).astype(BF16)
    kp_ref[:, LANES:] = jnp.where(low, lo, zero).astype(BF16)

    wi_ref[...] = (mm(c0 + LANES, c0 + 2 * LANES) * w_scale).T[:H_IDX]


def _idx_proj(x, g, w_idx, gki, tabs, tm=512):
    B, S, D = x.shape
    nq = H_IDX * D_IDX
    wq, wk, ww = w_idx[:, :nq], w_idx[:, nq:nq + D_IDX], w_idx[:, nq + D_IDX:]
    wpad = jnp.concatenate([
        wq, jnp.pad(wk, ((0, 0), (0, LANES - D_IDX))), jnp.pad(ww, ((0, 0), (0, LANES - H_IDX)))], axis=1)
    wh = wpad.astype(BF16)
    wl = (wpad - wh.astype(F32)).astype(BF16)
    ncol = nq + 2 * LANES
    gpad = jnp.pad(gki, (0, LANES - D_IDX)).reshape(1, LANES)
    const = lambda b, i: (0, 0)
    tspec = pl.BlockSpec((tm, LANES), lambda b, i: (i, 0))
    return pl.pallas_call(
        functools.partial(_idx_proj_kernel, w_scale=(D_IDX ** -0.5) * (H_IDX ** -0.5)),
        grid=(B, S // tm),
        in_specs=[
            pl.BlockSpec((None, tm, D), lambda b, i: (b, i, 0)),
            pl.BlockSpec((1, D), const),
            pl.BlockSpec((D, ncol), const),
            pl.BlockSpec((D, ncol), const),
            pl.BlockSpec((1, LANES), const),
            tspec, tspec, tspec,
        ],
        out_specs=[
            pl.BlockSpec((None, H_IDX, 256, tm), lambda b, i: (b, 0, 0, i)),
            pl.BlockSpec((None, tm, 256), lambda b, i: (b, i, 0)),
            pl.BlockSpec((None, H_IDX, tm), lambda b, i: (b, 0, i)),
        ],
        out_shape=[
            jax.ShapeDtypeStruct((B, H_IDX, 256, S), BF16),
            jax.ShapeDtypeStruct((B, S, 256), BF16),
            jax.ShapeDtypeStruct((B, H_IDX, S), F32),
        ],
        compiler_params=_cparams(("parallel", "parallel")),
        name="idx_proj",
    )(x, g.reshape(1, D), wh, wl, gpad, *tabs)


def _mla_proj_kernel(x_ref, g_ref, wd_ref, gqa_ref, gkva_ref, wuq_ref, wuk_ref, wuv_ref, gq_ref, gk_ref, bd_ref,
                     ct_ref, s1_ref, s2_ref, q_ref, k_ref, vt_ref):
    xb = _rms(x_ref[...], g_ref[...]).astype(BF16)
    d = _dot(xb, wd_ref[...])
    cq = _rms(d[:, :Q_LORA], gqa_ref[...]).astype(BF16)
    ckv = _rms(d[:, Q_LORA:Q_LORA + KV_LORA], gkva_ref[...]).astype(BF16)
    krz = pltpu.roll(d[:, Q_LORA + KV_LORA:], QK_NOPE, 1)
    krz2 = jnp.concatenate([krz, krz], axis=-1)
    bd = bd_ref[...]
    ct, s1, s2 = ct_ref[...], s1_ref[...], s2_ref[...]
    inv = 1.0 / B_QK
    for p in range(N_PAIRS):
        cols = slice(p * 256, (p + 1) * 256)
        y = _dot(cq, wuq_ref[:, cols])
        y = y * lax.rsqrt(_group_sumsq(y, bd) * inv + EPS) * gq_ref[...]
        q_ref[p] = _rope_wide(y, ct, s1, s2).astype(BF16)
        y = _dot(ckv, wuk_ref[:, cols]) + krz2
        y = y * lax.rsqrt(_group_sumsq(y, bd) * inv + EPS) * gk_ref[...]
        k_ref[p] = _rope_wide(y, ct, s1, s2).astype(BF16)
        _store_vt(vt_ref, p, _dot(ckv, wuv_ref[:, p * LANES:(p + 1) * LANES]))


def _mla_proj(x, g, w_down, gqa, gkva, w_uq, w_ukv, gq, gk, tabs, tm=512):
    B, S, D = x.shape
    nd = Q_LORA + KV_LORA + QK_ROPE
    wd = jnp.pad(w_down, ((0, 0), (0, 512 - nd))).astype(BF16)
    padh = LANES - B_QK
    wuq = jnp.pad(w_uq.reshape(Q_LORA, N_HEADS, B_QK), ((0, 0), (0, 0), (0, padh))).reshape(Q_LORA, N_HEADS * LANES)
    wukv = w_ukv.reshape(KV_LORA, N_HEADS, QK_NOPE + V_DIM)
    wuk = jnp.pad(wukv[:, :, :QK_NOPE], ((0, 0), (0, 0), (0, LANES - QK_NOPE))).reshape(KV_LORA, N_HEADS * LANES)
    wuv = wukv[:, :, QK_NOPE:].reshape(KV_LORA, N_HEADS * V_DIM)
    gq2 = jnp.tile(jnp.pad(gq * (B_QK ** -0.5 * LOG2E), (0, padh)), 2).reshape(1, 256)
    gk2 = jnp.tile(jnp.pad(gk, (0, padh)), 2).reshape(1, 256)
    bd = jnp.kron(jnp.eye(2, dtype=F32), jnp.ones((LANES, LANES), F32)).astype(BF16)
    const = lambda b, i: (0, 0)
    tspec = pl.BlockSpec((tm, LANES), lambda b, i: (i, 0))
    qk_shape = jax.ShapeDtypeStruct((B, N_PAIRS, S, 256), BF16)
    qk_spec = pl.BlockSpec((None, N_PAIRS, tm, 256), lambda b, i: (b, 0, i, 0))
    return pl.pallas_call(
        _mla_proj_kernel,
        grid=(B, S // tm),
        in_specs=[
            pl.BlockSpec((None, tm, D), lambda b, i: (b, i, 0)),
            pl.BlockSpec((1, D), const),
            pl.BlockSpec((D, 512), const),
            pl.BlockSpec((1, Q_LORA), const),
            pl.BlockSpec((1, KV_LORA), const),
            pl.BlockSpec((Q_LORA, N_HEADS * LANES), const),
            pl.BlockSpec((KV_LORA, N_HEADS * LANES), const),
            pl.BlockSpec((KV_LORA, N_HEADS * V_DIM), const),
            pl.BlockSpec((1, 256), const),
            pl.BlockSpec((1, 256), const),
            pl.BlockSpec((256, 256), const),
            tspec, tspec, tspec,
        ],
        out_specs=[qk_spec, qk_spec, _vt_spec(tm)],
        out_shape=[qk_shape, qk_shape, _vt_shape(B, S)],
        compiler_params=_cparams(("parallel", "parallel")),
        name="mla_proj",
    )(x, g.reshape(1, D), wd, gqa.reshape(1, Q_LORA), gkva.reshape(1, KV_LORA), wuq.astype(BF16),
      wuk.astype(BF16), wuv.astype(BF16), gq2, gk2, bd, *tabs)


def _stack_heads(q2, half):
    qf = q2.astype(F32)
    low = lax.broadcasted_iota(jnp.int32, qf.shape, 1) < half
    zero = jnp.zeros_like(qf)
    return jnp.concatenate([jnp.where(low, qf, zero).T, jnp.where(low, zero, qf).T], axis=1).astype(BF16)


def _flash_init(acc_ref):
    r = acc_ref.shape[1]
    acc_ref[...] = jnp.zeros(acc_ref.shape, F32)
    return jnp.full((1, r), M_INIT, F32), jnp.zeros((1, r), F32)


def _consume(s, vt, m, l, acc_ref):
    m_new = jnp.maximum(m, jnp.max(s, axis=0, keepdims=True))
    alpha = jnp.exp2(m - m_new)
    p = jnp.exp2(s - m_new)
    l_new = alpha * l + jnp.sum(p, axis=0, keepdims=True)
    acc_ref[...] = alpha * acc_ref[...] + _dot(vt, p.astype(BF16))
    return m_new, l_new


def _sweep(first, n, s_ref, scores, values, ml, acc_ref):
    def one(j, s, ml):
        return _consume(s, values(j), *ml, acc_ref)

    odd = n % 2
    ml = lax.cond(odd == 1, lambda ml: one(first, scores(first), ml), lambda ml: ml, ml)
    j0 = first + odd
    last = first + n - 1

    @pl.when(n >= 2)
    def _():
        s_ref[0] = scores(j0)

    def body(t, ml):
        j = j0 + 2 * t
        s1 = scores(j + 1)
        ml = one(j, s_ref[0], ml)
        s_ref[1] = s1
        s0 = scores(jnp.minimum(j + 2, last))
        ml = one(j + 1, s_ref[1], ml)
        s_ref[0] = s0
        return ml

    return lax.fori_loop(0, n // 2, body, ml)


def _flash_out(l, acc_ref, tq):
    o = acc_ref[...] / l
    ot = jnp.concatenate([o[:HEAD_DIM, :tq], o[HEAD_DIM:, tq:]], axis=0)
    return ot.T.astype(BF16)


def _load_vt(vt_ref, kb, tk):
    n = tk // VT_BLK
    parts = [vt_ref[kb * n + r] for r in range(n)]
    return parts[0] if n == 1 else jnp.concatenate(parts, axis=1)


def _band_attn_kernel(q_ref, k_ref, vt_ref, bias_ref, o_ref, acc_ref, s_ref, *, tq, nblk):
    i = pl.program_id(2)
    qs = _stack_heads(q_ref[...], HEAD_DIM)
    lead = i - (nblk - 1)

    def scores(kb):
        ks = pl.multiple_of(kb * tq, tq)
        return _dot(k_ref[pl.ds(ks, tq), :], qs) + bias_ref[kb - lead]

    _, l = _sweep(jnp.maximum(lead, 0), jnp.minimum(i + 1, nblk), s_ref, scores,
                  lambda kb: _load_vt(vt_ref, kb, tq), _flash_init(acc_ref), acc_ref)
    o_ref[...] = _flash_out(l, acc_ref, tq)


def _band_attn(q, k, vt, bias, tq=256):
    B, _, S, _ = q.shape
    nblk = bias.shape[1]
    return pl.pallas_call(
        functools.partial(_band_attn_kernel, tq=tq, nblk=nblk),
        grid=(B, N_PAIRS, S // tq),
        in_specs=[
            pl.BlockSpec((None, None, tq, LANES), lambda b, p, i: (b, p, i, 0)),
            pl.BlockSpec((None, None, S, LANES), lambda b, p, i: (b, p, 0, 0)),
            pl.BlockSpec((None, None, S // VT_BLK, LANES, VT_BLK), lambda b, p, i: (b, p, 0, 0, 0)),
            pl.BlockSpec((None, nblk, tq, 2 * tq), lambda b, p, i: (p, 0, 0, 0)),
        ],
        out_specs=pl.BlockSpec((None, tq, LANES), lambda b, p, i: (b, i, p)),
        out_shape=jax.ShapeDtypeStruct((B, S, N_HEADS * HEAD_DIM), BF16),
        scratch_shapes=[pltpu.VMEM((LANES, 2 * tq), F32), pltpu.VMEM((2, tq, 2 * tq), F32)],
        compiler_params=_cparams(("parallel", "parallel", "arbitrary")),
        name="band_attn",
    )(q, k, vt, bias)


def _mla_attn_kernel(q_ref, k_ref, vt_ref, o_ref, acc_ref, s_ref, *, tq, tk):
    i = pl.program_id(2)
    qs = _stack_heads(q_ref[...], LANES)
    t0 = i * tq
    n_full = t0 // tk
    n_blk = (t0 + tq + tk - 1) // tk
    lim = ((t0 + lax.broadcasted_iota(jnp.int32, (1, tq), 1)) // CHUNK + 1) * CHUNK

    def scores(kb):
        ks = pl.multiple_of(kb * tk, tk)
        return _dot(k_ref[pl.ds(ks, tk), :], qs)

    def edge_scores(kb):
        row = kb * tk + lax.broadcasted_iota(jnp.int32, (tk, tq), 0)
        add = jnp.where(row < lim, 0.0, NEG).astype(F32)
        return scores(kb) + jnp.concatenate([add, add], axis=1)

    values = lambda kb: _load_vt(vt_ref, kb, tk)
    ml = _sweep(0, n_full, s_ref, scores, values, _flash_init(acc_ref), acc_ref)
    _, l = _sweep(n_full, n_blk - n_full, s_ref, edge_scores, values, ml, acc_ref)
    o_ref[...] = _flash_out(l, acc_ref, tq)


def _mla_attn(q, k, vt, tq=256, tk=512):
    B, _, S, _ = q.shape
    tk = min(tk, S)
    return pl.pallas_call(
        functools.partial(_mla_attn_kernel, tq=tq, tk=tk),
        grid=(B, N_PAIRS, S // tq),
        in_specs=[
            pl.BlockSpec((None, None, tq, 256), lambda b, p, i: (b, p, i, 0)),
            pl.BlockSpec((None, None, S, 256), lambda b, p, i: (b, p, 0, 0)),
            pl.BlockSpec((None, None, S // VT_BLK, LANES, VT_BLK), lambda b, p, i: (b, p, 0, 0, 0)),
        ],
        out_specs=pl.BlockSpec((None, tq, LANES), lambda b, p, i: (b, i, p)),
        out_shape=jax.ShapeDtypeStruct((B, S, N_HEADS * V_DIM), BF16),
        scratch_shapes=[pltpu.VMEM((LANES, 2 * tq), F32), pltpu.VMEM((2, tk, 2 * tq), F32)],
        compiler_params=_cparams(("parallel", "parallel", "arbitrary")),
        name="mla_attn",
    )(q, k, vt)


def _dsa_attn_kernel(q_ref, qp_ref, wi_ref, bias_ref, k_ref, vt_ref, kp_ref, o_ref,
                     sc_ref, h_ref, acc_ref, stage_ref, s_ref, *, tq, tk, topk):
    i = pl.program_id(1)
    t0 = i * tq
    n_blk = (t0 + tq + tk - 1) // tk
    n_far = jnp.maximum((t0 // LANES - 1) // (tk // LANES), 0)
    lim = ((t0 + lax.broadcasted_iota(jnp.int32, (1, tq), 1)) // CHUNK + 1) * CHUNK

    wv = wi_ref[...]

    def score_body(kb, carry):
        ks = pl.multiple_of(kb * tk, tk)
        kp = kp_ref[pl.ds(ks, tk), :]
        acc = jnp.zeros((tk, tq), F32)
        for h in range(H_IDX):
            acc = acc + jnp.maximum(_dot(kp, qp_ref[h]), 0.0) * wv[h:h + 1, :]
        row = ks + lax.broadcasted_iota(jnp.int32, (tk, tq), 0)
        acc = jnp.where(row < lim, acc, -jnp.inf)
        bits = pltpu.bitcast(acc, jnp.int32)
        key = bits ^ ((bits >> 31) & 0x7FFFFFFF)
        sc_ref[kb] = key
        h_ref[kb] = (key >> 16).astype(jnp.int16)
        return carry

    lax.fori_loop(0, n_blk, score_body, 0)

    def count16(pred):
        def count_body(kb, part):
            hit = jnp.where(pred(h_ref[kb]), jnp.int16(1), jnp.int16(0))
            rows = hit[0:16]
            for r in range(16, tk, 16):
                rows = rows + hit[r:r + 16]
            return part + rows.astype(jnp.int32)

        part = lax.fori_loop(0, n_blk, count_body, jnp.zeros((16, tq), jnp.int32))
        return jnp.sum(part, axis=0, keepdims=True)

    def kth16(need):
        def bit_body(t, ans):
            cand = ans + jnp.left_shift(jnp.int32(1), 15 - t)
            cand16 = cand.astype(jnp.int16)
            return jnp.where(count16(lambda h: h >= cand16) >= need, cand, ans)

        return lax.fori_loop(0, 16, bit_body, jnp.full((1, tq), -(2 ** 15), jnp.int32))

    top = kth16(topk)
    top16 = top.astype(jnp.int16)
    need = topk - count16(lambda h: h > top16)

    def low_body(kb, carry):
        low = ((sc_ref[kb] & 0xFFFF) - 2 ** 15).astype(jnp.int16)
        h_ref[kb] = jnp.where(h_ref[kb] == top16, low, jnp.int16(-(2 ** 15)))
        return carry

    lax.fori_loop(0, n_blk, low_body, 0)
    thr = jnp.left_shift(top, 16) + (kth16(need) + 2 ** 15)

    def mask_body(kb, carry):
        sc_ref[kb] = jnp.where(sc_ref[kb] >= thr, 0, NEG_BITS)
        return carry

    lax.fori_loop(0, n_blk, mask_body, 0)

    def pair_body(p, carry):
        qs = _stack_heads(q_ref[p], HEAD_DIM)

        def far_scores(kb):
            ks = pl.multiple_of(kb * tk, tk)
            am = pltpu.bitcast(sc_ref[kb], F32)
            return _dot(k_ref[p, pl.ds(ks, tk), :], qs) + jnp.concatenate([am, am], axis=1)

        def near_scores(kb):
            cols = []
            for hh in range(2):
                t_same = bias_ref[p, hh, 0]
                t_prev = bias_ref[p, hh, 1]
                for a in range(tq // LANES):
                    tiles = []
                    for c in range(tk // LANES):
                        d = (kb * (tk // LANES) + c) - (i * (tq // LANES) + a)
                        tiles.append(jnp.where(d >= 1, NEG,
                                               jnp.where(d == 0, t_same, jnp.where(d == -1, t_prev, 0.0))))
                    cols.append(jnp.concatenate(tiles, axis=0))
            return far_scores(kb) + jnp.concatenate(cols, axis=1)

        values = lambda kb: _load_vt(vt_ref.at[p], kb, tk)
        ml = _sweep(0, n_far, s_ref, far_scores, values, _flash_init(acc_ref), acc_ref)
        _, l = _sweep(n_far, n_blk - n_far, s_ref, near_scores, values, ml, acc_ref)
        stage_ref[p] = _flash_out(l, acc_ref, tq)
        return carry

    lax.fori_loop(0, N_PAIRS, pair_body, 0)
    for p in range(N_PAIRS):
        o_ref[:, p * LANES:(p + 1) * LANES] = stage_ref[p]


def _dsa_attn(q, k, vt, qp, kp, wi, bias, tq=256, tk=512):
    B, _, S, _ = q.shape
    topk = min(TOPK_MAX, S // 4)
    once = pl.Buffered(1)
    return pl.pallas_call(
        functools.partial(_dsa_attn_kernel, tq=tq, tk=tk, topk=topk),
        grid=(B, S // tq),
        in_specs=[
            pl.BlockSpec((None, N_PAIRS, tq, LANES), lambda b, i: (b, 0, i, 0)),
            pl.BlockSpec((None, H_IDX, 256, tq), lambda b, i: (b, 0, 0, i)),
            pl.BlockSpec((None, H_IDX, tq), lambda b, i: (b, 0, i)),
            pl.BlockSpec((N_PAIRS, 2, 2, LANES, LANES), lambda b, i: (0, 0, 0, 0, 0), pipeline_mode=once),
            pl.BlockSpec((None, N_PAIRS, S, LANES), lambda b, i: (b, 0, 0, 0), pipeline_mode=once),
            pl.BlockSpec((None, N_PAIRS, S // VT_BLK, LANES, VT_BLK), lambda b, i: (b, 0, 0, 0, 0),
                         pipeline_mode=once),
            pl.BlockSpec((None, S, 256), lambda b, i: (b, 0, 0), pipeline_mode=once),
        ],
        out_specs=pl.BlockSpec((None, tq, N_HEADS * HEAD_DIM), lambda b, i: (b, i, 0)),
        out_shape=jax.ShapeDtypeStruct((B, S, N_HEADS * HEAD_DIM), BF16),
        scratch_shapes=[
            pltpu.VMEM((S // tk, tk, tq), jnp.int32),
            pltpu.VMEM((S // tk, tk, tq), jnp.int16),
            pltpu.VMEM((LANES, 2 * tq), F32),
            pltpu.VMEM((N_PAIRS, tq, LANES), BF16),
            pltpu.VMEM((2, tk, 2 * tq), F32),
        ],
        compiler_params=_cparams(("parallel", "arbitrary"), VMEM_LIMIT_DSA),
        name="dsa_attn",
    )(q, qp, wi, bias, k, vt, kp)


def _out_proj_kernel(x_ref, a_ref, w_ref, o_ref):
    o_ref[...] = x_ref[...] + _dot(a_ref[...], w_ref[...])


def _out_proj(x, a, w, tm=512):
    B, S, D = x.shape
    return pl.pallas_call(
        _out_proj_kernel,
        grid=(B, S // tm),
        in_specs=[
            pl.BlockSpec((None, tm, D), lambda b, i: (b, i, 0)),
            pl.BlockSpec((None, tm, a.shape[-1]), lambda b, i: (b, i, 0)),
            pl.BlockSpec(w.shape, lambda b, i: (0, 0)),
        ],
        out_specs=pl.BlockSpec((None, tm, D), lambda b, i: (b, i, 0)),
        out_shape=jax.ShapeDtypeStruct((B, S, D), F32),
        compiler_params=_cparams(("parallel", "parallel")),
        name="out_proj",
    )(x, a, w.astype(BF16))


def _ffn_kernel(x_ref, xp_ref, g_ref, wg_ref, wv_ref, cw_ref, cb_ref, wd_ref, o_ref, acc_ref, he_ref, u_ref,
                *, tm, n_chunks):
    i = pl.program_id(1)
    g = g_ref[...]
    he_ref[:HALO] = (_rms(xp_ref[...], g) * jnp.where(i > 0, 1.0, 0.0)).astype(BF16)
    he_ref[HALO:] = _rms(x_ref[...], g).astype(BF16)
    acc_ref[...] = jnp.zeros(acc_ref.shape, F32)

    def up(c, slot):
        he = he_ref[...]
        u_ref[slot, 0] = _dot(he, wg_ref[c])
        u_ref[slot, 1] = _dot(he, wv_ref[c])

    def conv(u, w, b):
        return (b + w[0:1] * u[HALO - 2:HALO - 2 + tm] + w[1:2] * u[HALO - 1:HALO - 1 + tm]
                + w[2:3] * u[HALO:HALO + tm])

    def down(c, slot):
        cw = cw_ref[c]
        cb = cb_ref[c]
        gate = conv(u_ref[slot, 0], cw[0], cb[0])
        val = conv(u_ref[slot, 1], cw[1], cb[1])
        act = gate * jax.nn.sigmoid(gate) * val
        acc_ref[...] += _dot(act.astype(BF16), wd_ref[c])

    up(0, 0)

    def pair_body(t, carry):
        c = 2 * t
        up(c + 1, 1)
        down(c, 0)
        up(c + 2, 0)
        down(c + 1, 1)
        return carry

    assert n_chunks % 2 == 1
    lax.fori_loop(0, n_chunks // 2, pair_body, 0)
    down(n_chunks - 1, 0)
    o_ref[...] = x_ref[...] + acc_ref[...]


def _ffn(x, g, w_up, conv_w, conv_b, w_down, tm=512):
    B, S, D = x.shape
    nc = D_FF // FF_CHUNK
    wup = w_up.astype(BF16).reshape(D, 2, nc, FF_CHUNK).transpose(1, 2, 0, 3)
    cw = conv_w.reshape(CONV_W, 2, nc, FF_CHUNK).transpose(2, 1, 0, 3)
    cb = conv_b.reshape(2, nc, 1, FF_CHUNK).transpose(1, 0, 2, 3)
    wd = w_down.astype(BF16).reshape(nc, FF_CHUNK, D)
    c3 = lambda b, i: (0, 0, 0)
    c4 = lambda b, i: (0, 0, 0, 0)
    return pl.pallas_call(
        functools.partial(_ffn_kernel, tm=tm, n_chunks=nc),
        grid=(B, S // tm),
        in_specs=[
            pl.BlockSpec((None, tm, D), lambda b, i: (b, i, 0)),
            pl.BlockSpec((None, HALO, D), lambda b, i: (b, jnp.maximum(i * (tm // HALO) - 1, 0), 0)),
            pl.BlockSpec((1, D), lambda b, i: (0, 0)),
            pl.BlockSpec((nc, D, FF_CHUNK), c3),
            pl.BlockSpec((nc, D, FF_CHUNK), c3),
            pl.BlockSpec((nc, 2, CONV_W, FF_CHUNK), c4),
            pl.BlockSpec((nc, 2, 1, FF_CHUNK), c4),
            pl.BlockSpec((nc, FF_CHUNK, D), c3),
        ],
        out_specs=pl.BlockSpec((None, tm, D), lambda b, i: (b, i, 0)),
        out_shape=jax.ShapeDtypeStruct((B, S, D), F32),
        scratch_shapes=[pltpu.VMEM((tm, D), F32), pltpu.VMEM((HALO + tm, D), BF16),
                        pltpu.VMEM((2, 2, HALO + tm, FF_CHUNK), F32)],
        compiler_params=_cparams(("parallel", "parallel")),
        name="conv_ffn",
    )(x, x, g.reshape(1, D), wup[0], wup[1], cw, cb, wd)


def _rope_tables(S, period, off):
    inv = 1.0 / (ROPE_THETA ** (jnp.arange(0, ROPE_DIM, 2, dtype=F32) / ROPE_DIM))
    ang = jnp.arange(S, dtype=F32)[:, None] * inv[None, :]
    cos, sin = jnp.cos(ang), jnp.sin(ang)
    half = ROPE_DIM // 2
    lane = np.arange(LANES) % period
    is1 = (lane >= off) & (lane < off + half)
    is2 = (lane >= off + half) & (lane < off + ROPE_DIM)
    idx = np.where(is1, lane - off, np.where(is2, lane - off - half, 0))
    ct = jnp.where(is1 | is2, cos[:, idx], 1.0)
    s1 = jnp.where(is1, -sin[:, idx], 0.0)
    s2 = jnp.where(is2, sin[:, idx], 0.0)
    return ct, s1, s2


def _t5_bucket(rel):
    nb = T5_BUCKETS // 2
    max_exact = nb // 2
    n = jnp.abs(rel)
    large = max_exact + (jnp.log(jnp.maximum(n, 1).astype(F32) / max_exact)
                         / math.log(T5_MAX_DIST / max_exact) * (nb - max_exact)).astype(jnp.int32)
    large = jnp.minimum(large, nb - 1)
    return jnp.where(rel > 0, nb, 0) + jnp.where(n < max_exact, n, large)


def _dsa_bias_tiles(t5_bias):
    kj = jnp.arange(LANES, dtype=jnp.int32)[:, None]
    qi = jnp.arange(LANES, dtype=jnp.int32)[None, :]
    far = t5_bias[_t5_bucket(jnp.int32(-4 * T5_MAX_DIST))]
    same = (jnp.transpose(t5_bias[_t5_bucket(kj - qi)], (2, 0, 1)) - far[:, None, None]) * LOG2E
    same = jnp.where((kj < (qi // CHUNK + 1) * CHUNK)[None], same, NEG)
    prev = (jnp.transpose(t5_bias[_t5_bucket(kj - LANES - qi)], (2, 0, 1)) - far[:, None, None]) * LOG2E
    return jnp.stack([same, prev], axis=1).reshape(N_PAIRS, 2, 2, LANES, LANES)


def _band_bias(rel_bias, tq):
    pad = LEFT_CHUNKS * CHUNK
    nblk = -(-pad // tq) + 1
    lead = (nblk - 1) * tq
    nk = nblk * tq
    kk = jnp.arange(nk, dtype=jnp.int32)[:, None] - lead
    i = jnp.arange(tq, dtype=jnp.int32)[None, :]
    first = (i // CHUNK) * CHUNK - pad
    ok = (kk >= first) & (kk < (i // CHUNK + 1) * CHUNK)
    dist = jnp.arange(lead - nk + 1, lead + tq, dtype=jnp.int32)
    v = rel_bias[:, jnp.clip(dist, -(CHUNK - 1), REL_MAX_PAST) + CHUNK - 1]
    L = nk + tq
    v = jnp.pad(v, ((0, 0), (0, L - v.shape[1])))
    rot = jnp.tile(v, (1, nk + 1))[:, :nk * (L + 1)].reshape(v.shape[0], nk, L + 1)
    b = rot[:, ::-1, :tq]
    b = jnp.where(ok[None], b * LOG2E, NEG).reshape(N_PAIRS, 2, nblk, tq, tq)
    return jnp.transpose(b, (0, 2, 3, 1, 4)).reshape(N_PAIRS, nblk, tq, 2 * tq)


def kernel(x, norm_mix, norm_ffn, t5_bias, a_w_in, a_q_norm, a_k_norm, a_kidx_norm, a_w_out, b_w_down, b_q_a_norm, b_kv_a_norm, b_w_uq, b_w_ukv, b_q_norm, b_k_norm, b_w_out, c_w_in, c_q_norm, c_k_norm, c_rel_bias, c_w_out, f_w_up, f_conv_w, f_conv_b, f_w_down):
    B, S, D = x.shape
    depth = norm_mix.shape[0]

    tq_c = 256
    tabs_idx = _rope_tables(S, D_IDX, D_IDX - ROPE_DIM)
    tabs_mla = _rope_tables(S, LANES, QK_NOPE)
    scale = HEAD_DIM ** -0.5 * LOG2E
    for layer in range(depth):
        kind, j = layer % N_MIXERS, layer // N_MIXERS
        g = norm_mix[layer]
        if kind == 0:
            q, k, vt = _qkv_proj(x, g, a_w_in[j][:, :A_QKV], a_q_norm[j] * scale, a_k_norm[j])
            qp, kp, wi = _idx_proj(x, g, a_w_in[j][:, A_QKV:], a_kidx_norm[j], tabs_idx)
            att = _dsa_attn(q, k, vt, qp, kp, wi, _dsa_bias_tiles(t5_bias))
            w_out = a_w_out[j]
        elif kind == 1:
            q, k, vt = _mla_proj(x, g, b_w_down[j], b_q_a_norm[j], b_kv_a_norm[j], b_w_uq[j], b_w_ukv[j],
                                 b_q_norm[j], b_k_norm[j], tabs_mla)
            att = _mla_attn(q, k, vt)
            w_out = b_w_out[j]
        else:
            q, k, vt = _qkv_proj(x, g, c_w_in[j], c_q_norm[j] * scale, c_k_norm[j])
            att = _band_attn(q, k, vt, _band_bias(c_rel_bias[j], tq_c), tq=tq_c)
            w_out = c_w_out[j]
        x = _out_proj(x, att, w_out)
        x = _ffn(x, norm_ffn[layer], f_w_up[layer], f_conv_w[layer], f_conv_b[layer], f_w_down[layer])
    return x
```

```python
import functools
import math

import numpy as np
import jax
import jax.numpy as jnp
from jax import lax
from jax.experimental import pallas as pl
from jax.experimental.pallas import tpu as pltpu

F32 = jnp.float32
BF16 = jnp.bfloat16

D_MODEL = 1024
CHUNK = 64
EPS = 1e-6
ROPE_THETA = 10000.0
NEG = -1e30
NEG_BITS = int(np.float32(NEG).view(np.int32))
M_INIT = -1e38
INT_MIN = -(2 ** 31)
LOG2E = math.log2(math.e)

N_MIXERS = 3
N_HEADS = 16
HEAD_DIM = 64
N_PAIRS = N_HEADS // 2
LANES = 128
VT_BLK = 256

H_IDX = 8
D_IDX = 64
ROPE_DIM = 32
TOPK_MAX = 256
T5_BUCKETS = 32
T5_MAX_DIST = 128
A_QKV = 3 * N_HEADS * HEAD_DIM
Q_LORA = 256
KV_LORA = 128
QK_NOPE = 64
QK_ROPE = 32
V_DIM = 64
B_QK = QK_NOPE + QK_ROPE
LEFT_CHUNKS = 8
REL_MAX_PAST = 128
D_FF = 2816
CONV_W = 3
FF_CHUNK = 256
HALO = 8

VMEM_LIMIT = 56 * 1024 * 1024
VMEM_LIMIT_DSA = 60 * 1024 * 1024


def _cparams(sem, vmem_limit=VMEM_LIMIT):
    return pltpu.CompilerParams(dimension_semantics=sem, vmem_limit_bytes=vmem_limit)


def _rms(x, g):
    return x * lax.rsqrt(jnp.mean(x * x, axis=-1, keepdims=True) + EPS) * g


def _dot(a, b):
    return jnp.dot(a, b, preferred_element_type=F32)


def _split_bf16(x):
    hi = x.astype(BF16)
    lo = (x - hi.astype(F32)).astype(BF16)
    return hi, lo


def _group_sumsq(y, bd):
    hi, lo = _split_bf16(y * y)
    return _dot(hi, bd) + _dot(lo, bd)


def _rope(y, ct, s1, s2):
    return y * ct + pltpu.roll(y, LANES - 16, 1) * s1 + pltpu.roll(y, 16, 1) * s2


def _rope_wide(y, ct, s1, s2):
    w = y.shape[-1]
    parts = [_rope(y[:, c:c + LANES], ct, s1, s2) for c in range(0, w, LANES)]
    return parts[0] if len(parts) == 1 else jnp.concatenate(parts, axis=-1)


def _store_vt(vt_ref, p, y):
    yt = y.T.astype(BF16)
    for r in range(y.shape[0] // VT_BLK):
        vt_ref[p, r] = yt[:, r * VT_BLK:(r + 1) * VT_BLK]


def _qkv_proj_kernel(x_ref, g_ref, w_ref, gq_ref, gk_ref, bd_ref, q_ref, k_ref, vt_ref):
    xb = _rms(x_ref[...], g_ref[...]).astype(BF16)
    bd = bd_ref[...]
    hd = N_HEADS * HEAD_DIM
    for which, (o_ref, gain_ref) in enumerate(((q_ref, gq_ref), (k_ref, gk_ref), (vt_ref, None))):
        for c in range(hd // 256):
            col = which * hd + c * 256
            y = _dot(xb, w_ref[:, col:col + 256])
            if gain_ref is None:
                _store_vt(o_ref, 2 * c, y[:, :LANES])
                _store_vt(o_ref, 2 * c + 1, y[:, LANES:])
            else:
                y = y * lax.rsqrt(_group_sumsq(y, bd) * (1.0 / HEAD_DIM) + EPS) * gain_ref[...]
                yb = y.astype(BF16)
                o_ref[2 * c] = yb[:, :LANES]
                o_ref[2 * c + 1] = yb[:, LANES:]


def _vt_spec(tm):
    return pl.BlockSpec((None, N_PAIRS, tm // VT_BLK, LANES, VT_BLK), lambda b, i: (b, 0, i, 0, 0))


def _vt_shape(B, S):
    return jax.ShapeDtypeStruct((B, N_PAIRS, S // VT_BLK, LANES, VT_BLK), BF16)


def _qkv_proj(x, g, w, gq, gk, tm=512):
    B, S, D = x.shape
    hd = N_HEADS * HEAD_DIM
    bd = jnp.kron(jnp.eye(256 // HEAD_DIM, dtype=F32), jnp.ones((HEAD_DIM, HEAD_DIM), F32)).astype(BF16)
    out = jax.ShapeDtypeStruct((B, N_PAIRS, S, LANES), BF16)
    const = lambda b, i: (0, 0)
    ospec = pl.BlockSpec((None, N_PAIRS, tm, LANES), lambda b, i: (b, 0, i, 0))
    return pl.pallas_call(
        _qkv_proj_kernel,
        grid=(B, S // tm),
        in_specs=[
            pl.BlockSpec((None, tm, D), lambda b, i: (b, i, 0)),
            pl.BlockSpec((1, D), const),
            pl.BlockSpec((D, 3 * hd), const),
            pl.BlockSpec((1, 256), const),
            pl.BlockSpec((1, 256), const),
            pl.BlockSpec((256, 256), const),
        ],
        out_specs=[ospec, ospec, _vt_spec(tm)],
        out_shape=[out, out, _vt_shape(B, S)],
        compiler_params=_cparams(("parallel", "parallel")),
        name="qkv_proj",
    )(x, g.reshape(1, D), w.astype(BF16), jnp.tile(gq, 4).reshape(1, 256), jnp.tile(gk, 4).reshape(1, 256), bd)


def _idx_proj_kernel(x_ref, g_ref, wh_ref, wl_ref, gki_ref, ct_ref, s1_ref, s2_ref, qp_ref, kp_ref, wi_ref,
                     *, w_scale):
    xn = _rms(x_ref[...], g_ref[...])
    xh, xl = _split_bf16(xn)

    def mm(c0, c1):
        wh = wh_ref[:, c0:c1]
        return _dot(xh, wh) + (_dot(xl, wh) + _dot(xh, wl_ref[:, c0:c1]))

    ct, s1, s2 = ct_ref[...], s1_ref[...], s2_ref[...]
    tm = xn.shape[0]
    low = lax.broadcasted_iota(jnp.int32, (tm, LANES), 1) < D_IDX
    zero = jnp.zeros((tm, LANES), F32)

    for c in range(H_IDX // 2):
        y = _rope(mm(c * LANES, (c + 1) * LANES), ct, s1, s2)
        hi = y.astype(BF16).astype(F32)
        lo = y - hi
        rhi = pltpu.roll(hi, D_IDX, 1)
        rlo = pltpu.roll(lo, D_IDX, 1)
        qp_ref[2 * c, :LANES, :] = jnp.where(low, hi, rlo).T.astype(BF16)
        qp_ref[2 * c, LANES:, :] = jnp.where(low, hi, zero).T.astype(BF16)
        qp_ref[2 * c + 1, :LANES, :] = jnp.where(low, rhi, lo).T.astype(BF16)
        qp_ref[2 * c + 1, LANES:, :] = jnp.where(low, rhi, zero).T.astype(BF16)

    c0 = H_IDX * D_IDX
    y = mm(c0, c0 + LANES)
    y = y * lax.rsqrt(jnp.sum(y * y, axis=-1, keepdims=True) * (1.0 / D_IDX) + EPS) * gki_ref[...]
    y = _rope(y, ct, s1, s2)
    hi = y.astype(BF16).astype(F32)
    lo = y - hi
    kp_ref[:, :LANES] = jnp.where(low, hi, pltpu.roll(hi, D_IDX, 1)).astype(BF16)
    kp_ref[:, LANES:] = jnp.where(low, lo, zero).astype(BF16)

    wi_ref[...] = (mm(c0 + LANES, c0 + 2 * LANES) * w_scale).T[:H_IDX]


def _idx_proj(x, g, w_idx, gki, tabs, tm=512):
    B, S, D = x.shape
    nq = H_IDX * D_IDX
    wq, wk, ww = w_idx[:, :nq], w_idx[:, nq:nq + D_IDX], w_idx[:, nq + D_IDX:]
    wpad = jnp.concatenate([
        wq, jnp.pad(wk, ((0, 0), (0, LANES - D_IDX))), jnp.pad(ww, ((0, 0), (0, LANES - H_IDX)))], axis=1)
    wh = wpad.astype(BF16)
    wl = (wpad - wh.astype(F32)).astype(BF16)
    ncol = nq + 2 * LANES
    gpad = jnp.pad(gki, (0, LANES - D_IDX)).reshape(1, LANES)
    const = lambda b, i: (0, 0)
    tspec = pl.BlockSpec((tm, LANES), lambda b, i: (i, 0))
    return pl.pallas_call(
        functools.partial(_idx_proj_kernel, w_scale=(D_IDX ** -0.5) * (H_IDX ** -0.5)),
        grid=(B, S // tm),
        in_specs=[
            pl.BlockSpec((None, tm, D), lambda b, i: (b, i, 0)),
            pl.BlockSpec((1, D), const),
            pl.BlockSpec((D, ncol), const),
            pl.BlockSpec((D, ncol), const),
            pl.BlockSpec((1, LANES), const),
            tspec, tspec, tspec,
        ],
        out_specs=[
            pl.BlockSpec((None, H_IDX, 256, tm), lambda b, i: (b, 0, 0, i)),
            pl.BlockSpec((None, tm, 256), lambda b, i: (b, i, 0)),
            pl.BlockSpec((None, H_IDX, tm), lambda b, i: (b, 0, i)),
        ],
        out_shape=[
            jax.ShapeDtypeStruct((B, H_IDX, 256, S), BF16),
            jax.ShapeDtypeStruct((B, S, 256), BF16),
            jax.ShapeDtypeStruct((B, H_IDX, S), F32),
        ],
        compiler_params=_cparams(("parallel", "parallel")),
        name="idx_proj",
    )(x, g.reshape(1, D), wh, wl, gpad, *tabs)


def _mla_proj_kernel(x_ref, g_ref, wd_ref, gqa_ref, gkva_ref, wuq_ref, wuk_ref, wuv_ref, gq_ref, gk_ref, bd_ref,
                     ct_ref, s1_ref, s2_ref, q_ref, k_ref, vt_ref):
    xb = _rms(x_ref[...], g_ref[...]).astype(BF16)
    d = _dot(xb, wd_ref[...])
    cq = _rms(d[:, :Q_LORA], gqa_ref[...]).astype(BF16)
    ckv = _rms(d[:, Q_LORA:Q_LORA + KV_LORA], gkva_ref[...]).astype(BF16)
    krz = pltpu.roll(d[:, Q_LORA + KV_LORA:], QK_NOPE, 1)
    krz2 = jnp.concatenate([krz, krz], axis=-1)
    bd = bd_ref[...]
    ct, s1, s2 = ct_ref[...], s1_ref[...], s2_ref[...]
    inv = 1.0 / B_QK
    for p in range(N_PAIRS):
        cols = slice(p * 256, (p + 1) * 256)
        y = _dot(cq, wuq_ref[:, cols])
        y = y * lax.rsqrt(_group_sumsq(y, bd) * inv + EPS) * gq_ref[...]
        q_ref[p] = _rope_wide(y, ct, s1, s2).astype(BF16)
        y = _dot(ckv, wuk_ref[:, cols]) + krz2
        y = y * lax.rsqrt(_group_sumsq(y, bd) * inv + EPS) * gk_ref[...]
        k_ref[p] = _rope_wide(y, ct, s1, s2).astype(BF16)
        _store_vt(vt_ref, p, _dot(ckv, wuv_ref[:, p * LANES:(p + 1) * LANES]))


def _mla_proj(x, g, w_down, gqa, gkva, w_uq, w_ukv, gq, gk, tabs, tm=512):
    B, S, D = x.shape
    nd = Q_LORA + KV_LORA + QK_ROPE
    wd = jnp.pad(w_down, ((0, 0), (0, 512 - nd))).astype(BF16)
    padh = LANES - B_QK
    wuq = jnp.pad(w_uq.reshape(Q_LORA, N_HEADS, B_QK), ((0, 0), (0, 0), (0, padh))).reshape(Q_LORA, N_HEADS * LANES)
    wukv = w_ukv.reshape(KV_LORA, N_HEADS, QK_NOPE + V_DIM)
    wuk = jnp.pad(wukv[:, :, :QK_NOPE], ((0, 0), (0, 0), (0, LANES - QK_NOPE))).reshape(KV_LORA, N_HEADS * LANES)
    wuv = wukv[:, :, QK_NOPE:].reshape(KV_LORA, N_HEADS * V_DIM)
    gq2 = jnp.tile(jnp.pad(gq * (B_QK ** -0.5 * LOG2E), (0, padh)), 2).reshape(1, 256)
    gk2 = jnp.tile(jnp.pad(gk, (0, padh)), 2).reshape(1, 256)
    bd = jnp.kron(jnp.eye(2, dtype=F32), jnp.ones((LANES, LANES), F32)).astype(BF16)
    const = lambda b, i: (0, 0)
    tspec = pl.BlockSpec((tm, LANES), lambda b, i: (i, 0))
    qk_shape = jax.ShapeDtypeStruct((B, N_PAIRS, S, 256), BF16)
    qk_spec = pl.BlockSpec((None, N_PAIRS, tm, 256), lambda b, i: (b, 0, i, 0))
    return pl.pallas_call(
        _mla_proj_kernel,
        grid=(B, S // tm),
        in_specs=[
            pl.BlockSpec((None, tm, D), lambda b, i: (b, i, 0)),
            pl.BlockSpec((1, D), const),
            pl.BlockSpec((D, 512), const),
            pl.BlockSpec((1, Q_LORA), const),
            pl.BlockSpec((1, KV_LORA), const),
            pl.BlockSpec((Q_LORA, N_HEADS * LANES), const),
            pl.BlockSpec((KV_LORA, N_HEADS * LANES), const),
            pl.BlockSpec((KV_LORA, N_HEADS * V_DIM), const),
            pl.BlockSpec((1, 256), const),
            pl.BlockSpec((1, 256), const),
            pl.BlockSpec((256, 256), const),
            tspec, tspec, tspec,
        ],
        out_specs=[qk_spec, qk_spec, _vt_spec(tm)],
        out_shape=[qk_shape, qk_shape, _vt_shape(B, S)],
        compiler_params=_cparams(("parallel", "parallel")),
        name="mla_proj",
    )(x, g.reshape(1, D), wd, gqa.reshape(1, Q_LORA), gkva.reshape(1, KV_LORA), wuq.astype(BF16),
      wuk.astype(BF16), wuv.astype(BF16), gq2, gk2, bd, *tabs)


def _stack_heads(q2, half):
    qf = q2.astype(F32)
    low = lax.broadcasted_iota(jnp.int32, qf.shape, 1) < half
    zero = jnp.zeros_like(qf)
    return jnp.concatenate([jnp.where(low, qf, zero).T, jnp.where(low, zero, qf).T], axis=1).astype(BF16)


def _flash_init(acc_ref):
    r = acc_ref.shape[1]
    acc_ref[...] = jnp.zeros(acc_ref.shape, F32)
    return jnp.full((1, r), M_INIT, F32), jnp.zeros((1, r), F32)


def _consume(s, vt, m, l, acc_ref):
    m_new = jnp.maximum(m, jnp.max(s, axis=0, keepdims=True))
    alpha = jnp.exp2(m - m_new)
    p = jnp.exp2(s - m_new)
    l_new = alpha * l + jnp.sum(p, axis=0, keepdims=True)
    acc_ref[...] = alpha * acc_ref[...] + _dot(vt, p.astype(BF16))
    return m_new, l_new


def _sweep(first, n, s_ref, scores, values, ml, acc_ref, peel=True):
    def one(j, s, ml):
        return _consume(s, values(j), *ml, acc_ref)

    if isinstance(n, int):
        tiles = [scores(first + j) for j in range(n)]
        for j in range(n):
            ml = one(first + j, tiles[j], ml)
        return ml

    odd = n % 2 if peel else 0
    if peel:
        ml = lax.cond(odd == 1, lambda ml: one(first, scores(first), ml), lambda ml: ml, ml)
    j0 = first + odd
    last = first + n - 1

    @pl.when(n >= 2)
    def _():
        s_ref[0] = scores(j0)

    def body(t, ml):
        j = j0 + 2 * t
        s1 = scores(j + 1)
        ml = one(j, s_ref[0], ml)
        s_ref[1] = s1
        s0 = scores(jnp.minimum(j + 2, last))
        ml = one(j + 1, s_ref[1], ml)
        s_ref[0] = s0
        return ml

    return lax.fori_loop(0, n // 2, body, ml)


def _flash_out(l, acc_ref, tq):
    o = acc_ref[...] / l
    ot = jnp.concatenate([o[:HEAD_DIM, :tq], o[HEAD_DIM:, tq:]], axis=0)
    return ot.T.astype(BF16)


def _load_vt(vt_ref, kb, tk):
    n = tk // VT_BLK
    parts = [vt_ref[kb * n + r] for r in range(n)]
    return parts[0] if n == 1 else jnp.concatenate(parts, axis=1)


def _band_attn_kernel(q_ref, k_ref, vt_ref, bias_ref, o_ref, acc_ref, s_ref, *, tq, nblk):
    i = pl.program_id(2)
    qs = _stack_heads(q_ref[...], HEAD_DIM)
    lead = i - (nblk - 1)

    def scores(kb):
        ks = pl.multiple_of(kb * tq, tq)
        return _dot(k_ref[pl.ds(ks, tq), :], qs) + bias_ref[kb - lead]

    _, l = _sweep(jnp.maximum(lead, 0), jnp.minimum(i + 1, nblk), s_ref, scores,
                  lambda kb: _load_vt(vt_ref, kb, tq), _flash_init(acc_ref), acc_ref)
    o_ref[...] = _flash_out(l, acc_ref, tq)


def _band_attn(q, k, vt, bias, tq=256):
    B, _, S, _ = q.shape
    nblk = bias.shape[1]
    return pl.pallas_call(
        functools.partial(_band_attn_kernel, tq=tq, nblk=nblk),
        grid=(B, N_PAIRS, S // tq),
        in_specs=[
            pl.BlockSpec((None, None, tq, LANES), lambda b, p, i: (b, p, i, 0)),
            pl.BlockSpec((None, None, S, LANES), lambda b, p, i: (b, p, 0, 0)),
            pl.BlockSpec((None, None, S // VT_BLK, LANES, VT_BLK), lambda b, p, i: (b, p, 0, 0, 0)),
            pl.BlockSpec((None, nblk, tq, 2 * tq), lambda b, p, i: (p, 0, 0, 0)),
        ],
        out_specs=pl.BlockSpec((None, tq, LANES), lambda b, p, i: (b, i, p)),
        out_shape=jax.ShapeDtypeStruct((B, S, N_HEADS * HEAD_DIM), BF16),
        scratch_shapes=[pltpu.VMEM((LANES, 2 * tq), F32), pltpu.VMEM((2, tq, 2 * tq), F32)],
        compiler_params=_cparams(("parallel", "parallel", "arbitrary")),
        name="band_attn",
    )(q, k, vt, bias)


def _mla_attn_kernel(q_ref, k_ref, vt_ref, o_ref, acc_ref, s_ref, *, tq, tk):
    i = pl.program_id(2)
    qs = _stack_heads(q_ref[...], LANES)
    t0 = i * tq
    assert tq % (2 * tk) == 0
    n_full = t0 // tk
    n_edge = tq // tk
    lim = ((t0 + lax.broadcasted_iota(jnp.int32, (1, tq), 1)) // CHUNK + 1) * CHUNK

    def scores(kb):
        ks = pl.multiple_of(kb * tk, tk)
        return _dot(k_ref[pl.ds(ks, tk), :], qs)

    def edge_scores(kb):
        row = kb * tk + lax.broadcasted_iota(jnp.int32, (tk, tq), 0)
        add = jnp.where(row < lim, 0.0, NEG).astype(F32)
        return scores(kb) + jnp.concatenate([add, add], axis=1)

    values = lambda kb: _load_vt(vt_ref, kb, tk)
    ml = _sweep(0, n_full, s_ref, scores, values, _flash_init(acc_ref), acc_ref, peel=False)
    _, l = _sweep(n_full, n_edge, s_ref, edge_scores, values, ml, acc_ref, peel=False)
    o_ref[...] = _flash_out(l, acc_ref, tq)


def _mla_attn(q, k, vt, tq=512, tk=256):
    B, _, S, _ = q.shape
    tk = min(tk, S)
    return pl.pallas_call(
        functools.partial(_mla_attn_kernel, tq=tq, tk=tk),
        grid=(B, N_PAIRS, S // tq),
        in_specs=[
            pl.BlockSpec((None, None, tq, 256), lambda b, p, i: (b, p, i, 0)),
            pl.BlockSpec((None, None, S, 256), lambda b, p, i: (b, p, 0, 0)),
            pl.BlockSpec((None, None, S // VT_BLK, LANES, VT_BLK), lambda b, p, i: (b, p, 0, 0, 0)),
        ],
        out_specs=pl.BlockSpec((None, tq, LANES), lambda b, p, i: (b, i, p)),
        out_shape=jax.ShapeDtypeStruct((B, S, N_HEADS * V_DIM), BF16),
        scratch_shapes=[pltpu.VMEM((LANES, 2 * tq), F32), pltpu.VMEM((2, tk, 2 * tq), F32)],
        compiler_params=_cparams(("parallel", "parallel", "arbitrary")),
        name="mla_attn",
    )(q, k, vt)


def _dsa_attn_kernel(q_ref, qp_ref, wi_ref, bias_ref, k_ref, vt_ref, kp_ref, o_ref,
                     sc_ref, h_ref, acc_ref, stage_ref, s_ref, *, tq, tk, topk):
    i = pl.program_id(1)
    t0 = i * tq
    n_blk = (t0 + tq + tk - 1) // tk
    n_far = jnp.maximum((t0 // LANES - 1) // (tk // LANES), 0)
    n_far_even = n_far - n_far % 2
    lim = ((t0 + lax.broadcasted_iota(jnp.int32, (1, tq), 1)) // CHUNK + 1) * CHUNK

    wv = wi_ref[...]

    def score_body(kb, carry):
        ks = pl.multiple_of(kb * tk, tk)
        kp = kp_ref[pl.ds(ks, tk), :]
        acc = jnp.zeros((tk, tq), F32)
        for h in range(H_IDX):
            acc = acc + jnp.maximum(_dot(kp, qp_ref[h]), 0.0) * wv[h:h + 1, :]
        row = ks + lax.broadcasted_iota(jnp.int32, (tk, tq), 0)
        acc = jnp.where(row < lim, acc, -jnp.inf)
        bits = pltpu.bitcast(acc, jnp.int32)
        key = bits ^ ((bits >> 31) & 0x7FFFFFFF)
        sc_ref[kb] = key
        h_ref[kb] = (key >> 16).astype(jnp.int16)
        return carry

    lax.fori_loop(0, n_blk, score_body, 0)

    def count16(pred):
        def count_body(kb, part):
            hit = jnp.where(pred(h_ref[kb]), jnp.int16(1), jnp.int16(0))
            rows = hit[0:16]
            for r in range(16, tk, 16):
                rows = rows + hit[r:r + 16]
            return part + rows.astype(jnp.int32)

        part = lax.fori_loop(0, n_blk, count_body, jnp.zeros((16, tq), jnp.int32))
        return jnp.sum(part, axis=0, keepdims=True)

    def kth16(need):
        def bit_body(t, ans):
            cand = ans + jnp.left_shift(jnp.int32(1), 15 - t)
            cand16 = cand.astype(jnp.int16)
            return jnp.where(count16(lambda h: h >= cand16) >= need, cand, ans)

        return lax.fori_loop(0, 16, bit_body, jnp.full((1, tq), -(2 ** 15), jnp.int32))

    top = kth16(topk)
    top16 = top.astype(jnp.int16)
    need = topk - count16(lambda h: h > top16)

    def low_body(kb, carry):
        low = ((sc_ref[kb] & 0xFFFF) - 2 ** 15).astype(jnp.int16)
        h_ref[kb] = jnp.where(h_ref[kb] == top16, low, jnp.int16(-(2 ** 15)))
        return carry

    lax.fori_loop(0, n_blk, low_body, 0)
    thr = jnp.left_shift(top, 16) + (kth16(need) + 2 ** 15)

    def mask_body(kb, carry):
        row = kb * tk + lax.broadcasted_iota(jnp.int32, (tk, tq), 0)
        sc_ref[kb] = jnp.where((sc_ref[kb] >= thr) & (row < lim), 0, NEG_BITS)
        return carry

    lax.fori_loop(0, n_blk, mask_body, 0)

    def pair_body(p, carry):
        qs = _stack_heads(q_ref[p], HEAD_DIM)

        def far_scores(kb):
            ks = pl.multiple_of(kb * tk, tk)
            am = pltpu.bitcast(sc_ref[kb], F32)
            return _dot(k_ref[p, pl.ds(ks, tk), :], qs) + jnp.concatenate([am, am], axis=1)

        def near_scores(kb):
            cols = []
            for hh in range(2):
                for a in range(tq // LANES):
                    tiles = []
                    for c in range(tk // LANES):
                        d = (kb * (tk // LANES) + c) - (i * (tq // LANES) + a)
                        kind = jnp.where(d == 0, 2, jnp.where(d == -1, 1, 0))
                        tiles.append(bias_ref[p, hh, kind])
                    cols.append(jnp.concatenate(tiles, axis=0))
            return far_scores(kb) + jnp.concatenate(cols, axis=1)

        values = lambda kb: _load_vt(vt_ref.at[p], kb, tk)
        ml = _sweep(0, n_far_even, s_ref, far_scores, values, _flash_init(acc_ref), acc_ref, peel=False)
        _, l = _sweep(n_far_even, n_blk - n_far_even, s_ref, near_scores, values, ml, acc_ref)
        stage_ref[p] = _flash_out(l, acc_ref, tq)
        return carry

    lax.fori_loop(0, N_PAIRS, pair_body, 0)
    for p in range(N_PAIRS):
        o_ref[:, p * LANES:(p + 1) * LANES] = stage_ref[p]


def _dsa_attn(q, k, vt, qp, kp, wi, bias, tq=256, tk=512):
    B, _, S, _ = q.shape
    topk = min(TOPK_MAX, S // 4)
    once = pl.Buffered(1)
    return pl.pallas_call(
        functools.partial(_dsa_attn_kernel, tq=tq, tk=tk, topk=topk),
        grid=(B, S // tq),
        in_specs=[
            pl.BlockSpec((None, N_PAIRS, tq, LANES), lambda b, i: (b, 0, i, 0)),
            pl.BlockSpec((None, H_IDX, 256, tq), lambda b, i: (b, 0, 0, i)),
            pl.BlockSpec((None, H_IDX, tq), lambda b, i: (b, 0, i)),
            pl.BlockSpec((N_PAIRS, 2, 3, LANES, LANES), lambda b, i: (0, 0, 0, 0, 0), pipeline_mode=once),
            pl.BlockSpec((None, N_PAIRS, S, LANES), lambda b, i: (b, 0, 0, 0), pipeline_mode=once),
            pl.BlockSpec((None, N_PAIRS, S // VT_BLK, LANES, VT_BLK), lambda b, i: (b, 0, 0, 0, 0),
                         pipeline_mode=once),
            pl.BlockSpec((None, S, 256), lambda b, i: (b, 0, 0), pipeline_mode=once),
        ],
        out_specs=pl.BlockSpec((None, tq, N_HEADS * HEAD_DIM), lambda b, i: (b, i, 0)),
        out_shape=jax.ShapeDtypeStruct((B, S, N_HEADS * HEAD_DIM), BF16),
        scratch_shapes=[
            pltpu.VMEM((S // tk, tk, tq), jnp.int32),
            pltpu.VMEM((S // tk, tk, tq), jnp.int16),
            pltpu.VMEM((LANES, 2 * tq), F32),
            pltpu.VMEM((N_PAIRS, tq, LANES), BF16),
            pltpu.VMEM((2, tk, 2 * tq), F32),
        ],
        compiler_params=_cparams(("parallel", "arbitrary"), VMEM_LIMIT_DSA),
        name="dsa_attn",
    )(q, qp, wi, bias, k, vt, kp)


def _out_proj_kernel(x_ref, a_ref, w_ref, o_ref):
    o_ref[...] = x_ref[...] + _dot(a_ref[...], w_ref[...])


def _out_proj(x, a, w, tm=512):
    B, S, D = x.shape
    return pl.pallas_call(
        _out_proj_kernel,
        grid=(B, S // tm),
        in_specs=[
            pl.BlockSpec((None, tm, D), lambda b, i: (b, i, 0)),
            pl.BlockSpec((None, tm, a.shape[-1]), lambda b, i: (b, i, 0)),
            pl.BlockSpec(w.shape, lambda b, i: (0, 0)),
        ],
        out_specs=pl.BlockSpec((None, tm, D), lambda b, i: (b, i, 0)),
        out_shape=jax.ShapeDtypeStruct((B, S, D), F32),
        compiler_params=_cparams(("parallel", "parallel")),
        name="out_proj",
    )(x, a, w.astype(BF16))


def _ffn_kernel(x_ref, xp_ref, g_ref, wg_ref, wv_ref, cw_ref, cb_ref, wd_ref, o_ref, acc_ref, he_ref, u_ref,
                *, tm, n_chunks):
    i = pl.program_id(1)
    g = g_ref[...]
    he_ref[:HALO] = (_rms(xp_ref[...], g) * jnp.where(i > 0, 1.0, 0.0)).astype(BF16)
    he_ref[HALO:] = _rms(x_ref[...], g).astype(BF16)
    acc_ref[...] = jnp.zeros(acc_ref.shape, F32)

    def up(c, slot):
        he = he_ref[...]
        u_ref[slot, 0] = _dot(he, wg_ref[c])
        u_ref[slot, 1] = _dot(he, wv_ref[c])

    def conv(u, w, b):
        return (b + w[0:1] * u[HALO - 2:HALO - 2 + tm] + w[1:2] * u[HALO - 1:HALO - 1 + tm]
                + w[2:3] * u[HALO:HALO + tm])

    def down(c, slot):
        cw = cw_ref[c]
        cb = cb_ref[c]
        gate = conv(u_ref[slot, 0], cw[0], cb[0])
        val = conv(u_ref[slot, 1], cw[1], cb[1])
        act = gate * jax.nn.sigmoid(gate) * val
        acc_ref[...] += _dot(act.astype(BF16), wd_ref[c])

    up(0, 0)

    def pair_body(t, carry):
        c = 2 * t
        up(c + 1, 1)
        down(c, 0)
        up(c + 2, 0)
        down(c + 1, 1)
        return carry

    assert n_chunks % 2 == 1
    lax.fori_loop(0, n_chunks // 2, pair_body, 0)
    down(n_chunks - 1, 0)
    o_ref[...] = x_ref[...] + acc_ref[...]


def _ffn(x, g, w_up, conv_w, conv_b, w_down, tm=512):
    B, S, D = x.shape
    nc = D_FF // FF_CHUNK
    wup = w_up.astype(BF16).reshape(D, 2, nc, FF_CHUNK).transpose(1, 2, 0, 3)
    cw = conv_w.reshape(CONV_W, 2, nc, FF_CHUNK).transpose(2, 1, 0, 3)
    cb = conv_b.reshape(2, nc, 1, FF_CHUNK).transpose(1, 0, 2, 3)
    wd = w_down.astype(BF16).reshape(nc, FF_CHUNK, D)
    c3 = lambda b, i: (0, 0, 0)
    c4 = lambda b, i: (0, 0, 0, 0)
    return pl.pallas_call(
        functools.partial(_ffn_kernel, tm=tm, n_chunks=nc),
        grid=(B, S // tm),
        in_specs=[
            pl.BlockSpec((None, tm, D), lambda b, i: (b, i, 0)),
            pl.BlockSpec((None, HALO, D), lambda b, i: (b, jnp.maximum(i * (tm // HALO) - 1, 0), 0)),
            pl.BlockSpec((1, D), lambda b, i: (0, 0)),
            pl.BlockSpec((nc, D, FF_CHUNK), c3),
            pl.BlockSpec((nc, D, FF_CHUNK), c3),
            pl.BlockSpec((nc, 2, CONV_W, FF_CHUNK), c4),
            pl.BlockSpec((nc, 2, 1, FF_CHUNK), c4),
            pl.BlockSpec((nc, FF_CHUNK, D), c3),
        ],
        out_specs=pl.BlockSpec((None, tm, D), lambda b, i: (b, i, 0)),
        out_shape=jax.ShapeDtypeStruct((B, S, D), F32),
        scratch_shapes=[pltpu.VMEM((tm, D), F32), pltpu.VMEM((HALO + tm, D), BF16),
                        pltpu.VMEM((2, 2, HALO + tm, FF_CHUNK), F32)],
        compiler_params=_cparams(("parallel", "parallel")),
        name="conv_ffn",
    )(x, x, g.reshape(1, D), wup[0], wup[1], cw, cb, wd)


def _rope_tables(S, period, off):
    inv = 1.0 / (ROPE_THETA ** (jnp.arange(0, ROPE_DIM, 2, dtype=F32) / ROPE_DIM))
    ang = jnp.arange(S, dtype=F32)[:, None] * inv[None, :]
    cos, sin = jnp.cos(ang), jnp.sin(ang)
    half = ROPE_DIM // 2
    lane = np.arange(LANES) % period
    is1 = (lane >= off) & (lane < off + half)
    is2 = (lane >= off + half) & (lane < off + ROPE_DIM)
    idx = np.where(is1, lane - off, np.where(is2, lane - off - half, 0))
    ct = jnp.where(is1 | is2, cos[:, idx], 1.0)
    s1 = jnp.where(is1, -sin[:, idx], 0.0)
    s2 = jnp.where(is2, sin[:, idx], 0.0)
    return ct, s1, s2


def _t5_bucket(rel):
    nb = T5_BUCKETS // 2
    max_exact = nb // 2
    n = jnp.abs(rel)
    large = max_exact + (jnp.log(jnp.maximum(n, 1).astype(F32) / max_exact)
                         / math.log(T5_MAX_DIST / max_exact) * (nb - max_exact)).astype(jnp.int32)
    large = jnp.minimum(large, nb - 1)
    return jnp.where(rel > 0, nb, 0) + jnp.where(n < max_exact, n, large)


def _dsa_bias_tiles(t5_bias):
    kj = jnp.arange(LANES, dtype=jnp.int32)[:, None]
    qi = jnp.arange(LANES, dtype=jnp.int32)[None, :]
    far = t5_bias[_t5_bucket(jnp.int32(-4 * T5_MAX_DIST))]
    same = (jnp.transpose(t5_bias[_t5_bucket(kj - qi)], (2, 0, 1)) - far[:, None, None]) * LOG2E
    prev = (jnp.transpose(t5_bias[_t5_bucket(kj - LANES - qi)], (2, 0, 1)) - far[:, None, None]) * LOG2E
    return jnp.stack([jnp.zeros_like(same), prev, same], axis=1).reshape(N_PAIRS, 2, 3, LANES, LANES)


def _band_bias(rel_bias, tq):
    pad = LEFT_CHUNKS * CHUNK
    nblk = -(-pad // tq) + 1
    lead = (nblk - 1) * tq
    nk = nblk * tq
    kk = jnp.arange(nk, dtype=jnp.int32)[:, None] - lead
    i = jnp.arange(tq, dtype=jnp.int32)[None, :]
    first = (i // CHUNK) * CHUNK - pad
    ok = (kk >= first) & (kk < (i // CHUNK + 1) * CHUNK)
    L = nk + tq
    dist = jnp.arange(L, dtype=jnp.int32)
    dist = lead + jnp.where(dist < tq, dist, dist - L)
    v = rel_bias[:, jnp.clip(dist, -(CHUNK - 1), REL_MAX_PAST) + CHUNK - 1]
    b = jnp.tile(v, (1, nk))[:, :nk * (L - 1)].reshape(v.shape[0], nk, L - 1)[:, :, :tq]
    b = jnp.where(ok[None], b * LOG2E, NEG).reshape(N_PAIRS, 2, nblk, tq, tq)
    return jnp.transpose(b, (0, 2, 3, 1, 4)).reshape(N_PAIRS, nblk, tq, 2 * tq)


def kernel(x, norm_mix, norm_ffn, t5_bias, a_w_in, a_q_norm, a_k_norm, a_kidx_norm, a_w_out, b_w_down, b_q_a_norm, b_kv_a_norm, b_w_uq, b_w_ukv, b_q_norm, b_k_norm, b_w_out, c_w_in, c_q_norm, c_k_norm, c_rel_bias, c_w_out, f_w_up, f_conv_w, f_conv_b, f_w_down):
    B, S, D = x.shape
    depth = norm_mix.shape[0]

    tq_c = 256
    tabs_idx = _rope_tables(S, D_IDX, D_IDX - ROPE_DIM)
    tabs_mla = _rope_tables(S, LANES, QK_NOPE)
    scale = HEAD_DIM ** -0.5 * LOG2E
    for layer in range(depth):
        kind, j = layer % N_MIXERS, layer // N_MIXERS
        g = norm_mix[layer]
        if kind == 0:
            q, k, vt = _qkv_proj(x, g, a_w_in[j][:, :A_QKV], a_q_norm[j] * scale, a_k_norm[j])
            qp, kp, wi = _idx_proj(x, g, a_w_in[j][:, A_QKV:], a_kidx_norm[j], tabs_idx)
            att = _dsa_attn(q, k, vt, qp, kp, wi, _dsa_bias_tiles(t5_bias))
            w_out = a_w_out[j]
        elif kind == 1:
            q, k, vt = _mla_proj(x, g, b_w_down[j], b_q_a_norm[j], b_kv_a_norm[j], b_w_uq[j], b_w_ukv[j],
                                 b_q_norm[j], b_k_norm[j], tabs_mla)
            att = _mla_attn(q, k, vt)
            w_out = b_w_out[j]
        else:
            q, k, vt = _qkv_proj(x, g, c_w_in[j], c_q_norm[j] * scale, c_k_norm[j])
            att = _band_attn(q, k, vt, _band_bias(c_rel_bias[j], tq_c), tq=tq_c)
            w_out = c_w_out[j]
        x = _out_proj(x, att, w_out)
        x = _ffn(x, norm_ffn[layer], f_w_up[layer], f_conv_w[layer], f_conv_b[layer], f_w_down[layer])
    return x
```

```python
import functools
import math

import numpy as np
import jax
import jax.numpy as jnp
from jax import lax
from jax.experimental import pallas as pl
from jax.experimental.pallas import tpu as pltpu

F32 = jnp.float32
BF16 = jnp.bfloat16

D_MODEL = 1024
CHUNK = 64
EPS = 1e-6
ROPE_THETA = 10000.0
NEG = -1e30
NEG_BITS = int(np.float32(NEG).view(np.int32))
M_INIT = -1e38
INT_MIN = -(2 ** 31)
LOG2E = math.log2(math.e)
SAFE_SPREAD = 100.0

N_MIXERS = 3
N_HEADS = 16
HEAD_DIM = 64
N_PAIRS = N_HEADS // 2
LANES = 128
VT_BLK = 256

H_IDX = 8
D_IDX = 64
ROPE_DIM = 32
TOPK_MAX = 256
T5_BUCKETS = 32
T5_MAX_DIST = 128
A_QKV = 3 * N_HEADS * HEAD_DIM
Q_LORA = 256
KV_LORA = 128
QK_NOPE = 64
QK_ROPE = 32
V_DIM = 64
B_QK = QK_NOPE + QK_ROPE
LEFT_CHUNKS = 8
REL_MAX_PAST = 128
D_FF = 2816
CONV_W = 3
FF_CHUNK = 256
HALO = 8

VMEM_LIMIT = 56 * 1024 * 1024
VMEM_LIMIT_DSA = 60 * 1024 * 1024


def _cparams(sem, vmem_limit=VMEM_LIMIT):
    return pltpu.CompilerParams(dimension_semantics=sem, vmem_limit_bytes=vmem_limit)


def _rms(x, g):
    return x * lax.rsqrt(jnp.mean(x * x, axis=-1, keepdims=True) + EPS) * g


def _dot(a, b):
    return jnp.dot(a, b, preferred_element_type=F32)


def _split_bf16(x):
    hi = x.astype(BF16)
    lo = (x - hi.astype(F32)).astype(BF16)
    return hi, lo


def _group_sumsq(y, bd):
    hi, lo = _split_bf16(y * y)
    return _dot(hi, bd) + _dot(lo, bd)


def _rope(y, ct, s1, s2):
    return y * ct + pltpu.roll(y, LANES - 16, 1) * s1 + pltpu.roll(y, 16, 1) * s2


def _rope_wide(y, ct, s1, s2):
    w = y.shape[-1]
    parts = [_rope(y[:, c:c + LANES], ct, s1, s2) for c in range(0, w, LANES)]
    return parts[0] if len(parts) == 1 else jnp.concatenate(parts, axis=-1)


def _store_vt(vt_ref, p, y):
    yt = y.T.astype(BF16)
    for r in range(y.shape[0] // VT_BLK):
        vt_ref[p, r] = yt[:, r * VT_BLK:(r + 1) * VT_BLK]


def _qkv_proj_kernel(x_ref, g_ref, w_ref, gq_ref, gk_ref, bd_ref, q_ref, k_ref, vt_ref):
    xb = _rms(x_ref[...], g_ref[...]).astype(BF16)
    bd = bd_ref[...]
    hd = N_HEADS * HEAD_DIM
    for which, (o_ref, gain_ref) in enumerate(((q_ref, gq_ref), (k_ref, gk_ref), (vt_ref, None))):
        for c in range(hd // 256):
            col = which * hd + c * 256
            y = _dot(xb, w_ref[:, col:col + 256])
            if gain_ref is None:
                _store_vt(o_ref, 2 * c, y[:, :LANES])
                _store_vt(o_ref, 2 * c + 1, y[:, LANES:])
            else:
                y = y * lax.rsqrt(_group_sumsq(y, bd) * (1.0 / HEAD_DIM) + EPS) * gain_ref[...]
                yb = y.astype(BF16)
                o_ref[2 * c] = yb[:, :LANES]
                o_ref[2 * c + 1] = yb[:, LANES:]


def _vt_spec(tm):
    return pl.BlockSpec((None, N_PAIRS, tm // VT_BLK, LANES, VT_BLK), lambda b, i: (b, 0, i, 0, 0))


def _vt_shape(B, S):
    return jax.ShapeDtypeStruct((B, N_PAIRS, S // VT_BLK, LANES, VT_BLK), BF16)


def _qkv_proj(x, g, w, gq, gk, tm=512):
    B, S, D = x.shape
    hd = N_HEADS * HEAD_DIM
    bd = jnp.kron(jnp.eye(256 // HEAD_DIM, dtype=F32), jnp.ones((HEAD_DIM, HEAD_DIM), F32)).astype(BF16)
    out = jax.ShapeDtypeStruct((B, N_PAIRS, S, LANES), BF16)
    const = lambda b, i: (0, 0)
    ospec = pl.BlockSpec((None, N_PAIRS, tm, LANES), lambda b, i: (b, 0, i, 0))
    return pl.pallas_call(
        _qkv_proj_kernel,
        grid=(B, S // tm),
        in_specs=[
            pl.BlockSpec((None, tm, D), lambda b, i: (b, i, 0)),
            pl.BlockSpec((1, D), const),
            pl.BlockSpec((D, 3 * hd), const),
            pl.BlockSpec((1, 256), const),
            pl.BlockSpec((1, 256), const),
            pl.BlockSpec((256, 256), const),
        ],
        out_specs=[ospec, ospec, _vt_spec(tm)],
        out_shape=[out, out, _vt_shape(B, S)],
        compiler_params=_cparams(("parallel", "parallel")),
        name="qkv_proj",
    )(x, g.reshape(1, D), w.astype(BF16), jnp.tile(gq, 4).reshape(1, 256), jnp.tile(gk, 4).reshape(1, 256), bd)


def _idx_proj_kernel(x_ref, g_ref, wh_ref, wl_ref, gki_ref, ct_ref, s1_ref, s2_ref, qp_ref, kp_ref, wi_ref,
                     *, w_scale):
    xn = _rms(x_ref[...], g_ref[...])
    xh, xl = _split_bf16(xn)

    def mm(c0, c1):
        wh = wh_ref[:, c0:c1]
        return _dot(xh, wh) + (_dot(xl, wh) + _dot(xh, wl_ref[:, c0:c1]))

    ct, s1, s2 = ct_ref[...], s1_ref[...], s2_ref[...]
    tm = xn.shape[0]
    low = lax.broadcasted_iota(jnp.int32, (tm, LANES), 1) < D_IDX
    zero = jnp.zeros((tm, LANES), F32)

    for c in range(H_IDX // 2):
        y = _rope(mm(c * LANES, (c + 1) * LANES), ct, s1, s2)
        hi = y.astype(BF16).astype(F32)
        lo = y - hi
        rhi = pltpu.roll(hi, D_IDX, 1)
        rlo = pltpu.roll(lo, D_IDX, 1)
        qp_ref[2 * c, :LANES, :] = jnp.where(low, hi, rlo).T.astype(BF16)
        qp_ref[2 * c, LANES:, :] = jnp.where(low, hi, zero).T.astype(BF16)
        qp_ref[2 * c + 1, :LANES, :] = jnp.where(low, rhi, lo).T.astype(BF16)
        qp_ref[2 * c + 1, LANES:, :] = jnp.where(low, rhi, zero).T.astype(BF16)

    c0 = H_IDX * D_IDX
    y = mm(c0, c0 + LANES)
    y = y * lax.rsqrt(jnp.sum(y * y, axis=-1, keepdims=True) * (1.0 / D_IDX) + EPS) * gki_ref[...]
    y = _rope(y, ct, s1, s2)
    hi = y.astype(BF16).astype(F32)
    lo = y - hi
    kp_ref[:, :LANES] = jnp.where(low, hi, pltpu.roll(hi, D_IDX, 1)).astype(BF16)
    kp_ref[:, LANES:] = jnp.where(low, lo, zero).astype(BF16)

    wi_ref[...] = (mm(c0 + LANES, c0 + 2 * LANES) * w_scale).T[:H_IDX]


def _idx_proj(x, g, w_idx, gki, tabs, tm=512):
    B, S, D = x.shape
    nq = H_IDX * D_IDX
    wq, wk, ww = w_idx[:, :nq], w_idx[:, nq:nq + D_IDX], w_idx[:, nq + D_IDX:]
    wpad = jnp.concatenate([
        wq, jnp.pad(wk, ((0, 0), (0, LANES - D_IDX))), jnp.pad(ww, ((0, 0), (0, LANES - H_IDX)))], axis=1)
    wh = wpad.astype(BF16)
    wl = (wpad - wh.astype(F32)).astype(BF16)
    ncol = nq + 2 * LANES
    gpad = jnp.pad(gki, (0, LANES - D_IDX)).reshape(1, LANES)
    const = lambda b, i: (0, 0)
    tspec = pl.BlockSpec((tm, LANES), lambda b, i: (i, 0))
    return pl.pallas_call(
        functools.partial(_idx_proj_kernel, w_scale=(D_IDX ** -0.5) * (H_IDX ** -0.5)),
        grid=(B, S // tm),
        in_specs=[
            pl.BlockSpec((None, tm, D), lambda b, i: (b, i, 0)),
            pl.BlockSpec((1, D), const),
            pl.BlockSpec((D, ncol), const),
            pl.BlockSpec((D, ncol), const),
            pl.BlockSpec((1, LANES), const),
            tspec, tspec, tspec,
        ],
        out_specs=[
            pl.BlockSpec((None, H_IDX, 256, tm), lambda b, i: (b, 0, 0, i)),
            pl.BlockSpec((None, tm, 256), lambda b, i: (b, i, 0)),
            pl.BlockSpec((None, H_IDX, tm), lambda b, i: (b, 0, i)),
        ],
        out_shape=[
            jax.ShapeDtypeStruct((B, H_IDX, 256, S), BF16),
            jax.ShapeDtypeStruct((B, S, 256), BF16),
            jax.ShapeDtypeStruct((B, H_IDX, S), F32),
        ],
        compiler_params=_cparams(("parallel", "parallel")),
        name="idx_proj",
    )(x, g.reshape(1, D), wh, wl, gpad, *tabs)


def _mla_proj_kernel(x_ref, g_ref, wd_ref, gqa_ref, gkva_ref, wuq_ref, wuk_ref, wuv_ref, gq_ref, gk_ref, bd_ref,
                     qfix_ref, kfix_ref, ct_ref, s1_ref, s2_ref, q_ref, k_ref, vt_ref):
    xb = _rms(x_ref[...], g_ref[...]).astype(BF16)
    d = _dot(xb, wd_ref[...])
    cq = _rms(d[:, :Q_LORA], gqa_ref[...]).astype(BF16)
    ckv = _rms(d[:, Q_LORA:Q_LORA + KV_LORA], gkva_ref[...]).astype(BF16)
    krz = pltpu.roll(d[:, Q_LORA + KV_LORA:], QK_NOPE, 1)
    krz2 = jnp.concatenate([krz, krz], axis=-1)
    bd = bd_ref[...]
    ct, s1, s2 = ct_ref[...], s1_ref[...], s2_ref[...]
    inv = 1.0 / B_QK
    for p in range(N_PAIRS):
        cols = slice(p * 256, (p + 1) * 256)
        y = _dot(cq, wuq_ref[:, cols])
        y = y * lax.rsqrt(_group_sumsq(y, bd) * inv + EPS) * gq_ref[...]
        q_ref[p] = (_rope_wide(y, ct, s1, s2) + qfix_ref[...]).astype(BF16)
        y = _dot(ckv, wuk_ref[:, cols]) + krz2
        y = y * lax.rsqrt(_group_sumsq(y, bd) * inv + EPS) * gk_ref[...]
        k_ref[p] = (_rope_wide(y, ct, s1, s2) + kfix_ref[...]).astype(BF16)
        _store_vt(vt_ref, p, _dot(ckv, wuv_ref[:, p * LANES:(p + 1) * LANES]))


def _mla_proj(x, g, w_down, gqa, gkva, w_uq, w_ukv, gq, gk, shift, tabs, tm=512):
    B, S, D = x.shape
    nd = Q_LORA + KV_LORA + QK_ROPE
    wd = jnp.pad(w_down, ((0, 0), (0, 512 - nd))).astype(BF16)
    padh = LANES - B_QK
    wuq = jnp.pad(w_uq.reshape(Q_LORA, N_HEADS, B_QK), ((0, 0), (0, 0), (0, padh))).reshape(Q_LORA, N_HEADS * LANES)
    wukv = w_ukv.reshape(KV_LORA, N_HEADS, QK_NOPE + V_DIM)
    wuk = jnp.pad(wukv[:, :, :QK_NOPE], ((0, 0), (0, 0), (0, LANES - QK_NOPE))).reshape(KV_LORA, N_HEADS * LANES)
    wuv = wukv[:, :, QK_NOPE:].reshape(KV_LORA, N_HEADS * V_DIM)
    gq2 = jnp.tile(jnp.pad(gq * (B_QK ** -0.5 * LOG2E), (0, padh)), 2).reshape(1, 256)
    gk2 = jnp.tile(jnp.pad(gk, (0, padh)), 2).reshape(1, 256)
    bd = jnp.kron(jnp.eye(2, dtype=F32), jnp.ones((LANES, LANES), F32)).astype(BF16)
    kfix = jnp.asarray((np.arange(256) % LANES == B_QK).astype(np.float32)).reshape(1, 256)
    qfix = kfix * shift
    const = lambda b, i: (0, 0)
    tspec = pl.BlockSpec((tm, LANES), lambda b, i: (i, 0))
    qk_shape = jax.ShapeDtypeStruct((B, N_PAIRS, S, 256), BF16)
    qk_spec = pl.BlockSpec((None, N_PAIRS, tm, 256), lambda b, i: (b, 0, i, 0))
    return pl.pallas_call(
        _mla_proj_kernel,
        grid=(B, S // tm),
        in_specs=[
            pl.BlockSpec((None, tm, D), lambda b, i: (b, i, 0)),
            pl.BlockSpec((1, D), const),
            pl.BlockSpec((D, 512), const),
            pl.BlockSpec((1, Q_LORA), const),
            pl.BlockSpec((1, KV_LORA), const),
            pl.BlockSpec((Q_LORA, N_HEADS * LANES), const),
            pl.BlockSpec((KV_LORA, N_HEADS * LANES), const),
            pl.BlockSpec((KV_LORA, N_HEADS * V_DIM), const),
            pl.BlockSpec((1, 256), const),
            pl.BlockSpec((1, 256), const),
            pl.BlockSpec((256, 256), const),
            pl.BlockSpec((1, 256), const),
            pl.BlockSpec((1, 256), const),
            tspec, tspec, tspec,
        ],
        out_specs=[qk_spec, qk_spec, _vt_spec(tm)],
        out_shape=[qk_shape, qk_shape, _vt_shape(B, S)],
        compiler_params=_cparams(("parallel", "parallel")),
        name="mla_proj",
    )(x, g.reshape(1, D), wd, gqa.reshape(1, Q_LORA), gkva.reshape(1, KV_LORA), wuq.astype(BF16),
      wuk.astype(BF16), wuv.astype(BF16), gq2, gk2, bd, qfix, kfix, *tabs)


def _stack_heads(q2, half):
    qf = q2.astype(F32)
    low = lax.broadcasted_iota(jnp.int32, qf.shape, 1) < half
    zero = jnp.zeros_like(qf)
    return jnp.concatenate([jnp.where(low, qf, zero).T, jnp.where(low, zero, qf).T], axis=1).astype(BF16)


def _flash_init(acc_ref):
    r = acc_ref.shape[1]
    acc_ref[...] = jnp.zeros(acc_ref.shape, F32)
    return jnp.full((1, r), M_INIT, F32), jnp.zeros((1, r), F32)


def _consume(s, vt, m, l, acc_ref, fixed):
    if fixed:
        p = jnp.exp2(s)
        acc_ref[...] += _dot(vt, p.astype(BF16))
        return m, l + jnp.sum(p, axis=0, keepdims=True)
    m_new = jnp.maximum(m, jnp.max(s, axis=0, keepdims=True))
    alpha = jnp.exp2(m - m_new)
    p = jnp.exp2(s - m_new)
    l_new = alpha * l + jnp.sum(p, axis=0, keepdims=True)
    acc_ref[...] = alpha * acc_ref[...] + _dot(vt, p.astype(BF16))
    return m_new, l_new


def _sweep(first, n, s_ref, scores, values, ml, acc_ref, fixed, peel=True):
    def one(j, s, ml):
        return _consume(s, values(j), *ml, acc_ref, fixed)

    if isinstance(n, int):
        tiles = [scores(first + j) for j in range(n)]
        for j in range(n):
            ml = one(first + j, tiles[j], ml)
        return ml

    odd = n % 2 if peel else 0
    if peel:
        ml = lax.cond(odd == 1, lambda ml: one(first, scores(first), ml), lambda ml: ml, ml)
    j0 = first + odd
    last = first + n - 1

    @pl.when(n >= 2)
    def _():
        s_ref[0] = scores(j0)

    def body(t, ml):
        j = j0 + 2 * t
        s1 = scores(j + 1)
        ml = one(j, s_ref[0], ml)
        s_ref[1] = s1
        s0 = scores(jnp.minimum(j + 2, last))
        ml = one(j + 1, s_ref[1], ml)
        s_ref[0] = s0
        return ml

    return lax.fori_loop(0, n // 2, body, ml)


def _flash_out(l, acc_ref, tq):
    o = acc_ref[...] / l
    ot = jnp.concatenate([o[:HEAD_DIM, :tq], o[HEAD_DIM:, tq:]], axis=0)
    return ot.T.astype(BF16)


def _load_vt(vt_ref, kb, tk):
    n = tk // VT_BLK
    parts = [vt_ref[kb * n + r] for r in range(n)]
    return parts[0] if n == 1 else jnp.concatenate(parts, axis=1)


def _band_attn_kernel(q_ref, k_ref, vt_ref, bias_ref, o_ref, acc_ref, s_ref, *, tq, nblk, fixed):
    i = pl.program_id(2)
    qs = _stack_heads(q_ref[...], HEAD_DIM)
    lead = i - (nblk - 1)

    def scores(kb):
        ks = pl.multiple_of(kb * tq, tq)
        return _dot(k_ref[pl.ds(ks, tq), :], qs) + bias_ref[kb - lead]

    _, l = _sweep(jnp.maximum(lead, 0), jnp.minimum(i + 1, nblk), s_ref, scores,
                  lambda kb: _load_vt(vt_ref, kb, tq), _flash_init(acc_ref), acc_ref, fixed)
    o_ref[...] = _flash_out(l, acc_ref, tq)


def _band_attn(q, k, vt, bias, fixed, tq=256):
    B, _, S, _ = q.shape
    nblk = bias.shape[1]
    return pl.pallas_call(
        functools.partial(_band_attn_kernel, tq=tq, nblk=nblk, fixed=fixed),
        grid=(B, N_PAIRS, S // tq),
        in_specs=[
            pl.BlockSpec((None, None, tq, LANES), lambda b, p, i: (b, p, i, 0)),
            pl.BlockSpec((None, None, S, LANES), lambda b, p, i: (b, p, 0, 0)),
            pl.BlockSpec((None, None, S // VT_BLK, LANES, VT_BLK), lambda b, p, i: (b, p, 0, 0, 0)),
            pl.BlockSpec((None, nblk, tq, 2 * tq), lambda b, p, i: (p, 0, 0, 0)),
        ],
        out_specs=pl.BlockSpec((None, tq, LANES), lambda b, p, i: (b, i, p)),
        out_shape=jax.ShapeDtypeStruct((B, S, N_HEADS * HEAD_DIM), BF16),
        scratch_shapes=[pltpu.VMEM((LANES, 2 * tq), F32), pltpu.VMEM((2, tq, 2 * tq), F32)],
        compiler_params=_cparams(("parallel", "parallel", "arbitrary")),
        name="band_attn",
    )(q, k, vt, bias)


def _mla_attn_kernel(q_ref, k_ref, vt_ref, o_ref, acc_ref, s_ref, *, tq, tk, fixed):
    i = pl.program_id(2)
    qs = _stack_heads(q_ref[...], LANES)
    t0 = i * tq
    assert tq % (2 * tk) == 0
    n_full = t0 // tk
    n_edge = tq // tk
    lim = ((t0 + lax.broadcasted_iota(jnp.int32, (1, tq), 1)) // CHUNK + 1) * CHUNK

    def scores(kb):
        ks = pl.multiple_of(kb * tk, tk)
        return _dot(k_ref[pl.ds(ks, tk), :], qs)

    def edge_scores(kb):
        row = kb * tk + lax.broadcasted_iota(jnp.int32, (tk, tq), 0)
        add = jnp.where(row < lim, 0.0, NEG).astype(F32)
        return scores(kb) + jnp.concatenate([add, add], axis=1)

    values = lambda kb: _load_vt(vt_ref, kb, tk)
    ml = _sweep(0, n_full, s_ref, scores, values, _flash_init(acc_ref), acc_ref, fixed, peel=False)
    _, l = _sweep(n_full, n_edge, s_ref, edge_scores, values, ml, acc_ref, fixed, peel=False)
    o_ref[...] = _flash_out(l, acc_ref, tq)


def _mla_attn(q, k, vt, fixed, tq=512, tk=256):
    B, _, S, _ = q.shape
    tk = min(tk, S)
    return pl.pallas_call(
        functools.partial(_mla_attn_kernel, tq=tq, tk=tk, fixed=fixed),
        grid=(B, N_PAIRS, S // tq),
        in_specs=[
            pl.BlockSpec((None, None, tq, 256), lambda b, p, i: (b, p, i, 0)),
            pl.BlockSpec((None, None, S, 256), lambda b, p, i: (b, p, 0, 0)),
            pl.BlockSpec((None, None, S // VT_BLK, LANES, VT_BLK), lambda b, p, i: (b, p, 0, 0, 0)),
        ],
        out_specs=pl.BlockSpec((None, tq, LANES), lambda b, p, i: (b, i, p)),
        out_shape=jax.ShapeDtypeStruct((B, S, N_HEADS * V_DIM), BF16),
        scratch_shapes=[pltpu.VMEM((LANES, 2 * tq), F32), pltpu.VMEM((2, tk, 2 * tq), F32)],
        compiler_params=_cparams(("parallel", "parallel", "arbitrary")),
        name="mla_attn",
    )(q, k, vt)


def _dsa_attn_kernel(q_ref, qp_ref, wi_ref, shift_ref, bias_ref, k_ref, vt_ref, kp_ref, o_ref,
                     sc_ref, h_ref, acc_ref, stage_ref, s_ref, *, tq, tk, topk, fixed):
    i = pl.program_id(1)
    t0 = i * tq
    n_blk = (t0 + tq + tk - 1) // tk
    n_far = jnp.maximum((t0 // LANES - 1) // (tk // LANES), 0)
    n_far_even = n_far - n_far % 2
    lim = ((t0 + lax.broadcasted_iota(jnp.int32, (1, tq), 1)) // CHUNK + 1) * CHUNK

    wv = wi_ref[...]

    def score_body(kb, carry):
        ks = pl.multiple_of(kb * tk, tk)
        kp = kp_ref[pl.ds(ks, tk), :]
        acc = jnp.zeros((tk, tq), F32)
        for h in range(H_IDX):
            acc = acc + jnp.maximum(_dot(kp, qp_ref[h]), 0.0) * wv[h:h + 1, :]
        row = ks + lax.broadcasted_iota(jnp.int32, (tk, tq), 0)
        acc = jnp.where(row < lim, acc, -jnp.inf)
        bits = pltpu.bitcast(acc, jnp.int32)
        key = bits ^ ((bits >> 31) & 0x7FFFFFFF)
        sc_ref[kb] = key
        h_ref[kb] = (key >> 16).astype(jnp.int16)
        return carry

    lax.fori_loop(0, n_blk, score_body, 0)

    def count16(pred):
        def count_body(kb, part):
            hit = jnp.where(pred(h_ref[kb]), jnp.int16(1), jnp.int16(0))
            rows = hit[0:16]
            for r in range(16, tk, 16):
                rows = rows + hit[r:r + 16]
            return part + rows.astype(jnp.int32)

        part = lax.fori_loop(0, n_blk, count_body, jnp.zeros((16, tq), jnp.int32))
        return jnp.sum(part, axis=0, keepdims=True)

    def kth16(need):
        def bit_body(t, ans):
            cand = ans + jnp.left_shift(jnp.int32(1), 15 - t)
            cand16 = cand.astype(jnp.int16)
            return jnp.where(count16(lambda h: h >= cand16) >= need, cand, ans)

        return lax.fori_loop(0, 16, bit_body, jnp.full((1, tq), -(2 ** 15), jnp.int32))

    top = kth16(topk)
    top16 = top.astype(jnp.int16)
    need = topk - count16(lambda h: h > top16)

    def low_body(kb, carry):
        low = ((sc_ref[kb] & 0xFFFF) - 2 ** 15).astype(jnp.int16)
        h_ref[kb] = jnp.where(h_ref[kb] == top16, low, jnp.int16(-(2 ** 15)))
        return carry

    lax.fori_loop(0, n_blk, low_body, 0)
    thr = jnp.left_shift(top, 16) + (kth16(need) + 2 ** 15)

    keep_bits = pltpu.bitcast(shift_ref[...], jnp.int32)

    def mask_body(kb, carry):
        row = kb * tk + lax.broadcasted_iota(jnp.int32, (tk, tq), 0)
        sc_ref[kb] = jnp.where((sc_ref[kb] >= thr) & (row < lim), keep_bits, NEG_BITS)
        return carry

    lax.fori_loop(0, n_blk, mask_body, 0)

    def pair_body(p, carry):
        qs = _stack_heads(q_ref[p], HEAD_DIM)

        def far_scores(kb):
            ks = pl.multiple_of(kb * tk, tk)
            am = pltpu.bitcast(sc_ref[kb], F32)
            return _dot(k_ref[p, pl.ds(ks, tk), :], qs) + jnp.concatenate([am, am], axis=1)

        def near_scores(kb):
            cols = []
            for hh in range(2):
                for a in range(tq // LANES):
                    tiles = []
                    for c in range(tk // LANES):
                        d = (kb * (tk // LANES) + c) - (i * (tq // LANES) + a)
                        kind = jnp.where(d == 0, 2, jnp.where(d == -1, 1, 0))
                        tiles.append(bias_ref[p, hh, kind])
                    cols.append(jnp.concatenate(tiles, axis=0))
            return far_scores(kb) + jnp.concatenate(cols, axis=1)

        values = lambda kb: _load_vt(vt_ref.at[p], kb, tk)
        ml = _sweep(0, n_far_even, s_ref, far_scores, values, _flash_init(acc_ref), acc_ref, fixed, peel=False)
        _, l = _sweep(n_far_even, n_blk - n_far_even, s_ref, near_scores, values, ml, acc_ref, fixed)
        stage_ref[p] = _flash_out(l, acc_ref, tq)
        return carry

    lax.fori_loop(0, N_PAIRS, pair_body, 0)
    for p in range(N_PAIRS):
        o_ref[:, p * LANES:(p + 1) * LANES] = stage_ref[p]


def _dsa_attn(q, k, vt, qp, kp, wi, shift, bias, fixed, tq=256, tk=512):
    B, _, S, _ = q.shape
    topk = min(TOPK_MAX, S // 4)
    once = pl.Buffered(1)
    return pl.pallas_call(
        functools.partial(_dsa_attn_kernel, tq=tq, tk=tk, topk=topk, fixed=fixed),
        grid=(B, S // tq),
        in_specs=[
            pl.BlockSpec((None, N_PAIRS, tq, LANES), lambda b, i: (b, 0, i, 0)),
            pl.BlockSpec((None, H_IDX, 256, tq), lambda b, i: (b, 0, 0, i)),
            pl.BlockSpec((None, H_IDX, tq), lambda b, i: (b, 0, i)),
            pl.BlockSpec((1, tq), lambda b, i: (0, 0)),
            pl.BlockSpec((N_PAIRS, 2, 3, LANES, LANES), lambda b, i: (0, 0, 0, 0, 0), pipeline_mode=once),
            pl.BlockSpec((None, N_PAIRS, S, LANES), lambda b, i: (b, 0, 0, 0), pipeline_mode=once),
            pl.BlockSpec((None, N_PAIRS, S // VT_BLK, LANES, VT_BLK), lambda b, i: (b, 0, 0, 0, 0),
                         pipeline_mode=once),
            pl.BlockSpec((None, S, 256), lambda b, i: (b, 0, 0), pipeline_mode=once),
        ],
        out_specs=pl.BlockSpec((None, tq, N_HEADS * HEAD_DIM), lambda b, i: (b, i, 0)),
        out_shape=jax.ShapeDtypeStruct((B, S, N_HEADS * HEAD_DIM), BF16),
        scratch_shapes=[
            pltpu.VMEM((S // tk, tk, tq), jnp.int32),
            pltpu.VMEM((S // tk, tk, tq), jnp.int16),
            pltpu.VMEM((LANES, 2 * tq), F32),
            pltpu.VMEM((N_PAIRS, tq, LANES), BF16),
            pltpu.VMEM((2, tk, 2 * tq), F32),
        ],
        compiler_params=_cparams(("parallel", "arbitrary"), VMEM_LIMIT_DSA),
        name="dsa_attn",
    )(q, qp, wi, jnp.full((1, tq), shift, F32), bias, k, vt, kp)


def _out_proj_kernel(x_ref, a_ref, w_ref, o_ref):
    o_ref[...] = x_ref[...] + _dot(a_ref[...], w_ref[...])


def _out_proj(x, a, w, tm=512):
    B, S, D = x.shape
    return pl.pallas_call(
        _out_proj_kernel,
        grid=(B, S // tm),
        in_specs=[
            pl.BlockSpec((None, tm, D), lambda b, i: (b, i, 0)),
            pl.BlockSpec((None, tm, a.shape[-1]), lambda b, i: (b, i, 0)),
            pl.BlockSpec(w.shape, lambda b, i: (0, 0)),
        ],
        out_specs=pl.BlockSpec((None, tm, D), lambda b, i: (b, i, 0)),
        out_shape=jax.ShapeDtypeStruct((B, S, D), F32),
        compiler_params=_cparams(("parallel", "parallel")),
        name="out_proj",
    )(x, a, w.astype(BF16))


def _ffn_kernel(x_ref, xp_ref, g_ref, wg_ref, wv_ref, cw_ref, cb_ref, wd_ref, o_ref, acc_ref, he_ref, u_ref,
                *, tm, n_chunks):
    i = pl.program_id(1)
    g = g_ref[...]
    he_ref[:HALO] = (_rms(xp_ref[...], g) * jnp.where(i > 0, 1.0, 0.0)).astype(BF16)
    he_ref[HALO:] = _rms(x_ref[...], g).astype(BF16)
    acc_ref[...] = jnp.zeros(acc_ref.shape, F32)

    def up(c, slot):
        he = he_ref[...]
        u_ref[slot, 0] = _dot(he, wg_ref[c])
        u_ref[slot, 1] = _dot(he, wv_ref[c])

    def conv(u, w, b):
        return (b + w[0:1] * u[HALO - 2:HALO - 2 + tm] + w[1:2] * u[HALO - 1:HALO - 1 + tm]
                + w[2:3] * u[HALO:HALO + tm])

    def down(c, slot):
        cw = cw_ref[c]
        cb = cb_ref[c]
        gate = conv(u_ref[slot, 0], cw[0], cb[0])
        val = conv(u_ref[slot, 1], cw[1], cb[1])
        act = gate * jax.nn.sigmoid(gate) * val
        acc_ref[...] += _dot(act.astype(BF16), wd_ref[c])

    up(0, 0)

    def pair_body(t, carry):
        c = 2 * t
        up(c + 1, 1)
        down(c, 0)
        up(c + 2, 0)
        down(c + 1, 1)
        return carry

    assert n_chunks % 2 == 1
    lax.fori_loop(0, n_chunks // 2, pair_body, 0)
    down(n_chunks - 1, 0)
    o_ref[...] = x_ref[...] + acc_ref[...]


def _ffn(x, g, w_up, conv_w, conv_b, w_down, tm=512):
    B, S, D = x.shape
    nc = D_FF // FF_CHUNK
    wup = w_up.astype(BF16).reshape(D, 2, nc, FF_CHUNK).transpose(1, 2, 0, 3)
    cw = conv_w.reshape(CONV_W, 2, nc, FF_CHUNK).transpose(2, 1, 0, 3)
    cb = conv_b.reshape(2, nc, 1, FF_CHUNK).transpose(1, 0, 2, 3)
    wd = w_down.astype(BF16).reshape(nc, FF_CHUNK, D)
    c3 = lambda b, i: (0, 0, 0)
    c4 = lambda b, i: (0, 0, 0, 0)
    return pl.pallas_call(
        functools.partial(_ffn_kernel, tm=tm, n_chunks=nc),
        grid=(B, S // tm),
        in_specs=[
            pl.BlockSpec((None, tm, D), lambda b, i: (b, i, 0)),
            pl.BlockSpec((None, HALO, D), lambda b, i: (b, jnp.maximum(i * (tm // HALO) - 1, 0), 0)),
            pl.BlockSpec((1, D), lambda b, i: (0, 0)),
            pl.BlockSpec((nc, D, FF_CHUNK), c3),
            pl.BlockSpec((nc, D, FF_CHUNK), c3),
            pl.BlockSpec((nc, 2, CONV_W, FF_CHUNK), c4),
            pl.BlockSpec((nc, 2, 1, FF_CHUNK), c4),
            pl.BlockSpec((nc, FF_CHUNK, D), c3),
        ],
        out_specs=pl.BlockSpec((None, tm, D), lambda b, i: (b, i, 0)),
        out_shape=jax.ShapeDtypeStruct((B, S, D), F32),
        scratch_shapes=[pltpu.VMEM((tm, D), F32), pltpu.VMEM((HALO + tm, D), BF16),
                        pltpu.VMEM((2, 2, HALO + tm, FF_CHUNK), F32)],
        compiler_params=_cparams(("parallel", "parallel")),
        name="conv_ffn",
    )(x, x, g.reshape(1, D), wup[0], wup[1], cw, cb, wd)


def _rope_tables(S, period, off):
    inv = 1.0 / (ROPE_THETA ** (jnp.arange(0, ROPE_DIM, 2, dtype=F32) / ROPE_DIM))
    ang = jnp.arange(S, dtype=F32)[:, None] * inv[None, :]
    cos, sin = jnp.cos(ang), jnp.sin(ang)
    half = ROPE_DIM // 2
    lane = np.arange(LANES) % period
    is1 = (lane >= off) & (lane < off + half)
    is2 = (lane >= off + half) & (lane < off + ROPE_DIM)
    idx = np.where(is1, lane - off, np.where(is2, lane - off - half, 0))
    ct = jnp.where(is1 | is2, cos[:, idx], 1.0)
    s1 = jnp.where(is1, -sin[:, idx], 0.0)
    s2 = jnp.where(is2, sin[:, idx], 0.0)
    return ct, s1, s2


def _t5_bucket(rel):
    nb = T5_BUCKETS // 2
    max_exact = nb // 2
    n = jnp.abs(rel)
    large = max_exact + (jnp.log(jnp.maximum(n, 1).astype(F32) / max_exact)
                         / math.log(T5_MAX_DIST / max_exact) * (nb - max_exact)).astype(jnp.int32)
    large = jnp.minimum(large, nb - 1)
    return jnp.where(rel > 0, nb, 0) + jnp.where(n < max_exact, n, large)


def _dsa_bias_tiles(t5_bias):
    kj = jnp.arange(LANES, dtype=jnp.int32)[:, None]
    qi = jnp.arange(LANES, dtype=jnp.int32)[None, :]
    far = t5_bias[_t5_bucket(jnp.int32(-4 * T5_MAX_DIST))]
    same = (jnp.transpose(t5_bias[_t5_bucket(kj - qi)], (2, 0, 1)) - far[:, None, None]) * LOG2E
    prev = (jnp.transpose(t5_bias[_t5_bucket(kj - LANES - qi)], (2, 0, 1)) - far[:, None, None]) * LOG2E
    return jnp.stack([jnp.zeros_like(same), prev, same], axis=1).reshape(N_PAIRS, 2, 3, LANES, LANES)


def _softmax_shift(gq, gk, dim, bias_hi=0.0, bias_lo=0.0):
    qk = 1.02 * dim * jnp.max(jnp.abs(gq)) * jnp.max(jnp.abs(gk))
    top = qk + bias_hi
    fixed = (top + qk - bias_lo) <= SAFE_SPREAD
    return fixed, jnp.where(fixed, -top, 0.0)


def _band_bias(rel_bias, shift, tq):
    pad = LEFT_CHUNKS * CHUNK
    nblk = -(-pad // tq) + 1
    lead = (nblk - 1) * tq
    nk = nblk * tq
    kk = jnp.arange(nk, dtype=jnp.int32)[:, None] - lead
    i = jnp.arange(tq, dtype=jnp.int32)[None, :]
    first = (i // CHUNK) * CHUNK - pad
    ok = (kk >= first) & (kk < (i // CHUNK + 1) * CHUNK)
    L = nk + tq
    dist = jnp.arange(L, dtype=jnp.int32)
    dist = lead + jnp.where(dist < tq, dist, dist - L)
    v = rel_bias[:, jnp.clip(dist, -(CHUNK - 1), REL_MAX_PAST) + CHUNK - 1]
    b = jnp.tile(v, (1, nk))[:, :nk * (L - 1)].reshape(v.shape[0], nk, L - 1)[:, :, :tq]
    b = jnp.where(ok[None], b + shift, NEG).reshape(N_PAIRS, 2, nblk, tq, tq)
    return jnp.transpose(b, (0, 2, 3, 1, 4)).reshape(N_PAIRS, nblk, tq, 2 * tq)


def kernel(x, norm_mix, norm_ffn, t5_bias, a_w_in, a_q_norm, a_k_norm, a_kidx_norm, a_w_out, b_w_down, b_q_a_norm, b_kv_a_norm, b_w_uq, b_w_ukv, b_q_norm, b_k_norm, b_w_out, c_w_in, c_q_norm, c_k_norm, c_rel_bias, c_w_out, f_w_up, f_conv_w, f_conv_b, f_w_down):
    B, S, D = x.shape
    depth = norm_mix.shape[0]

    tq_c = 256
    tabs_idx = _rope_tables(S, D_IDX, D_IDX - ROPE_DIM)
    tabs_mla = _rope_tables(S, LANES, QK_NOPE)
    scale = HEAD_DIM ** -0.5 * LOG2E
    for layer in range(depth):
        kind, j = layer % N_MIXERS, layer // N_MIXERS
        g = norm_mix[layer]
        if kind == 0:
            gq = a_q_norm[j] * scale
            q, k, vt = _qkv_proj(x, g, a_w_in[j][:, :A_QKV], gq, a_k_norm[j])
            qp, kp, wi = _idx_proj(x, g, a_w_in[j][:, A_QKV:], a_kidx_norm[j], tabs_idx)
            tiles = _dsa_bias_tiles(t5_bias)
            fixed, shift = _softmax_shift(gq, a_k_norm[j], HEAD_DIM, jnp.max(tiles), jnp.min(tiles))
            att = lax.cond(fixed, functools.partial(_dsa_attn, fixed=True), functools.partial(_dsa_attn, fixed=False),
                           q, k, vt, qp, kp, wi, shift, tiles)
            w_out = a_w_out[j]
        elif kind == 1:
            fixed, shift = _softmax_shift(b_q_norm[j] * (B_QK ** -0.5 * LOG2E), b_k_norm[j], B_QK)
            q, k, vt = _mla_proj(x, g, b_w_down[j], b_q_a_norm[j], b_kv_a_norm[j], b_w_uq[j], b_w_ukv[j],
                                 b_q_norm[j], b_k_norm[j], shift, tabs_mla)
            att = lax.cond(fixed, functools.partial(_mla_attn, fixed=True), functools.partial(_mla_attn, fixed=False),
                           q, k, vt)
            w_out = b_w_out[j]
        else:
            gq = c_q_norm[j] * scale
            q, k, vt = _qkv_proj(x, g, c_w_in[j], gq, c_k_norm[j])
            rel = c_rel_bias[j] * LOG2E
            fixed, shift = _softmax_shift(gq, c_k_norm[j], HEAD_DIM, jnp.max(rel), jnp.min(rel))
            att = lax.cond(fixed, functools.partial(_band_attn, fixed=True, tq=tq_c),
                           functools.partial(_band_attn, fixed=False, tq=tq_c),
                           q, k, vt, _band_bias(rel, shift, tq_c))
            w_out = c_w_out[j]
        x = _out_proj(x, att, w_out)
        x = _ffn(x, norm_ffn[layer], f_w_up[layer], f_conv_w[layer], f_conv_b[layer], f_w_down[layer])
    return x
```

```python
import functools
import math

import numpy as np
import jax
import jax.numpy as jnp
from jax import lax
from jax.experimental import pallas as pl
from jax.experimental.pallas import tpu as pltpu

F32 = jnp.float32
BF16 = jnp.bfloat16

D_MODEL = 1024
CHUNK = 64
EPS = 1e-6
ROPE_THETA = 10000.0
NEG = -1e30
NEG_BITS = int(np.float32(NEG).view(np.int32))
M_INIT = -1e38
INT_MIN = -(2 ** 31)
LOG2E = math.log2(math.e)
SAFE_SPREAD = 100.0

N_MIXERS = 3
N_HEADS = 16
HEAD_DIM = 64
N_PAIRS = N_HEADS // 2
LANES = 128
VT_BLK = 256

H_IDX = 8
D_IDX = 64
ROPE_DIM = 32
TOPK_MAX = 256
T5_BUCKETS = 32
T5_MAX_DIST = 128
A_QKV = 3 * N_HEADS * HEAD_DIM
Q_LORA = 256
KV_LORA = 128
QK_NOPE = 64
QK_ROPE = 32
V_DIM = 64
B_QK = QK_NOPE + QK_ROPE
LEFT_CHUNKS = 8
REL_MAX_PAST = 128
D_FF = 2816
CONV_W = 3
FF_CHUNK = 256
HALO = 8

VMEM_LIMIT = 56 * 1024 * 1024
VMEM_LIMIT_DSA = 60 * 1024 * 1024


def _cparams(sem, vmem_limit=VMEM_LIMIT):
    return pltpu.CompilerParams(dimension_semantics=sem, vmem_limit_bytes=vmem_limit)


def _rms(x, g):
    return x * lax.rsqrt(jnp.mean(x * x, axis=-1, keepdims=True) + EPS) * g


def _dot(a, b):
    return jnp.dot(a, b, preferred_element_type=F32)


def _split_bf16(x):
    hi = x.astype(BF16)
    lo = (x - hi.astype(F32)).astype(BF16)
    return hi, lo


def _group_sumsq(y, bd):
    hi, lo = _split_bf16(y * y)
    return _dot(hi, bd) + _dot(lo, bd)


def _rope(y, ct, s1, s2):
    return y * ct + pltpu.roll(y, LANES - 16, 1) * s1 + pltpu.roll(y, 16, 1) * s2


def _rope_wide(y, ct, s1, s2):
    w = y.shape[-1]
    parts = [_rope(y[:, c:c + LANES], ct, s1, s2) for c in range(0, w, LANES)]
    return parts[0] if len(parts) == 1 else jnp.concatenate(parts, axis=-1)


def _store_vt(vt_ref, p, y):
    yt = y.T.astype(BF16)
    for r in range(y.shape[0] // VT_BLK):
        vt_ref[p, r] = yt[:, r * VT_BLK:(r + 1) * VT_BLK]


def _qkv_proj_kernel(x_ref, g_ref, w_ref, gq_ref, gk_ref, bd_ref, q_ref, k_ref, vt_ref):
    xb = _rms(x_ref[...], g_ref[...]).astype(BF16)
    bd = bd_ref[...]
    hd = N_HEADS * HEAD_DIM
    for which, (o_ref, gain_ref) in enumerate(((q_ref, gq_ref), (k_ref, gk_ref), (vt_ref, None))):
        for c in range(hd // 256):
            col = which * hd + c * 256
            y = _dot(xb, w_ref[:, col:col + 256])
            if gain_ref is None:
                _store_vt(o_ref, 2 * c, y[:, :LANES])
                _store_vt(o_ref, 2 * c + 1, y[:, LANES:])
            else:
                y = y * lax.rsqrt(_group_sumsq(y, bd) * (1.0 / HEAD_DIM) + EPS) * gain_ref[...]
                yb = y.astype(BF16)
                o_ref[2 * c] = yb[:, :LANES]
                o_ref[2 * c + 1] = yb[:, LANES:]


def _vt_spec(tm):
    return pl.BlockSpec((None, N_PAIRS, tm // VT_BLK, LANES, VT_BLK), lambda b, i: (b, 0, i, 0, 0))


def _vt_shape(B, S):
    return jax.ShapeDtypeStruct((B, N_PAIRS, S // VT_BLK, LANES, VT_BLK), BF16)


def _qkv_proj(x, g, w, gq, gk, tm=512):
    B, S, D = x.shape
    hd = N_HEADS * HEAD_DIM
    bd = jnp.kron(jnp.eye(256 // HEAD_DIM, dtype=F32), jnp.ones((HEAD_DIM, HEAD_DIM), F32)).astype(BF16)
    out = jax.ShapeDtypeStruct((B, N_PAIRS, S, LANES), BF16)
    const = lambda b, i: (0, 0)
    ospec = pl.BlockSpec((None, N_PAIRS, tm, LANES), lambda b, i: (b, 0, i, 0))
    return pl.pallas_call(
        _qkv_proj_kernel,
        grid=(B, S // tm),
        in_specs=[
            pl.BlockSpec((None, tm, D), lambda b, i: (b, i, 0)),
            pl.BlockSpec((1, D), const),
            pl.BlockSpec((D, 3 * hd), const),
            pl.BlockSpec((1, 256), const),
            pl.BlockSpec((1, 256), const),
            pl.BlockSpec((256, 256), const),
        ],
        out_specs=[ospec, ospec, _vt_spec(tm)],
        out_shape=[out, out, _vt_shape(B, S)],
        compiler_params=_cparams(("parallel", "parallel")),
        name="qkv_proj",
    )(x, g.reshape(1, D), w.astype(BF16), jnp.tile(gq, 4).reshape(1, 256), jnp.tile(gk, 4).reshape(1, 256), bd)


def _idx_proj_kernel(x_ref, g_ref, wh_ref, wl_ref, gki_ref, ct_ref, s1_ref, s2_ref, qp_ref, kp_ref, wi_ref,
                     *, w_scale):
    xn = _rms(x_ref[...], g_ref[...])
    xh, xl = _split_bf16(xn)

    def mm(c0, c1):
        wh = wh_ref[:, c0:c1]
        return _dot(xh, wh) + (_dot(xl, wh) + _dot(xh, wl_ref[:, c0:c1]))

    ct, s1, s2 = ct_ref[...], s1_ref[...], s2_ref[...]
    tm = xn.shape[0]
    low = lax.broadcasted_iota(jnp.int32, (tm, LANES), 1) < D_IDX
    zero = jnp.zeros((tm, LANES), F32)

    for c in range(H_IDX // 2):
        y = _rope(mm(c * LANES, (c + 1) * LANES), ct, s1, s2)
        hi = y.astype(BF16).astype(F32)
        lo = y - hi
        rhi = pltpu.roll(hi, D_IDX, 1)
        rlo = pltpu.roll(lo, D_IDX, 1)
        qp_ref[2 * c, :LANES, :] = jnp.where(low, hi, rlo).T.astype(BF16)
        qp_ref[2 * c, LANES:, :] = jnp.where(low, hi, zero).T.astype(BF16)
        qp_ref[2 * c + 1, :LANES, :] = jnp.where(low, rhi, lo).T.astype(BF16)
        qp_ref[2 * c + 1, LANES:, :] = jnp.where(low, rhi, zero).T.astype(BF16)

    c0 = H_IDX * D_IDX
    y = mm(c0, c0 + LANES)
    y = y * lax.rsqrt(jnp.sum(y * y, axis=-1, keepdims=True) * (1.0 / D_IDX) + EPS) * gki_ref[...]
    y = _rope(y, ct, s1, s2)
    hi = y.astype(BF16).astype(F32)
    lo = y - hi
    kp_ref[:, :LANES] = jnp.where(low, hi, pltpu.roll(hi, D_IDX, 1)).astype(BF16)
    kp_ref[:, LANES:] = jnp.where(low, lo, zero).astype(BF16)

    wi_ref[...] = (mm(c0 + LANES, c0 + 2 * LANES) * w_scale).T[:H_IDX]


def _idx_proj(x, g, w_idx, gki, tabs, tm=512):
    B, S, D = x.shape
    nq = H_IDX * D_IDX
    wq, wk, ww = w_idx[:, :nq], w_idx[:, nq:nq + D_IDX], w_idx[:, nq + D_IDX:]
    wpad = jnp.concatenate([
        wq, jnp.pad(wk, ((0, 0), (0, LANES - D_IDX))), jnp.pad(ww, ((0, 0), (0, LANES - H_IDX)))], axis=1)
    wh = wpad.astype(BF16)
    wl = (wpad - wh.astype(F32)).astype(BF16)
    ncol = nq + 2 * LANES
    gpad = jnp.pad(gki, (0, LANES - D_IDX)).reshape(1, LANES)
    const = lambda b, i: (0, 0)
    tspec = pl.BlockSpec((tm, LANES), lambda b, i: (i, 0))
    return pl.pallas_call(
        functools.partial(_idx_proj_kernel, w_scale=(D_IDX ** -0.5) * (H_IDX ** -0.5)),
        grid=(B, S // tm),
        in_specs=[
            pl.BlockSpec((None, tm, D), lambda b, i: (b, i, 0)),
            pl.BlockSpec((1, D), const),
            pl.BlockSpec((D, ncol), const),
            pl.BlockSpec((D, ncol), const),
            pl.BlockSpec((1, LANES), const),
            tspec, tspec, tspec,
        ],
        out_specs=[
            pl.BlockSpec((None, H_IDX, 256, tm), lambda b, i: (b, 0, 0, i)),
            pl.BlockSpec((None, tm, 256), lambda b, i: (b, i, 0)),
            pl.BlockSpec((None, H_IDX, tm), lambda b, i: (b, 0, i)),
        ],
        out_shape=[
            jax.ShapeDtypeStruct((B, H_IDX, 256, S), BF16),
            jax.ShapeDtypeStruct((B, S, 256), BF16),
            jax.ShapeDtypeStruct((B, H_IDX, S), F32),
        ],
        compiler_params=_cparams(("parallel", "parallel")),
        name="idx_proj",
    )(x, g.reshape(1, D), wh, wl, gpad, *tabs)


def _mla_proj_kernel(x_ref, g_ref, wd_ref, gqa_ref, gkva_ref, wuq_ref, wuk_ref, wuv_ref, gq_ref, gk_ref, bd_ref,
                     qfix_ref, kfix_ref, ct_ref, s1_ref, s2_ref, q_ref, k_ref, vt_ref):
    xb = _rms(x_ref[...], g_ref[...]).astype(BF16)
    d = _dot(xb, wd_ref[...])
    cq = _rms(d[:, :Q_LORA], gqa_ref[...]).astype(BF16)
    ckv = _rms(d[:, Q_LORA:Q_LORA + KV_LORA], gkva_ref[...]).astype(BF16)
    krz = pltpu.roll(d[:, Q_LORA + KV_LORA:], QK_NOPE, 1)
    krz2 = jnp.concatenate([krz, krz], axis=-1)
    bd = bd_ref[...]
    ct, s1, s2 = ct_ref[...], s1_ref[...], s2_ref[...]
    inv = 1.0 / B_QK
    for p in range(N_PAIRS):
        cols = slice(p * 256, (p + 1) * 256)
        y = _dot(cq, wuq_ref[:, cols])
        y = y * lax.rsqrt(_group_sumsq(y, bd) * inv + EPS) * gq_ref[...]
        q_ref[p] = (_rope_wide(y, ct, s1, s2) + qfix_ref[...]).astype(BF16)
        y = _dot(ckv, wuk_ref[:, cols]) + krz2
        y = y * lax.rsqrt(_group_sumsq(y, bd) * inv + EPS) * gk_ref[...]
        k_ref[p] = (_rope_wide(y, ct, s1, s2) + kfix_ref[...]).astype(BF16)
        _store_vt(vt_ref, p, _dot(ckv, wuv_ref[:, p * LANES:(p + 1) * LANES]))


def _mla_proj(x, g, w_down, gqa, gkva, w_uq, w_ukv, gq, gk, shift, tabs, tm=512):
    B, S, D = x.shape
    nd = Q_LORA + KV_LORA + QK_ROPE
    wd = jnp.pad(w_down, ((0, 0), (0, 512 - nd))).astype(BF16)
    padh = LANES - B_QK
    wuq = jnp.pad(w_uq.reshape(Q_LORA, N_HEADS, B_QK), ((0, 0), (0, 0), (0, padh))).reshape(Q_LORA, N_HEADS * LANES)
    wukv = w_ukv.reshape(KV_LORA, N_HEADS, QK_NOPE + V_DIM)
    wuk = jnp.pad(wukv[:, :, :QK_NOPE], ((0, 0), (0, 0), (0, LANES - QK_NOPE))).reshape(KV_LORA, N_HEADS * LANES)
    wuv = wukv[:, :, QK_NOPE:].reshape(KV_LORA, N_HEADS * V_DIM)
    gq2 = jnp.tile(jnp.pad(gq * (B_QK ** -0.5 * LOG2E), (0, padh)), 2).reshape(1, 256)
    gk2 = jnp.tile(jnp.pad(gk, (0, padh)), 2).reshape(1, 256)
    bd = jnp.kron(jnp.eye(2, dtype=F32), jnp.ones((LANES, LANES), F32)).astype(BF16)
    kfix = jnp.asarray((np.arange(256) % LANES == B_QK).astype(np.float32)).reshape(1, 256)
    qfix = kfix * shift
    const = lambda b, i: (0, 0)
    tspec = pl.BlockSpec((tm, LANES), lambda b, i: (i, 0))
    qk_shape = jax.ShapeDtypeStruct((B, N_PAIRS, S, 256), BF16)
    qk_spec = pl.BlockSpec((None, N_PAIRS, tm, 256), lambda b, i: (b, 0, i, 0))
    return pl.pallas_call(
        _mla_proj_kernel,
        grid=(B, S // tm),
        in_specs=[
            pl.BlockSpec((None, tm, D), lambda b, i: (b, i, 0)),
            pl.BlockSpec((1, D), const),
            pl.BlockSpec((D, 512), const),
            pl.BlockSpec((1, Q_LORA), const),
            pl.BlockSpec((1, KV_LORA), const),
            pl.BlockSpec((Q_LORA, N_HEADS * LANES), const),
            pl.BlockSpec((KV_LORA, N_HEADS * LANES), const),
            pl.BlockSpec((KV_LORA, N_HEADS * V_DIM), const),
            pl.BlockSpec((1, 256), const),
            pl.BlockSpec((1, 256), const),
            pl.BlockSpec((256, 256), const),
            pl.BlockSpec((1, 256), const),
            pl.BlockSpec((1, 256), const),
            tspec, tspec, tspec,
        ],
        out_specs=[qk_spec, qk_spec, _vt_spec(tm)],
        out_shape=[qk_shape, qk_shape, _vt_shape(B, S)],
        compiler_params=_cparams(("parallel", "parallel")),
        name="mla_proj",
    )(x, g.reshape(1, D), wd, gqa.reshape(1, Q_LORA), gkva.reshape(1, KV_LORA), wuq.astype(BF16),
      wuk.astype(BF16), wuv.astype(BF16), gq2, gk2, bd, qfix, kfix, *tabs)


def _stack_heads(q2, half):
    qf = q2.astype(F32)
    low = lax.broadcasted_iota(jnp.int32, qf.shape, 1) < half
    zero = jnp.zeros_like(qf)
    return jnp.concatenate([jnp.where(low, qf, zero).T, jnp.where(low, zero, qf).T], axis=1).astype(BF16)


def _flash_init(acc_ref):
    r = acc_ref.shape[1]
    acc_ref[...] = jnp.zeros(acc_ref.shape, F32)
    return jnp.full((1, r), M_INIT, F32), jnp.zeros((1, r), F32)


def _consume(s, vt, m, l, acc_ref, fixed):
    if fixed:
        p = jnp.exp2(s)
        acc_ref[...] += _dot(vt, p.astype(BF16))
        return m, l + jnp.sum(p, axis=0, keepdims=True)
    m_new = jnp.maximum(m, jnp.max(s, axis=0, keepdims=True))
    alpha = jnp.exp2(m - m_new)
    p = jnp.exp2(s - m_new)
    l_new = alpha * l + jnp.sum(p, axis=0, keepdims=True)
    acc_ref[...] = alpha * acc_ref[...] + _dot(vt, p.astype(BF16))
    return m_new, l_new


def _sweep(first, n, s_ref, scores, values, ml, acc_ref, fixed, peel=True):
    def one(j, s, ml):
        return _consume(s, values(j), *ml, acc_ref, fixed)

    if isinstance(n, int):
        tiles = [scores(first + j) for j in range(n)]
        for j in range(n):
            ml = one(first + j, tiles[j], ml)
        return ml

    odd = n % 2 if peel else 0
    if peel:
        ml = lax.cond(odd == 1, lambda ml: one(first, scores(first), ml), lambda ml: ml, ml)
    j0 = first + odd
    last = first + n - 1

    @pl.when(n >= 2)
    def _():
        s_ref[0] = scores(j0)

    def body(t, ml):
        j = j0 + 2 * t
        s1 = scores(j + 1)
        ml = one(j, s_ref[0], ml)
        s_ref[1] = s1
        s0 = scores(jnp.minimum(j + 2, last))
        ml = one(j + 1, s_ref[1], ml)
        s_ref[0] = s0
        return ml

    return lax.fori_loop(0, n // 2, body, ml)


def _flash_out(l, acc_ref, tq):
    o = acc_ref[...] / l
    ot = jnp.concatenate([o[:HEAD_DIM, :tq], o[HEAD_DIM:, tq:]], axis=0)
    return ot.T.astype(BF16)


def _load_vt(vt_ref, kb, tk):
    n = tk // VT_BLK
    parts = [vt_ref[kb * n + r] for r in range(n)]
    return parts[0] if n == 1 else jnp.concatenate(parts, axis=1)


def _band_attn_kernel(q_ref, k_ref, vt_ref, bias_ref, o_ref, acc_ref, s_ref, *, tq, nblk, fixed):
    i = pl.program_id(2)
    qs = _stack_heads(q_ref[...], HEAD_DIM)
    lead = i - (nblk - 1)

    def scores(kb):
        ks = pl.multiple_of(kb * tq, tq)
        return _dot(k_ref[pl.ds(ks, tq), :], qs) + bias_ref[kb - lead]

    _, l = _sweep(jnp.maximum(lead, 0), jnp.minimum(i + 1, nblk), s_ref, scores,
                  lambda kb: _load_vt(vt_ref, kb, tq), _flash_init(acc_ref), acc_ref, fixed)
    o_ref[...] = _flash_out(l, acc_ref, tq)


def _band_attn(q, k, vt, bias, fixed, tq=256):
    B, _, S, _ = q.shape
    nblk = bias.shape[1]
    return pl.pallas_call(
        functools.partial(_band_attn_kernel, tq=tq, nblk=nblk, fixed=fixed),
        grid=(B, N_PAIRS, S // tq),
        in_specs=[
            pl.BlockSpec((None, None, tq, LANES), lambda b, p, i: (b, p, i, 0)),
            pl.BlockSpec((None, None, S, LANES), lambda b, p, i: (b, p, 0, 0)),
            pl.BlockSpec((None, None, S // VT_BLK, LANES, VT_BLK), lambda b, p, i: (b, p, 0, 0, 0)),
            pl.BlockSpec((None, nblk, tq, 2 * tq), lambda b, p, i: (p, 0, 0, 0)),
        ],
        out_specs=pl.BlockSpec((None, tq, LANES), lambda b, p, i: (b, i, p)),
        out_shape=jax.ShapeDtypeStruct((B, S, N_HEADS * HEAD_DIM), BF16),
        scratch_shapes=[pltpu.VMEM((LANES, 2 * tq), F32), pltpu.VMEM((2, tq, 2 * tq), F32)],
        compiler_params=_cparams(("parallel", "parallel", "arbitrary")),
        name="band_attn",
    )(q, k, vt, bias)


def _mla_attn_kernel(q_ref, k_ref, vt_ref, o_ref, acc_ref, s_ref, *, tq, tk, fixed):
    i = pl.program_id(2)
    qs = _stack_heads(q_ref[...], LANES)
    t0 = i * tq
    assert tq % (2 * tk) == 0
    n_full = t0 // tk
    n_edge = tq // tk
    lim = ((t0 + lax.broadcasted_iota(jnp.int32, (1, tq), 1)) // CHUNK + 1) * CHUNK

    def scores(kb):
        ks = pl.multiple_of(kb * tk, tk)
        return _dot(k_ref[pl.ds(ks, tk), :], qs)

    def edge_scores(kb):
        row = kb * tk + lax.broadcasted_iota(jnp.int32, (tk, tq), 0)
        add = jnp.where(row < lim, 0.0, NEG).astype(F32)
        return scores(kb) + jnp.concatenate([add, add], axis=1)

    values = lambda kb: _load_vt(vt_ref, kb, tk)
    ml = _sweep(0, n_full, s_ref, scores, values, _flash_init(acc_ref), acc_ref, fixed, peel=False)
    _, l = _sweep(n_full, n_edge, s_ref, edge_scores, values, ml, acc_ref, fixed, peel=False)
    o_ref[...] = _flash_out(l, acc_ref, tq)


def _mla_attn(q, k, vt, fixed, tq=1024, tk=256):
    B, _, S, _ = q.shape
    tk = min(tk, S)
    return pl.pallas_call(
        functools.partial(_mla_attn_kernel, tq=tq, tk=tk, fixed=fixed),
        grid=(B, N_PAIRS, S // tq),
        in_specs=[
            pl.BlockSpec((None, None, tq, 256), lambda b, p, i: (b, p, i, 0)),
            pl.BlockSpec((None, None, S, 256), lambda b, p, i: (b, p, 0, 0)),
            pl.BlockSpec((None, None, S // VT_BLK, LANES, VT_BLK), lambda b, p, i: (b, p, 0, 0, 0)),
        ],
        out_specs=pl.BlockSpec((None, tq, LANES), lambda b, p, i: (b, i, p)),
        out_shape=jax.ShapeDtypeStruct((B, S, N_HEADS * V_DIM), BF16),
        scratch_shapes=[pltpu.VMEM((LANES, 2 * tq), F32), pltpu.VMEM((2, tk, 2 * tq), F32)],
        compiler_params=_cparams(("parallel", "parallel", "arbitrary")),
        name="mla_attn",
    )(q, k, vt)


def _dsa_attn_kernel(q_ref, qp_ref, wi_ref, shift_ref, bias_ref, k_ref, vt_ref, kp_ref, o_ref,
                     sc_ref, h_ref, acc_ref, stage_ref, s_ref, *, tq, tk, topk, fixed):
    i = pl.program_id(1)
    t0 = i * tq
    n_blk = (t0 + tq + tk - 1) // tk
    n_far = jnp.maximum((t0 // LANES - 1) // (tk // LANES), 0)
    n_far_even = n_far - n_far % 2
    lim = ((t0 + lax.broadcasted_iota(jnp.int32, (1, tq), 1)) // CHUNK + 1) * CHUNK

    wv = wi_ref[...]

    def score_body(kb, carry):
        ks = pl.multiple_of(kb * tk, tk)
        kp = kp_ref[pl.ds(ks, tk), :]
        acc = jnp.zeros((tk, tq), F32)
        for h in range(H_IDX):
            acc = acc + jnp.maximum(_dot(kp, qp_ref[h]), 0.0) * wv[h:h + 1, :]
        row = ks + lax.broadcasted_iota(jnp.int32, (tk, tq), 0)
        acc = jnp.where(row < lim, acc, -jnp.inf)
        bits = pltpu.bitcast(acc, jnp.int32)
        key = bits ^ ((bits >> 31) & 0x7FFFFFFF)
        sc_ref[kb] = key
        h_ref[kb] = (key >> 16).astype(jnp.int16)
        return carry

    lax.fori_loop(0, n_blk, score_body, 0)

    def count16(pred):
        def count_body(kb, part):
            hit = jnp.where(pred(h_ref[kb]), jnp.int16(1), jnp.int16(0))
            rows = hit[0:16]
            for r in range(16, tk, 16):
                rows = rows + hit[r:r + 16]
            return part + rows.astype(jnp.int32)

        part = lax.fori_loop(0, n_blk, count_body, jnp.zeros((16, tq), jnp.int32))
        return jnp.sum(part, axis=0, keepdims=True)

    def kth16(need):
        def bit_body(t, ans):
            cand = ans + jnp.left_shift(jnp.int32(1), 15 - t)
            cand16 = cand.astype(jnp.int16)
            return jnp.where(count16(lambda h: h >= cand16) >= need, cand, ans)

        return lax.fori_loop(0, 16, bit_body, jnp.full((1, tq), -(2 ** 15), jnp.int32))

    top = kth16(topk)
    top16 = top.astype(jnp.int16)
    need = topk - count16(lambda h: h > top16)

    def low_body(kb, carry):
        low = ((sc_ref[kb] & 0xFFFF) - 2 ** 15).astype(jnp.int16)
        h_ref[kb] = jnp.where(h_ref[kb] == top16, low, jnp.int16(-(2 ** 15)))
        return carry

    lax.fori_loop(0, n_blk, low_body, 0)
    thr = jnp.left_shift(top, 16) + (kth16(need) + 2 ** 15)

    keep_bits = pltpu.bitcast(shift_ref[...], jnp.int32)

    def mask_body(kb, carry):
        row = kb * tk + lax.broadcasted_iota(jnp.int32, (tk, tq), 0)
        sc_ref[kb] = jnp.where((sc_ref[kb] >= thr) & (row < lim), keep_bits, NEG_BITS)
        return carry

    lax.fori_loop(0, n_blk, mask_body, 0)

    def pair_body(p, carry):
        qs = _stack_heads(q_ref[p], HEAD_DIM)

        def far_scores(kb):
            ks = pl.multiple_of(kb * tk, tk)
            am = pltpu.bitcast(sc_ref[kb], F32)
            return _dot(k_ref[p, pl.ds(ks, tk), :], qs) + jnp.concatenate([am, am], axis=1)

        def near_scores(kb):
            cols = []
            for hh in range(2):
                for a in range(tq // LANES):
                    tiles = []
                    for c in range(tk // LANES):
                        d = (kb * (tk // LANES) + c) - (i * (tq // LANES) + a)
                        kind = jnp.where(d == 0, 2, jnp.where(d == -1, 1, 0))
                        tiles.append(bias_ref[p, hh, kind])
                    cols.append(jnp.concatenate(tiles, axis=0))
            return far_scores(kb) + jnp.concatenate(cols, axis=1)

        values = lambda kb: _load_vt(vt_ref.at[p], kb, tk)
        _, l = _sweep(0, n_blk, s_ref, near_scores, values, _flash_init(acc_ref), acc_ref, fixed)
        stage_ref[p] = _flash_out(l, acc_ref, tq)
        return carry

    lax.fori_loop(0, N_PAIRS, pair_body, 0)
    for p in range(N_PAIRS):
        o_ref[:, p * LANES:(p + 1) * LANES] = stage_ref[p]


def _dsa_attn(q, k, vt, qp, kp, wi, shift, bias, fixed, tq=256, tk=512):
    B, _, S, _ = q.shape
    topk = min(TOPK_MAX, S // 4)
    once = pl.Buffered(1)
    return pl.pallas_call(
        functools.partial(_dsa_attn_kernel, tq=tq, tk=tk, topk=topk, fixed=fixed),
        grid=(B, S // tq),
        in_specs=[
            pl.BlockSpec((None, N_PAIRS, tq, LANES), lambda b, i: (b, 0, i, 0)),
            pl.BlockSpec((None, H_IDX, 256, tq), lambda b, i: (b, 0, 0, i)),
            pl.BlockSpec((None, H_IDX, tq), lambda b, i: (b, 0, i)),
            pl.BlockSpec((1, tq), lambda b, i: (0, 0)),
            pl.BlockSpec((N_PAIRS, 2, 3, LANES, LANES), lambda b, i: (0, 0, 0, 0, 0), pipeline_mode=once),
            pl.BlockSpec((None, N_PAIRS, S, LANES), lambda b, i: (b, 0, 0, 0), pipeline_mode=once),
            pl.BlockSpec((None, N_PAIRS, S // VT_BLK, LANES, VT_BLK), lambda b, i: (b, 0, 0, 0, 0),
                         pipeline_mode=once),
            pl.BlockSpec((None, S, 256), lambda b, i: (b, 0, 0), pipeline_mode=once),
        ],
        out_specs=pl.BlockSpec((None, tq, N_HEADS * HEAD_DIM), lambda b, i: (b, i, 0)),
        out_shape=jax.ShapeDtypeStruct((B, S, N_HEADS * HEAD_DIM), BF16),
        scratch_shapes=[
            pltpu.VMEM((S // tk, tk, tq), jnp.int32),
            pltpu.VMEM((S // tk, tk, tq), jnp.int16),
            pltpu.VMEM((LANES, 2 * tq), F32),
            pltpu.VMEM((N_PAIRS, tq, LANES), BF16),
            pltpu.VMEM((2, tk, 2 * tq), F32),
        ],
        compiler_params=_cparams(("parallel", "arbitrary"), VMEM_LIMIT_DSA),
        name="dsa_attn",
    )(q, qp, wi, jnp.full((1, tq), shift, F32), bias, k, vt, kp)


def _out_proj_kernel(x_ref, a_ref, w_ref, o_ref):
    o_ref[...] = x_ref[...] + _dot(a_ref[...], w_ref[...])


def _out_proj(x, a, w, tm=512):
    B, S, D = x.shape
    return pl.pallas_call(
        _out_proj_kernel,
        grid=(B, S // tm),
        in_specs=[
            pl.BlockSpec((None, tm, D), lambda b, i: (b, i, 0)),
            pl.BlockSpec((None, tm, a.shape[-1]), lambda b, i: (b, i, 0)),
            pl.BlockSpec(w.shape, lambda b, i: (0, 0)),
        ],
        out_specs=pl.BlockSpec((None, tm, D), lambda b, i: (b, i, 0)),
        out_shape=jax.ShapeDtypeStruct((B, S, D), F32),
        compiler_params=_cparams(("parallel", "parallel")),
        name="out_proj",
    )(x, a, w.astype(BF16))


def _ffn_kernel(x_ref, xp_ref, g_ref, wg_ref, wv_ref, cw_ref, cb_ref, wd_ref, o_ref, acc_ref, he_ref, u_ref,
                *, tm, n_chunks):
    i = pl.program_id(1)
    g = g_ref[...]
    he_ref[:HALO] = (_rms(xp_ref[...], g) * jnp.where(i > 0, 1.0, 0.0)).astype(BF16)
    he_ref[HALO:] = _rms(x_ref[...], g).astype(BF16)
    acc_ref[...] = jnp.zeros(acc_ref.shape, F32)

    def up(c, slot):
        he = he_ref[...]
        u_ref[slot, 0] = _dot(he, wg_ref[c])
        u_ref[slot, 1] = _dot(he, wv_ref[c])

    def conv(u, w, b):
        return (b + w[0:1] * u[HALO - 2:HALO - 2 + tm] + w[1:2] * u[HALO - 1:HALO - 1 + tm]
                + w[2:3] * u[HALO:HALO + tm])

    def down(c, slot):
        cw = cw_ref[c]
        cb = cb_ref[c]
        gate = conv(u_ref[slot, 0], cw[0], cb[0])
        val = conv(u_ref[slot, 1], cw[1], cb[1])
        act = gate * jax.nn.sigmoid(gate) * val
        acc_ref[...] += _dot(act.astype(BF16), wd_ref[c])

    up(0, 0)

    def pair_body(t, carry):
        c = 2 * t
        up(c + 1, 1)
        down(c, 0)
        up(c + 2, 0)
        down(c + 1, 1)
        return carry

    assert n_chunks % 2 == 1
    lax.fori_loop(0, n_chunks // 2, pair_body, 0)
    down(n_chunks - 1, 0)
    o_ref[...] = x_ref[...] + acc_ref[...]


def _ffn(x, g, w_up, conv_w, conv_b, w_down, tm=512):
    B, S, D = x.shape
    nc = D_FF // FF_CHUNK
    wup = w_up.astype(BF16).reshape(D, 2, nc, FF_CHUNK).transpose(1, 2, 0, 3)
    cw = conv_w.reshape(CONV_W, 2, nc, FF_CHUNK).transpose(2, 1, 0, 3)
    cb = conv_b.reshape(2, nc, 1, FF_CHUNK).transpose(1, 0, 2, 3)
    wd = w_down.astype(BF16).reshape(nc, FF_CHUNK, D)
    c3 = lambda b, i: (0, 0, 0)
    c4 = lambda b, i: (0, 0, 0, 0)
    return pl.pallas_call(
        functools.partial(_ffn_kernel, tm=tm, n_chunks=nc),
        grid=(B, S // tm),
        in_specs=[
            pl.BlockSpec((None, tm, D), lambda b, i: (b, i, 0)),
            pl.BlockSpec((None, HALO, D), lambda b, i: (b, jnp.maximum(i * (tm // HALO) - 1, 0), 0)),
            pl.BlockSpec((1, D), lambda b, i: (0, 0)),
            pl.BlockSpec((nc, D, FF_CHUNK), c3),
            pl.BlockSpec((nc, D, FF_CHUNK), c3),
            pl.BlockSpec((nc, 2, CONV_W, FF_CHUNK), c4),
            pl.BlockSpec((nc, 2, 1, FF_CHUNK), c4),
            pl.BlockSpec((nc, FF_CHUNK, D), c3),
        ],
        out_specs=pl.BlockSpec((None, tm, D), lambda b, i: (b, i, 0)),
        out_shape=jax.ShapeDtypeStruct((B, S, D), F32),
        scratch_shapes=[pltpu.VMEM((tm, D), F32), pltpu.VMEM((HALO + tm, D), BF16),
                        pltpu.VMEM((2, 2, HALO + tm, FF_CHUNK), F32)],
        compiler_params=_cparams(("parallel", "parallel")),
        name="conv_ffn",
    )(x, x, g.reshape(1, D), wup[0], wup[1], cw, cb, wd)


def _rope_tables(S, period, off):
    inv = 1.0 / (ROPE_THETA ** (jnp.arange(0, ROPE_DIM, 2, dtype=F32) / ROPE_DIM))
    ang = jnp.arange(S, dtype=F32)[:, None] * inv[None, :]
    cos, sin = jnp.cos(ang), jnp.sin(ang)
    one = jnp.ones((S, 1), F32)
    zero = jnp.zeros((S, 1), F32)

    def lay(x1, x2, fill):
        group = jnp.concatenate([jnp.tile(fill, (1, off)), x1, x2,
                                 jnp.tile(fill, (1, period - off - ROPE_DIM))], axis=1)
        return jnp.tile(group, (1, LANES // period))

    return lay(cos, cos, one), lay(-sin, zero * sin, zero), lay(zero * sin, sin, zero)


def _t5_bucket(rel):
    nb = T5_BUCKETS // 2
    max_exact = nb // 2
    n = jnp.abs(rel)
    large = max_exact + (jnp.log(jnp.maximum(n, 1).astype(F32) / max_exact)
                         / math.log(T5_MAX_DIST / max_exact) * (nb - max_exact)).astype(jnp.int32)
    large = jnp.minimum(large, nb - 1)
    return jnp.where(rel > 0, nb, 0) + jnp.where(n < max_exact, n, large)


def _dsa_bias_tiles(t5_bias):
    kj = jnp.arange(LANES, dtype=jnp.int32)[:, None]
    qi = jnp.arange(LANES, dtype=jnp.int32)[None, :]
    far = t5_bias[_t5_bucket(jnp.int32(-4 * T5_MAX_DIST))]
    same = (jnp.transpose(t5_bias[_t5_bucket(kj - qi)], (2, 0, 1)) - far[:, None, None]) * LOG2E
    prev = (jnp.transpose(t5_bias[_t5_bucket(kj - LANES - qi)], (2, 0, 1)) - far[:, None, None]) * LOG2E
    return jnp.stack([jnp.zeros_like(same), prev, same], axis=1).reshape(N_PAIRS, 2, 3, LANES, LANES)


def _softmax_shift(gq, gk, dim, bias_hi=0.0, bias_lo=0.0):
    qk = 1.02 * dim * jnp.max(jnp.abs(gq)) * jnp.max(jnp.abs(gk))
    top = qk + bias_hi
    fixed = (top + qk - bias_lo) <= SAFE_SPREAD
    return fixed, jnp.where(fixed, -top, 0.0)


def _band_bias(rel_bias, shift, tq):
    pad = LEFT_CHUNKS * CHUNK
    nblk = -(-pad // tq) + 1
    lead = (nblk - 1) * tq
    nk = nblk * tq
    kk = jnp.arange(nk, dtype=jnp.int32)[:, None] - lead
    i = jnp.arange(tq, dtype=jnp.int32)[None, :]
    first = (i // CHUNK) * CHUNK - pad
    ok = (kk >= first) & (kk < (i // CHUNK + 1) * CHUNK)
    L = nk + tq
    dist = jnp.arange(L, dtype=jnp.int32)
    dist = lead + jnp.where(dist < tq, dist, dist - L)
    v = rel_bias[:, jnp.clip(dist, -(CHUNK - 1), REL_MAX_PAST) + CHUNK - 1]
    b = jnp.tile(v, (1, nk))[:, :nk * (L - 1)].reshape(v.shape[0], nk, L - 1)[:, :, :tq]
    b = jnp.where(ok[None], b + shift, NEG).reshape(N_PAIRS, 2, nblk, tq, tq)
    return jnp.transpose(b, (0, 2, 3, 1, 4)).reshape(N_PAIRS, nblk, tq, 2 * tq)


def kernel(x, norm_mix, norm_ffn, t5_bias, a_w_in, a_q_norm, a_k_norm, a_kidx_norm, a_w_out, b_w_down, b_q_a_norm, b_kv_a_norm, b_w_uq, b_w_ukv, b_q_norm, b_k_norm, b_w_out, c_w_in, c_q_norm, c_k_norm, c_rel_bias, c_w_out, f_w_up, f_conv_w, f_conv_b, f_w_down):
    B, S, D = x.shape
    depth = norm_mix.shape[0]

    tq_c = 256
    tabs_idx = _rope_tables(S, D_IDX, D_IDX - ROPE_DIM)
    tabs_mla = _rope_tables(S, LANES, QK_NOPE)
    scale = HEAD_DIM ** -0.5 * LOG2E
    for layer in range(depth):
        kind, j = layer % N_MIXERS, layer // N_MIXERS
        g = norm_mix[layer]
        if kind == 0:
            gq = a_q_norm[j] * scale
            q, k, vt = _qkv_proj(x, g, a_w_in[j][:, :A_QKV], gq, a_k_norm[j])
            qp, kp, wi = _idx_proj(x, g, a_w_in[j][:, A_QKV:], a_kidx_norm[j], tabs_idx)
            tiles = _dsa_bias_tiles(t5_bias)
            fixed, shift = _softmax_shift(gq, a_k_norm[j], HEAD_DIM, jnp.max(tiles), jnp.min(tiles))
            att = lax.cond(fixed, functools.partial(_dsa_attn, fixed=True), functools.partial(_dsa_attn, fixed=False),
                           q, k, vt, qp, kp, wi, shift, tiles)
            w_out = a_w_out[j]
        elif kind == 1:
            fixed, shift = _softmax_shift(b_q_norm[j] * (B_QK ** -0.5 * LOG2E), b_k_norm[j], B_QK)
            q, k, vt = _mla_proj(x, g, b_w_down[j], b_q_a_norm[j], b_kv_a_norm[j], b_w_uq[j], b_w_ukv[j],
                                 b_q_norm[j], b_k_norm[j], shift, tabs_mla)
            att = lax.cond(fixed, functools.partial(_mla_attn, fixed=True), functools.partial(_mla_attn, fixed=False),
                           q, k, vt)
            w_out = b_w_out[j]
        else:
            gq = c_q_norm[j] * scale
            q, k, vt = _qkv_proj(x, g, c_w_in[j], gq, c_k_norm[j])
            rel = c_rel_bias[j] * LOG2E
            fixed, shift = _softmax_shift(gq, c_k_norm[j], HEAD_DIM, jnp.max(rel), jnp.min(rel))
            att = lax.cond(fixed, functools.partial(_band_attn, fixed=True, tq=tq_c),
                           functools.partial(_band_attn, fixed=False, tq=tq_c),
                           q, k, vt, _band_bias(rel, shift, tq_c))
            w_out = c_w_out[j]
        x = _out_proj(x, att, w_out)
        x = _ffn(x, norm_ffn[layer], f_w_up[layer], f_conv_w[layer], f_conv_b[layer], f_w_down[layer])
    return x
```

```python
import functools
import math

import numpy as np
import jax
import jax.numpy as jnp
from jax import lax
from jax.experimental import pallas as pl
from jax.experimental.pallas import tpu as pltpu

F32 = jnp.float32
BF16 = jnp.bfloat16

D_MODEL = 1024
CHUNK = 64
EPS = 1e-6
ROPE_THETA = 10000.0
NEG = -1e30
NEG_BITS = int(np.float32(NEG).view(np.int32))
M_INIT = -1e38
INT_MIN = -(2 ** 31)
LOG2E = math.log2(math.e)
SAFE_SPREAD = 100.0

N_MIXERS = 3
N_HEADS = 16
HEAD_DIM = 64
N_PAIRS = N_HEADS // 2
LANES = 128
VT_BLK = 256

H_IDX = 8
D_IDX = 64
ROPE_DIM = 32
TOPK_MAX = 256
T5_BUCKETS = 32
T5_MAX_DIST = 128
A_QKV = 3 * N_HEADS * HEAD_DIM
Q_LORA = 256
KV_LORA = 128
QK_NOPE = 64
QK_ROPE = 32
V_DIM = 64
B_QK = QK_NOPE + QK_ROPE
LEFT_CHUNKS = 8
REL_MAX_PAST = 128
D_FF = 2816
CONV_W = 3
FF_CHUNK = 256
HALO = 8

VMEM_LIMIT = 56 * 1024 * 1024
VMEM_LIMIT_DSA = 60 * 1024 * 1024


def _cparams(sem, vmem_limit=VMEM_LIMIT):
    return pltpu.CompilerParams(dimension_semantics=sem, vmem_limit_bytes=vmem_limit)


def _rms(x, g):
    return x * lax.rsqrt(jnp.mean(x * x, axis=-1, keepdims=True) + EPS) * g


def _dot(a, b):
    return jnp.dot(a, b, preferred_element_type=F32)


def _split_bf16(x):
    hi = x.astype(BF16)
    lo = (x - hi.astype(F32)).astype(BF16)
    return hi, lo


def _group_sumsq(y, bd):
    hi, lo = _split_bf16(y * y)
    return _dot(hi, bd) + _dot(lo, bd)


def _rope(y, ct, s1, s2):
    return y * ct + pltpu.roll(y, LANES - 16, 1) * s1 + pltpu.roll(y, 16, 1) * s2


def _rope_wide(y, ct, s1, s2):
    w = y.shape[-1]
    parts = [_rope(y[:, c:c + LANES], ct, s1, s2) for c in range(0, w, LANES)]
    return parts[0] if len(parts) == 1 else jnp.concatenate(parts, axis=-1)


def _store_vt(vt_ref, p, y):
    yt = y.T.astype(BF16)
    for r in range(y.shape[0] // VT_BLK):
        vt_ref[p, r] = yt[:, r * VT_BLK:(r + 1) * VT_BLK]


def _qkv_proj_kernel(x_ref, g_ref, w_ref, gq_ref, gk_ref, bd_ref, q_ref, k_ref, vt_ref):
    xb = _rms(x_ref[...], g_ref[...]).astype(BF16)
    bd = bd_ref[...]
    hd = N_HEADS * HEAD_DIM
    for which, (o_ref, gain_ref) in enumerate(((q_ref, gq_ref), (k_ref, gk_ref), (vt_ref, None))):
        for c in range(hd // 256):
            col = which * hd + c * 256
            y = _dot(xb, w_ref[:, col:col + 256])
            if gain_ref is None:
                _store_vt(o_ref, 2 * c, y[:, :LANES])
                _store_vt(o_ref, 2 * c + 1, y[:, LANES:])
            else:
                y = y * lax.rsqrt(_group_sumsq(y, bd) * (1.0 / HEAD_DIM) + EPS) * gain_ref[...]
                yb = y.astype(BF16)
                o_ref[2 * c] = yb[:, :LANES]
                o_ref[2 * c + 1] = yb[:, LANES:]


def _vt_spec(tm):
    return pl.BlockSpec((None, N_PAIRS, tm // VT_BLK, LANES, VT_BLK), lambda b, i: (b, 0, i, 0, 0))


def _vt_shape(B, S):
    return jax.ShapeDtypeStruct((B, N_PAIRS, S // VT_BLK, LANES, VT_BLK), BF16)


def _qkv_proj(x, g, w, gq, gk, tm=512):
    B, S, D = x.shape
    hd = N_HEADS * HEAD_DIM
    bd = jnp.kron(jnp.eye(256 // HEAD_DIM, dtype=F32), jnp.ones((HEAD_DIM, HEAD_DIM), F32)).astype(BF16)
    out = jax.ShapeDtypeStruct((B, N_PAIRS, S, LANES), BF16)
    const = lambda b, i: (0, 0)
    ospec = pl.BlockSpec((None, N_PAIRS, tm, LANES), lambda b, i: (b, 0, i, 0))
    return pl.pallas_call(
        _qkv_proj_kernel,
        grid=(B, S // tm),
        in_specs=[
            pl.BlockSpec((None, tm, D), lambda b, i: (b, i, 0)),
            pl.BlockSpec((1, D), const),
            pl.BlockSpec((D, 3 * hd), const),
            pl.BlockSpec((1, 256), const),
            pl.BlockSpec((1, 256), const),
            pl.BlockSpec((256, 256), const),
        ],
        out_specs=[ospec, ospec, _vt_spec(tm)],
        out_shape=[out, out, _vt_shape(B, S)],
        compiler_params=_cparams(("parallel", "parallel")),
        name="qkv_proj",
    )(x, g.reshape(1, D), w.astype(BF16), jnp.tile(gq, 4).reshape(1, 256), jnp.tile(gk, 4).reshape(1, 256), bd)


def _idx_proj_kernel(x_ref, g_ref, wh_ref, wl_ref, gki_ref, ct_ref, s1_ref, s2_ref, qp_ref, kp_ref, wi_ref,
                     *, w_scale):
    xn = _rms(x_ref[...], g_ref[...])
    xh, xl = _split_bf16(xn)

    def mm(c0, c1):
        wh = wh_ref[:, c0:c1]
        return _dot(xh, wh) + (_dot(xl, wh) + _dot(xh, wl_ref[:, c0:c1]))

    ct, s1, s2 = ct_ref[...], s1_ref[...], s2_ref[...]
    tm = xn.shape[0]
    low = lax.broadcasted_iota(jnp.int32, (tm, LANES), 1) < D_IDX
    zero = jnp.zeros((tm, LANES), F32)

    for c in range(H_IDX // 2):
        y = _rope(mm(c * LANES, (c + 1) * LANES), ct, s1, s2)
        hi = y.astype(BF16).astype(F32)
        lo = y - hi
        rhi = pltpu.roll(hi, D_IDX, 1)
        rlo = pltpu.roll(lo, D_IDX, 1)
        qp_ref[2 * c, :LANES, :] = jnp.where(low, hi, rlo).T.astype(BF16)
        qp_ref[2 * c, LANES:, :] = jnp.where(low, hi, zero).T.astype(BF16)
        qp_ref[2 * c + 1, :LANES, :] = jnp.where(low, rhi, lo).T.astype(BF16)
        qp_ref[2 * c + 1, LANES:, :] = jnp.where(low, rhi, zero).T.astype(BF16)

    c0 = H_IDX * D_IDX
    y = mm(c0, c0 + LANES)
    y = y * lax.rsqrt(jnp.sum(y * y, axis=-1, keepdims=True) * (1.0 / D_IDX) + EPS) * gki_ref[...]
    y = _rope(y, ct, s1, s2)
    hi = y.astype(BF16).astype(F32)
    lo = y - hi
    kp_ref[:, :LANES] = jnp.where(low, hi, pltpu.roll(hi, D_IDX, 1)).astype(BF16)
    kp_ref[:, LANES:] = jnp.where(low, lo, zero).astype(BF16)

    wi_ref[...] = (mm(c0 + LANES, c0 + 2 * LANES) * w_scale).T[:H_IDX]


def _idx_proj(x, g, w_idx, gki, tabs, tm=512):
    B, S, D = x.shape
    nq = H_IDX * D_IDX
    wq, wk, ww = w_idx[:, :nq], w_idx[:, nq:nq + D_IDX], w_idx[:, nq + D_IDX:]
    wpad = jnp.concatenate([
        wq, jnp.pad(wk, ((0, 0), (0, LANES - D_IDX))), jnp.pad(ww, ((0, 0), (0, LANES - H_IDX)))], axis=1)
    wh = wpad.astype(BF16)
    wl = (wpad - wh.astype(F32)).astype(BF16)
    ncol = nq + 2 * LANES
    gpad = jnp.pad(gki, (0, LANES - D_IDX)).reshape(1, LANES)
    const = lambda b, i: (0, 0)
    tspec = pl.BlockSpec((tm, LANES), lambda b, i: (i, 0))
    return pl.pallas_call(
        functools.partial(_idx_proj_kernel, w_scale=(D_IDX ** -0.5) * (H_IDX ** -0.5)),
        grid=(B, S // tm),
        in_specs=[
            pl.BlockSpec((None, tm, D), lambda b, i: (b, i, 0)),
            pl.BlockSpec((1, D), const),
            pl.BlockSpec((D, ncol), const),
            pl.BlockSpec((D, ncol), const),
            pl.BlockSpec((1, LANES), const),
            tspec, tspec, tspec,
        ],
        out_specs=[
            pl.BlockSpec((None, H_IDX, 256, tm), lambda b, i: (b, 0, 0, i)),
            pl.BlockSpec((None, tm, 256), lambda b, i: (b, i, 0)),
            pl.BlockSpec((None, H_IDX, tm), lambda b, i: (b, 0, i)),
        ],
        out_shape=[
            jax.ShapeDtypeStruct((B, H_IDX, 256, S), BF16),
            jax.ShapeDtypeStruct((B, S, 256), BF16),
            jax.ShapeDtypeStruct((B, H_IDX, S), F32),
        ],
        compiler_params=_cparams(("parallel", "parallel")),
        name="idx_proj",
    )(x, g.reshape(1, D), wh, wl, gpad, *tabs)


def _mla_proj_kernel(x_ref, g_ref, wd_ref, gqa_ref, gkva_ref, wuq_ref, wuk_ref, wuv_ref, gq_ref, gk_ref, bd_ref,
                     qfix_ref, kfix_ref, ct_ref, s1_ref, s2_ref, q_ref, k_ref, vt_ref):
    xb = _rms(x_ref[...], g_ref[...]).astype(BF16)
    d = _dot(xb, wd_ref[...])
    cq = _rms(d[:, :Q_LORA], gqa_ref[...]).astype(BF16)
    ckv = _rms(d[:, Q_LORA:Q_LORA + KV_LORA], gkva_ref[...]).astype(BF16)
    krz = pltpu.roll(d[:, Q_LORA + KV_LORA:], QK_NOPE, 1)
    krz2 = jnp.concatenate([krz, krz], axis=-1)
    bd = bd_ref[...]
    ct, s1, s2 = ct_ref[...], s1_ref[...], s2_ref[...]
    inv = 1.0 / B_QK
    for p in range(N_PAIRS):
        cols = slice(p * 256, (p + 1) * 256)
        y = _dot(cq, wuq_ref[:, cols])
        y = y * lax.rsqrt(_group_sumsq(y, bd) * inv + EPS) * gq_ref[...]
        q_ref[p] = (_rope_wide(y, ct, s1, s2) + qfix_ref[...]).astype(BF16)
        y = _dot(ckv, wuk_ref[:, cols]) + krz2
        y = y * lax.rsqrt(_group_sumsq(y, bd) * inv + EPS) * gk_ref[...]
        k_ref[p] = (_rope_wide(y, ct, s1, s2) + kfix_ref[...]).astype(BF16)
        _store_vt(vt_ref, p, _dot(ckv, wuv_ref[:, p * LANES:(p + 1) * LANES]))


def _mla_proj(x, g, w_down, gqa, gkva, w_uq, w_ukv, gq, gk, shift, tabs, tm=512):
    B, S, D = x.shape
    nd = Q_LORA + KV_LORA + QK_ROPE
    wd = jnp.pad(w_down, ((0, 0), (0, 512 - nd))).astype(BF16)
    padh = LANES - B_QK
    wuq = jnp.pad(w_uq.reshape(Q_LORA, N_HEADS, B_QK), ((0, 0), (0, 0), (0, padh))).reshape(Q_LORA, N_HEADS * LANES)
    wukv = w_ukv.reshape(KV_LORA, N_HEADS, QK_NOPE + V_DIM)
    wuk = jnp.pad(wukv[:, :, :QK_NOPE], ((0, 0), (0, 0), (0, LANES - QK_NOPE))).reshape(KV_LORA, N_HEADS * LANES)
    wuv = wukv[:, :, QK_NOPE:].reshape(KV_LORA, N_HEADS * V_DIM)
    gq2 = jnp.tile(jnp.pad(gq * (B_QK ** -0.5 * LOG2E), (0, padh)), 2).reshape(1, 256)
    gk2 = jnp.tile(jnp.pad(gk, (0, padh)), 2).reshape(1, 256)
    bd = jnp.kron(jnp.eye(2, dtype=F32), jnp.ones((LANES, LANES), F32)).astype(BF16)
    kfix = jnp.asarray((np.arange(256) % LANES == B_QK).astype(np.float32)).reshape(1, 256)
    qfix = kfix * shift
    const = lambda b, i: (0, 0)
    tspec = pl.BlockSpec((tm, LANES), lambda b, i: (i, 0))
    qk_shape = jax.ShapeDtypeStruct((B, N_PAIRS, S, 256), BF16)
    qk_spec = pl.BlockSpec((None, N_PAIRS, tm, 256), lambda b, i: (b, 0, i, 0))
    return pl.pallas_call(
        _mla_proj_kernel,
        grid=(B, S // tm),
        in_specs=[
            pl.BlockSpec((None, tm, D), lambda b, i: (b, i, 0)),
            pl.BlockSpec((1, D), const),
            pl.BlockSpec((D, 512), const),
            pl.BlockSpec((1, Q_LORA), const),
            pl.BlockSpec((1, KV_LORA), const),
            pl.BlockSpec((Q_LORA, N_HEADS * LANES), const),
            pl.BlockSpec((KV_LORA, N_HEADS * LANES), const),
            pl.BlockSpec((KV_LORA, N_HEADS * V_DIM), const),
            pl.BlockSpec((1, 256), const),
            pl.BlockSpec((1, 256), const),
            pl.BlockSpec((256, 256), const),
            pl.BlockSpec((1, 256), const),
            pl.BlockSpec((1, 256), const),
            tspec, tspec, tspec,
        ],
        out_specs=[qk_spec, qk_spec, _vt_spec(tm)],
        out_shape=[qk_shape, qk_shape, _vt_shape(B, S)],
        compiler_params=_cparams(("parallel", "parallel")),
        name="mla_proj",
    )(x, g.reshape(1, D), wd, gqa.reshape(1, Q_LORA), gkva.reshape(1, KV_LORA), wuq.astype(BF16),
      wuk.astype(BF16), wuv.astype(BF16), gq2, gk2, bd, qfix, kfix, *tabs)


def _stack_heads(q2, half):
    qf = q2.astype(F32)
    low = lax.broadcasted_iota(jnp.int32, qf.shape, 1) < half
    zero = jnp.zeros_like(qf)
    return jnp.concatenate([jnp.where(low, qf, zero).T, jnp.where(low, zero, qf).T], axis=1).astype(BF16)


def _flash_init(acc_ref):
    r = acc_ref.shape[1]
    acc_ref[...] = jnp.zeros(acc_ref.shape, F32)
    return jnp.full((1, r), M_INIT, F32), jnp.zeros((1, r), F32)


def _consume(s, vt, m, l, acc_ref, fixed):
    if fixed:
        p = jnp.exp2(s)
        acc_ref[...] += _dot(vt, p.astype(BF16))
        return m, l + jnp.sum(p, axis=0, keepdims=True)
    m_new = jnp.maximum(m, jnp.max(s, axis=0, keepdims=True))
    alpha = jnp.exp2(m - m_new)
    p = jnp.exp2(s - m_new)
    l_new = alpha * l + jnp.sum(p, axis=0, keepdims=True)
    acc_ref[...] = alpha * acc_ref[...] + _dot(vt, p.astype(BF16))
    return m_new, l_new


def _sweep(first, n, s_ref, scores, values, ml, acc_ref, fixed, peel=True):
    def one(j, s, ml):
        return _consume(s, values(j), *ml, acc_ref, fixed)

    if isinstance(n, int):
        tiles = [scores(first + j) for j in range(n)]
        for j in range(n):
            ml = one(first + j, tiles[j], ml)
        return ml

    odd = n % 2 if peel else 0
    if peel:
        ml = lax.cond(odd == 1, lambda ml: one(first, scores(first), ml), lambda ml: ml, ml)
    j0 = first + odd
    last = first + n - 1

    @pl.when(n >= 2)
    def _():
        s_ref[0] = scores(j0)

    def body(t, ml):
        j = j0 + 2 * t
        s1 = scores(j + 1)
        ml = one(j, s_ref[0], ml)
        s_ref[1] = s1
        s0 = scores(jnp.minimum(j + 2, last))
        ml = one(j + 1, s_ref[1], ml)
        s_ref[0] = s0
        return ml

    return lax.fori_loop(0, n // 2, body, ml)


def _flash_out(l, acc_ref, tq):
    o = acc_ref[...] / l
    ot = jnp.concatenate([o[:HEAD_DIM, :tq], o[HEAD_DIM:, tq:]], axis=0)
    return ot.T.astype(BF16)


def _load_vt(vt_ref, kb, tk):
    n = tk // VT_BLK
    parts = [vt_ref[kb * n + r] for r in range(n)]
    return parts[0] if n == 1 else jnp.concatenate(parts, axis=1)


def _band_attn_kernel(q_ref, k_ref, vt_ref, bias_ref, o_ref, acc_ref, s_ref, *, tq, nblk, fixed):
    i = pl.program_id(2)
    qs = _stack_heads(q_ref[...], HEAD_DIM)
    lead = i - (nblk - 1)

    def scores(kb):
        ks = pl.multiple_of(kb * tq, tq)
        return _dot(k_ref[pl.ds(ks, tq), :], qs) + bias_ref[kb - lead]

    values = lambda kb: _load_vt(vt_ref, kb, tq)
    ml0 = _flash_init(acc_ref)
    _, l = lax.cond(
        lead >= 0,
        lambda ml: _sweep(lead, nblk, s_ref, scores, values, ml, acc_ref, fixed),
        lambda ml: _sweep(0, i + 1, s_ref, scores, values, ml, acc_ref, fixed),
        ml0)
    o_ref[...] = _flash_out(l, acc_ref, tq)


def _band_attn(q, k, vt, bias, fixed, tq=256):
    B, _, S, _ = q.shape
    nblk = bias.shape[1]
    return pl.pallas_call(
        functools.partial(_band_attn_kernel, tq=tq, nblk=nblk, fixed=fixed),
        grid=(B, N_PAIRS, S // tq),
        in_specs=[
            pl.BlockSpec((None, None, tq, LANES), lambda b, p, i: (b, p, i, 0)),
            pl.BlockSpec((None, None, S, LANES), lambda b, p, i: (b, p, 0, 0)),
            pl.BlockSpec((None, None, S // VT_BLK, LANES, VT_BLK), lambda b, p, i: (b, p, 0, 0, 0)),
            pl.BlockSpec((None, nblk, tq, 2 * tq), lambda b, p, i: (p, 0, 0, 0)),
        ],
        out_specs=pl.BlockSpec((None, tq, LANES), lambda b, p, i: (b, i, p)),
        out_shape=jax.ShapeDtypeStruct((B, S, N_HEADS * HEAD_DIM), BF16),
        scratch_shapes=[pltpu.VMEM((LANES, 2 * tq), F32), pltpu.VMEM((2, tq, 2 * tq), F32)],
        compiler_params=_cparams(("parallel", "parallel", "arbitrary")),
        name="band_attn",
    )(q, k, vt, bias)


def _mla_attn_kernel(q_ref, k_ref, vt_ref, o_ref, acc_ref, s_ref, *, tq, tk, fixed):
    i = pl.program_id(2)
    qs = _stack_heads(q_ref[...], LANES)
    t0 = i * tq
    assert tq % (2 * tk) == 0
    n_full = t0 // tk
    n_edge = tq // tk
    lim = ((t0 + lax.broadcasted_iota(jnp.int32, (1, tq), 1)) // CHUNK + 1) * CHUNK

    def scores(kb):
        ks = pl.multiple_of(kb * tk, tk)
        return _dot(k_ref[pl.ds(ks, tk), :], qs)

    def edge_scores(kb):
        row = kb * tk + lax.broadcasted_iota(jnp.int32, (tk, tq), 0)
        add = jnp.where(row < lim, 0.0, NEG).astype(F32)
        return scores(kb) + jnp.concatenate([add, add], axis=1)

    values = lambda kb: _load_vt(vt_ref, kb, tk)
    ml = _sweep(0, n_full, s_ref, scores, values, _flash_init(acc_ref), acc_ref, fixed, peel=False)
    _, l = _sweep(n_full, n_edge, s_ref, edge_scores, values, ml, acc_ref, fixed, peel=False)
    o_ref[...] = _flash_out(l, acc_ref, tq)


def _mla_attn(q, k, vt, fixed, tq=1024, tk=256):
    B, _, S, _ = q.shape
    tk = min(tk, S)
    return pl.pallas_call(
        functools.partial(_mla_attn_kernel, tq=tq, tk=tk, fixed=fixed),
        grid=(B, N_PAIRS, S // tq),
        in_specs=[
            pl.BlockSpec((None, None, tq, 256), lambda b, p, i: (b, p, i, 0)),
            pl.BlockSpec((None, None, S, 256), lambda b, p, i: (b, p, 0, 0)),
            pl.BlockSpec((None, None, S // VT_BLK, LANES, VT_BLK), lambda b, p, i: (b, p, 0, 0, 0)),
        ],
        out_specs=pl.BlockSpec((None, tq, LANES), lambda b, p, i: (b, i, p)),
        out_shape=jax.ShapeDtypeStruct((B, S, N_HEADS * V_DIM), BF16),
        scratch_shapes=[pltpu.VMEM((LANES, 2 * tq), F32), pltpu.VMEM((2, tk, 2 * tq), F32)],
        compiler_params=_cparams(("parallel", "parallel", "arbitrary")),
        name="mla_attn",
    )(q, k, vt)


def _dsa_attn_kernel(q_ref, qp_ref, wi_ref, shift_ref, bias_ref, k_ref, vt_ref, kp_ref, o_ref,
                     sc_ref, h_ref, acc_ref, stage_ref, s_ref, *, tq, tk, topk, fixed):
    i = pl.program_id(1)
    t0 = i * tq
    n_blk = (t0 + tq + tk - 1) // tk
    n_far = jnp.maximum((t0 // LANES - 1) // (tk // LANES), 0)
    n_far_even = n_far - n_far % 2
    lim = ((t0 + lax.broadcasted_iota(jnp.int32, (1, tq), 1)) // CHUNK + 1) * CHUNK

    wv = wi_ref[...]

    def score_body(kb, carry):
        ks = pl.multiple_of(kb * tk, tk)
        kp = kp_ref[pl.ds(ks, tk), :]
        acc = jnp.zeros((tk, tq), F32)
        for h in range(H_IDX):
            acc = acc + jnp.maximum(_dot(kp, qp_ref[h]), 0.0) * wv[h:h + 1, :]
        row = ks + lax.broadcasted_iota(jnp.int32, (tk, tq), 0)
        acc = jnp.where(row < lim, acc, -jnp.inf)
        bits = pltpu.bitcast(acc, jnp.int32)
        key = bits ^ ((bits >> 31) & 0x7FFFFFFF)
        sc_ref[kb] = key
        h_ref[kb] = (key >> 16).astype(jnp.int16)
        return carry

    lax.fori_loop(0, n_blk, score_body, 0)

    def count16(pred):
        def count_body(kb, part):
            hit = jnp.where(pred(h_ref[kb]), jnp.int16(1), jnp.int16(0))
            rows = hit[0:16]
            for r in range(16, tk, 16):
                rows = rows + hit[r:r + 16]
            return part + rows.astype(jnp.int32)

        part = lax.fori_loop(0, n_blk, count_body, jnp.zeros((16, tq), jnp.int32))
        return jnp.sum(part, axis=0, keepdims=True)

    def kth16(need):
        def bit_body(t, ans):
            cand = ans + jnp.left_shift(jnp.int32(1), 15 - t)
            cand16 = cand.astype(jnp.int16)
            return jnp.where(count16(lambda h: h >= cand16) >= need, cand, ans)

        return lax.fori_loop(0, 16, bit_body, jnp.full((1, tq), -(2 ** 15), jnp.int32))

    top = kth16(topk)
    top16 = top.astype(jnp.int16)
    need = topk - count16(lambda h: h > top16)

    def low_body(kb, carry):
        low = ((sc_ref[kb] & 0xFFFF) - 2 ** 15).astype(jnp.int16)
        h_ref[kb] = jnp.where(h_ref[kb] == top16, low, jnp.int16(-(2 ** 15)))
        return carry

    lax.fori_loop(0, n_blk, low_body, 0)
    low = kth16(need)
    thr = jnp.left_shift(top, 16) + (low + 2 ** 15)

    low16 = low.astype(jnp.int16)
    excess = count16(lambda h: h >= low16) - need

    def tie_cut():
        def count32(pred):
            def body(kb, part):
                row = kb * tk + lax.broadcasted_iota(jnp.int32, (tk, tq), 0)
                hit = jnp.where(pred(sc_ref[kb], row), 1, 0)
                return part + jnp.sum(hit.reshape(tk // 8, 8, tq), axis=0)

            part = lax.fori_loop(0, n_blk, body, jnp.zeros((8, tq), jnp.int32))
            return jnp.sum(part, axis=0, keepdims=True)

        def count_eq_before(pos):
            return count32(lambda key, row: (key == thr) & (row < pos))

        keep = topk - count32(lambda key, row: key > thr)

        def bit_body(t, ans):
            cand = ans + jnp.left_shift(jnp.int32(1), 15 - t)
            return jnp.where(count_eq_before(cand) <= keep, cand, ans)

        return lax.fori_loop(0, 16, bit_body, jnp.zeros((1, tq), jnp.int32))

    cut = lax.cond(jnp.max(excess) > 0, tie_cut, lambda: jnp.full((1, tq), 2 ** 16 - 1, jnp.int32))

    keep_bits = pltpu.bitcast(shift_ref[...], jnp.int32)

    def mask_body(kb, carry):
        row = kb * tk + lax.broadcasted_iota(jnp.int32, (tk, tq), 0)
        thr_row = jnp.where(row >= cut, thr + 1, thr)
        sc_ref[kb] = jnp.where((sc_ref[kb] >= thr_row) & (row < lim), keep_bits, NEG_BITS)
        return carry

    lax.fori_loop(0, n_blk, mask_body, 0)

    def pair_body(p, carry):
        qs = _stack_heads(q_ref[p], HEAD_DIM)

        def far_scores(kb):
            ks = pl.multiple_of(kb * tk, tk)
            am = pltpu.bitcast(sc_ref[kb], F32)
            return _dot(k_ref[p, pl.ds(ks, tk), :], qs) + jnp.concatenate([am, am], axis=1)

        def near_scores(kb):
            cols = []
            for hh in range(2):
                for a in range(tq // LANES):
                    tiles = []
                    for c in range(tk // LANES):
                        d = (kb * (tk // LANES) + c) - (i * (tq // LANES) + a)
                        kind = jnp.where(d == 0, 2, jnp.where(d == -1, 1, 0))
                        tiles.append(bias_ref[p, hh, kind])
                    cols.append(jnp.concatenate(tiles, axis=0))
            return far_scores(kb) + jnp.concatenate(cols, axis=1)

        values = lambda kb: _load_vt(vt_ref.at[p], kb, tk)
        _, l = _sweep(0, n_blk, s_ref, near_scores, values, _flash_init(acc_ref), acc_ref, fixed)
        stage_ref[p] = _flash_out(l, acc_ref, tq)
        return carry

    lax.fori_loop(0, N_PAIRS, pair_body, 0)
    for p in range(N_PAIRS):
        o_ref[:, p * LANES:(p + 1) * LANES] = stage_ref[p]


def _dsa_attn(q, k, vt, qp, kp, wi, shift, bias, fixed, tq=256, tk=512):
    B, _, S, _ = q.shape
    topk = min(TOPK_MAX, S // 4)
    once = pl.Buffered(1)
    return pl.pallas_call(
        functools.partial(_dsa_attn_kernel, tq=tq, tk=tk, topk=topk, fixed=fixed),
        grid=(B, S // tq),
        in_specs=[
            pl.BlockSpec((None, N_PAIRS, tq, LANES), lambda b, i: (b, 0, i, 0)),
            pl.BlockSpec((None, H_IDX, 256, tq), lambda b, i: (b, 0, 0, i)),
            pl.BlockSpec((None, H_IDX, tq), lambda b, i: (b, 0, i)),
            pl.BlockSpec((1, tq), lambda b, i: (0, 0)),
            pl.BlockSpec((N_PAIRS, 2, 3, LANES, LANES), lambda b, i: (0, 0, 0, 0, 0), pipeline_mode=once),
            pl.BlockSpec((None, N_PAIRS, S, LANES), lambda b, i: (b, 0, 0, 0), pipeline_mode=once),
            pl.BlockSpec((None, N_PAIRS, S // VT_BLK, LANES, VT_BLK), lambda b, i: (b, 0, 0, 0, 0),
                         pipeline_mode=once),
            pl.BlockSpec((None, S, 256), lambda b, i: (b, 0, 0), pipeline_mode=once),
        ],
        out_specs=pl.BlockSpec((None, tq, N_HEADS * HEAD_DIM), lambda b, i: (b, i, 0)),
        out_shape=jax.ShapeDtypeStruct((B, S, N_HEADS * HEAD_DIM), BF16),
        scratch_shapes=[
            pltpu.VMEM((S // tk, tk, tq), jnp.int32),
            pltpu.VMEM((S // tk, tk, tq), jnp.int16),
            pltpu.VMEM((LANES, 2 * tq), F32),
            pltpu.VMEM((N_PAIRS, tq, LANES), BF16),
            pltpu.VMEM((2, tk, 2 * tq), F32),
        ],
        compiler_params=_cparams(("parallel", "arbitrary"), VMEM_LIMIT_DSA),
        name="dsa_attn",
    )(q, qp, wi, jnp.full((1, tq), shift, F32), bias, k, vt, kp)


def _out_proj_kernel(x_ref, a_ref, w_ref, o_ref):
    o_ref[...] = x_ref[...] + _dot(a_ref[...], w_ref[...])


def _out_proj(x, a, w, tm=512):
    B, S, D = x.shape
    return pl.pallas_call(
        _out_proj_kernel,
        grid=(B, S // tm),
        in_specs=[
            pl.BlockSpec((None, tm, D), lambda b, i: (b, i, 0)),
            pl.BlockSpec((None, tm, a.shape[-1]), lambda b, i: (b, i, 0)),
            pl.BlockSpec(w.shape, lambda b, i: (0, 0)),
        ],
        out_specs=pl.BlockSpec((None, tm, D), lambda b, i: (b, i, 0)),
        out_shape=jax.ShapeDtypeStruct((B, S, D), F32),
        compiler_params=_cparams(("parallel", "parallel")),
        name="out_proj",
    )(x, a, w.astype(BF16))


def _ffn_kernel(x_ref, xp_ref, g_ref, wg_ref, wv_ref, cw_ref, cb_ref, wd_ref, o_ref, acc_ref, he_ref, u_ref,
                *, tm, n_chunks):
    i = pl.program_id(1)
    g = g_ref[...]
    he_ref[:HALO] = (_rms(xp_ref[...], g) * jnp.where(i > 0, 1.0, 0.0)).astype(BF16)
    he_ref[HALO:] = _rms(x_ref[...], g).astype(BF16)
    acc_ref[...] = jnp.zeros(acc_ref.shape, F32)

    def up(c, slot):
        he = he_ref[...]
        u_ref[slot, 0] = _dot(he, wg_ref[c])
        u_ref[slot, 1] = _dot(he, wv_ref[c])

    def conv(u, w, b):
        return (b + w[0:1] * u[HALO - 2:HALO - 2 + tm] + w[1:2] * u[HALO - 1:HALO - 1 + tm]
                + w[2:3] * u[HALO:HALO + tm])

    def down(c, slot):
        cw = cw_ref[c]
        cb = cb_ref[c]
        gate = conv(u_ref[slot, 0], cw[0], cb[0])
        val = conv(u_ref[slot, 1], cw[1], cb[1])
        act = gate * jax.nn.sigmoid(gate) * val
        acc_ref[...] += _dot(act.astype(BF16), wd_ref[c])

    up(0, 0)

    def pair_body(t, carry):
        c = 2 * t
        up(c + 1, 1)
        down(c, 0)
        up(c + 2, 0)
        down(c + 1, 1)
        return carry

    assert n_chunks % 2 == 1
    lax.fori_loop(0, n_chunks // 2, pair_body, 0)
    down(n_chunks - 1, 0)
    o_ref[...] = x_ref[...] + acc_ref[...]


def _ffn(x, g, w_up, conv_w, conv_b, w_down, tm=512):
    B, S, D = x.shape
    nc = D_FF // FF_CHUNK
    wup = w_up.astype(BF16).reshape(D, 2, nc, FF_CHUNK).transpose(1, 2, 0, 3)
    cw = conv_w.reshape(CONV_W, 2, nc, FF_CHUNK).transpose(2, 1, 0, 3)
    cb = conv_b.reshape(2, nc, 1, FF_CHUNK).transpose(1, 0, 2, 3)
    wd = w_down.astype(BF16).reshape(nc, FF_CHUNK, D)
    c3 = lambda b, i: (0, 0, 0)
    c4 = lambda b, i: (0, 0, 0, 0)
    return pl.pallas_call(
        functools.partial(_ffn_kernel, tm=tm, n_chunks=nc),
        grid=(B, S // tm),
        in_specs=[
            pl.BlockSpec((None, tm, D), lambda b, i: (b, i, 0)),
            pl.BlockSpec((None, HALO, D), lambda b, i: (b, jnp.maximum(i * (tm // HALO) - 1, 0), 0)),
            pl.BlockSpec((1, D), lambda b, i: (0, 0)),
            pl.BlockSpec((nc, D, FF_CHUNK), c3),
            pl.BlockSpec((nc, D, FF_CHUNK), c3),
            pl.BlockSpec((nc, 2, CONV_W, FF_CHUNK), c4),
            pl.BlockSpec((nc, 2, 1, FF_CHUNK), c4),
            pl.BlockSpec((nc, FF_CHUNK, D), c3),
        ],
        out_specs=pl.BlockSpec((None, tm, D), lambda b, i: (b, i, 0)),
        out_shape=jax.ShapeDtypeStruct((B, S, D), F32),
        scratch_shapes=[pltpu.VMEM((tm, D), F32), pltpu.VMEM((HALO + tm, D), BF16),
                        pltpu.VMEM((2, 2, HALO + tm, FF_CHUNK), F32)],
        compiler_params=_cparams(("parallel", "parallel")),
        name="conv_ffn",
    )(x, x, g.reshape(1, D), wup[0], wup[1], cw, cb, wd)


def _rope_tables(S, period, off):
    inv = 1.0 / (ROPE_THETA ** (jnp.arange(0, ROPE_DIM, 2, dtype=F32) / ROPE_DIM))
    ang = jnp.arange(S, dtype=F32)[:, None] * inv[None, :]
    cos, sin = jnp.cos(ang), jnp.sin(ang)
    one = jnp.ones((S, 1), F32)
    zero = jnp.zeros((S, 1), F32)

    def lay(x1, x2, fill):
        group = jnp.concatenate([jnp.tile(fill, (1, off)), x1, x2,
                                 jnp.tile(fill, (1, period - off - ROPE_DIM))], axis=1)
        return jnp.tile(group, (1, LANES // period))

    return lay(cos, cos, one), lay(-sin, zero * sin, zero), lay(zero * sin, sin, zero)


def _t5_bucket(rel):
    nb = T5_BUCKETS // 2
    max_exact = nb // 2
    n = jnp.abs(rel)
    large = max_exact + (jnp.log(jnp.maximum(n, 1).astype(F32) / max_exact)
                         / math.log(T5_MAX_DIST / max_exact) * (nb - max_exact)).astype(jnp.int32)
    large = jnp.minimum(large, nb - 1)
    return jnp.where(rel > 0, nb, 0) + jnp.where(n < max_exact, n, large)


def _dsa_bias_tiles(t5_bias):
    kj = jnp.arange(LANES, dtype=jnp.int32)[:, None]
    qi = jnp.arange(LANES, dtype=jnp.int32)[None, :]
    far = t5_bias[_t5_bucket(jnp.int32(-4 * T5_MAX_DIST))]
    same = (jnp.transpose(t5_bias[_t5_bucket(kj - qi)], (2, 0, 1)) - far[:, None, None]) * LOG2E
    prev = (jnp.transpose(t5_bias[_t5_bucket(kj - LANES - qi)], (2, 0, 1)) - far[:, None, None]) * LOG2E
    return jnp.stack([jnp.zeros_like(same), prev, same], axis=1).reshape(N_PAIRS, 2, 3, LANES, LANES)


def _softmax_shift(gq, gk, dim, bias_hi=0.0, bias_lo=0.0):
    qk = 1.02 * dim * jnp.max(jnp.abs(gq)) * jnp.max(jnp.abs(gk))
    top = qk + bias_hi
    fixed = (top + qk - bias_lo) <= SAFE_SPREAD
    return fixed, jnp.where(fixed, -top, 0.0)


def _band_bias(rel_bias, shift, tq):
    pad = LEFT_CHUNKS * CHUNK
    nblk = -(-pad // tq) + 1
    lead = (nblk - 1) * tq
    nk = nblk * tq
    kk = jnp.arange(nk, dtype=jnp.int32)[:, None] - lead
    i = jnp.arange(tq, dtype=jnp.int32)[None, :]
    first = (i // CHUNK) * CHUNK - pad
    ok = (kk >= first) & (kk < (i // CHUNK + 1) * CHUNK)
    L = nk + tq
    dist = jnp.arange(L, dtype=jnp.int32)
    dist = lead + jnp.where(dist < tq, dist, dist - L)
    v = rel_bias[:, jnp.clip(dist, -(CHUNK - 1), REL_MAX_PAST) + CHUNK - 1]
    b = jnp.tile(v, (1, nk))[:, :nk * (L - 1)].reshape(v.shape[0], nk, L - 1)[:, :, :tq]
    b = jnp.where(ok[None], b + shift, NEG).reshape(N_PAIRS, 2, nblk, tq, tq)
    return jnp.transpose(b, (0, 2, 3, 1, 4)).reshape(N_PAIRS, nblk, tq, 2 * tq)


def kernel(x, norm_mix, norm_ffn, t5_bias, a_w_in, a_q_norm, a_k_norm, a_kidx_norm, a_w_out, b_w_down, b_q_a_norm, b_kv_a_norm, b_w_uq, b_w_ukv, b_q_norm, b_k_norm, b_w_out, c_w_in, c_q_norm, c_k_norm, c_rel_bias, c_w_out, f_w_up, f_conv_w, f_conv_b, f_w_down):
    B, S, D = x.shape
    depth = norm_mix.shape[0]

    tq_c = 256
    tabs_idx = _rope_tables(S, D_IDX, D_IDX - ROPE_DIM)
    tabs_mla = _rope_tables(S, LANES, QK_NOPE)
    scale = HEAD_DIM ** -0.5 * LOG2E
    for layer in range(depth):
        kind, j = layer % N_MIXERS, layer // N_MIXERS
        g = norm_mix[layer]
        if kind == 0:
            gq = a_q_norm[j] * scale
            q, k, vt = _qkv_proj(x, g, a_w_in[j][:, :A_QKV], gq, a_k_norm[j])
            qp, kp, wi = _idx_proj(x, g, a_w_in[j][:, A_QKV:], a_kidx_norm[j], tabs_idx)
            tiles = _dsa_bias_tiles(t5_bias)
            fixed, shift = _softmax_shift(gq, a_k_norm[j], HEAD_DIM, jnp.max(tiles), jnp.min(tiles))
            att = lax.cond(fixed, functools.partial(_dsa_attn, fixed=True), functools.partial(_dsa_attn, fixed=False),
                           q, k, vt, qp, kp, wi, shift, tiles)
            w_out = a_w_out[j]
        elif kind == 1:
            fixed, shift = _softmax_shift(b_q_norm[j] * (B_QK ** -0.5 * LOG2E), b_k_norm[j], B_QK)
            q, k, vt = _mla_proj(x, g, b_w_down[j], b_q_a_norm[j], b_kv_a_norm[j], b_w_uq[j], b_w_ukv[j],
                                 b_q_norm[j], b_k_norm[j], shift, tabs_mla)
            att = lax.cond(fixed, functools.partial(_mla_attn, fixed=True), functools.partial(_mla_attn, fixed=False),
                           q, k, vt)
            w_out = b_w_out[j]
        else:
            gq = c_q_norm[j] * scale
            q, k, vt = _qkv_proj(x, g, c_w_in[j], gq, c_k_norm[j])
            rel = c_rel_bias[j] * LOG2E
            fixed, shift = _softmax_shift(gq, c_k_norm[j], HEAD_DIM, jnp.max(rel), jnp.min(rel))
            att = lax.cond(fixed, functools.partial(_band_attn, fixed=True, tq=tq_c),
                           functools.partial(_band_attn, fixed=False, tq=tq_c),
                           q, k, vt, _band_bias(rel, shift, tq_c))
            w_out = c_w_out[j]
        x = _out_proj(x, att, w_out)
        x = _ffn(x, norm_ffn[layer], f_w_up[layer], f_conv_w[layer], f_conv_b[layer], f_w_down[layer])
    return x
```

```python
import functools
import math

import numpy as np
import jax
import jax.numpy as jnp
from jax import lax
from jax.experimental import pallas as pl
from jax.experimental.pallas import tpu as pltpu

F32 = jnp.float32
BF16 = jnp.bfloat16

D_MODEL = 1024
CHUNK = 64
EPS = 1e-6
ROPE_THETA = 10000.0
NEG = -1e30
NEG_BITS = int(np.float32(NEG).view(np.int32))
M_INIT = -1e38
INT_MIN = -(2 ** 31)
LOG2E = math.log2(math.e)
SAFE_SPREAD = 100.0

N_MIXERS = 3
N_HEADS = 16
HEAD_DIM = 64
N_PAIRS = N_HEADS // 2
LANES = 128
VT_BLK = 256

H_IDX = 8
D_IDX = 64
ROPE_DIM = 32
TOPK_MAX = 256
T5_BUCKETS = 32
T5_MAX_DIST = 128
A_QKV = 3 * N_HEADS * HEAD_DIM
Q_LORA = 256
KV_LORA = 128
QK_NOPE = 64
QK_ROPE = 32
V_DIM = 64
B_QK = QK_NOPE + QK_ROPE
LEFT_CHUNKS = 8
REL_MAX_PAST = 128
D_FF = 2816
CONV_W = 3
FF_CHUNK = 256
HALO = 8

VMEM_LIMIT = 56 * 1024 * 1024
VMEM_LIMIT_DSA = 60 * 1024 * 1024


def _cparams(sem, vmem_limit=VMEM_LIMIT):
    return pltpu.CompilerParams(dimension_semantics=sem, vmem_limit_bytes=vmem_limit)


def _rms(x, g):
    return x * lax.rsqrt(jnp.mean(x * x, axis=-1, keepdims=True) + EPS) * g


def _dot(a, b):
    return jnp.dot(a, b, preferred_element_type=F32)


def _split_bf16(x):
    hi = x.astype(BF16)
    lo = (x - hi.astype(F32)).astype(BF16)
    return hi, lo


def _group_sumsq(y, bd):
    hi, lo = _split_bf16(y * y)
    return _dot(hi, bd) + _dot(lo, bd)


def _rope(y, ct, s1, s2):
    return y * ct + pltpu.roll(y, LANES - 16, 1) * s1 + pltpu.roll(y, 16, 1) * s2


def _rope_wide(y, ct, s1, s2):
    w = y.shape[-1]
    parts = [_rope(y[:, c:c + LANES], ct, s1, s2) for c in range(0, w, LANES)]
    return parts[0] if len(parts) == 1 else jnp.concatenate(parts, axis=-1)


def _store_vt(vt_ref, p, y):
    yt = y.T.astype(BF16)
    for r in range(y.shape[0] // VT_BLK):
        vt_ref[p, r] = yt[:, r * VT_BLK:(r + 1) * VT_BLK]


def _qkv_proj_kernel(x_ref, g_ref, w_ref, gq_ref, gk_ref, bd_ref, q_ref, k_ref, vt_ref):
    xb = _rms(x_ref[...], g_ref[...]).astype(BF16)
    bd = bd_ref[...]
    hd = N_HEADS * HEAD_DIM
    for which, (o_ref, gain_ref) in enumerate(((q_ref, gq_ref), (k_ref, gk_ref), (vt_ref, None))):
        for c in range(hd // 256):
            col = which * hd + c * 256
            y = _dot(xb, w_ref[:, col:col + 256])
            if gain_ref is None:
                _store_vt(o_ref, 2 * c, y[:, :LANES])
                _store_vt(o_ref, 2 * c + 1, y[:, LANES:])
            else:
                y = y * lax.rsqrt(_group_sumsq(y, bd) * (1.0 / HEAD_DIM) + EPS) * gain_ref[...]
                yb = y.astype(BF16)
                o_ref[2 * c] = yb[:, :LANES]
                o_ref[2 * c + 1] = yb[:, LANES:]


def _vt_spec(tm):
    return pl.BlockSpec((None, N_PAIRS, tm // VT_BLK, LANES, VT_BLK), lambda b, i: (b, 0, i, 0, 0))


def _vt_shape(B, S):
    return jax.ShapeDtypeStruct((B, N_PAIRS, S // VT_BLK, LANES, VT_BLK), BF16)


def _qkv_proj(x, g, w, gq, gk, tm=512):
    B, S, D = x.shape
    hd = N_HEADS * HEAD_DIM
    bd = jnp.kron(jnp.eye(256 // HEAD_DIM, dtype=F32), jnp.ones((HEAD_DIM, HEAD_DIM), F32)).astype(BF16)
    out = jax.ShapeDtypeStruct((B, N_PAIRS, S, LANES), BF16)
    const = lambda b, i: (0, 0)
    ospec = pl.BlockSpec((None, N_PAIRS, tm, LANES), lambda b, i: (b, 0, i, 0))
    return pl.pallas_call(
        _qkv_proj_kernel,
        grid=(B, S // tm),
        in_specs=[
            pl.BlockSpec((None, tm, D), lambda b, i: (b, i, 0)),
            pl.BlockSpec((1, D), const),
            pl.BlockSpec((D, 3 * hd), const),
            pl.BlockSpec((1, 256), const),
            pl.BlockSpec((1, 256), const),
            pl.BlockSpec((256, 256), const),
        ],
        out_specs=[ospec, ospec, _vt_spec(tm)],
        out_shape=[out, out, _vt_shape(B, S)],
        compiler_params=_cparams(("parallel", "parallel")),
        name="qkv_proj",
    )(x, g.reshape(1, D), w.astype(BF16), jnp.tile(gq, 4).reshape(1, 256), jnp.tile(gk, 4).reshape(1, 256), bd)


def _idx_proj_kernel(x_ref, g_ref, wh_ref, wl_ref, gki_ref, ct_ref, s1_ref, s2_ref, qp_ref, kp_ref, wi_ref,
                     *, w_scale):
    xn = _rms(x_ref[...], g_ref[...])
    xh, xl = _split_bf16(xn)

    def mm(c0, c1):
        wh = wh_ref[:, c0:c1]
        return _dot(xh, wh) + (_dot(xl, wh) + _dot(xh, wl_ref[:, c0:c1]))

    ct, s1, s2 = ct_ref[...], s1_ref[...], s2_ref[...]
    tm = xn.shape[0]
    low = lax.broadcasted_iota(jnp.int32, (tm, LANES), 1) < D_IDX
    zero = jnp.zeros((tm, LANES), F32)

    for c in range(H_IDX // 2):
        y = _rope(mm(c * LANES, (c + 1) * LANES), ct, s1, s2)
        hi = y.astype(BF16).astype(F32)
        lo = y - hi
        rhi = pltpu.roll(hi, D_IDX, 1)
        rlo = pltpu.roll(lo, D_IDX, 1)
        qp_ref[2 * c, :LANES, :] = jnp.where(low, hi, rlo).T.astype(BF16)
        qp_ref[2 * c, LANES:, :] = jnp.where(low, hi, zero).T.astype(BF16)
        qp_ref[2 * c + 1, :LANES, :] = jnp.where(low, rhi, lo).T.astype(BF16)
        qp_ref[2 * c + 1, LANES:, :] = jnp.where(low, rhi, zero).T.astype(BF16)

    c0 = H_IDX * D_IDX
    y = mm(c0, c0 + LANES)
    y = y * lax.rsqrt(jnp.sum(y * y, axis=-1, keepdims=True) * (1.0 / D_IDX) + EPS) * gki_ref[...]
    y = _rope(y, ct, s1, s2)
    hi = y.astype(BF16).astype(F32)
    lo = y - hi
    kp_ref[:, :LANES] = jnp.where(low, hi, pltpu.roll(hi, D_IDX, 1)).astype(BF16)
    kp_ref[:, LANES:] = jnp.where(low, lo, zero).astype(BF16)

    wi_ref[...] = (mm(c0 + LANES, c0 + 2 * LANES) * w_scale).T[:H_IDX]


def _idx_proj(x, g, w_idx, gki, tabs, tm=512):
    B, S, D = x.shape
    nq = H_IDX * D_IDX
    wq, wk, ww = w_idx[:, :nq], w_idx[:, nq:nq + D_IDX], w_idx[:, nq + D_IDX:]
    wpad = jnp.concatenate([
        wq, jnp.pad(wk, ((0, 0), (0, LANES - D_IDX))), jnp.pad(ww, ((0, 0), (0, LANES - H_IDX)))], axis=1)
    wh = wpad.astype(BF16)
    wl = (wpad - wh.astype(F32)).astype(BF16)
    ncol = nq + 2 * LANES
    gpad = jnp.pad(gki, (0, LANES - D_IDX)).reshape(1, LANES)
    const = lambda b, i: (0, 0)
    tspec = pl.BlockSpec((tm, LANES), lambda b, i: (i, 0))
    return pl.pallas_call(
        functools.partial(_idx_proj_kernel, w_scale=(D_IDX ** -0.5) * (H_IDX ** -0.5)),
        grid=(B, S // tm),
        in_specs=[
            pl.BlockSpec((None, tm, D), lambda b, i: (b, i, 0)),
            pl.BlockSpec((1, D), const),
            pl.BlockSpec((D, ncol), const),
            pl.BlockSpec((D, ncol), const),
            pl.BlockSpec((1, LANES), const),
            tspec, tspec, tspec,
        ],
        out_specs=[
            pl.BlockSpec((None, H_IDX, 256, tm), lambda b, i: (b, 0, 0, i)),
            pl.BlockSpec((None, tm, 256), lambda b, i: (b, i, 0)),
            pl.BlockSpec((None, H_IDX, tm), lambda b, i: (b, 0, i)),
        ],
        out_shape=[
            jax.ShapeDtypeStruct((B, H_IDX, 256, S), BF16),
            jax.ShapeDtypeStruct((B, S, 256), BF16),
            jax.ShapeDtypeStruct((B, H_IDX, S), F32),
        ],
        compiler_params=_cparams(("parallel", "parallel")),
        name="idx_proj",
    )(x, g.reshape(1, D), wh, wl, gpad, *tabs)


def _mla_proj_kernel(x_ref, g_ref, wd_ref, gqa_ref, gkva_ref, wuq_ref, wuk_ref, wuv_ref, gq_ref, gk_ref, bd_ref,
                     qfix_ref, kfix_ref, ct_ref, s1_ref, s2_ref, q_ref, k_ref, vt_ref):
    xb = _rms(x_ref[...], g_ref[...]).astype(BF16)
    d = _dot(xb, wd_ref[...])
    cq = _rms(d[:, :Q_LORA], gqa_ref[...]).astype(BF16)
    ckv = _rms(d[:, Q_LORA:Q_LORA + KV_LORA], gkva_ref[...]).astype(BF16)
    krz = pltpu.roll(d[:, Q_LORA + KV_LORA:], QK_NOPE, 1)
    krz2 = jnp.concatenate([krz, krz], axis=-1)
    bd = bd_ref[...]
    ct, s1, s2 = ct_ref[...], s1_ref[...], s2_ref[...]
    inv = 1.0 / B_QK
    for p in range(N_PAIRS):
        cols = slice(p * 256, (p + 1) * 256)
        y = _dot(cq, wuq_ref[:, cols])
        y = y * lax.rsqrt(_group_sumsq(y, bd) * inv + EPS) * gq_ref[...]
        q_ref[p] = (_rope_wide(y, ct, s1, s2) + qfix_ref[...]).astype(BF16)
        y = _dot(ckv, wuk_ref[:, cols]) + krz2
        y = y * lax.rsqrt(_group_sumsq(y, bd) * inv + EPS) * gk_ref[...]
        k_ref[p] = (_rope_wide(y, ct, s1, s2) + kfix_ref[...]).astype(BF16)
        _store_vt(vt_ref, p, _dot(ckv, wuv_ref[:, p * LANES:(p + 1) * LANES]))


def _mla_proj(x, g, w_down, gqa, gkva, w_uq, w_ukv, gq, gk, shift, tabs, tm=512):
    B, S, D = x.shape
    nd = Q_LORA + KV_LORA + QK_ROPE
    wd = jnp.pad(w_down, ((0, 0), (0, 512 - nd))).astype(BF16)
    padh = LANES - B_QK
    wuq = jnp.pad(w_uq.reshape(Q_LORA, N_HEADS, B_QK), ((0, 0), (0, 0), (0, padh))).reshape(Q_LORA, N_HEADS * LANES)
    wukv = w_ukv.reshape(KV_LORA, N_HEADS, QK_NOPE + V_DIM)
    wuk = jnp.pad(wukv[:, :, :QK_NOPE], ((0, 0), (0, 0), (0, LANES - QK_NOPE))).reshape(KV_LORA, N_HEADS * LANES)
    wuv = wukv[:, :, QK_NOPE:].reshape(KV_LORA, N_HEADS * V_DIM)
    gq2 = jnp.tile(jnp.pad(gq * (B_QK ** -0.5 * LOG2E), (0, padh)), 2).reshape(1, 256)
    gk2 = jnp.tile(jnp.pad(gk, (0, padh)), 2).reshape(1, 256)
    bd = jnp.kron(jnp.eye(2, dtype=F32), jnp.ones((LANES, LANES), F32)).astype(BF16)
    kfix = jnp.asarray((np.arange(256) % LANES == B_QK).astype(np.float32)).reshape(1, 256)
    qfix = kfix * shift
    const = lambda b, i: (0, 0)
    tspec = pl.BlockSpec((tm, LANES), lambda b, i: (i, 0))
    qk_shape = jax.ShapeDtypeStruct((B, N_PAIRS, S, 256), BF16)
    qk_spec = pl.BlockSpec((None, N_PAIRS, tm, 256), lambda b, i: (b, 0, i, 0))
    return pl.pallas_call(
        _mla_proj_kernel,
        grid=(B, S // tm),
        in_specs=[
            pl.BlockSpec((None, tm, D), lambda b, i: (b, i, 0)),
            pl.BlockSpec((1, D), const),
            pl.BlockSpec((D, 512), const),
            pl.BlockSpec((1, Q_LORA), const),
            pl.BlockSpec((1, KV_LORA), const),
            pl.BlockSpec((Q_LORA, N_HEADS * LANES), const),
            pl.BlockSpec((KV_LORA, N_HEADS * LANES), const),
            pl.BlockSpec((KV_LORA, N_HEADS * V_DIM), const),
            pl.BlockSpec((1, 256), const),
            pl.BlockSpec((1, 256), const),
            pl.BlockSpec((256, 256), const),
            pl.BlockSpec((1, 256), const),
            pl.BlockSpec((1, 256), const),
            tspec, tspec, tspec,
        ],
        out_specs=[qk_spec, qk_spec, _vt_spec(tm)],
        out_shape=[qk_shape, qk_shape, _vt_shape(B, S)],
        compiler_params=_cparams(("parallel", "parallel")),
        name="mla_proj",
    )(x, g.reshape(1, D), wd, gqa.reshape(1, Q_LORA), gkva.reshape(1, KV_LORA), wuq.astype(BF16),
      wuk.astype(BF16), wuv.astype(BF16), gq2, gk2, bd, qfix, kfix, *tabs)


def _stack_heads(q2, half):
    qf = q2.astype(F32)
    low = lax.broadcasted_iota(jnp.int32, qf.shape, 1) < half
    zero = jnp.zeros_like(qf)
    return jnp.concatenate([jnp.where(low, qf, zero).T, jnp.where(low, zero, qf).T], axis=1).astype(BF16)


def _flash_init(acc_ref):
    r = acc_ref.shape[1]
    acc_ref[...] = jnp.zeros(acc_ref.shape, F32)
    return jnp.full((1, r), M_INIT, F32), jnp.zeros((1, r), F32)


def _consume(s, vt, m, l, acc_ref, fixed):
    if fixed:
        p = jnp.exp2(s)
        acc_ref[...] += _dot(vt, p.astype(BF16))
        return m, l + jnp.sum(p, axis=0, keepdims=True)
    m_new = jnp.maximum(m, jnp.max(s, axis=0, keepdims=True))
    alpha = jnp.exp2(m - m_new)
    p = jnp.exp2(s - m_new)
    l_new = alpha * l + jnp.sum(p, axis=0, keepdims=True)
    acc_ref[...] = alpha * acc_ref[...] + _dot(vt, p.astype(BF16))
    return m_new, l_new


def _sweep(first, n, s_ref, scores, values, ml, acc_ref, fixed, peel=True):
    def one(j, s, ml):
        return _consume(s, values(j), *ml, acc_ref, fixed)

    if isinstance(n, int):
        tiles = [scores(first + j) for j in range(n)]
        for j in range(n):
            ml = one(first + j, tiles[j], ml)
        return ml

    odd = n % 2 if peel else 0
    if peel:
        ml = lax.cond(odd == 1, lambda ml: one(first, scores(first), ml), lambda ml: ml, ml)
    j0 = first + odd
    last = first + n - 1

    @pl.when(n >= 2)
    def _():
        s_ref[0] = scores(j0)

    def body(t, ml):
        j = j0 + 2 * t
        s1 = scores(j + 1)
        ml = one(j, s_ref[0], ml)
        s_ref[1] = s1
        s0 = scores(jnp.minimum(j + 2, last))
        ml = one(j + 1, s_ref[1], ml)
        s_ref[0] = s0
        return ml

    return lax.fori_loop(0, n // 2, body, ml)


def _flash_out(l, acc_ref, tq):
    o = acc_ref[...] / l
    ot = jnp.concatenate([o[:HEAD_DIM, :tq], o[HEAD_DIM:, tq:]], axis=0)
    return ot.T.astype(BF16)


def _load_vt(vt_ref, kb, tk):
    n = tk // VT_BLK
    parts = [vt_ref[kb * n + r] for r in range(n)]
    return parts[0] if n == 1 else jnp.concatenate(parts, axis=1)


def _band_attn_kernel(q_ref, k_ref, vt_ref, bias_ref, o_ref, acc_ref, s_ref, *, tq, nblk, fixed):
    i = pl.program_id(2)
    qs = _stack_heads(q_ref[...], HEAD_DIM)
    lead = i - (nblk - 1)

    def scores(kb):
        ks = pl.multiple_of(kb * tq, tq)
        return _dot(k_ref[pl.ds(ks, tq), :], qs) + bias_ref[kb - lead]

    values = lambda kb: _load_vt(vt_ref, kb, tq)
    ml0 = _flash_init(acc_ref)
    _, l = lax.cond(
        lead >= 0,
        lambda ml: _sweep(lead, nblk, s_ref, scores, values, ml, acc_ref, fixed),
        lambda ml: _sweep(0, i + 1, s_ref, scores, values, ml, acc_ref, fixed),
        ml0)
    o_ref[...] = _flash_out(l, acc_ref, tq)


def _band_attn(q, k, vt, bias, fixed, tq=256):
    B, _, S, _ = q.shape
    nblk = bias.shape[1]
    return pl.pallas_call(
        functools.partial(_band_attn_kernel, tq=tq, nblk=nblk, fixed=fixed),
        grid=(B, N_PAIRS, S // tq),
        in_specs=[
            pl.BlockSpec((None, None, tq, LANES), lambda b, p, i: (b, p, i, 0)),
            pl.BlockSpec((None, None, S, LANES), lambda b, p, i: (b, p, 0, 0)),
            pl.BlockSpec((None, None, S // VT_BLK, LANES, VT_BLK), lambda b, p, i: (b, p, 0, 0, 0)),
            pl.BlockSpec((None, nblk, tq, 2 * tq), lambda b, p, i: (p, 0, 0, 0)),
        ],
        out_specs=pl.BlockSpec((None, tq, LANES), lambda b, p, i: (b, i, p)),
        out_shape=jax.ShapeDtypeStruct((B, S, N_HEADS * HEAD_DIM), BF16),
        scratch_shapes=[pltpu.VMEM((LANES, 2 * tq), F32), pltpu.VMEM((2, tq, 2 * tq), F32)],
        compiler_params=_cparams(("parallel", "parallel", "arbitrary")),
        name="band_attn",
    )(q, k, vt, bias)


def _mla_attn_kernel(q_ref, k_ref, vt_ref, o_ref, acc_ref, s_ref, *, tq, tk, fixed):
    i = pl.program_id(2)
    qs = _stack_heads(q_ref[...], LANES)
    t0 = i * tq
    assert tq % (2 * tk) == 0
    n_full = t0 // tk
    n_edge = tq // tk
    lim = ((t0 + lax.broadcasted_iota(jnp.int32, (1, tq), 1)) // CHUNK + 1) * CHUNK

    def scores(kb):
        ks = pl.multiple_of(kb * tk, tk)
        return _dot(k_ref[pl.ds(ks, tk), :], qs)

    def edge_scores(kb):
        row = kb * tk + lax.broadcasted_iota(jnp.int32, (tk, tq), 0)
        add = jnp.where(row < lim, 0.0, NEG).astype(F32)
        return scores(kb) + jnp.concatenate([add, add], axis=1)

    values = lambda kb: _load_vt(vt_ref, kb, tk)
    ml = _sweep(0, n_full, s_ref, scores, values, _flash_init(acc_ref), acc_ref, fixed, peel=False)
    _, l = _sweep(n_full, n_edge, s_ref, edge_scores, values, ml, acc_ref, fixed, peel=False)
    o_ref[...] = _flash_out(l, acc_ref, tq)


def _mla_attn(q, k, vt, fixed, tq=1024, tk=256):
    B, _, S, _ = q.shape
    tk = min(tk, S)
    return pl.pallas_call(
        functools.partial(_mla_attn_kernel, tq=tq, tk=tk, fixed=fixed),
        grid=(B, N_PAIRS, S // tq),
        in_specs=[
            pl.BlockSpec((None, None, tq, 256), lambda b, p, i: (b, p, i, 0)),
            pl.BlockSpec((None, None, S, 256), lambda b, p, i: (b, p, 0, 0)),
            pl.BlockSpec((None, None, S // VT_BLK, LANES, VT_BLK), lambda b, p, i: (b, p, 0, 0, 0)),
        ],
        out_specs=pl.BlockSpec((None, tq, LANES), lambda b, p, i: (b, i, p)),
        out_shape=jax.ShapeDtypeStruct((B, S, N_HEADS * V_DIM), BF16),
        scratch_shapes=[pltpu.VMEM((LANES, 2 * tq), F32), pltpu.VMEM((2, tk, 2 * tq), F32)],
        compiler_params=_cparams(("parallel", "parallel", "arbitrary")),
        name="mla_attn",
    )(q, k, vt)


def _dsa_attn_kernel(q_ref, qp_ref, wi_ref, shift_ref, bias_ref, k_ref, vt_ref, kp_ref, o_ref,
                     sc_ref, h_ref, acc_ref, stage_ref, s_ref, *, tq, tk, topk, fixed):
    i = pl.program_id(1)
    t0 = i * tq
    n_blk = (t0 + tq + tk - 1) // tk
    n_far = jnp.maximum((t0 // LANES - 1) // (tk // LANES), 0)
    n_far_even = n_far - n_far % 2
    lim = ((t0 + lax.broadcasted_iota(jnp.int32, (1, tq), 1)) // CHUNK + 1) * CHUNK

    wv = wi_ref[...]

    def score_body(kb, carry):
        ks = pl.multiple_of(kb * tk, tk)
        kp = kp_ref[pl.ds(ks, tk), :]
        acc = jnp.zeros((tk, tq), F32)
        for h in range(H_IDX):
            acc = acc + jnp.maximum(_dot(kp, qp_ref[h]), 0.0) * wv[h:h + 1, :]
        row = ks + lax.broadcasted_iota(jnp.int32, (tk, tq), 0)
        acc = jnp.where(row < lim, acc, -jnp.inf)
        bits = pltpu.bitcast(acc, jnp.int32)
        key = bits ^ ((bits >> 31) & 0x7FFFFFFF)
        sc_ref[kb] = key
        h_ref[kb] = (key >> 16).astype(jnp.int16)
        return carry

    lax.fori_loop(0, n_blk, score_body, 0)

    def count16(pred):
        def count_body(kb, part):
            hit = jnp.where(pred(h_ref[kb]), jnp.int16(1), jnp.int16(0))
            rows = hit[0:16]
            for r in range(16, tk, 16):
                rows = rows + hit[r:r + 16]
            return part + rows.astype(jnp.int32)

        part = lax.fori_loop(0, n_blk, count_body, jnp.zeros((16, tq), jnp.int32))
        return jnp.sum(part, axis=0, keepdims=True)

    def kth16(need):
        def bit_body(t, ans):
            cand = ans + jnp.left_shift(jnp.int32(1), 15 - t)
            cand16 = cand.astype(jnp.int16)
            return jnp.where(count16(lambda h: h >= cand16) >= need, cand, ans)

        return lax.fori_loop(0, 16, bit_body, jnp.full((1, tq), -(2 ** 15), jnp.int32))

    top = kth16(topk)
    top16 = top.astype(jnp.int16)
    need = topk - count16(lambda h: h > top16)

    def low_body(kb, carry):
        low = ((sc_ref[kb] & 0xFFFF) - 2 ** 15).astype(jnp.int16)
        h_ref[kb] = jnp.where(h_ref[kb] == top16, low, jnp.int16(-(2 ** 15)))
        return carry

    lax.fori_loop(0, n_blk, low_body, 0)
    low = kth16(need)
    thr = jnp.left_shift(top, 16) + (low + 2 ** 15)

    low16 = low.astype(jnp.int16)
    ties = (need - count16(lambda h: h > low16)).astype(F32)
    keep_bits = pltpu.bitcast(shift_ref[...], jnp.int32)
    below = (lax.broadcasted_iota(jnp.int32, (tk, tk), 0) > lax.broadcasted_iota(jnp.int32, (tk, tk), 1))
    below = jnp.where(below, 1.0, 0.0).astype(BF16)

    def mask_body(kb, seen):
        key = sc_ref[kb]
        row = kb * tk + lax.broadcasted_iota(jnp.int32, (tk, tq), 0)
        tie = jnp.where(key == thr, 1.0, 0.0)
        before = seen + _dot(below, tie.astype(BF16))
        thr_row = thr + jnp.where(before >= ties, 1, 0)
        sc_ref[kb] = jnp.where((key >= thr_row) & (row < lim), keep_bits, NEG_BITS)
        return seen + jnp.sum(tie, axis=0, keepdims=True)

    lax.fori_loop(0, n_blk, mask_body, jnp.zeros((1, tq), F32))

    def pair_body(p, carry):
        qs = _stack_heads(q_ref[p], HEAD_DIM)

        def far_scores(kb):
            ks = pl.multiple_of(kb * tk, tk)
            am = pltpu.bitcast(sc_ref[kb], F32)
            return _dot(k_ref[p, pl.ds(ks, tk), :], qs) + jnp.concatenate([am, am], axis=1)

        def near_scores(kb):
            cols = []
            for hh in range(2):
                for a in range(tq // LANES):
                    tiles = []
                    for c in range(tk // LANES):
                        d = (kb * (tk // LANES) + c) - (i * (tq // LANES) + a)
                        kind = jnp.where(d == 0, 2, jnp.where(d == -1, 1, 0))
                        tiles.append(bias_ref[p, hh, kind])
                    cols.append(jnp.concatenate(tiles, axis=0))
            return far_scores(kb) + jnp.concatenate(cols, axis=1)

        values = lambda kb: _load_vt(vt_ref.at[p], kb, tk)
        _, l = _sweep(0, n_blk, s_ref, near_scores, values, _flash_init(acc_ref), acc_ref, fixed)
        stage_ref[p] = _flash_out(l, acc_ref, tq)
        return carry

    lax.fori_loop(0, N_PAIRS, pair_body, 0)
    for p in range(N_PAIRS):
        o_ref[:, p * LANES:(p + 1) * LANES] = stage_ref[p]


def _dsa_attn(q, k, vt, qp, kp, wi, shift, bias, fixed, tq=256, tk=512):
    B, _, S, _ = q.shape
    topk = min(TOPK_MAX, S // 4)
    once = pl.Buffered(1)
    return pl.pallas_call(
        functools.partial(_dsa_attn_kernel, tq=tq, tk=tk, topk=topk, fixed=fixed),
        grid=(B, S // tq),
        in_specs=[
            pl.BlockSpec((None, N_PAIRS, tq, LANES), lambda b, i: (b, 0, i, 0)),
            pl.BlockSpec((None, H_IDX, 256, tq), lambda b, i: (b, 0, 0, i)),
            pl.BlockSpec((None, H_IDX, tq), lambda b, i: (b, 0, i)),
            pl.BlockSpec((1, tq), lambda b, i: (0, 0)),
            pl.BlockSpec((N_PAIRS, 2, 3, LANES, LANES), lambda b, i: (0, 0, 0, 0, 0), pipeline_mode=once),
            pl.BlockSpec((None, N_PAIRS, S, LANES), lambda b, i: (b, 0, 0, 0), pipeline_mode=once),
            pl.BlockSpec((None, N_PAIRS, S // VT_BLK, LANES, VT_BLK), lambda b, i: (b, 0, 0, 0, 0),
                         pipeline_mode=once),
            pl.BlockSpec((None, S, 256), lambda b, i: (b, 0, 0), pipeline_mode=once),
        ],
        out_specs=pl.BlockSpec((None, tq, N_HEADS * HEAD_DIM), lambda b, i: (b, i, 0)),
        out_shape=jax.ShapeDtypeStruct((B, S, N_HEADS * HEAD_DIM), BF16),
        scratch_shapes=[
            pltpu.VMEM((S // tk, tk, tq), jnp.int32),
            pltpu.VMEM((S // tk, tk, tq), jnp.int16),
            pltpu.VMEM((LANES, 2 * tq), F32),
            pltpu.VMEM((N_PAIRS, tq, LANES), BF16),
            pltpu.VMEM((2, tk, 2 * tq), F32),
        ],
        compiler_params=_cparams(("parallel", "arbitrary"), VMEM_LIMIT_DSA),
        name="dsa_attn",
    )(q, qp, wi, jnp.full((1, tq), shift, F32), bias, k, vt, kp)


def _out_proj_kernel(x_ref, a_ref, w_ref, o_ref):
    o_ref[...] = x_ref[...] + _dot(a_ref[...], w_ref[...])


def _out_proj(x, a, w, tm=512):
    B, S, D = x.shape
    return pl.pallas_call(
        _out_proj_kernel,
        grid=(B, S // tm),
        in_specs=[
            pl.BlockSpec((None, tm, D), lambda b, i: (b, i, 0)),
            pl.BlockSpec((None, tm, a.shape[-1]), lambda b, i: (b, i, 0)),
            pl.BlockSpec(w.shape, lambda b, i: (0, 0)),
        ],
        out_specs=pl.BlockSpec((None, tm, D), lambda b, i: (b, i, 0)),
        out_shape=jax.ShapeDtypeStruct((B, S, D), F32),
        compiler_params=_cparams(("parallel", "parallel")),
        name="out_proj",
    )(x, a, w.astype(BF16))


def _ffn_kernel(x_ref, xp_ref, g_ref, wg_ref, wv_ref, cw_ref, cb_ref, wd_ref, o_ref, acc_ref, he_ref, u_ref,
                *, tm, n_chunks):
    i = pl.program_id(1)
    g = g_ref[...]
    he_ref[:HALO] = (_rms(xp_ref[...], g) * jnp.where(i > 0, 1.0, 0.0)).astype(BF16)
    he_ref[HALO:] = _rms(x_ref[...], g).astype(BF16)
    acc_ref[...] = jnp.zeros(acc_ref.shape, F32)

    def up(c, slot):
        he = he_ref[...]
        u_ref[slot, 0] = _dot(he, wg_ref[c])
        u_ref[slot, 1] = _dot(he, wv_ref[c])

    def conv(u, w, b):
        return (b + w[0:1] * u[HALO - 2:HALO - 2 + tm] + w[1:2] * u[HALO - 1:HALO - 1 + tm]
                + w[2:3] * u[HALO:HALO + tm])

    def down(c, slot):
        cw = cw_ref[c]
        cb = cb_ref[c]
        gate = conv(u_ref[slot, 0], cw[0], cb[0])
        val = conv(u_ref[slot, 1], cw[1], cb[1])
        act = gate * jax.nn.sigmoid(gate) * val
        acc_ref[...] += _dot(act.astype(BF16), wd_ref[c])

    up(0, 0)

    def pair_body(t, carry):
        c = 2 * t
        up(c + 1, 1)
        down(c, 0)
        up(c + 2, 0)
        down(c + 1, 1)
        return carry

    assert n_chunks % 2 == 1
    lax.fori_loop(0, n_chunks // 2, pair_body, 0)
    down(n_chunks - 1, 0)
    o_ref[...] = x_ref[...] + acc_ref[...]


def _ffn(x, g, w_up, conv_w, conv_b, w_down, tm=512):
    B, S, D = x.shape
    nc = D_FF // FF_CHUNK
    wup = w_up.astype(BF16).reshape(D, 2, nc, FF_CHUNK).transpose(1, 2, 0, 3)
    cw = conv_w.reshape(CONV_W, 2, nc, FF_CHUNK).transpose(2, 1, 0, 3)
    cb = conv_b.reshape(2, nc, 1, FF_CHUNK).transpose(1, 0, 2, 3)
    wd = w_down.astype(BF16).reshape(nc, FF_CHUNK, D)
    c3 = lambda b, i: (0, 0, 0)
    c4 = lambda b, i: (0, 0, 0, 0)
    return pl.pallas_call(
        functools.partial(_ffn_kernel, tm=tm, n_chunks=nc),
        grid=(B, S // tm),
        in_specs=[
            pl.BlockSpec((None, tm, D), lambda b, i: (b, i, 0)),
            pl.BlockSpec((None, HALO, D), lambda b, i: (b, jnp.maximum(i * (tm // HALO) - 1, 0), 0)),
            pl.BlockSpec((1, D), lambda b, i: (0, 0)),
            pl.BlockSpec((nc, D, FF_CHUNK), c3),
            pl.BlockSpec((nc, D, FF_CHUNK), c3),
            pl.BlockSpec((nc, 2, CONV_W, FF_CHUNK), c4),
            pl.BlockSpec((nc, 2, 1, FF_CHUNK), c4),
            pl.BlockSpec((nc, FF_CHUNK, D), c3),
        ],
        out_specs=pl.BlockSpec((None, tm, D), lambda b, i: (b, i, 0)),
        out_shape=jax.ShapeDtypeStruct((B, S, D), F32),
        scratch_shapes=[pltpu.VMEM((tm, D), F32), pltpu.VMEM((HALO + tm, D), BF16),
                        pltpu.VMEM((2, 2, HALO + tm, FF_CHUNK), F32)],
        compiler_params=_cparams(("parallel", "parallel")),
        name="conv_ffn",
    )(x, x, g.reshape(1, D), wup[0], wup[1], cw, cb, wd)


def _rope_tables(S, period, off):
    inv = 1.0 / (ROPE_THETA ** (jnp.arange(0, ROPE_DIM, 2, dtype=F32) / ROPE_DIM))
    ang = jnp.arange(S, dtype=F32)[:, None] * inv[None, :]
    cos, sin = jnp.cos(ang), jnp.sin(ang)
    one = jnp.ones((S, 1), F32)
    zero = jnp.zeros((S, 1), F32)

    def lay(x1, x2, fill):
        group = jnp.concatenate([jnp.tile(fill, (1, off)), x1, x2,
                                 jnp.tile(fill, (1, period - off - ROPE_DIM))], axis=1)
        return jnp.tile(group, (1, LANES // period))

    return lay(cos, cos, one), lay(-sin, zero * sin, zero), lay(zero * sin, sin, zero)


def _t5_bucket(rel):
    nb = T5_BUCKETS // 2
    max_exact = nb // 2
    n = jnp.abs(rel)
    large = max_exact + (jnp.log(jnp.maximum(n, 1).astype(F32) / max_exact)
                         / math.log(T5_MAX_DIST / max_exact) * (nb - max_exact)).astype(jnp.int32)
    large = jnp.minimum(large, nb - 1)
    return jnp.where(rel > 0, nb, 0) + jnp.where(n < max_exact, n, large)


def _dsa_bias_tiles(t5_bias):
    kj = jnp.arange(LANES, dtype=jnp.int32)[:, None]
    qi = jnp.arange(LANES, dtype=jnp.int32)[None, :]
    far = t5_bias[_t5_bucket(jnp.int32(-4 * T5_MAX_DIST))]
    same = (jnp.transpose(t5_bias[_t5_bucket(kj - qi)], (2, 0, 1)) - far[:, None, None]) * LOG2E
    prev = (jnp.transpose(t5_bias[_t5_bucket(kj - LANES - qi)], (2, 0, 1)) - far[:, None, None]) * LOG2E
    return jnp.stack([jnp.zeros_like(same), prev, same], axis=1).reshape(N_PAIRS, 2, 3, LANES, LANES)


def _softmax_shift(gq, gk, dim, bias_hi=0.0, bias_lo=0.0):
    qk = 1.02 * dim * jnp.max(jnp.abs(gq)) * jnp.max(jnp.abs(gk))
    top = qk + bias_hi
    fixed = (top + qk - bias_lo) <= SAFE_SPREAD
    return fixed, jnp.where(fixed, -top, 0.0)


def _band_bias(rel_bias, shift, tq):
    pad = LEFT_CHUNKS * CHUNK
    nblk = -(-pad // tq) + 1
    lead = (nblk - 1) * tq
    nk = nblk * tq
    kk = jnp.arange(nk, dtype=jnp.int32)[:, None] - lead
    i = jnp.arange(tq, dtype=jnp.int32)[None, :]
    first = (i // CHUNK) * CHUNK - pad
    ok = (kk >= first) & (kk < (i // CHUNK + 1) * CHUNK)
    L = nk + tq
    dist = jnp.arange(L, dtype=jnp.int32)
    dist = lead + jnp.where(dist < tq, dist, dist - L)
    v = rel_bias[:, jnp.clip(dist, -(CHUNK - 1), REL_MAX_PAST) + CHUNK - 1]
    b = jnp.tile(v, (1, nk))[:, :nk * (L - 1)].reshape(v.shape[0], nk, L - 1)[:, :, :tq]
    b = jnp.where(ok[None], b + shift, NEG).reshape(N_PAIRS, 2, nblk, tq, tq)
    return jnp.transpose(b, (0, 2, 3, 1, 4)).reshape(N_PAIRS, nblk, tq, 2 * tq)


def kernel(x, norm_mix, norm_ffn, t5_bias, a_w_in, a_q_norm, a_k_norm, a_kidx_norm, a_w_out, b_w_down, b_q_a_norm, b_kv_a_norm, b_w_uq, b_w_ukv, b_q_norm, b_k_norm, b_w_out, c_w_in, c_q_norm, c_k_norm, c_rel_bias, c_w_out, f_w_up, f_conv_w, f_conv_b, f_w_down):
    B, S, D = x.shape
    depth = norm_mix.shape[0]

    tq_c = 256
    tabs_idx = _rope_tables(S, D_IDX, D_IDX - ROPE_DIM)
    tabs_mla = _rope_tables(S, LANES, QK_NOPE)
    scale = HEAD_DIM ** -0.5 * LOG2E
    for layer in range(depth):
        kind, j = layer % N_MIXERS, layer // N_MIXERS
        g = norm_mix[layer]
        if kind == 0:
            gq = a_q_norm[j] * scale
            q, k, vt = _qkv_proj(x, g, a_w_in[j][:, :A_QKV], gq, a_k_norm[j])
            qp, kp, wi = _idx_proj(x, g, a_w_in[j][:, A_QKV:], a_kidx_norm[j], tabs_idx)
            tiles = _dsa_bias_tiles(t5_bias)
            fixed, shift = _softmax_shift(gq, a_k_norm[j], HEAD_DIM, jnp.max(tiles), jnp.min(tiles))
            att = lax.cond(fixed, functools.partial(_dsa_attn, fixed=True), functools.partial(_dsa_attn, fixed=False),
                           q, k, vt, qp, kp, wi, shift, tiles)
            w_out = a_w_out[j]
        elif kind == 1:
            fixed, shift = _softmax_shift(b_q_norm[j] * (B_QK ** -0.5 * LOG2E), b_k_norm[j], B_QK)
            q, k, vt = _mla_proj(x, g, b_w_down[j], b_q_a_norm[j], b_kv_a_norm[j], b_w_uq[j], b_w_ukv[j],
                                 b_q_norm[j], b_k_norm[j], shift, tabs_mla)
            att = lax.cond(fixed, functools.partial(_mla_attn, fixed=True), functools.partial(_mla_attn, fixed=False),
                           q, k, vt)
            w_out = b_w_out[j]
        else:
            gq = c_q_norm[j] * scale
            q, k, vt = _qkv_proj(x, g, c_w_in[j], gq, c_k_norm[j])
            rel = c_rel_bias[j] * LOG2E
            fixed, shift = _softmax_shift(gq, c_k_norm[j], HEAD_DIM, jnp.max(rel), jnp.min(rel))
            att = lax.cond(fixed, functools.partial(_band_attn, fixed=True, tq=tq_c),
                           functools.partial(_band_attn, fixed=False, tq=tq_c),
                           q, k, vt, _band_bias(rel, shift, tq_c))
            w_out = c_w_out[j]
        x = _out_proj(x, att, w_out)
        x = _ffn(x, norm_ffn[layer], f_w_up[layer], f_conv_w[layer], f_conv_b[layer], f_w_down[layer])
    return x
```

```python
import functools
import math

import numpy as np
import jax
import jax.numpy as jnp
from jax import lax
from jax.experimental import pallas as pl
from jax.experimental.pallas import tpu as pltpu

F32 = jnp.float32
BF16 = jnp.bfloat16

D_MODEL = 1024
CHUNK = 64
EPS = 1e-6
ROPE_THETA = 10000.0
NEG = -1e30
NEG_BITS = int(np.float32(NEG).view(np.int32))
M_INIT = -1e38
INT_MIN = -(2 ** 31)
LOG2E = math.log2(math.e)
SAFE_SPREAD = 100.0

N_MIXERS = 3
N_HEADS = 16
HEAD_DIM = 64
N_PAIRS = N_HEADS // 2
LANES = 128
VT_BLK = 256

H_IDX = 8
D_IDX = 64
ROPE_DIM = 32
TOPK_MAX = 256
T5_BUCKETS = 32
T5_MAX_DIST = 128
A_QKV = 3 * N_HEADS * HEAD_DIM
Q_LORA = 256
KV_LORA = 128
QK_NOPE = 64
QK_ROPE = 32
V_DIM = 64
B_QK = QK_NOPE + QK_ROPE
LEFT_CHUNKS = 8
REL_MAX_PAST = 128
D_FF = 2816
CONV_W = 3
FF_CHUNK = 256
HALO = 8

VMEM_LIMIT = 56 * 1024 * 1024
VMEM_LIMIT_DSA = 60 * 1024 * 1024


def _cparams(sem, vmem_limit=VMEM_LIMIT):
    return pltpu.CompilerParams(dimension_semantics=sem, vmem_limit_bytes=vmem_limit)


def _rms(x, g):
    return x * lax.rsqrt(jnp.mean(x * x, axis=-1, keepdims=True) + EPS) * g


def _dot(a, b):
    return jnp.dot(a, b, preferred_element_type=F32)


def _split_bf16(x):
    hi = x.astype(BF16)
    lo = (x - hi.astype(F32)).astype(BF16)
    return hi, lo


def _group_sumsq(y, bd):
    hi, lo = _split_bf16(y * y)
    return _dot(hi, bd) + _dot(lo, bd)


def _rope(y, ct, s1, s2):
    return y * ct + pltpu.roll(y, LANES - 16, 1) * s1 + pltpu.roll(y, 16, 1) * s2


def _rope_wide(y, ct, s1, s2):
    w = y.shape[-1]
    parts = [_rope(y[:, c:c + LANES], ct, s1, s2) for c in range(0, w, LANES)]
    return parts[0] if len(parts) == 1 else jnp.concatenate(parts, axis=-1)


def _store_vt(vt_ref, p, y):
    yt = y.T.astype(BF16)
    for r in range(y.shape[0] // VT_BLK):
        vt_ref[p, r] = yt[:, r * VT_BLK:(r + 1) * VT_BLK]


def _qkv_proj_kernel(x_ref, g_ref, w_ref, gq_ref, gk_ref, bd_ref, q_ref, k_ref, vt_ref):
    xb = _rms(x_ref[...], g_ref[...]).astype(BF16)
    bd = bd_ref[...]
    hd = N_HEADS * HEAD_DIM
    for which, (o_ref, gain_ref) in enumerate(((q_ref, gq_ref), (k_ref, gk_ref), (vt_ref, None))):
        for c in range(hd // 256):
            col = which * hd + c * 256
            y = _dot(xb, w_ref[:, col:col + 256])
            if gain_ref is None:
                _store_vt(o_ref, 2 * c, y[:, :LANES])
                _store_vt(o_ref, 2 * c + 1, y[:, LANES:])
            else:
                y = y * lax.rsqrt(_group_sumsq(y, bd) * (1.0 / HEAD_DIM) + EPS) * gain_ref[...]
                yb = y.astype(BF16)
                o_ref[2 * c] = yb[:, :LANES]
                o_ref[2 * c + 1] = yb[:, LANES:]


def _vt_spec(tm):
    return pl.BlockSpec((None, N_PAIRS, tm // VT_BLK, LANES, VT_BLK), lambda b, i: (b, 0, i, 0, 0))


def _vt_shape(B, S):
    return jax.ShapeDtypeStruct((B, N_PAIRS, S // VT_BLK, LANES, VT_BLK), BF16)


def _qkv_proj(x, g, w, gq, gk, tm=512):
    B, S, D = x.shape
    hd = N_HEADS * HEAD_DIM
    bd = jnp.kron(jnp.eye(256 // HEAD_DIM, dtype=F32), jnp.ones((HEAD_DIM, HEAD_DIM), F32)).astype(BF16)
    out = jax.ShapeDtypeStruct((B, N_PAIRS, S, LANES), BF16)
    const = lambda b, i: (0, 0)
    ospec = pl.BlockSpec((None, N_PAIRS, tm, LANES), lambda b, i: (b, 0, i, 0))
    return pl.pallas_call(
        _qkv_proj_kernel,
        grid=(B, S // tm),
        in_specs=[
            pl.BlockSpec((None, tm, D), lambda b, i: (b, i, 0)),
            pl.BlockSpec((1, D), const),
            pl.BlockSpec((D, 3 * hd), const),
            pl.BlockSpec((1, 256), const),
            pl.BlockSpec((1, 256), const),
            pl.BlockSpec((256, 256), const),
        ],
        out_specs=[ospec, ospec, _vt_spec(tm)],
        out_shape=[out, out, _vt_shape(B, S)],
        compiler_params=_cparams(("parallel", "parallel")),
        name="qkv_proj",
    )(x, g.reshape(1, D), w.astype(BF16), jnp.tile(gq, 4).reshape(1, 256), jnp.tile(gk, 4).reshape(1, 256), bd)


def _idx_proj_kernel(x_ref, g_ref, wh_ref, wl_ref, gki_ref, ct_ref, s1_ref, s2_ref, qp_ref, kp_ref, wi_ref,
                     *, w_scale):
    xn = _rms(x_ref[...], g_ref[...])
    xh, xl = _split_bf16(xn)

    def mm(c0, c1):
        wh = wh_ref[:, c0:c1]
        return _dot(xh, wh) + (_dot(xl, wh) + _dot(xh, wl_ref[:, c0:c1]))

    ct, s1, s2 = ct_ref[...], s1_ref[...], s2_ref[...]
    tm = xn.shape[0]
    low = lax.broadcasted_iota(jnp.int32, (tm, LANES), 1) < D_IDX
    zero = jnp.zeros((tm, LANES), F32)

    for c in range(H_IDX // 2):
        y = _rope(mm(c * LANES, (c + 1) * LANES), ct, s1, s2)
        hi = y.astype(BF16).astype(F32)
        lo = y - hi
        rhi = pltpu.roll(hi, D_IDX, 1)
        rlo = pltpu.roll(lo, D_IDX, 1)
        qp_ref[2 * c, :LANES, :] = jnp.where(low, hi, rlo).T.astype(BF16)
        qp_ref[2 * c, LANES:, :] = jnp.where(low, hi, zero).T.astype(BF16)
        qp_ref[2 * c + 1, :LANES, :] = jnp.where(low, rhi, lo).T.astype(BF16)
        qp_ref[2 * c + 1, LANES:, :] = jnp.where(low, rhi, zero).T.astype(BF16)

    c0 = H_IDX * D_IDX
    y = mm(c0, c0 + LANES)
    y = y * lax.rsqrt(jnp.sum(y * y, axis=-1, keepdims=True) * (1.0 / D_IDX) + EPS) * gki_ref[...]
    y = _rope(y, ct, s1, s2)
    hi = y.astype(BF16).astype(F32)
    lo = y - hi
    kp_ref[:, :LANES] = jnp.where(low, hi, pltpu.roll(hi, D_IDX, 1)).astype(BF16)
    kp_ref[:, LANES:] = jnp.where(low, lo, zero).astype(BF16)

    wi_ref[...] = (mm(c0 + LANES, c0 + 2 * LANES) * w_scale).T[:H_IDX]


def _idx_proj(x, g, w_idx, gki, tabs, tm=512):
    B, S, D = x.shape
    nq = H_IDX * D_IDX
    wq, wk, ww = w_idx[:, :nq], w_idx[:, nq:nq + D_IDX], w_idx[:, nq + D_IDX:]
    wpad = jnp.concatenate([
        wq, jnp.pad(wk, ((0, 0), (0, LANES - D_IDX))), jnp.pad(ww, ((0, 0), (0, LANES - H_IDX)))], axis=1)
    wh = wpad.astype(BF16)
    wl = (wpad - wh.astype(F32)).astype(BF16)
    ncol = nq + 2 * LANES
    gpad = jnp.pad(gki, (0, LANES - D_IDX)).reshape(1, LANES)
    const = lambda b, i: (0, 0)
    tspec = pl.BlockSpec((tm, LANES), lambda b, i: (i, 0))
    return pl.pallas_call(
        functools.partial(_idx_proj_kernel, w_scale=(D_IDX ** -0.5) * (H_IDX ** -0.5)),
        grid=(B, S // tm),
        in_specs=[
            pl.BlockSpec((None, tm, D), lambda b, i: (b, i, 0)),
            pl.BlockSpec((1, D), const),
            pl.BlockSpec((D, ncol), const),
            pl.BlockSpec((D, ncol), const),
            pl.BlockSpec((1, LANES), const),
            tspec, tspec, tspec,
        ],
        out_specs=[
            pl.BlockSpec((None, H_IDX, 256, tm), lambda b, i: (b, 0, 0, i)),
            pl.BlockSpec((None, tm, 256), lambda b, i: (b, i, 0)),
            pl.BlockSpec((None, H_IDX, tm), lambda b, i: (b, 0, i)),
        ],
        out_shape=[
            jax.ShapeDtypeStruct((B, H_IDX, 256, S), BF16),
            jax.ShapeDtypeStruct((B, S, 256), BF16),
            jax.ShapeDtypeStruct((B, H_IDX, S), F32),
        ],
        compiler_params=_cparams(("parallel", "parallel")),
        name="idx_proj",
    )(x, g.reshape(1, D), wh, wl, gpad, *tabs)


def _mla_proj_kernel(x_ref, g_ref, wd_ref, gqa_ref, gkva_ref, wuq_ref, wuk_ref, wuv_ref, gq_ref, gk_ref, bd_ref,
                     qfix_ref, kfix_ref, ct_ref, s1_ref, s2_ref, q_ref, k_ref, vt_ref):
    xb = _rms(x_ref[...], g_ref[...]).astype(BF16)
    d = _dot(xb, wd_ref[...])
    cq = _rms(d[:, :Q_LORA], gqa_ref[...]).astype(BF16)
    ckv = _rms(d[:, Q_LORA:Q_LORA + KV_LORA], gkva_ref[...]).astype(BF16)
    krz = pltpu.roll(d[:, Q_LORA + KV_LORA:], QK_NOPE, 1)
    krz2 = jnp.concatenate([krz, krz], axis=-1)
    bd = bd_ref[...]
    ct, s1, s2 = ct_ref[...], s1_ref[...], s2_ref[...]
    inv = 1.0 / B_QK
    for p in range(N_PAIRS):
        cols = slice(p * 256, (p + 1) * 256)
        y = _dot(cq, wuq_ref[:, cols])
        y = y * lax.rsqrt(_group_sumsq(y, bd) * inv + EPS) * gq_ref[...]
        q_ref[p] = (_rope_wide(y, ct, s1, s2) + qfix_ref[...]).astype(BF16)
        y = _dot(ckv, wuk_ref[:, cols]) + krz2
        y = y * lax.rsqrt(_group_sumsq(y, bd) * inv + EPS) * gk_ref[...]
        k_ref[p] = (_rope_wide(y, ct, s1, s2) + kfix_ref[...]).astype(BF16)
        _store_vt(vt_ref, p, _dot(ckv, wuv_ref[:, p * LANES:(p + 1) * LANES]))


def _mla_proj(x, g, w_down, gqa, gkva, w_uq, w_ukv, gq, gk, shift, tabs, tm=512):
    B, S, D = x.shape
    nd = Q_LORA + KV_LORA + QK_ROPE
    wd = jnp.pad(w_down, ((0, 0), (0, 512 - nd))).astype(BF16)
    padh = LANES - B_QK
    wuq = jnp.pad(w_uq.reshape(Q_LORA, N_HEADS, B_QK), ((0, 0), (0, 0), (0, padh))).reshape(Q_LORA, N_HEADS * LANES)
    wukv = w_ukv.reshape(KV_LORA, N_HEADS, QK_NOPE + V_DIM)
    wuk = jnp.pad(wukv[:, :, :QK_NOPE], ((0, 0), (0, 0), (0, LANES - QK_NOPE))).reshape(KV_LORA, N_HEADS * LANES)
    wuv = wukv[:, :, QK_NOPE:].reshape(KV_LORA, N_HEADS * V_DIM)
    gq2 = jnp.tile(jnp.pad(gq * (B_QK ** -0.5 * LOG2E), (0, padh)), 2).reshape(1, 256)
    gk2 = jnp.tile(jnp.pad(gk, (0, padh)), 2).reshape(1, 256)
    bd = jnp.kron(jnp.eye(2, dtype=F32), jnp.ones((LANES, LANES), F32)).astype(BF16)
    kfix = jnp.asarray((np.arange(256) % LANES == B_QK).astype(np.float32)).reshape(1, 256)
    qfix = kfix * shift
    const = lambda b, i: (0, 0)
    tspec = pl.BlockSpec((tm, LANES), lambda b, i: (i, 0))
    qk_shape = jax.ShapeDtypeStruct((B, N_PAIRS, S, 256), BF16)
    qk_spec = pl.BlockSpec((None, N_PAIRS, tm, 256), lambda b, i: (b, 0, i, 0))
    return pl.pallas_call(
        _mla_proj_kernel,
        grid=(B, S // tm),
        in_specs=[
            pl.BlockSpec((None, tm, D), lambda b, i: (b, i, 0)),
            pl.BlockSpec((1, D), const),
            pl.BlockSpec((D, 512), const),
            pl.BlockSpec((1, Q_LORA), const),
            pl.BlockSpec((1, KV_LORA), const),
            pl.BlockSpec((Q_LORA, N_HEADS * LANES), const),
            pl.BlockSpec((KV_LORA, N_HEADS * LANES), const),
            pl.BlockSpec((KV_LORA, N_HEADS * V_DIM), const),
            pl.BlockSpec((1, 256), const),
            pl.BlockSpec((1, 256), const),
            pl.BlockSpec((256, 256), const),
            pl.BlockSpec((1, 256), const),
            pl.BlockSpec((1, 256), const),
            tspec, tspec, tspec,
        ],
        out_specs=[qk_spec, qk_spec, _vt_spec(tm)],
        out_shape=[qk_shape, qk_shape, _vt_shape(B, S)],
        compiler_params=_cparams(("parallel", "parallel")),
        name="mla_proj",
    )(x, g.reshape(1, D), wd, gqa.reshape(1, Q_LORA), gkva.reshape(1, KV_LORA), wuq.astype(BF16),
      wuk.astype(BF16), wuv.astype(BF16), gq2, gk2, bd, qfix, kfix, *tabs)


def _stack_heads(q2, half):
    qf = q2.astype(F32)
    low = lax.broadcasted_iota(jnp.int32, qf.shape, 1) < half
    zero = jnp.zeros_like(qf)
    return jnp.concatenate([jnp.where(low, qf, zero).T, jnp.where(low, zero, qf).T], axis=1).astype(BF16)


def _flash_init(acc_ref):
    r = acc_ref.shape[1]
    acc_ref[...] = jnp.zeros(acc_ref.shape, F32)
    return jnp.full((1, r), M_INIT, F32), jnp.zeros((1, r), F32)


def _consume(s, vt, m, l, acc_ref, fixed):
    if fixed:
        p = jnp.exp2(s)
        acc_ref[...] += _dot(vt, p.astype(BF16))
        return m, l + jnp.sum(p, axis=0, keepdims=True)
    m_new = jnp.maximum(m, jnp.max(s, axis=0, keepdims=True))
    alpha = jnp.exp2(m - m_new)
    p = jnp.exp2(s - m_new)
    l_new = alpha * l + jnp.sum(p, axis=0, keepdims=True)
    acc_ref[...] = alpha * acc_ref[...] + _dot(vt, p.astype(BF16))
    return m_new, l_new


def _sweep(first, n, s_ref, scores, values, ml, acc_ref, fixed, peel=True, consume=None):
    def one(j, s, ml):
        if consume is not None:
            return consume(j, s, ml)
        return _consume(s, values(j), *ml, acc_ref, fixed)

    if isinstance(n, int):
        tiles = [scores(first + j) for j in range(n)]
        for j in range(n):
            ml = one(first + j, tiles[j], ml)
        return ml

    odd = n % 2 if peel else 0
    if peel:
        ml = lax.cond(odd == 1, lambda ml: one(first, scores(first), ml), lambda ml: ml, ml)
    j0 = first + odd
    last = first + n - 1

    @pl.when(n >= 2)
    def _():
        s_ref[0] = scores(j0)

    def body(t, ml):
        j = j0 + 2 * t
        s1 = scores(j + 1)
        ml = one(j, s_ref[0], ml)
        s_ref[1] = s1
        s0 = scores(jnp.minimum(j + 2, last))
        ml = one(j + 1, s_ref[1], ml)
        s_ref[0] = s0
        return ml

    return lax.fori_loop(0, n // 2, body, ml)


def _flash_out(l, acc_ref, tq):
    o = acc_ref[...] / l
    ot = jnp.concatenate([o[:HEAD_DIM, :tq], o[HEAD_DIM:, tq:]], axis=0)
    return ot.T.astype(BF16)


def _load_vt(vt_ref, kb, tk):
    n = tk // VT_BLK
    parts = [vt_ref[kb * n + r] for r in range(n)]
    return parts[0] if n == 1 else jnp.concatenate(parts, axis=1)


def _band_attn_kernel(q_ref, k_ref, vt_ref, bias_ref, o_ref, acc_ref, s_ref, *, tq, nblk, fixed):
    i = pl.program_id(2)
    qs = _stack_heads(q_ref[...], HEAD_DIM)
    lead = i - (nblk - 1)

    def scores(kb):
        ks = pl.multiple_of(kb * tq, tq)
        return _dot(k_ref[pl.ds(ks, tq), :], qs) + bias_ref[kb - lead]

    values = lambda kb: _load_vt(vt_ref, kb, tq)
    ml0 = _flash_init(acc_ref)
    _, l = lax.cond(
        lead >= 0,
        lambda ml: _sweep(lead, nblk, s_ref, scores, values, ml, acc_ref, fixed),
        lambda ml: _sweep(0, i + 1, s_ref, scores, values, ml, acc_ref, fixed),
        ml0)
    o_ref[...] = _flash_out(l, acc_ref, tq)


def _band_attn(q, k, vt, bias, fixed, tq=256):
    B, _, S, _ = q.shape
    nblk = bias.shape[1]
    return pl.pallas_call(
        functools.partial(_band_attn_kernel, tq=tq, nblk=nblk, fixed=fixed),
        grid=(B, N_PAIRS, S // tq),
        in_specs=[
            pl.BlockSpec((None, None, tq, LANES), lambda b, p, i: (b, p, i, 0)),
            pl.BlockSpec((None, None, S, LANES), lambda b, p, i: (b, p, 0, 0)),
            pl.BlockSpec((None, None, S // VT_BLK, LANES, VT_BLK), lambda b, p, i: (b, p, 0, 0, 0)),
            pl.BlockSpec((None, nblk, tq, 2 * tq), lambda b, p, i: (p, 0, 0, 0)),
        ],
        out_specs=pl.BlockSpec((None, tq, LANES), lambda b, p, i: (b, i, p)),
        out_shape=jax.ShapeDtypeStruct((B, S, N_HEADS * HEAD_DIM), BF16),
        scratch_shapes=[pltpu.VMEM((LANES, 2 * tq), F32), pltpu.VMEM((2, tq, 2 * tq), F32)],
        compiler_params=_cparams(("parallel", "parallel", "arbitrary")),
        name="band_attn",
    )(q, k, vt, bias)


def _mla_attn_kernel(q_ref, k_ref, vt_ref, o_ref, acc_ref, s_ref, *, tq, tk, fixed):
    i = pl.program_id(2)
    qs = _stack_heads(q_ref[...], LANES)
    t0 = i * tq
    assert tq % (2 * tk) == 0
    n_full = t0 // tk
    n_edge = tq // tk
    lim = ((t0 + lax.broadcasted_iota(jnp.int32, (1, tq), 1)) // CHUNK + 1) * CHUNK

    def scores(kb):
        ks = pl.multiple_of(kb * tk, tk)
        return _dot(k_ref[pl.ds(ks, tk), :], qs)

    def edge_scores(kb):
        row = kb * tk + lax.broadcasted_iota(jnp.int32, (tk, tq), 0)
        add = jnp.where(row < lim, 0.0, NEG).astype(F32)
        return scores(kb) + jnp.concatenate([add, add], axis=1)

    values = lambda kb: _load_vt(vt_ref, kb, tk)
    ml = _sweep(0, n_full, s_ref, scores, values, _flash_init(acc_ref), acc_ref, fixed, peel=False)
    _, l = _sweep(n_full, n_edge, s_ref, edge_scores, values, ml, acc_ref, fixed, peel=False)
    o_ref[...] = _flash_out(l, acc_ref, tq)


def _mla_attn(q, k, vt, fixed, tq=1024, tk=256):
    B, _, S, _ = q.shape
    tk = min(tk, S)
    return pl.pallas_call(
        functools.partial(_mla_attn_kernel, tq=tq, tk=tk, fixed=fixed),
        grid=(B, N_PAIRS, S // tq),
        in_specs=[
            pl.BlockSpec((None, None, tq, 256), lambda b, p, i: (b, p, i, 0)),
            pl.BlockSpec((None, None, S, 256), lambda b, p, i: (b, p, 0, 0)),
            pl.BlockSpec((None, None, S // VT_BLK, LANES, VT_BLK), lambda b, p, i: (b, p, 0, 0, 0)),
        ],
        out_specs=pl.BlockSpec((None, tq, LANES), lambda b, p, i: (b, i, p)),
        out_shape=jax.ShapeDtypeStruct((B, S, N_HEADS * V_DIM), BF16),
        scratch_shapes=[pltpu.VMEM((LANES, 2 * tq), F32), pltpu.VMEM((2, tk, 2 * tq), F32)],
        compiler_params=_cparams(("parallel", "parallel", "arbitrary")),
        name="mla_attn",
    )(q, k, vt)


def _dsa_attn_kernel(q_ref, qp_ref, wi_ref, shift_ref, bias_ref, k_ref, vt_ref, kp_ref, o_ref,
                     sc_ref, h_ref, acc_ref, stage_ref, s_ref, qs_ref, m_ref, l_ref, *, tq, tk, topk, fixed):
    i = pl.program_id(1)
    t0 = i * tq
    n_blk = (t0 + tq + tk - 1) // tk
    n_far = jnp.maximum((t0 // LANES - 1) // (tk // LANES), 0)
    n_far_even = n_far - n_far % 2
    lim = ((t0 + lax.broadcasted_iota(jnp.int32, (1, tq), 1)) // CHUNK + 1) * CHUNK

    wv = wi_ref[...]

    def score_body(kb, carry):
        ks = pl.multiple_of(kb * tk, tk)
        kp = kp_ref[pl.ds(ks, tk), :]
        acc = jnp.zeros((tk, tq), F32)
        for h in range(H_IDX):
            acc = acc + jnp.maximum(_dot(kp, qp_ref[h]), 0.0) * wv[h:h + 1, :]
        row = ks + lax.broadcasted_iota(jnp.int32, (tk, tq), 0)
        acc = jnp.where(row < lim, acc, -jnp.inf)
        bits = pltpu.bitcast(acc, jnp.int32)
        key = bits ^ ((bits >> 31) & 0x7FFFFFFF)
        sc_ref[kb] = key
        h_ref[kb] = (key >> 16).astype(jnp.int16)
        return carry

    lax.fori_loop(0, n_blk, score_body, 0)

    def count16(pred):
        def count_body(kb, part):
            hit = jnp.where(pred(h_ref[kb]), jnp.int16(1), jnp.int16(0))
            rows = hit[0:16]
            for r in range(16, tk, 16):
                rows = rows + hit[r:r + 16]
            return part + rows.astype(jnp.int32)

        part = lax.fori_loop(0, n_blk, count_body, jnp.zeros((16, tq), jnp.int32))
        return jnp.sum(part, axis=0, keepdims=True)

    def kth16(need):
        def bit_body(t, ans):
            cand = ans + jnp.left_shift(jnp.int32(1), 15 - t)
            cand16 = cand.astype(jnp.int16)
            return jnp.where(count16(lambda h: h >= cand16) >= need, cand, ans)

        return lax.fori_loop(0, 16, bit_body, jnp.full((1, tq), -(2 ** 15), jnp.int32))

    top = kth16(topk)
    top16 = top.astype(jnp.int16)
    need = topk - count16(lambda h: h > top16)

    def low_body(kb, carry):
        low = ((sc_ref[kb] & 0xFFFF) - 2 ** 15).astype(jnp.int16)
        h_ref[kb] = jnp.where(h_ref[kb] == top16, low, jnp.int16(-(2 ** 15)))
        return carry

    lax.fori_loop(0, n_blk, low_body, 0)
    low = kth16(need)
    thr = jnp.left_shift(top, 16) + (low + 2 ** 15)

    low16 = low.astype(jnp.int16)
    ties = (need - count16(lambda h: h > low16)).astype(F32)
    keep_bits = pltpu.bitcast(shift_ref[...], jnp.int32)
    below = (lax.broadcasted_iota(jnp.int32, (tk, tk), 0) > lax.broadcasted_iota(jnp.int32, (tk, tk), 1))
    below = jnp.where(below, 1.0, 0.0).astype(BF16)

    def mask_body(kb, seen):
        key = sc_ref[kb]
        row = kb * tk + lax.broadcasted_iota(jnp.int32, (tk, tq), 0)
        tie = jnp.where(key == thr, 1.0, 0.0)
        before = seen + _dot(below, tie.astype(BF16))
        thr_row = thr + jnp.where(before >= ties, 1, 0)
        sc_ref[kb] = jnp.where((key >= thr_row) & (row < lim), keep_bits, NEG_BITS)
        return seen + jnp.sum(tie, axis=0, keepdims=True)

    lax.fori_loop(0, n_blk, mask_body, jnp.zeros((1, tq), F32))

    def prep(p, carry):
        qs_ref[p] = _stack_heads(q_ref[p], HEAD_DIM)
        m_ref[p] = jnp.full((1, 2 * tq), M_INIT, F32)
        l_ref[p] = jnp.zeros((1, 2 * tq), F32)
        acc_ref[p] = jnp.zeros((LANES, 2 * tq), F32)
        return carry

    lax.fori_loop(0, N_PAIRS, prep, 0)

    def scores(t):
        kb, p = t // N_PAIRS, t % N_PAIRS
        ks = pl.multiple_of(kb * tk, tk)
        am = pltpu.bitcast(sc_ref[kb], F32)
        cols = []
        for hh in range(2):
            for a in range(tq // LANES):
                tiles = []
                for c in range(tk // LANES):
                    d = (kb * (tk // LANES) + c) - (i * (tq // LANES) + a)
                    kind = jnp.where(d == 0, 2, jnp.where(d == -1, 1, 0))
                    tiles.append(bias_ref[p, hh, kind])
                cols.append(jnp.concatenate(tiles, axis=0))
        return (_dot(k_ref[p, pl.ds(ks, tk), :], qs_ref[p]) + jnp.concatenate([am, am], axis=1)
                + jnp.concatenate(cols, axis=1))

    def consume(t, s, carry):
        kb, p = t // N_PAIRS, t % N_PAIRS
        m, l = _consume(s, _load_vt(vt_ref.at[p], kb, tk), m_ref[p], l_ref[p], acc_ref.at[p], fixed)
        l_ref[p] = l
        if not fixed:
            m_ref[p] = m
        return carry

    _sweep(0, n_blk * N_PAIRS, s_ref, scores, None, 0, None, fixed, peel=False, consume=consume)

    def finish(p, carry):
        stage_ref[p] = _flash_out(l_ref[p], acc_ref.at[p], tq)
        return carry

    lax.fori_loop(0, N_PAIRS, finish, 0)
    for p in range(N_PAIRS):
        o_ref[:, p * LANES:(p + 1) * LANES] = stage_ref[p]


def _dsa_attn(q, k, vt, qp, kp, wi, shift, bias, fixed, tq=256, tk=512):
    B, _, S, _ = q.shape
    topk = min(TOPK_MAX, S // 4)
    once = pl.Buffered(1)
    return pl.pallas_call(
        functools.partial(_dsa_attn_kernel, tq=tq, tk=tk, topk=topk, fixed=fixed),
        grid=(B, S // tq),
        in_specs=[
            pl.BlockSpec((None, N_PAIRS, tq, LANES), lambda b, i: (b, 0, i, 0)),
            pl.BlockSpec((None, H_IDX, 256, tq), lambda b, i: (b, 0, 0, i)),
            pl.BlockSpec((None, H_IDX, tq), lambda b, i: (b, 0, i)),
            pl.BlockSpec((1, tq), lambda b, i: (0, 0)),
            pl.BlockSpec((N_PAIRS, 2, 3, LANES, LANES), lambda b, i: (0, 0, 0, 0, 0), pipeline_mode=once),
            pl.BlockSpec((None, N_PAIRS, S, LANES), lambda b, i: (b, 0, 0, 0), pipeline_mode=once),
            pl.BlockSpec((None, N_PAIRS, S // VT_BLK, LANES, VT_BLK), lambda b, i: (b, 0, 0, 0, 0),
                         pipeline_mode=once),
            pl.BlockSpec((None, S, 256), lambda b, i: (b, 0, 0), pipeline_mode=once),
        ],
        out_specs=pl.BlockSpec((None, tq, N_HEADS * HEAD_DIM), lambda b, i: (b, i, 0)),
        out_shape=jax.ShapeDtypeStruct((B, S, N_HEADS * HEAD_DIM), BF16),
        scratch_shapes=[
            pltpu.VMEM((S // tk, tk, tq), jnp.int32),
            pltpu.VMEM((S // tk, tk, tq), jnp.int16),
            pltpu.VMEM((N_PAIRS, LANES, 2 * tq), F32),
            pltpu.VMEM((N_PAIRS, tq, LANES), BF16),
            pltpu.VMEM((2, tk, 2 * tq), F32),
            pltpu.VMEM((N_PAIRS, LANES, 2 * tq), BF16),
            pltpu.VMEM((N_PAIRS, 1, 2 * tq), F32),
            pltpu.VMEM((N_PAIRS, 1, 2 * tq), F32),
        ],
        compiler_params=_cparams(("parallel", "arbitrary"), VMEM_LIMIT_DSA),
        name="dsa_attn",
    )(q, qp, wi, jnp.full((1, tq), shift, F32), bias, k, vt, kp)


def _out_proj_kernel(x_ref, a_ref, w_ref, o_ref):
    o_ref[...] = x_ref[...] + _dot(a_ref[...], w_ref[...])


def _out_proj(x, a, w, tm=512):
    B, S, D = x.shape
    return pl.pallas_call(
        _out_proj_kernel,
        grid=(B, S // tm),
        in_specs=[
            pl.BlockSpec((None, tm, D), lambda b, i: (b, i, 0)),
            pl.BlockSpec((None, tm, a.shape[-1]), lambda b, i: (b, i, 0)),
            pl.BlockSpec(w.shape, lambda b, i: (0, 0)),
        ],
        out_specs=pl.BlockSpec((None, tm, D), lambda b, i: (b, i, 0)),
        out_shape=jax.ShapeDtypeStruct((B, S, D), F32),
        compiler_params=_cparams(("parallel", "parallel")),
        name="out_proj",
    )(x, a, w.astype(BF16))


def _ffn_kernel(x_ref, xp_ref, g_ref, wg_ref, wv_ref, cw_ref, cb_ref, wd_ref, o_ref, acc_ref, he_ref, u_ref,
                *, tm, n_chunks):
    i = pl.program_id(1)
    g = g_ref[...]
    he_ref[:HALO] = (_rms(xp_ref[...], g) * jnp.where(i > 0, 1.0, 0.0)).astype(BF16)
    he_ref[HALO:] = _rms(x_ref[...], g).astype(BF16)
    acc_ref[...] = jnp.zeros(acc_ref.shape, F32)

    def up(c, slot):
        he = he_ref[...]
        u_ref[slot, 0] = _dot(he, wg_ref[c])
        u_ref[slot, 1] = _dot(he, wv_ref[c])

    def conv(u, w, b):
        return (b + w[0:1] * u[HALO - 2:HALO - 2 + tm] + w[1:2] * u[HALO - 1:HALO - 1 + tm]
                + w[2:3] * u[HALO:HALO + tm])

    def down(c, slot):
        cw = cw_ref[c]
        cb = cb_ref[c]
        gate = conv(u_ref[slot, 0], cw[0], cb[0])
        val = conv(u_ref[slot, 1], cw[1], cb[1])
        act = gate * jax.nn.sigmoid(gate) * val
        acc_ref[...] += _dot(act.astype(BF16), wd_ref[c])

    up(0, 0)

    def pair_body(t, carry):
        c = 2 * t
        up(c + 1, 1)
        down(c, 0)
        up(c + 2, 0)
        down(c + 1, 1)
        return carry

    assert n_chunks % 2 == 1
    lax.fori_loop(0, n_chunks // 2, pair_body, 0)
    down(n_chunks - 1, 0)
    o_ref[...] = x_ref[...] + acc_ref[...]


def _ffn(x, g, w_up, conv_w, conv_b, w_down, tm=512):
    B, S, D = x.shape
    nc = D_FF // FF_CHUNK
    wup = w_up.astype(BF16).reshape(D, 2, nc, FF_CHUNK).transpose(1, 2, 0, 3)
    cw = conv_w.reshape(CONV_W, 2, nc, FF_CHUNK).transpose(2, 1, 0, 3)
    cb = conv_b.reshape(2, nc, 1, FF_CHUNK).transpose(1, 0, 2, 3)
    wd = w_down.astype(BF16).reshape(nc, FF_CHUNK, D)
    c3 = lambda b, i: (0, 0, 0)
    c4 = lambda b, i: (0, 0, 0, 0)
    return pl.pallas_call(
        functools.partial(_ffn_kernel, tm=tm, n_chunks=nc),
        grid=(B, S // tm),
        in_specs=[
            pl.BlockSpec((None, tm, D), lambda b, i: (b, i, 0)),
            pl.BlockSpec((None, HALO, D), lambda b, i: (b, jnp.maximum(i * (tm // HALO) - 1, 0), 0)),
            pl.BlockSpec((1, D), lambda b, i: (0, 0)),
            pl.BlockSpec((nc, D, FF_CHUNK), c3),
            pl.BlockSpec((nc, D, FF_CHUNK), c3),
            pl.BlockSpec((nc, 2, CONV_W, FF_CHUNK), c4),
            pl.BlockSpec((nc, 2, 1, FF_CHUNK), c4),
            pl.BlockSpec((nc, FF_CHUNK, D), c3),
        ],
        out_specs=pl.BlockSpec((None, tm, D), lambda b, i: (b, i, 0)),
        out_shape=jax.ShapeDtypeStruct((B, S, D), F32),
        scratch_shapes=[pltpu.VMEM((tm, D), F32), pltpu.VMEM((HALO + tm, D), BF16),
                        pltpu.VMEM((2, 2, HALO + tm, FF_CHUNK), F32)],
        compiler_params=_cparams(("parallel", "parallel")),
        name="conv_ffn",
    )(x, x, g.reshape(1, D), wup[0], wup[1], cw, cb, wd)


def _rope_tables(S, period, off):
    inv = 1.0 / (ROPE_THETA ** (jnp.arange(0, ROPE_DIM, 2, dtype=F32) / ROPE_DIM))
    ang = jnp.arange(S, dtype=F32)[:, None] * inv[None, :]
    cos, sin = jnp.cos(ang), jnp.sin(ang)
    one = jnp.ones((S, 1), F32)
    zero = jnp.zeros((S, 1), F32)

    def lay(x1, x2, fill):
        group = jnp.concatenate([jnp.tile(fill, (1, off)), x1, x2,
                                 jnp.tile(fill, (1, period - off - ROPE_DIM))], axis=1)
        return jnp.tile(group, (1, LANES // period))

    return lay(cos, cos, one), lay(-sin, zero * sin, zero), lay(zero * sin, sin, zero)


def _t5_bucket(rel):
    nb = T5_BUCKETS // 2
    max_exact = nb // 2
    n = jnp.abs(rel)
    large = max_exact + (jnp.log(jnp.maximum(n, 1).astype(F32) / max_exact)
                         / math.log(T5_MAX_DIST / max_exact) * (nb - max_exact)).astype(jnp.int32)
    large = jnp.minimum(large, nb - 1)
    return jnp.where(rel > 0, nb, 0) + jnp.where(n < max_exact, n, large)


def _dsa_bias_tiles(t5_bias):
    kj = jnp.arange(LANES, dtype=jnp.int32)[:, None]
    qi = jnp.arange(LANES, dtype=jnp.int32)[None, :]
    far = t5_bias[_t5_bucket(jnp.int32(-4 * T5_MAX_DIST))]
    same = (jnp.transpose(t5_bias[_t5_bucket(kj - qi)], (2, 0, 1)) - far[:, None, None]) * LOG2E
    prev = (jnp.transpose(t5_bias[_t5_bucket(kj - LANES - qi)], (2, 0, 1)) - far[:, None, None]) * LOG2E
    return jnp.stack([jnp.zeros_like(same), prev, same], axis=1).reshape(N_PAIRS, 2, 3, LANES, LANES)


def _softmax_shift(gq, gk, dim, bias_hi=0.0, bias_lo=0.0):
    qk = 1.02 * dim * jnp.max(jnp.abs(gq)) * jnp.max(jnp.abs(gk))
    top = qk + bias_hi
    fixed = (top + qk - bias_lo) <= SAFE_SPREAD
    return fixed, jnp.where(fixed, -top, 0.0)


def _band_bias(rel_bias, shift, tq):
    pad = LEFT_CHUNKS * CHUNK
    nblk = -(-pad // tq) + 1
    lead = (nblk - 1) * tq
    nk = nblk * tq
    kk = jnp.arange(nk, dtype=jnp.int32)[:, None] - lead
    i = jnp.arange(tq, dtype=jnp.int32)[None, :]
    first = (i // CHUNK) * CHUNK - pad
    ok = (kk >= first) & (kk < (i // CHUNK + 1) * CHUNK)
    L = nk + tq
    dist = jnp.arange(L, dtype=jnp.int32)
    dist = lead + jnp.where(dist < tq, dist, dist - L)
    v = rel_bias[:, jnp.clip(dist, -(CHUNK - 1), REL_MAX_PAST) + CHUNK - 1]
    b = jnp.tile(v, (1, nk))[:, :nk * (L - 1)].reshape(v.shape[0], nk, L - 1)[:, :, :tq]
    b = jnp.where(ok[None], b + shift, NEG).reshape(N_PAIRS, 2, nblk, tq, tq)
    return jnp.transpose(b, (0, 2, 3, 1, 4)).reshape(N_PAIRS, nblk, tq, 2 * tq)


def kernel(x, norm_mix, norm_ffn, t5_bias, a_w_in, a_q_norm, a_k_norm, a_kidx_norm, a_w_out, b_w_down, b_q_a_norm, b_kv_a_norm, b_w_uq, b_w_ukv, b_q_norm, b_k_norm, b_w_out, c_w_in, c_q_norm, c_k_norm, c_rel_bias, c_w_out, f_w_up, f_conv_w, f_conv_b, f_w_down):
    B, S, D = x.shape
    depth = norm_mix.shape[0]

    tq_c = 256
    tabs_idx = _rope_tables(S, D_IDX, D_IDX - ROPE_DIM)
    tabs_mla = _rope_tables(S, LANES, QK_NOPE)
    scale = HEAD_DIM ** -0.5 * LOG2E
    for layer in range(depth):
        kind, j = layer % N_MIXERS, layer // N_MIXERS
        g = norm_mix[layer]
        if kind == 0:
            gq = a_q_norm[j] * scale
            q, k, vt = _qkv_proj(x, g, a_w_in[j][:, :A_QKV], gq, a_k_norm[j])
            qp, kp, wi = _idx_proj(x, g, a_w_in[j][:, A_QKV:], a_kidx_norm[j], tabs_idx)
            tiles = _dsa_bias_tiles(t5_bias)
            fixed, shift = _softmax_shift(gq, a_k_norm[j], HEAD_DIM, jnp.max(tiles), jnp.min(tiles))
            att = lax.cond(fixed, functools.partial(_dsa_attn, fixed=True), functools.partial(_dsa_attn, fixed=False),
                           q, k, vt, qp, kp, wi, shift, tiles)
            w_out = a_w_out[j]
        elif kind == 1:
            fixed, shift = _softmax_shift(b_q_norm[j] * (B_QK ** -0.5 * LOG2E), b_k_norm[j], B_QK)
            q, k, vt = _mla_proj(x, g, b_w_down[j], b_q_a_norm[j], b_kv_a_norm[j], b_w_uq[j], b_w_ukv[j],
                                 b_q_norm[j], b_k_norm[j], shift, tabs_mla)
            att = lax.cond(fixed, functools.partial(_mla_attn, fixed=True), functools.partial(_mla_attn, fixed=False),
                           q, k, vt)
            w_out = b_w_out[j]
        else:
            gq = c_q_norm[j] * scale
            q, k, vt = _qkv_proj(x, g, c_w_in[j], gq, c_k_norm[j])
            rel = c_rel_bias[j] * LOG2E
            fixed, shift = _softmax_shift(gq, c_k_norm[j], HEAD_DIM, jnp.max(rel), jnp.min(rel))
            att = lax.cond(fixed, functools.partial(_band_attn, fixed=True, tq=tq_c),
                           functools.partial(_band_attn, fixed=False, tq=tq_c),
                           q, k, vt, _band_bias(rel, shift, tq_c))
            w_out = c_w_out[j]
        x = _out_proj(x, att, w_out)
        x = _ffn(x, norm_ffn[layer], f_w_up[layer], f_conv_w[layer], f_conv_b[layer], f_w_down[layer])
    return x
```

```python
import functools
import math

import numpy as np
import jax
import jax.numpy as jnp
from jax import lax
from jax.experimental import pallas as pl
from jax.experimental.pallas import tpu as pltpu

F32 = jnp.float32
BF16 = jnp.bfloat16

D_MODEL = 1024
CHUNK = 64
EPS = 1e-6
ROPE_THETA = 10000.0
NEG = -1e30
NEG_BITS = int(np.float32(NEG).view(np.int32))
M_INIT = -1e38
LOG2E = math.log2(math.e)
SAFE_SPREAD = 100.0

N_MIXERS = 3
N_HEADS = 16
HEAD_DIM = 64
N_PAIRS = N_HEADS // 2
LANES = 128
VT_BLK = 256

H_IDX = 8
D_IDX = 64
ROPE_DIM = 32
TOPK_MAX = 256
T5_BUCKETS = 32
T5_MAX_DIST = 128
A_QKV = 3 * N_HEADS * HEAD_DIM
Q_LORA = 256
KV_LORA = 128
QK_NOPE = 64
QK_ROPE = 32
V_DIM = 64
B_QK = QK_NOPE + QK_ROPE
LEFT_CHUNKS = 8
REL_MAX_PAST = 128
D_FF = 2816
CONV_W = 3
FF_CHUNK = 256
HALO = 8

VMEM_LIMIT = 56 * 1024 * 1024
VMEM_LIMIT_DSA = 60 * 1024 * 1024


def _cparams(sem, vmem_limit=VMEM_LIMIT):
    return pltpu.CompilerParams(dimension_semantics=sem, vmem_limit_bytes=vmem_limit)


def _rms(x, g):
    return x * lax.rsqrt(jnp.mean(x * x, axis=-1, keepdims=True) + EPS) * g


def _dot(a, b):
    return jnp.dot(a, b, preferred_element_type=F32)


def _split_bf16(x):
    hi = x.astype(BF16)
    lo = (x - hi.astype(F32)).astype(BF16)
    return hi, lo


def _group_sumsq(y, bd):
    hi, lo = _split_bf16(y * y)
    return _dot(hi, bd) + _dot(lo, bd)


def _rope(y, ct, s1, s2):
    return y * ct + pltpu.roll(y, LANES - 16, 1) * s1 + pltpu.roll(y, 16, 1) * s2


def _rope_wide(y, ct, s1, s2):
    w = y.shape[-1]
    parts = [_rope(y[:, c:c + LANES], ct, s1, s2) for c in range(0, w, LANES)]
    return parts[0] if len(parts) == 1 else jnp.concatenate(parts, axis=-1)


def _store_vt(vt_ref, p, y):
    yt = y.T.astype(BF16)
    for r in range(y.shape[0] // VT_BLK):
        vt_ref[p, r] = yt[:, r * VT_BLK:(r + 1) * VT_BLK]


def _qkv_proj_kernel(x_ref, g_ref, w_ref, gq_ref, gk_ref, bd_ref, q_ref, k_ref, vt_ref):
    xb = _rms(x_ref[...], g_ref[...]).astype(BF16)
    bd = bd_ref[...]
    hd = N_HEADS * HEAD_DIM
    for which, (o_ref, gain_ref) in enumerate(((q_ref, gq_ref), (k_ref, gk_ref), (vt_ref, None))):
        for c in range(hd // 256):
            col = which * hd + c * 256
            y = _dot(xb, w_ref[:, col:col + 256])
            if gain_ref is None:
                _store_vt(o_ref, 2 * c, y[:, :LANES])
                _store_vt(o_ref, 2 * c + 1, y[:, LANES:])
            else:
                y = y * lax.rsqrt(_group_sumsq(y, bd) * (1.0 / HEAD_DIM) + EPS) * gain_ref[...]
                yb = y.astype(BF16)
                o_ref[2 * c] = yb[:, :LANES]
                o_ref[2 * c + 1] = yb[:, LANES:]


def _vt_spec(tm):
    return pl.BlockSpec((None, N_PAIRS, tm // VT_BLK, LANES, VT_BLK), lambda b, i: (b, 0, i, 0, 0))


def _vt_shape(B, S):
    return jax.ShapeDtypeStruct((B, N_PAIRS, S // VT_BLK, LANES, VT_BLK), BF16)


def _qkv_proj(x, g, w, gq, gk, tm=512):
    B, S, D = x.shape
    hd = N_HEADS * HEAD_DIM
    bd = jnp.kron(jnp.eye(256 // HEAD_DIM, dtype=F32), jnp.ones((HEAD_DIM, HEAD_DIM), F32)).astype(BF16)
    out = jax.ShapeDtypeStruct((B, N_PAIRS, S, LANES), BF16)
    const = lambda b, i: (0, 0)
    ospec = pl.BlockSpec((None, N_PAIRS, tm, LANES), lambda b, i: (b, 0, i, 0))
    return pl.pallas_call(
        _qkv_proj_kernel,
        grid=(B, S // tm),
        in_specs=[
            pl.BlockSpec((None, tm, D), lambda b, i: (b, i, 0)),
            pl.BlockSpec((1, D), const),
            pl.BlockSpec((D, 3 * hd), const),
            pl.BlockSpec((1, 256), const),
            pl.BlockSpec((1, 256), const),
            pl.BlockSpec((256, 256), const),
        ],
        out_specs=[ospec, ospec, _vt_spec(tm)],
        out_shape=[out, out, _vt_shape(B, S)],
        compiler_params=_cparams(("parallel", "parallel")),
        name="qkv_proj",
    )(x, g.reshape(1, D), w.astype(BF16), jnp.tile(gq, 4).reshape(1, 256), jnp.tile(gk, 4).reshape(1, 256), bd)


def _idx_proj_kernel(x_ref, g_ref, wh_ref, wl_ref, gki_ref, ct_ref, s1_ref, s2_ref, qp_ref, kp_ref, wi_ref,
                     *, w_scale):
    xn = _rms(x_ref[...], g_ref[...])
    xh, xl = _split_bf16(xn)

    def mm(c0, c1):
        wh = wh_ref[:, c0:c1]
        return _dot(xh, wh) + (_dot(xl, wh) + _dot(xh, wl_ref[:, c0:c1]))

    ct, s1, s2 = ct_ref[...], s1_ref[...], s2_ref[...]
    tm = xn.shape[0]
    low = lax.broadcasted_iota(jnp.int32, (tm, LANES), 1) < D_IDX
    zero = jnp.zeros((tm, LANES), F32)

    for c in range(H_IDX // 2):
        y = _rope(mm(c * LANES, (c + 1) * LANES), ct, s1, s2)
        hi = y.astype(BF16).astype(F32)
        lo = y - hi
        rhi = pltpu.roll(hi, D_IDX, 1)
        rlo = pltpu.roll(lo, D_IDX, 1)
        qp_ref[2 * c, :LANES, :] = jnp.where(low, hi, rlo).T.astype(BF16)
        qp_ref[2 * c, LANES:, :] = jnp.where(low, hi, zero).T.astype(BF16)
        qp_ref[2 * c + 1, :LANES, :] = jnp.where(low, rhi, lo).T.astype(BF16)
        qp_ref[2 * c + 1, LANES:, :] = jnp.where(low, rhi, zero).T.astype(BF16)

    c0 = H_IDX * D_IDX
    y = mm(c0, c0 + LANES)
    y = y * lax.rsqrt(jnp.sum(y * y, axis=-1, keepdims=True) * (1.0 / D_IDX) + EPS) * gki_ref[...]
    y = _rope(y, ct, s1, s2)
    hi = y.astype(BF16).astype(F32)
    lo = y - hi
    kp_ref[:, :LANES] = jnp.where(low, hi, pltpu.roll(hi, D_IDX, 1)).astype(BF16)
    kp_ref[:, LANES:] = jnp.where(low, lo, zero).astype(BF16)

    wi_ref[...] = (mm(c0 + LANES, c0 + 2 * LANES) * w_scale).T[:H_IDX]


def _idx_proj(x, g, w_idx, gki, tabs, tm=512):
    B, S, D = x.shape
    nq = H_IDX * D_IDX
    wq, wk, ww = w_idx[:, :nq], w_idx[:, nq:nq + D_IDX], w_idx[:, nq + D_IDX:]
    wpad = jnp.concatenate([
        wq, jnp.pad(wk, ((0, 0), (0, LANES - D_IDX))), jnp.pad(ww, ((0, 0), (0, LANES - H_IDX)))], axis=1)
    wh = wpad.astype(BF16)
    wl = (wpad - wh.astype(F32)).astype(BF16)
    ncol = nq + 2 * LANES
    gpad = jnp.pad(gki, (0, LANES - D_IDX)).reshape(1, LANES)
    const = lambda b, i: (0, 0)
    tspec = pl.BlockSpec((tm, LANES), lambda b, i: (i, 0))
    return pl.pallas_call(
        functools.partial(_idx_proj_kernel, w_scale=(D_IDX ** -0.5) * (H_IDX ** -0.5)),
        grid=(B, S // tm),
        in_specs=[
            pl.BlockSpec((None, tm, D), lambda b, i: (b, i, 0)),
            pl.BlockSpec((1, D), const),
            pl.BlockSpec((D, ncol), const),
            pl.BlockSpec((D, ncol), const),
            pl.BlockSpec((1, LANES), const),
            tspec, tspec, tspec,
        ],
        out_specs=[
            pl.BlockSpec((None, H_IDX, 256, tm), lambda b, i: (b, 0, 0, i)),
            pl.BlockSpec((None, tm, 256), lambda b, i: (b, i, 0)),
            pl.BlockSpec((None, H_IDX, tm), lambda b, i: (b, 0, i)),
        ],
        out_shape=[
            jax.ShapeDtypeStruct((B, H_IDX, 256, S), BF16),
            jax.ShapeDtypeStruct((B, S, 256), BF16),
            jax.ShapeDtypeStruct((B, H_IDX, S), F32),
        ],
        compiler_params=_cparams(("parallel", "parallel")),
        name="idx_proj",
    )(x, g.reshape(1, D), wh, wl, gpad, *tabs)


def _mla_proj_kernel(x_ref, g_ref, wd_ref, gqa_ref, gkva_ref, wuq_ref, wuk_ref, wuv_ref, gq_ref, gk_ref, bd_ref,
                     qfix_ref, kfix_ref, ct_ref, s1_ref, s2_ref, q_ref, k_ref, vt_ref):
    xb = _rms(x_ref[...], g_ref[...]).astype(BF16)
    d = _dot(xb, wd_ref[...])
    cq = _rms(d[:, :Q_LORA], gqa_ref[...]).astype(BF16)
    ckv = _rms(d[:, Q_LORA:Q_LORA + KV_LORA], gkva_ref[...]).astype(BF16)
    krz = pltpu.roll(d[:, Q_LORA + KV_LORA:], QK_NOPE, 1)
    krz2 = jnp.concatenate([krz, krz], axis=-1)
    bd = bd_ref[...]
    ct, s1, s2 = ct_ref[...], s1_ref[...], s2_ref[...]
    inv = 1.0 / B_QK
    for p in range(N_PAIRS):
        cols = slice(p * 256, (p + 1) * 256)
        y = _dot(cq, wuq_ref[:, cols])
        y = y * lax.rsqrt(_group_sumsq(y, bd) * inv + EPS) * gq_ref[...]
        q_ref[p] = (_rope_wide(y, ct, s1, s2) + qfix_ref[...]).astype(BF16)
        y = _dot(ckv, wuk_ref[:, cols]) + krz2
        y = y * lax.rsqrt(_group_sumsq(y, bd) * inv + EPS) * gk_ref[...]
        k_ref[p] = (_rope_wide(y, ct, s1, s2) + kfix_ref[...]).astype(BF16)
        _store_vt(vt_ref, p, _dot(ckv, wuv_ref[:, p * LANES:(p + 1) * LANES]))


def _mla_proj(x, g, w_down, gqa, gkva, w_uq, w_ukv, gq, gk, shift, tabs, tm=512):
    B, S, D = x.shape
    nd = Q_LORA + KV_LORA + QK_ROPE
    wd = jnp.pad(w_down, ((0, 0), (0, 512 - nd))).astype(BF16)
    padh = LANES - B_QK
    wuq = jnp.pad(w_uq.reshape(Q_LORA, N_HEADS, B_QK), ((0, 0), (0, 0), (0, padh))).reshape(Q_LORA, N_HEADS * LANES)
    wukv = w_ukv.reshape(KV_LORA, N_HEADS, QK_NOPE + V_DIM)
    wuk = jnp.pad(wukv[:, :, :QK_NOPE], ((0, 0), (0, 0), (0, LANES - QK_NOPE))).reshape(KV_LORA, N_HEADS * LANES)
    wuv = wukv[:, :, QK_NOPE:].reshape(KV_LORA, N_HEADS * V_DIM)
    gq2 = jnp.tile(jnp.pad(gq * (B_QK ** -0.5 * LOG2E), (0, padh)), 2).reshape(1, 256)
    gk2 = jnp.tile(jnp.pad(gk, (0, padh)), 2).reshape(1, 256)
    bd = jnp.kron(jnp.eye(2, dtype=F32), jnp.ones((LANES, LANES), F32)).astype(BF16)
    kfix = jnp.asarray((np.arange(256) % LANES == B_QK).astype(np.float32)).reshape(1, 256)
    qfix = kfix * shift
    const = lambda b, i: (0, 0)
    tspec = pl.BlockSpec((tm, LANES), lambda b, i: (i, 0))
    qk_shape = jax.ShapeDtypeStruct((B, N_PAIRS, S, 256), BF16)
    qk_spec = pl.BlockSpec((None, N_PAIRS, tm, 256), lambda b, i: (b, 0, i, 0))
    return pl.pallas_call(
        _mla_proj_kernel,
        grid=(B, S // tm),
        in_specs=[
            pl.BlockSpec((None, tm, D), lambda b, i: (b, i, 0)),
            pl.BlockSpec((1, D), const),
            pl.BlockSpec((D, 512), const),
            pl.BlockSpec((1, Q_LORA), const),
            pl.BlockSpec((1, KV_LORA), const),
            pl.BlockSpec((Q_LORA, N_HEADS * LANES), const),
            pl.BlockSpec((KV_LORA, N_HEADS * LANES), const),
            pl.BlockSpec((KV_LORA, N_HEADS * V_DIM), const),
            pl.BlockSpec((1, 256), const),
            pl.BlockSpec((1, 256), const),
            pl.BlockSpec((256, 256), const),
            pl.BlockSpec((1, 256), const),
            pl.BlockSpec((1, 256), const),
            tspec, tspec, tspec,
        ],
        out_specs=[qk_spec, qk_spec, _vt_spec(tm)],
        out_shape=[qk_shape, qk_shape, _vt_shape(B, S)],
        compiler_params=_cparams(("parallel", "parallel")),
        name="mla_proj",
    )(x, g.reshape(1, D), wd, gqa.reshape(1, Q_LORA), gkva.reshape(1, KV_LORA), wuq.astype(BF16),
      wuk.astype(BF16), wuv.astype(BF16), gq2, gk2, bd, qfix, kfix, *tabs)


def _stack_heads(q2, half):
    qf = q2.astype(F32)
    low = lax.broadcasted_iota(jnp.int32, qf.shape, 1) < half
    zero = jnp.zeros_like(qf)
    return jnp.concatenate([jnp.where(low, qf, zero).T, jnp.where(low, zero, qf).T], axis=1).astype(BF16)


def _flash_init(acc_ref):
    r = acc_ref.shape[1]
    acc_ref[...] = jnp.zeros(acc_ref.shape, F32)
    return jnp.full((1, r), M_INIT, F32), jnp.zeros((1, r), F32)


def _consume(s, vt, m, l, acc_ref, fixed):
    if fixed:
        p = jnp.exp2(s)
        acc_ref[...] += _dot(vt, p.astype(BF16))
        return m, l + jnp.sum(p, axis=0, keepdims=True)
    m_new = jnp.maximum(m, jnp.max(s, axis=0, keepdims=True))
    alpha = jnp.exp2(m - m_new)
    p = jnp.exp2(s - m_new)
    l_new = alpha * l + jnp.sum(p, axis=0, keepdims=True)
    acc_ref[...] = alpha * acc_ref[...] + _dot(vt, p.astype(BF16))
    return m_new, l_new


def _sweep(first, n, s_ref, scores, values, ml, acc_ref, fixed, peel=True, consume=None):
    def one(j, s, ml):
        if consume is not None:
            return consume(j, s, ml)
        return _consume(s, values(j), *ml, acc_ref, fixed)

    if isinstance(n, int):
        tiles = [scores(first + j) for j in range(n)]
        for j in range(n):
            ml = one(first + j, tiles[j], ml)
        return ml

    odd = n % 2 if peel else 0
    if peel:
        ml = lax.cond(odd == 1, lambda ml: one(first, scores(first), ml), lambda ml: ml, ml)
    j0 = first + odd
    last = first + n - 1

    @pl.when(n >= 2)
    def _():
        s_ref[0] = scores(j0)

    def body(t, ml):
        j = j0 + 2 * t
        s1 = scores(j + 1)
        ml = one(j, s_ref[0], ml)
        s_ref[1] = s1
        s0 = scores(jnp.minimum(j + 2, last))
        ml = one(j + 1, s_ref[1], ml)
        s_ref[0] = s0
        return ml

    return lax.fori_loop(0, n // 2, body, ml)


def _flash_out(l, acc_ref, tq):
    o = acc_ref[...] / l
    ot = jnp.concatenate([o[:HEAD_DIM, :tq], o[HEAD_DIM:, tq:]], axis=0)
    return ot.T.astype(BF16)


def _load_vt(vt_ref, kb, tk):
    n = tk // VT_BLK
    parts = [vt_ref[kb * n + r] for r in range(n)]
    return parts[0] if n == 1 else jnp.concatenate(parts, axis=1)


def _band_attn_kernel(q_ref, k_ref, vt_ref, bias_ref, o_ref, acc_ref, s_ref, *, tq, nblk, fixed):
    i = pl.program_id(2)
    qs = _stack_heads(q_ref[...], HEAD_DIM)
    lead = i - (nblk - 1)

    def scores(kb):
        ks = pl.multiple_of(kb * tq, tq)
        return _dot(k_ref[pl.ds(ks, tq), :], qs) + bias_ref[kb - lead]

    values = lambda kb: _load_vt(vt_ref, kb, tq)
    ml0 = _flash_init(acc_ref)
    _, l = lax.cond(
        lead >= 0,
        lambda ml: _sweep(lead, nblk, s_ref, scores, values, ml, acc_ref, fixed),
        lambda ml: _sweep(0, i + 1, s_ref, scores, values, ml, acc_ref, fixed),
        ml0)
    o_ref[...] = _flash_out(l, acc_ref, tq)


def _band_attn(q, k, vt, bias, fixed, tq=256):
    B, _, S, _ = q.shape
    nblk = bias.shape[1]
    return pl.pallas_call(
        functools.partial(_band_attn_kernel, tq=tq, nblk=nblk, fixed=fixed),
        grid=(B, N_PAIRS, S // tq),
        in_specs=[
            pl.BlockSpec((None, None, tq, LANES), lambda b, p, i: (b, p, i, 0)),
            pl.BlockSpec((None, None, S, LANES), lambda b, p, i: (b, p, 0, 0)),
            pl.BlockSpec((None, None, S // VT_BLK, LANES, VT_BLK), lambda b, p, i: (b, p, 0, 0, 0)),
            pl.BlockSpec((None, nblk, tq, 2 * tq), lambda b, p, i: (p, 0, 0, 0)),
        ],
        out_specs=pl.BlockSpec((None, tq, LANES), lambda b, p, i: (b, i, p)),
        out_shape=jax.ShapeDtypeStruct((B, S, N_HEADS * HEAD_DIM), BF16),
        scratch_shapes=[pltpu.VMEM((LANES, 2 * tq), F32), pltpu.VMEM((2, tq, 2 * tq), F32)],
        compiler_params=_cparams(("parallel", "parallel", "arbitrary")),
        name="band_attn",
    )(q, k, vt, bias)


def _mla_attn_kernel(q_ref, k_ref, vt_ref, o_ref, acc_ref, s_ref, *, tq, tk, fixed):
    i = pl.program_id(2)
    qs = _stack_heads(q_ref[...], LANES)
    t0 = i * tq
    assert tq % (2 * tk) == 0
    n_full = t0 // tk
    n_edge = tq // tk
    lim = ((t0 + lax.broadcasted_iota(jnp.int32, (1, tq), 1)) // CHUNK + 1) * CHUNK

    def scores(kb):
        ks = pl.multiple_of(kb * tk, tk)
        return _dot(k_ref[pl.ds(ks, tk), :], qs)

    def edge_scores(kb):
        row = kb * tk + lax.broadcasted_iota(jnp.int32, (tk, tq), 0)
        add = jnp.where(row < lim, 0.0, NEG).astype(F32)
        return scores(kb) + jnp.concatenate([add, add], axis=1)

    values = lambda kb: _load_vt(vt_ref, kb, tk)
    ml = _sweep(0, n_full, s_ref, scores, values, _flash_init(acc_ref), acc_ref, fixed, peel=False)
    _, l = _sweep(n_full, n_edge, s_ref, edge_scores, values, ml, acc_ref, fixed, peel=False)
    o_ref[...] = _flash_out(l, acc_ref, tq)


def _mla_attn(q, k, vt, fixed, tq=1024, tk=256):
    B, _, S, _ = q.shape
    tk = min(tk, S)
    return pl.pallas_call(
        functools.partial(_mla_attn_kernel, tq=tq, tk=tk, fixed=fixed),
        grid=(B, N_PAIRS, S // tq),
        in_specs=[
            pl.BlockSpec((None, None, tq, 256), lambda b, p, i: (b, p, i, 0)),
            pl.BlockSpec((None, None, S, 256), lambda b, p, i: (b, p, 0, 0)),
            pl.BlockSpec((None, None, S // VT_BLK, LANES, VT_BLK), lambda b, p, i: (b, p, 0, 0, 0)),
        ],
        out_specs=pl.BlockSpec((None, tq, LANES), lambda b, p, i: (b, i, p)),
        out_shape=jax.ShapeDtypeStruct((B, S, N_HEADS * V_DIM), BF16),
        scratch_shapes=[pltpu.VMEM((LANES, 2 * tq), F32), pltpu.VMEM((2, tk, 2 * tq), F32)],
        compiler_params=_cparams(("parallel", "parallel", "arbitrary")),
        name="mla_attn",
    )(q, k, vt)


def _dsa_attn_kernel(q_ref, qp_ref, wi_ref, shift_ref, bias_ref, k_ref, vt_ref, kp_ref, o_ref,
                     sc_ref, h_ref, acc_ref, stage_ref, s_ref, qs_ref, m_ref, l_ref, *, tq, tk, topk, fixed):
    i = pl.program_id(1)
    t0 = i * tq
    n_blk = (t0 + tq + tk - 1) // tk
    n_far = jnp.maximum((t0 // LANES - 1) // (tk // LANES), 0)
    n_far_even = n_far - n_far % 2
    lim = ((t0 + lax.broadcasted_iota(jnp.int32, (1, tq), 1)) // CHUNK + 1) * CHUNK

    wv = wi_ref[...]

    def score_body(kb, carry):
        ks = pl.multiple_of(kb * tk, tk)
        kp = kp_ref[pl.ds(ks, tk), :]
        acc = jnp.zeros((tk, tq), F32)
        for h in range(H_IDX):
            acc = acc + jnp.maximum(_dot(kp, qp_ref[h]), 0.0) * wv[h:h + 1, :]
        row = ks + lax.broadcasted_iota(jnp.int32, (tk, tq), 0)
        acc = jnp.where(row < lim, acc, -jnp.inf)
        bits = pltpu.bitcast(acc, jnp.int32)
        key = bits ^ ((bits >> 31) & 0x7FFFFFFF)
        sc_ref[kb] = key
        h_ref[kb] = (key >> 16).astype(jnp.int16)
        return carry

    lax.fori_loop(0, n_blk, score_body, 0)

    def count16(pred):
        def count_body(kb, part):
            hit = jnp.where(pred(h_ref[kb]), jnp.int16(1), jnp.int16(0))
            lanes = [hit[r:r + 16] for r in range(0, tk, 16)]
            while len(lanes) > 1:
                lanes = [a + b for a, b in zip(lanes[0::2], lanes[1::2])]
            return part + lanes[0].astype(jnp.int32)

        part = lax.fori_loop(0, n_blk, count_body, jnp.zeros((16, tq), jnp.int32))
        return jnp.sum(part, axis=0, keepdims=True)

    def kth16(need):
        def bit_body(t, ans):
            cand = ans + jnp.left_shift(jnp.int32(1), 15 - t)
            cand16 = cand.astype(jnp.int16)
            return jnp.where(count16(lambda h: h >= cand16) >= need, cand, ans)

        return lax.fori_loop(0, 16, bit_body, jnp.full((1, tq), -(2 ** 15), jnp.int32))

    top = kth16(topk)
    top16 = top.astype(jnp.int16)
    need = topk - count16(lambda h: h > top16)

    def low_body(kb, carry):
        low = ((sc_ref[kb] & 0xFFFF) - 2 ** 15).astype(jnp.int16)
        h_ref[kb] = jnp.where(h_ref[kb] == top16, low, jnp.int16(-(2 ** 15)))
        return carry

    lax.fori_loop(0, n_blk, low_body, 0)
    low = kth16(need)
    thr = jnp.left_shift(top, 16) + (low + 2 ** 15)

    low16 = low.astype(jnp.int16)
    ties = (need - count16(lambda h: h > low16)).astype(F32)
    keep_bits = pltpu.bitcast(shift_ref[...], jnp.int32)
    below = (lax.broadcasted_iota(jnp.int32, (tk, tk), 0) > lax.broadcasted_iota(jnp.int32, (tk, tk), 1))
    below = jnp.where(below, 1.0, 0.0).astype(BF16)

    def mask_body(kb, seen):
        key = sc_ref[kb]
        row = kb * tk + lax.broadcasted_iota(jnp.int32, (tk, tq), 0)
        tie = jnp.where(key == thr, 1.0, 0.0)
        before = seen + _dot(below, tie.astype(BF16))
        thr_row = thr + jnp.where(before >= ties, 1, 0)
        sc_ref[kb] = jnp.where((key >= thr_row) & (row < lim), keep_bits, NEG_BITS)
        return seen + jnp.sum(tie, axis=0, keepdims=True)

    lax.fori_loop(0, n_blk, mask_body, jnp.zeros((1, tq), F32))

    def prep(p, carry):
        qs_ref[p] = _stack_heads(q_ref[p], HEAD_DIM)
        m_ref[p] = jnp.full((1, 2 * tq), M_INIT, F32)
        l_ref[p] = jnp.zeros((1, 2 * tq), F32)
        acc_ref[p] = jnp.zeros((LANES, 2 * tq), F32)
        return carry

    lax.fori_loop(0, N_PAIRS, prep, 0)

    def scores(t):
        kb, p = t // N_PAIRS, t % N_PAIRS
        ks = pl.multiple_of(kb * tk, tk)
        am = pltpu.bitcast(sc_ref[kb], F32)
        cols = []
        for hh in range(2):
            for a in range(tq // LANES):
                tiles = []
                for c in range(tk // LANES):
                    d = (kb * (tk // LANES) + c) - (i * (tq // LANES) + a)
                    kind = jnp.where(d == 0, 2, jnp.where(d == -1, 1, 0))
                    tiles.append(bias_ref[p, hh, kind])
                cols.append(jnp.concatenate(tiles, axis=0))
        return (_dot(k_ref[p, pl.ds(ks, tk), :], qs_ref[p]) + jnp.concatenate([am, am], axis=1)
                + jnp.concatenate(cols, axis=1))

    def consume(t, s, carry):
        kb, p = t // N_PAIRS, t % N_PAIRS
        m, l = _consume(s, _load_vt(vt_ref.at[p], kb, tk), m_ref[p], l_ref[p], acc_ref.at[p], fixed)
        l_ref[p] = l
        if not fixed:
            m_ref[p] = m
        return carry

    _sweep(0, n_blk * N_PAIRS, s_ref, scores, None, 0, None, fixed, peel=False, consume=consume)

    def finish(p, carry):
        stage_ref[p] = _flash_out(l_ref[p], acc_ref.at[p], tq)
        return carry

    lax.fori_loop(0, N_PAIRS, finish, 0)
    for p in range(N_PAIRS):
        o_ref[:, p * LANES:(p + 1) * LANES] = stage_ref[p]


def _dsa_attn(q, k, vt, qp, kp, wi, shift, bias, fixed, tq=256, tk=512):
    B, _, S, _ = q.shape
    topk = min(TOPK_MAX, S // 4)
    once = pl.Buffered(1)
    return pl.pallas_call(
        functools.partial(_dsa_attn_kernel, tq=tq, tk=tk, topk=topk, fixed=fixed),
        grid=(B, S // tq),
        in_specs=[
            pl.BlockSpec((None, N_PAIRS, tq, LANES), lambda b, i: (b, 0, i, 0)),
            pl.BlockSpec((None, H_IDX, 256, tq), lambda b, i: (b, 0, 0, i)),
            pl.BlockSpec((None, H_IDX, tq), lambda b, i: (b, 0, i)),
            pl.BlockSpec((1, tq), lambda b, i: (0, 0)),
            pl.BlockSpec((N_PAIRS, 2, 3, LANES, LANES), lambda b, i: (0, 0, 0, 0, 0), pipeline_mode=once),
            pl.BlockSpec((None, N_PAIRS, S, LANES), lambda b, i: (b, 0, 0, 0), pipeline_mode=once),
            pl.BlockSpec((None, N_PAIRS, S // VT_BLK, LANES, VT_BLK), lambda b, i: (b, 0, 0, 0, 0),
                         pipeline_mode=once),
            pl.BlockSpec((None, S, 256), lambda b, i: (b, 0, 0), pipeline_mode=once),
        ],
        out_specs=pl.BlockSpec((None, tq, N_HEADS * HEAD_DIM), lambda b, i: (b, i, 0)),
        out_shape=jax.ShapeDtypeStruct((B, S, N_HEADS * HEAD_DIM), BF16),
        scratch_shapes=[
            pltpu.VMEM((S // tk, tk, tq), jnp.int32),
            pltpu.VMEM((S // tk, tk, tq), jnp.int16),
            pltpu.VMEM((N_PAIRS, LANES, 2 * tq), F32),
            pltpu.VMEM((N_PAIRS, tq, LANES), BF16),
            pltpu.VMEM((2, tk, 2 * tq), F32),
            pltpu.VMEM((N_PAIRS, LANES, 2 * tq), BF16),
            pltpu.VMEM((N_PAIRS, 1, 2 * tq), F32),
            pltpu.VMEM((N_PAIRS, 1, 2 * tq), F32),
        ],
        compiler_params=_cparams(("parallel", "arbitrary"), VMEM_LIMIT_DSA),
        name="dsa_attn",
    )(q, qp, wi, jnp.full((1, tq), shift, F32), bias, k, vt, kp)


def _out_proj_kernel(x_ref, a_ref, w_ref, o_ref):
    o_ref[...] = x_ref[...] + _dot(a_ref[...], w_ref[...])


def _out_proj(x, a, w, tm=512):
    B, S, D = x.shape
    return pl.pallas_call(
        _out_proj_kernel,
        grid=(B, S // tm),
        in_specs=[
            pl.BlockSpec((None, tm, D), lambda b, i: (b, i, 0)),
            pl.BlockSpec((None, tm, a.shape[-1]), lambda b, i: (b, i, 0)),
            pl.BlockSpec(w.shape, lambda b, i: (0, 0)),
        ],
        out_specs=pl.BlockSpec((None, tm, D), lambda b, i: (b, i, 0)),
        out_shape=jax.ShapeDtypeStruct((B, S, D), F32),
        compiler_params=_cparams(("parallel", "parallel")),
        name="out_proj",
    )(x, a, w.astype(BF16))


def _ffn_kernel(x_ref, xp_ref, g_ref, wg_ref, wv_ref, cw_ref, cb_ref, wd_ref, o_ref, acc_ref, he_ref, u_ref,
                *, tm, n_chunks):
    i = pl.program_id(1)
    g = g_ref[...]
    he_ref[:HALO] = (_rms(xp_ref[...], g) * jnp.where(i > 0, 1.0, 0.0)).astype(BF16)
    he_ref[HALO:] = _rms(x_ref[...], g).astype(BF16)
    acc_ref[...] = jnp.zeros(acc_ref.shape, F32)

    def up(c, slot):
        he = he_ref[...]
        u_ref[slot, 0] = _dot(he, wg_ref[c])
        u_ref[slot, 1] = _dot(he, wv_ref[c])

    def conv(u, w, b):
        return (b + w[0:1] * u[HALO - 2:HALO - 2 + tm] + w[1:2] * u[HALO - 1:HALO - 1 + tm]
                + w[2:3] * u[HALO:HALO + tm])

    def down(c, slot):
        cw = cw_ref[c]
        cb = cb_ref[c]
        gate = conv(u_ref[slot, 0], cw[0], cb[0])
        val = conv(u_ref[slot, 1], cw[1], cb[1])
        act = gate * jax.nn.sigmoid(gate) * val
        acc_ref[...] += _dot(act.astype(BF16), wd_ref[c])

    up(0, 0)

    def pair_body(t, carry):
        c = 2 * t
        up(c + 1, 1)
        down(c, 0)
        up(c + 2, 0)
        down(c + 1, 1)
        return carry

    assert n_chunks % 2 == 1
    lax.fori_loop(0, n_chunks // 2, pair_body, 0)
    down(n_chunks - 1, 0)
    o_ref[...] = x_ref[...] + acc_ref[...]


def _ffn(x, g, w_up, conv_w, conv_b, w_down, tm=512):
    B, S, D = x.shape
    nc = D_FF // FF_CHUNK
    wup = w_up.astype(BF16).reshape(D, 2, nc, FF_CHUNK).transpose(1, 2, 0, 3)
    cw = conv_w.reshape(CONV_W, 2, nc, FF_CHUNK).transpose(2, 1, 0, 3)
    cb = conv_b.reshape(2, nc, 1, FF_CHUNK).transpose(1, 0, 2, 3)
    wd = w_down.astype(BF16).reshape(nc, FF_CHUNK, D)
    c3 = lambda b, i: (0, 0, 0)
    c4 = lambda b, i: (0, 0, 0, 0)
    return pl.pallas_call(
        functools.partial(_ffn_kernel, tm=tm, n_chunks=nc),
        grid=(B, S // tm),
        in_specs=[
            pl.BlockSpec((None, tm, D), lambda b, i: (b, i, 0)),
            pl.BlockSpec((None, HALO, D), lambda b, i: (b, jnp.maximum(i * (tm // HALO) - 1, 0), 0)),
            pl.BlockSpec((1, D), lambda b, i: (0, 0)),
            pl.BlockSpec((nc, D, FF_CHUNK), c3),
            pl.BlockSpec((nc, D, FF_CHUNK), c3),
            pl.BlockSpec((nc, 2, CONV_W, FF_CHUNK), c4),
            pl.BlockSpec((nc, 2, 1, FF_CHUNK), c4),
            pl.BlockSpec((nc, FF_CHUNK, D), c3),
        ],
        out_specs=pl.BlockSpec((None, tm, D), lambda b, i: (b, i, 0)),
        out_shape=jax.ShapeDtypeStruct((B, S, D), F32),
        scratch_shapes=[pltpu.VMEM((tm, D), F32), pltpu.VMEM((HALO + tm, D), BF16),
                        pltpu.VMEM((2, 2, HALO + tm, FF_CHUNK), F32)],
        compiler_params=_cparams(("parallel", "parallel")),
        name="conv_ffn",
    )(x, x, g.reshape(1, D), wup[0], wup[1], cw, cb, wd)


def _rope_tables(S, period, off):
    inv = 1.0 / (ROPE_THETA ** (jnp.arange(0, ROPE_DIM, 2, dtype=F32) / ROPE_DIM))
    ang = jnp.arange(S, dtype=F32)[:, None] * inv[None, :]
    cos, sin = jnp.cos(ang), jnp.sin(ang)
    one = jnp.ones((S, 1), F32)
    zero = jnp.zeros((S, 1), F32)

    def lay(x1, x2, fill):
        group = jnp.concatenate([jnp.tile(fill, (1, off)), x1, x2,
                                 jnp.tile(fill, (1, period - off - ROPE_DIM))], axis=1)
        return jnp.tile(group, (1, LANES // period))

    return lay(cos, cos, one), lay(-sin, zero * sin, zero), lay(zero * sin, sin, zero)


def _t5_bucket(rel):
    nb = T5_BUCKETS // 2
    max_exact = nb // 2
    n = jnp.abs(rel)
    large = max_exact + (jnp.log(jnp.maximum(n, 1).astype(F32) / max_exact)
                         / math.log(T5_MAX_DIST / max_exact) * (nb - max_exact)).astype(jnp.int32)
    large = jnp.minimum(large, nb - 1)
    return jnp.where(rel > 0, nb, 0) + jnp.where(n < max_exact, n, large)


def _dsa_bias_tiles(t5_bias):
    kj = jnp.arange(LANES, dtype=jnp.int32)[:, None]
    qi = jnp.arange(LANES, dtype=jnp.int32)[None, :]
    far = t5_bias[_t5_bucket(jnp.int32(-4 * T5_MAX_DIST))]

    def tile(rel):
        pick = jax.nn.one_hot(_t5_bucket(rel), T5_BUCKETS, dtype=F32)
        vals = jnp.einsum("kqb,bh->hkq", pick, t5_bias, precision=lax.Precision.HIGHEST)
        return (vals - far[:, None, None]) * LOG2E

    same, prev = tile(kj - qi), tile(kj - LANES - qi)
    return jnp.stack([jnp.zeros_like(same), prev, same], axis=1).reshape(N_PAIRS, 2, 3, LANES, LANES)


def _softmax_shift(gq, gk, dim, bias_hi=0.0, bias_lo=0.0):
    qk = 1.02 * dim * jnp.max(jnp.abs(gq)) * jnp.max(jnp.abs(gk))
    top = qk + bias_hi
    fixed = (top + qk - bias_lo) <= SAFE_SPREAD
    return fixed, jnp.where(fixed, -top, 0.0)


def _band_bias(rel_bias, shift, tq):
    pad = LEFT_CHUNKS * CHUNK
    nblk = -(-pad // tq) + 1
    lead = (nblk - 1) * tq
    nk = nblk * tq
    kk = jnp.arange(nk, dtype=jnp.int32)[:, None] - lead
    i = jnp.arange(tq, dtype=jnp.int32)[None, :]
    first = (i // CHUNK) * CHUNK - pad
    ok = (kk >= first) & (kk < (i // CHUNK + 1) * CHUNK)
    L = nk + tq
    dist = jnp.arange(L, dtype=jnp.int32)
    dist = lead + jnp.where(dist < tq, dist, dist - L)
    v = rel_bias[:, jnp.clip(dist, -(CHUNK - 1), REL_MAX_PAST) + CHUNK - 1]
    b = jnp.tile(v, (1, nk))[:, :nk * (L - 1)].reshape(v.shape[0], nk, L - 1)[:, :, :tq]
    b = jnp.where(ok[None], b + shift, NEG).reshape(N_PAIRS, 2, nblk, tq, tq)
    return jnp.transpose(b, (0, 2, 3, 1, 4)).reshape(N_PAIRS, nblk, tq, 2 * tq)


def kernel(x, norm_mix, norm_ffn, t5_bias, a_w_in, a_q_norm, a_k_norm, a_kidx_norm, a_w_out, b_w_down, b_q_a_norm, b_kv_a_norm, b_w_uq, b_w_ukv, b_q_norm, b_k_norm, b_w_out, c_w_in, c_q_norm, c_k_norm, c_rel_bias, c_w_out, f_w_up, f_conv_w, f_conv_b, f_w_down):
    B, S, D = x.shape
    depth = norm_mix.shape[0]

    tq_c = 256
    tabs_idx = _rope_tables(S, D_IDX, D_IDX - ROPE_DIM)
    tabs_mla = _rope_tables(S, LANES, QK_NOPE)
    scale = HEAD_DIM ** -0.5 * LOG2E
    for layer in range(depth):
        kind, j = layer % N_MIXERS, layer // N_MIXERS
        g = norm_mix[layer]
        if kind == 0:
            gq = a_q_norm[j] * scale
            q, k, vt = _qkv_proj(x, g, a_w_in[j][:, :A_QKV], gq, a_k_norm[j])
            qp, kp, wi = _idx_proj(x, g, a_w_in[j][:, A_QKV:], a_kidx_norm[j], tabs_idx)
            tiles = _dsa_bias_tiles(t5_bias)
            fixed, shift = _softmax_shift(gq, a_k_norm[j], HEAD_DIM, jnp.max(tiles), jnp.min(tiles))
            att = lax.cond(fixed, functools.partial(_dsa_attn, fixed=True), functools.partial(_dsa_attn, fixed=False),
                           q, k, vt, qp, kp, wi, shift, tiles)
            w_out = a_w_out[j]
        elif kind == 1:
            fixed, shift = _softmax_shift(b_q_norm[j] * (B_QK ** -0.5 * LOG2E), b_k_norm[j], B_QK)
            q, k, vt = _mla_proj(x, g, b_w_down[j], b_q_a_norm[j], b_kv_a_norm[j], b_w_uq[j], b_w_ukv[j],
                                 b_q_norm[j], b_k_norm[j], shift, tabs_mla)
            att = lax.cond(fixed, functools.partial(_mla_attn, fixed=True), functools.partial(_mla_attn, fixed=False),
                           q, k, vt)
            w_out = b_w_out[j]
        else:
            gq = c_q_norm[j] * scale
            q, k, vt = _qkv_proj(x, g, c_w_in[j], gq, c_k_norm[j])
            rel = c_rel_bias[j] * LOG2E
            fixed, shift = _softmax_shift(gq, c_k_norm[j], HEAD_DIM, jnp.max(rel), jnp.min(rel))
            att = lax.cond(fixed, functools.partial(_band_attn, fixed=True, tq=tq_c),
                           functools.partial(_band_attn, fixed=False, tq=tq_c),
                           q, k, vt, _band_bias(rel, shift, tq_c))
            w_out = c_w_out[j]
        x = _out_proj(x, att, w_out)
        x = _ffn(x, norm_ffn[layer], f_w_up[layer], f_conv_w[layer], f_conv_b[layer], f_w_down[layer])
    return x
```

```python
import functools
import math

import numpy as np
import jax
import jax.numpy as jnp
from jax import lax
from jax.experimental import pallas as pl
from jax.experimental.pallas import tpu as pltpu

F32 = jnp.float32
BF16 = jnp.bfloat16

D_MODEL = 1024
CHUNK = 64
EPS = 1e-6
ROPE_THETA = 10000.0
NEG = -1e30
NEG_BITS = int(np.float32(NEG).view(np.int32))
M_INIT = -1e38
LOG2E = math.log2(math.e)
SAFE_SPREAD = 100.0

N_MIXERS = 3
N_HEADS = 16
HEAD_DIM = 64
N_PAIRS = N_HEADS // 2
LANES = 128
VT_BLK = 256

H_IDX = 8
D_IDX = 64
ROPE_DIM = 32
TOPK_MAX = 256
T5_BUCKETS = 32
T5_MAX_DIST = 128
A_QKV = 3 * N_HEADS * HEAD_DIM
Q_LORA = 256
KV_LORA = 128
QK_NOPE = 64
QK_ROPE = 32
V_DIM = 64
B_QK = QK_NOPE + QK_ROPE
LEFT_CHUNKS = 8
REL_MAX_PAST = 128
D_FF = 2816
CONV_W = 3
FF_CHUNK = 256
HALO = 8

VMEM_LIMIT = 56 * 1024 * 1024
VMEM_LIMIT_DSA = 60 * 1024 * 1024


def _cparams(sem, vmem_limit=VMEM_LIMIT):
    return pltpu.CompilerParams(dimension_semantics=sem, vmem_limit_bytes=vmem_limit)


def _rms(x, g):
    return x * lax.rsqrt(jnp.mean(x * x, axis=-1, keepdims=True) + EPS) * g


def _dot(a, b):
    return jnp.dot(a, b, preferred_element_type=F32)


def _split_bf16(x):
    hi = x.astype(BF16)
    lo = (x - hi.astype(F32)).astype(BF16)
    return hi, lo


def _group_sumsq(y, bd):
    hi, lo = _split_bf16(y * y)
    return _dot(hi, bd) + _dot(lo, bd)


def _rope(y, ct, s1, s2):
    return y * ct + pltpu.roll(y, LANES - 16, 1) * s1 + pltpu.roll(y, 16, 1) * s2


def _rope_wide(y, ct, s1, s2):
    w = y.shape[-1]
    parts = [_rope(y[:, c:c + LANES], ct, s1, s2) for c in range(0, w, LANES)]
    return parts[0] if len(parts) == 1 else jnp.concatenate(parts, axis=-1)


def _store_vt(vt_ref, p, y):
    yt = y.T.astype(BF16)
    for r in range(y.shape[0] // VT_BLK):
        vt_ref[p, r] = yt[:, r * VT_BLK:(r + 1) * VT_BLK]


def _qkv_proj_kernel(x_ref, g_ref, w_ref, gq_ref, gk_ref, bd_ref, q_ref, k_ref, vt_ref, xb_ref, y_ref):
    xb_ref[...] = _rms(x_ref[...], g_ref[...]).astype(BF16)
    bd = bd_ref[...]
    per = N_HEADS * HEAD_DIM // 256
    outs = ((q_ref, gq_ref), (k_ref, gk_ref), (vt_ref, None))

    def project(n):
        y_ref[n % 2] = _dot(xb_ref[...], w_ref[:, n * 256:(n + 1) * 256])

    def finish(n):
        (o_ref, gain_ref), c = outs[n // per], n % per
        y = y_ref[n % 2]
        if gain_ref is None:
            _store_vt(o_ref, 2 * c, y[:, :LANES])
            _store_vt(o_ref, 2 * c + 1, y[:, LANES:])
        else:
            y = y * lax.rsqrt(_group_sumsq(y, bd) * (1.0 / HEAD_DIM) + EPS) * gain_ref[...]
            yb = y.astype(BF16)
            o_ref[2 * c] = yb[:, :LANES]
            o_ref[2 * c + 1] = yb[:, LANES:]

    project(0)
    for n in range(3 * per):
        if n + 1 < 3 * per:
            project(n + 1)
        finish(n)


def _vt_spec(tm):
    return pl.BlockSpec((None, N_PAIRS, tm // VT_BLK, LANES, VT_BLK), lambda b, i: (b, 0, i, 0, 0))


def _vt_shape(B, S):
    return jax.ShapeDtypeStruct((B, N_PAIRS, S // VT_BLK, LANES, VT_BLK), BF16)


def _qkv_proj(x, g, w, gq, gk, tm=512):
    B, S, D = x.shape
    hd = N_HEADS * HEAD_DIM
    bd = jnp.kron(jnp.eye(256 // HEAD_DIM, dtype=F32), jnp.ones((HEAD_DIM, HEAD_DIM), F32)).astype(BF16)
    out = jax.ShapeDtypeStruct((B, N_PAIRS, S, LANES), BF16)
    const = lambda b, i: (0, 0)
    ospec = pl.BlockSpec((None, N_PAIRS, tm, LANES), lambda b, i: (b, 0, i, 0))
    return pl.pallas_call(
        _qkv_proj_kernel,
        grid=(B, S // tm),
        in_specs=[
            pl.BlockSpec((None, tm, D), lambda b, i: (b, i, 0)),
            pl.BlockSpec((1, D), const),
            pl.BlockSpec((D, 3 * hd), const),
            pl.BlockSpec((1, 256), const),
            pl.BlockSpec((1, 256), const),
            pl.BlockSpec((256, 256), const),
        ],
        out_specs=[ospec, ospec, _vt_spec(tm)],
        out_shape=[out, out, _vt_shape(B, S)],
        scratch_shapes=[pltpu.VMEM((tm, D), BF16), pltpu.VMEM((2, tm, 256), F32)],
        compiler_params=_cparams(("parallel", "parallel")),
        name="qkv_proj",
    )(x, g.reshape(1, D), w.astype(BF16), jnp.tile(gq, 4).reshape(1, 256), jnp.tile(gk, 4).reshape(1, 256), bd)


def _idx_proj_kernel(x_ref, g_ref, wh_ref, wl_ref, gki_ref, ct_ref, s1_ref, s2_ref, qp_ref, kp_ref, wi_ref,
                     *, w_scale):
    xn = _rms(x_ref[...], g_ref[...])
    xh, xl = _split_bf16(xn)

    def mm(c0, c1):
        wh = wh_ref[:, c0:c1]
        return _dot(xh, wh) + (_dot(xl, wh) + _dot(xh, wl_ref[:, c0:c1]))

    ct, s1, s2 = ct_ref[...], s1_ref[...], s2_ref[...]
    tm = xn.shape[0]
    low = lax.broadcasted_iota(jnp.int32, (tm, LANES), 1) < D_IDX
    zero = jnp.zeros((tm, LANES), F32)

    for c in range(H_IDX // 2):
        y = _rope(mm(c * LANES, (c + 1) * LANES), ct, s1, s2)
        hi = y.astype(BF16).astype(F32)
        lo = y - hi
        rhi = pltpu.roll(hi, D_IDX, 1)
        rlo = pltpu.roll(lo, D_IDX, 1)
        qp_ref[2 * c, :LANES, :] = jnp.where(low, hi, rlo).T.astype(BF16)
        qp_ref[2 * c, LANES:, :] = jnp.where(low, hi, zero).T.astype(BF16)
        qp_ref[2 * c + 1, :LANES, :] = jnp.where(low, rhi, lo).T.astype(BF16)
        qp_ref[2 * c + 1, LANES:, :] = jnp.where(low, rhi, zero).T.astype(BF16)

    c0 = H_IDX * D_IDX
    y = mm(c0, c0 + LANES)
    y = y * lax.rsqrt(jnp.sum(y * y, axis=-1, keepdims=True) * (1.0 / D_IDX) + EPS) * gki_ref[...]
    y = _rope(y, ct, s1, s2)
    hi = y.astype(BF16).astype(F32)
    lo = y - hi
    kp_ref[:, :LANES] = jnp.where(low, hi, pltpu.roll(hi, D_IDX, 1)).astype(BF16)
    kp_ref[:, LANES:] = jnp.where(low, lo, zero).astype(BF16)

    wi_ref[...] = (mm(c0 + LANES, c0 + 2 * LANES) * w_scale).T[:H_IDX]


def _idx_proj(x, g, w_idx, gki, tabs, tm=512):
    B, S, D = x.shape
    nq = H_IDX * D_IDX
    wq, wk, ww = w_idx[:, :nq], w_idx[:, nq:nq + D_IDX], w_idx[:, nq + D_IDX:]
    wpad = jnp.concatenate([
        wq, jnp.pad(wk, ((0, 0), (0, LANES - D_IDX))), jnp.pad(ww, ((0, 0), (0, LANES - H_IDX)))], axis=1)
    wh = wpad.astype(BF16)
    wl = (wpad - wh.astype(F32)).astype(BF16)
    ncol = nq + 2 * LANES
    gpad = jnp.pad(gki, (0, LANES - D_IDX)).reshape(1, LANES)
    const = lambda b, i: (0, 0)
    tspec = pl.BlockSpec((tm, LANES), lambda b, i: (i, 0))
    return pl.pallas_call(
        functools.partial(_idx_proj_kernel, w_scale=(D_IDX ** -0.5) * (H_IDX ** -0.5)),
        grid=(B, S // tm),
        in_specs=[
            pl.BlockSpec((None, tm, D), lambda b, i: (b, i, 0)),
            pl.BlockSpec((1, D), const),
            pl.BlockSpec((D, ncol), const),
            pl.BlockSpec((D, ncol), const),
            pl.BlockSpec((1, LANES), const),
            tspec, tspec, tspec,
        ],
        out_specs=[
            pl.BlockSpec((None, H_IDX, 256, tm), lambda b, i: (b, 0, 0, i)),
            pl.BlockSpec((None, tm, 256), lambda b, i: (b, i, 0)),
            pl.BlockSpec((None, H_IDX, tm), lambda b, i: (b, 0, i)),
        ],
        out_shape=[
            jax.ShapeDtypeStruct((B, H_IDX, 256, S), BF16),
            jax.ShapeDtypeStruct((B, S, 256), BF16),
            jax.ShapeDtypeStruct((B, H_IDX, S), F32),
        ],
        compiler_params=_cparams(("parallel", "parallel")),
        name="idx_proj",
    )(x, g.reshape(1, D), wh, wl, gpad, *tabs)


def _mla_proj_kernel(x_ref, g_ref, wd_ref, gqa_ref, gkva_ref, wuq_ref, wuk_ref, wuv_ref, gq_ref, gk_ref, bd_ref,
                     qfix_ref, kfix_ref, ct_ref, s1_ref, s2_ref, q_ref, k_ref, vt_ref):
    xb = _rms(x_ref[...], g_ref[...]).astype(BF16)
    d = _dot(xb, wd_ref[...])
    cq = _rms(d[:, :Q_LORA], gqa_ref[...]).astype(BF16)
    ckv = _rms(d[:, Q_LORA:Q_LORA + KV_LORA], gkva_ref[...]).astype(BF16)
    krz = pltpu.roll(d[:, Q_LORA + KV_LORA:], QK_NOPE, 1)
    krz2 = jnp.concatenate([krz, krz], axis=-1)
    bd = bd_ref[...]
    ct, s1, s2 = ct_ref[...], s1_ref[...], s2_ref[...]
    inv = 1.0 / B_QK
    for p in range(N_PAIRS):
        cols = slice(p * 256, (p + 1) * 256)
        y = _dot(cq, wuq_ref[:, cols])
        y = y * lax.rsqrt(_group_sumsq(y, bd) * inv + EPS) * gq_ref[...]
        q_ref[p] = (_rope_wide(y, ct, s1, s2) + qfix_ref[...]).astype(BF16)
        y = _dot(ckv, wuk_ref[:, cols]) + krz2
        y = y * lax.rsqrt(_group_sumsq(y, bd) * inv + EPS) * gk_ref[...]
        k_ref[p] = (_rope_wide(y, ct, s1, s2) + kfix_ref[...]).astype(BF16)
        _store_vt(vt_ref, p, _dot(ckv, wuv_ref[:, p * LANES:(p + 1) * LANES]))


def _mla_proj(x, g, w_down, gqa, gkva, w_uq, w_ukv, gq, gk, shift, tabs, tm=512):
    B, S, D = x.shape
    nd = Q_LORA + KV_LORA + QK_ROPE
    wd = jnp.pad(w_down, ((0, 0), (0, 512 - nd))).astype(BF16)
    padh = LANES - B_QK
    wuq = jnp.pad(w_uq.reshape(Q_LORA, N_HEADS, B_QK), ((0, 0), (0, 0), (0, padh))).reshape(Q_LORA, N_HEADS * LANES)
    wukv = w_ukv.reshape(KV_LORA, N_HEADS, QK_NOPE + V_DIM)
    wuk = jnp.pad(wukv[:, :, :QK_NOPE], ((0, 0), (0, 0), (0, LANES - QK_NOPE))).reshape(KV_LORA, N_HEADS * LANES)
    wuv = wukv[:, :, QK_NOPE:].reshape(KV_LORA, N_HEADS * V_DIM)
    gq2 = jnp.tile(jnp.pad(gq * (B_QK ** -0.5 * LOG2E), (0, padh)), 2).reshape(1, 256)
    gk2 = jnp.tile(jnp.pad(gk, (0, padh)), 2).reshape(1, 256)
    bd = jnp.kron(jnp.eye(2, dtype=F32), jnp.ones((LANES, LANES), F32)).astype(BF16)
    kfix = jnp.asarray((np.arange(256) % LANES == B_QK).astype(np.float32)).reshape(1, 256)
    qfix = kfix * shift
    const = lambda b, i: (0, 0)
    tspec = pl.BlockSpec((tm, LANES), lambda b, i: (i, 0))
    qk_shape = jax.ShapeDtypeStruct((B, N_PAIRS, S, 256), BF16)
    qk_spec = pl.BlockSpec((None, N_PAIRS, tm, 256), lambda b, i: (b, 0, i, 0))
    return pl.pallas_call(
        _mla_proj_kernel,
        grid=(B, S // tm),
        in_specs=[
            pl.BlockSpec((None, tm, D), lambda b, i: (b, i, 0)),
            pl.BlockSpec((1, D), const),
            pl.BlockSpec((D, 512), const),
            pl.BlockSpec((1, Q_LORA), const),
            pl.BlockSpec((1, KV_LORA), const),
            pl.BlockSpec((Q_LORA, N_HEADS * LANES), const),
            pl.BlockSpec((KV_LORA, N_HEADS * LANES), const),
            pl.BlockSpec((KV_LORA, N_HEADS * V_DIM), const),
            pl.BlockSpec((1, 256), const),
            pl.BlockSpec((1, 256), const),
            pl.BlockSpec((256, 256), const),
            pl.BlockSpec((1, 256), const),
            pl.BlockSpec((1, 256), const),
            tspec, tspec, tspec,
        ],
        out_specs=[qk_spec, qk_spec, _vt_spec(tm)],
        out_shape=[qk_shape, qk_shape, _vt_shape(B, S)],
        compiler_params=_cparams(("parallel", "parallel")),
        name="mla_proj",
    )(x, g.reshape(1, D), wd, gqa.reshape(1, Q_LORA), gkva.reshape(1, KV_LORA), wuq.astype(BF16),
      wuk.astype(BF16), wuv.astype(BF16), gq2, gk2, bd, qfix, kfix, *tabs)


def _stack_heads(q2, half):
    qf = q2.astype(F32)
    low = lax.broadcasted_iota(jnp.int32, qf.shape, 1) < half
    zero = jnp.zeros_like(qf)
    return jnp.concatenate([jnp.where(low, qf, zero).T, jnp.where(low, zero, qf).T], axis=1).astype(BF16)


def _flash_init(acc_ref):
    r = acc_ref.shape[1]
    acc_ref[...] = jnp.zeros(acc_ref.shape, F32)
    return jnp.full((1, r), M_INIT, F32), jnp.zeros((1, r), F32)


def _consume(s, vt, m, l, acc_ref, fixed):
    if fixed:
        p = jnp.exp2(s)
        acc_ref[...] += _dot(vt, p.astype(BF16))
        return m, l + jnp.sum(p, axis=0, keepdims=True)
    m_new = jnp.maximum(m, jnp.max(s, axis=0, keepdims=True))
    alpha = jnp.exp2(m - m_new)
    p = jnp.exp2(s - m_new)
    l_new = alpha * l + jnp.sum(p, axis=0, keepdims=True)
    acc_ref[...] = alpha * acc_ref[...] + _dot(vt, p.astype(BF16))
    return m_new, l_new


def _sweep(first, n, s_ref, scores, values, ml, acc_ref, fixed, peel=True, consume=None):
    def one(j, s, ml):
        if consume is not None:
            return consume(j, s, ml)
        return _consume(s, values(j), *ml, acc_ref, fixed)

    if isinstance(n, int):
        tiles = [scores(first + j) for j in range(n)]
        for j in range(n):
            ml = one(first + j, tiles[j], ml)
        return ml

    odd = n % 2 if peel else 0
    if peel:
        ml = lax.cond(odd == 1, lambda ml: one(first, scores(first), ml), lambda ml: ml, ml)
    j0 = first + odd
    last = first + n - 1

    @pl.when(n >= 2)
    def _():
        s_ref[0] = scores(j0)

    def body(t, ml):
        j = j0 + 2 * t
        s1 = scores(j + 1)
        ml = one(j, s_ref[0], ml)
        s_ref[1] = s1
        s0 = scores(jnp.minimum(j + 2, last))
        ml = one(j + 1, s_ref[1], ml)
        s_ref[0] = s0
        return ml

    return lax.fori_loop(0, n // 2, body, ml)


def _flash_out(l, acc_ref, tq):
    o = acc_ref[...] / l
    ot = jnp.concatenate([o[:HEAD_DIM, :tq], o[HEAD_DIM:, tq:]], axis=0)
    return ot.T.astype(BF16)


def _load_vt(vt_ref, kb, tk):
    n = tk // VT_BLK
    parts = [vt_ref[kb * n + r] for r in range(n)]
    return parts[0] if n == 1 else jnp.concatenate(parts, axis=1)


def _band_attn_kernel(q_ref, k_ref, vt_ref, bias_ref, o_ref, acc_ref, s_ref, *, tq, nblk, fixed):
    i = pl.program_id(2)
    qs = _stack_heads(q_ref[...], HEAD_DIM)
    lead = i - (nblk - 1)

    def scores(kb):
        ks = pl.multiple_of(kb * tq, tq)
        return _dot(k_ref[pl.ds(ks, tq), :], qs) + bias_ref[kb - lead]

    values = lambda kb: _load_vt(vt_ref, kb, tq)
    ml0 = _flash_init(acc_ref)
    _, l = lax.cond(
        lead >= 0,
        lambda ml: _sweep(lead, nblk, s_ref, scores, values, ml, acc_ref, fixed),
        lambda ml: _sweep(0, i + 1, s_ref, scores, values, ml, acc_ref, fixed),
        ml0)
    o_ref[...] = _flash_out(l, acc_ref, tq)


def _band_attn(q, k, vt, bias, fixed, tq=256):
    B, _, S, _ = q.shape
    nblk = bias.shape[1]
    return pl.pallas_call(
        functools.partial(_band_attn_kernel, tq=tq, nblk=nblk, fixed=fixed),
        grid=(B, N_PAIRS, S // tq),
        in_specs=[
            pl.BlockSpec((None, None, tq, LANES), lambda b, p, i: (b, p, i, 0)),
            pl.BlockSpec((None, None, S, LANES), lambda b, p, i: (b, p, 0, 0)),
            pl.BlockSpec((None, None, S // VT_BLK, LANES, VT_BLK), lambda b, p, i: (b, p, 0, 0, 0)),
            pl.BlockSpec((None, nblk, tq, 2 * tq), lambda b, p, i: (p, 0, 0, 0)),
        ],
        out_specs=pl.BlockSpec((None, tq, LANES), lambda b, p, i: (b, i, p)),
        out_shape=jax.ShapeDtypeStruct((B, S, N_HEADS * HEAD_DIM), BF16),
        scratch_shapes=[pltpu.VMEM((LANES, 2 * tq), F32), pltpu.VMEM((2, tq, 2 * tq), F32)],
        compiler_params=_cparams(("parallel", "parallel", "arbitrary")),
        name="band_attn",
    )(q, k, vt, bias)


def _mla_attn_kernel(q_ref, k_ref, vt_ref, o_ref, acc_ref, s_ref, *, tq, tk, fixed):
    i = pl.program_id(2)
    qs = _stack_heads(q_ref[...], LANES)
    t0 = i * tq
    assert tq % (2 * tk) == 0
    n_full = t0 // tk
    n_edge = tq // tk
    lim = ((t0 + lax.broadcasted_iota(jnp.int32, (1, tq), 1)) // CHUNK + 1) * CHUNK

    def scores(kb):
        ks = pl.multiple_of(kb * tk, tk)
        return _dot(k_ref[pl.ds(ks, tk), :], qs)

    def edge_scores(kb):
        row = kb * tk + lax.broadcasted_iota(jnp.int32, (tk, tq), 0)
        add = jnp.where(row < lim, 0.0, NEG).astype(F32)
        return scores(kb) + jnp.concatenate([add, add], axis=1)

    values = lambda kb: _load_vt(vt_ref, kb, tk)
    ml = _sweep(0, n_full, s_ref, scores, values, _flash_init(acc_ref), acc_ref, fixed, peel=False)
    _, l = _sweep(n_full, n_edge, s_ref, edge_scores, values, ml, acc_ref, fixed, peel=False)
    o_ref[...] = _flash_out(l, acc_ref, tq)


def _mla_attn(q, k, vt, fixed, tq=1024, tk=256):
    B, _, S, _ = q.shape
    tk = min(tk, S)
    return pl.pallas_call(
        functools.partial(_mla_attn_kernel, tq=tq, tk=tk, fixed=fixed),
        grid=(B, N_PAIRS, S // tq),
        in_specs=[
            pl.BlockSpec((None, None, tq, 256), lambda b, p, i: (b, p, i, 0)),
            pl.BlockSpec((None, None, S, 256), lambda b, p, i: (b, p, 0, 0)),
            pl.BlockSpec((None, None, S // VT_BLK, LANES, VT_BLK), lambda b, p, i: (b, p, 0, 0, 0)),
        ],
        out_specs=pl.BlockSpec((None, tq, LANES), lambda b, p, i: (b, i, p)),
        out_shape=jax.ShapeDtypeStruct((B, S, N_HEADS * V_DIM), BF16),
        scratch_shapes=[pltpu.VMEM((LANES, 2 * tq), F32), pltpu.VMEM((2, tk, 2 * tq), F32)],
        compiler_params=_cparams(("parallel", "parallel", "arbitrary")),
        name="mla_attn",
    )(q, k, vt)


def _dsa_attn_kernel(q_ref, qp_ref, wi_ref, shift_ref, bias_ref, k_ref, vt_ref, kp_ref, o_ref,
                     sc_ref, h_ref, acc_ref, stage_ref, s_ref, qs_ref, m_ref, l_ref, *, tq, tk, topk, fixed):
    i = pl.program_id(1)
    t0 = i * tq
    n_blk = (t0 + tq + tk - 1) // tk
    n_far = jnp.maximum((t0 // LANES - 1) // (tk // LANES), 0)
    n_far_even = n_far - n_far % 2
    lim = ((t0 + lax.broadcasted_iota(jnp.int32, (1, tq), 1)) // CHUNK + 1) * CHUNK

    wv = wi_ref[...]

    def score_body(kb, carry):
        ks = pl.multiple_of(kb * tk, tk)
        kp = kp_ref[pl.ds(ks, tk), :]
        acc = jnp.zeros((tk, tq), F32)
        for h in range(H_IDX):
            acc = acc + jnp.maximum(_dot(kp, qp_ref[h]), 0.0) * wv[h:h + 1, :]
        row = ks + lax.broadcasted_iota(jnp.int32, (tk, tq), 0)
        acc = jnp.where(row < lim, acc, -jnp.inf)
        bits = pltpu.bitcast(acc, jnp.int32)
        key = bits ^ ((bits >> 31) & 0x7FFFFFFF)
        sc_ref[kb] = key
        h_ref[kb] = (key >> 16).astype(jnp.int16)
        return carry

    lax.fori_loop(0, n_blk, score_body, 0)

    def count16(pred):
        def count_body(kb, part):
            hit = jnp.where(pred(h_ref[kb]), jnp.int16(1), jnp.int16(0))
            lanes = [hit[r:r + 16] for r in range(0, tk, 16)]
            while len(lanes) > 1:
                lanes = [a + b for a, b in zip(lanes[0::2], lanes[1::2])]
            return part + lanes[0].astype(jnp.int32)

        part = lax.fori_loop(0, n_blk, count_body, jnp.zeros((16, tq), jnp.int32))
        return jnp.sum(part, axis=0, keepdims=True)

    def kth16(need):
        def bit_body(t, ans):
            cand = ans + jnp.left_shift(jnp.int32(1), 15 - t)
            cand16 = cand.astype(jnp.int16)
            return jnp.where(count16(lambda h: h >= cand16) >= need, cand, ans)

        return lax.fori_loop(0, 16, bit_body, jnp.full((1, tq), -(2 ** 15), jnp.int32))

    top = kth16(topk)
    top16 = top.astype(jnp.int16)
    need = topk - count16(lambda h: h > top16)

    def low_body(kb, carry):
        low = ((sc_ref[kb] & 0xFFFF) - 2 ** 15).astype(jnp.int16)
        h_ref[kb] = jnp.where(h_ref[kb] == top16, low, jnp.int16(-(2 ** 15)))
        return carry

    lax.fori_loop(0, n_blk, low_body, 0)
    low = kth16(need)
    thr = jnp.left_shift(top, 16) + (low + 2 ** 15)

    low16 = low.astype(jnp.int16)
    ties = (need - count16(lambda h: h > low16)).astype(F32)
    keep_bits = pltpu.bitcast(shift_ref[...], jnp.int32)
    below = (lax.broadcasted_iota(jnp.int32, (tk, tk), 0) > lax.broadcasted_iota(jnp.int32, (tk, tk), 1))
    below = jnp.where(below, 1.0, 0.0).astype(BF16)

    def mask_body(kb, seen):
        key = sc_ref[kb]
        row = kb * tk + lax.broadcasted_iota(jnp.int32, (tk, tq), 0)
        tie = jnp.where(key == thr, 1.0, 0.0)
        before = seen + _dot(below, tie.astype(BF16))
        thr_row = thr + jnp.where(before >= ties, 1, 0)
        sc_ref[kb] = jnp.where((key >= thr_row) & (row < lim), keep_bits, NEG_BITS)
        return seen + jnp.sum(tie, axis=0, keepdims=True)

    lax.fori_loop(0, n_blk, mask_body, jnp.zeros((1, tq), F32))

    def prep(p, carry):
        qs_ref[p] = _stack_heads(q_ref[p], HEAD_DIM)
        m_ref[p] = jnp.full((1, 2 * tq), M_INIT, F32)
        l_ref[p] = jnp.zeros((1, 2 * tq), F32)
        acc_ref[p] = jnp.zeros((LANES, 2 * tq), F32)
        return carry

    lax.fori_loop(0, N_PAIRS, prep, 0)

    def scores(t):
        kb, p = t // N_PAIRS, t % N_PAIRS
        ks = pl.multiple_of(kb * tk, tk)
        am = pltpu.bitcast(sc_ref[kb], F32)
        cols = []
        for hh in range(2):
            for a in range(tq // LANES):
                tiles = []
                for c in range(tk // LANES):
                    d = (kb * (tk // LANES) + c) - (i * (tq // LANES) + a)
                    kind = jnp.where(d == 0, 2, jnp.where(d == -1, 1, 0))
                    tiles.append(bias_ref[p, hh, kind])
                cols.append(jnp.concatenate(tiles, axis=0))
        return (_dot(k_ref[p, pl.ds(ks, tk), :], qs_ref[p]) + jnp.concatenate([am, am], axis=1)
                + jnp.concatenate(cols, axis=1))

    def consume(t, s, carry):
        kb, p = t // N_PAIRS, t % N_PAIRS
        m, l = _consume(s, _load_vt(vt_ref.at[p], kb, tk), m_ref[p], l_ref[p], acc_ref.at[p], fixed)
        l_ref[p] = l
        if not fixed:
            m_ref[p] = m
        return carry

    _sweep(0, n_blk * N_PAIRS, s_ref, scores, None, 0, None, fixed, peel=False, consume=consume)

    def finish(p, carry):
        stage_ref[p] = _flash_out(l_ref[p], acc_ref.at[p], tq)
        return carry

    lax.fori_loop(0, N_PAIRS, finish, 0)
    for p in range(N_PAIRS):
        o_ref[:, p * LANES:(p + 1) * LANES] = stage_ref[p]


def _dsa_attn(q, k, vt, qp, kp, wi, shift, bias, fixed, tq=256, tk=512):
    B, _, S, _ = q.shape
    topk = min(TOPK_MAX, S // 4)
    once = pl.Buffered(1)
    return pl.pallas_call(
        functools.partial(_dsa_attn_kernel, tq=tq, tk=tk, topk=topk, fixed=fixed),
        grid=(B, S // tq),
        in_specs=[
            pl.BlockSpec((None, N_PAIRS, tq, LANES), lambda b, i: (b, 0, i, 0)),
            pl.BlockSpec((None, H_IDX, 256, tq), lambda b, i: (b, 0, 0, i)),
            pl.BlockSpec((None, H_IDX, tq), lambda b, i: (b, 0, i)),
            pl.BlockSpec((1, tq), lambda b, i: (0, 0)),
            pl.BlockSpec((N_PAIRS, 2, 3, LANES, LANES), lambda b, i: (0, 0, 0, 0, 0), pipeline_mode=once),
            pl.BlockSpec((None, N_PAIRS, S, LANES), lambda b, i: (b, 0, 0, 0), pipeline_mode=once),
            pl.BlockSpec((None, N_PAIRS, S // VT_BLK, LANES, VT_BLK), lambda b, i: (b, 0, 0, 0, 0),
                         pipeline_mode=once),
            pl.BlockSpec((None, S, 256), lambda b, i: (b, 0, 0), pipeline_mode=once),
        ],
        out_specs=pl.BlockSpec((None, tq, N_HEADS * HEAD_DIM), lambda b, i: (b, i, 0)),
        out_shape=jax.ShapeDtypeStruct((B, S, N_HEADS * HEAD_DIM), BF16),
        scratch_shapes=[
            pltpu.VMEM((S // tk, tk, tq), jnp.int32),
            pltpu.VMEM((S // tk, tk, tq), jnp.int16),
            pltpu.VMEM((N_PAIRS, LANES, 2 * tq), F32),
            pltpu.VMEM((N_PAIRS, tq, LANES), BF16),
            pltpu.VMEM((2, tk, 2 * tq), F32),
            pltpu.VMEM((N_PAIRS, LANES, 2 * tq), BF16),
            pltpu.VMEM((N_PAIRS, 1, 2 * tq), F32),
            pltpu.VMEM((N_PAIRS, 1, 2 * tq), F32),
        ],
        compiler_params=_cparams(("parallel", "arbitrary"), VMEM_LIMIT_DSA),
        name="dsa_attn",
    )(q, qp, wi, jnp.full((1, tq), shift, F32), bias, k, vt, kp)


def _out_proj_kernel(x_ref, a_ref, w_ref, o_ref):
    o_ref[...] = x_ref[...] + _dot(a_ref[...], w_ref[...])


def _out_proj(x, a, w, tm=512):
    B, S, D = x.shape
    return pl.pallas_call(
        _out_proj_kernel,
        grid=(B, S // tm),
        in_specs=[
            pl.BlockSpec((None, tm, D), lambda b, i: (b, i, 0)),
            pl.BlockSpec((None, tm, a.shape[-1]), lambda b, i: (b, i, 0)),
            pl.BlockSpec(w.shape, lambda b, i: (0, 0)),
        ],
        out_specs=pl.BlockSpec((None, tm, D), lambda b, i: (b, i, 0)),
        out_shape=jax.ShapeDtypeStruct((B, S, D), F32),
        compiler_params=_cparams(("parallel", "parallel")),
        name="out_proj",
    )(x, a, w.astype(BF16))


def _ffn_kernel(x_ref, xp_ref, g_ref, wg_ref, wv_ref, cw_ref, cb_ref, wd_ref, o_ref, acc_ref, he_ref, u_ref,
                *, tm, n_chunks):
    i = pl.program_id(1)
    g = g_ref[...]
    he_ref[:HALO] = (_rms(xp_ref[...], g) * jnp.where(i > 0, 1.0, 0.0)).astype(BF16)
    he_ref[HALO:] = _rms(x_ref[...], g).astype(BF16)
    acc_ref[...] = jnp.zeros(acc_ref.shape, F32)

    def up(c, slot):
        he = he_ref[...]
        u_ref[slot, 0] = _dot(he, wg_ref[c])
        u_ref[slot, 1] = _dot(he, wv_ref[c])

    def conv(u, w, b):
        return (b + w[0:1] * u[HALO - 2:HALO - 2 + tm] + w[1:2] * u[HALO - 1:HALO - 1 + tm]
                + w[2:3] * u[HALO:HALO + tm])

    def down(c, slot):
        cw = cw_ref[c]
        cb = cb_ref[c]
        gate = conv(u_ref[slot, 0], cw[0], cb[0])
        val = conv(u_ref[slot, 1], cw[1], cb[1])
        act = gate * jax.nn.sigmoid(gate) * val
        acc_ref[...] += _dot(act.astype(BF16), wd_ref[c])

    up(0, 0)

    def pair_body(t, carry):
        c = 2 * t
        up(c + 1, 1)
        down(c, 0)
        up(c + 2, 0)
        down(c + 1, 1)
        return carry

    assert n_chunks % 2 == 1
    lax.fori_loop(0, n_chunks // 2, pair_body, 0)
    down(n_chunks - 1, 0)
    o_ref[...] = x_ref[...] + acc_ref[...]


def _ffn(x, g, w_up, conv_w, conv_b, w_down, tm=512):
    B, S, D = x.shape
    nc = D_FF // FF_CHUNK
    wup = w_up.astype(BF16).reshape(D, 2, nc, FF_CHUNK).transpose(1, 2, 0, 3)
    cw = conv_w.reshape(CONV_W, 2, nc, FF_CHUNK).transpose(2, 1, 0, 3)
    cb = conv_b.reshape(2, nc, 1, FF_CHUNK).transpose(1, 0, 2, 3)
    wd = w_down.astype(BF16).reshape(nc, FF_CHUNK, D)
    c3 = lambda b, i: (0, 0, 0)
    c4 = lambda b, i: (0, 0, 0, 0)
    return pl.pallas_call(
        functools.partial(_ffn_kernel, tm=tm, n_chunks=nc),
        grid=(B, S // tm),
        in_specs=[
            pl.BlockSpec((None, tm, D), lambda b, i: (b, i, 0)),
            pl.BlockSpec((None, HALO, D), lambda b, i: (b, jnp.maximum(i * (tm // HALO) - 1, 0), 0)),
            pl.BlockSpec((1, D), lambda b, i: (0, 0)),
            pl.BlockSpec((nc, D, FF_CHUNK), c3),
            pl.BlockSpec((nc, D, FF_CHUNK), c3),
            pl.BlockSpec((nc, 2, CONV_W, FF_CHUNK), c4),
            pl.BlockSpec((nc, 2, 1, FF_CHUNK), c4),
            pl.BlockSpec((nc, FF_CHUNK, D), c3),
        ],
        out_specs=pl.BlockSpec((None, tm, D), lambda b, i: (b, i, 0)),
        out_shape=jax.ShapeDtypeStruct((B, S, D), F32),
        scratch_shapes=[pltpu.VMEM((tm, D), F32), pltpu.VMEM((HALO + tm, D), BF16),
                        pltpu.VMEM((2, 2, HALO + tm, FF_CHUNK), F32)],
        compiler_params=_cparams(("parallel", "parallel")),
        name="conv_ffn",
    )(x, x, g.reshape(1, D), wup[0], wup[1], cw, cb, wd)


def _rope_tables(S, period, off):
    inv = 1.0 / (ROPE_THETA ** (jnp.arange(0, ROPE_DIM, 2, dtype=F32) / ROPE_DIM))
    ang = jnp.arange(S, dtype=F32)[:, None] * inv[None, :]
    cos, sin = jnp.cos(ang), jnp.sin(ang)
    one = jnp.ones((S, 1), F32)
    zero = jnp.zeros((S, 1), F32)

    def lay(x1, x2, fill):
        group = jnp.concatenate([jnp.tile(fill, (1, off)), x1, x2,
                                 jnp.tile(fill, (1, period - off - ROPE_DIM))], axis=1)
        return jnp.tile(group, (1, LANES // period))

    return lay(cos, cos, one), lay(-sin, zero * sin, zero), lay(zero * sin, sin, zero)


def _t5_bucket(rel):
    nb = T5_BUCKETS // 2
    max_exact = nb // 2
    n = jnp.abs(rel)
    large = max_exact + (jnp.log(jnp.maximum(n, 1).astype(F32) / max_exact)
                         / math.log(T5_MAX_DIST / max_exact) * (nb - max_exact)).astype(jnp.int32)
    large = jnp.minimum(large, nb - 1)
    return jnp.where(rel > 0, nb, 0) + jnp.where(n < max_exact, n, large)


def _dsa_bias_tiles(t5_bias):
    kj = jnp.arange(LANES, dtype=jnp.int32)[:, None]
    qi = jnp.arange(LANES, dtype=jnp.int32)[None, :]
    far = t5_bias[_t5_bucket(jnp.int32(-4 * T5_MAX_DIST))]

    def tile(rel):
        pick = jax.nn.one_hot(_t5_bucket(rel), T5_BUCKETS, dtype=F32)
        vals = jnp.einsum("kqb,bh->hkq", pick, t5_bias, precision=lax.Precision.HIGHEST)
        return (vals - far[:, None, None]) * LOG2E

    same, prev = tile(kj - qi), tile(kj - LANES - qi)
    return jnp.stack([jnp.zeros_like(same), prev, same], axis=1).reshape(N_PAIRS, 2, 3, LANES, LANES)


def _softmax_shift(gq, gk, dim, bias_hi=0.0, bias_lo=0.0):
    qk = 1.02 * dim * jnp.max(jnp.abs(gq)) * jnp.max(jnp.abs(gk))
    top = qk + bias_hi
    fixed = (top + qk - bias_lo) <= SAFE_SPREAD
    return fixed, jnp.where(fixed, -top, 0.0)


def _toeplitz_kernel(v_ref, o_ref, *, cols):
    rows, width = o_ref.shape[0], v_ref.shape[1]
    x = jnp.broadcast_to(v_ref[...], (rows, width))
    o_ref[...] = pltpu.roll(x, 0, 1, stride=1, stride_axis=0)[:, :cols]


def _toeplitz(v, rows, cols):
    H, L = v.shape
    return pl.pallas_call(
        functools.partial(_toeplitz_kernel, cols=cols),
        grid=(H,),
        in_specs=[pl.BlockSpec((None, 1, L), lambda h: (h, 0, 0))],
        out_specs=pl.BlockSpec((None, rows, cols), lambda h: (h, 0, 0)),
        out_shape=jax.ShapeDtypeStruct((H, rows, cols), F32),
        compiler_params=_cparams(("parallel",)),
        name="toeplitz",
    )(v.reshape(H, 1, L))


def _band_bias(rel_bias, shift, tq):
    pad = LEFT_CHUNKS * CHUNK
    nblk = -(-pad // tq) + 1
    lead = (nblk - 1) * tq
    nk = nblk * tq
    kk = jnp.arange(nk, dtype=jnp.int32)[:, None] - lead
    i = jnp.arange(tq, dtype=jnp.int32)[None, :]
    first = (i // CHUNK) * CHUNK - pad
    ok = (kk >= first) & (kk < (i // CHUNK + 1) * CHUNK)
    L = nk + tq
    dist = jnp.arange(L, dtype=jnp.int32)
    dist = lead + jnp.where(dist < tq, dist, dist - L)
    v = rel_bias[:, jnp.clip(dist, -(CHUNK - 1), REL_MAX_PAST) + CHUNK - 1]
    b = _toeplitz(v, nk, tq)
    b = jnp.where(ok[None], b + shift, NEG).reshape(N_PAIRS, 2, nblk, tq, tq)
    return jnp.transpose(b, (0, 2, 3, 1, 4)).reshape(N_PAIRS, nblk, tq, 2 * tq)


def kernel(x, norm_mix, norm_ffn, t5_bias, a_w_in, a_q_norm, a_k_norm, a_kidx_norm, a_w_out, b_w_down, b_q_a_norm, b_kv_a_norm, b_w_uq, b_w_ukv, b_q_norm, b_k_norm, b_w_out, c_w_in, c_q_norm, c_k_norm, c_rel_bias, c_w_out, f_w_up, f_conv_w, f_conv_b, f_w_down):
    B, S, D = x.shape
    depth = norm_mix.shape[0]

    tq_c = 256
    tabs_idx = _rope_tables(S, D_IDX, D_IDX - ROPE_DIM)
    tabs_mla = _rope_tables(S, LANES, QK_NOPE)
    scale = HEAD_DIM ** -0.5 * LOG2E
    for layer in range(depth):
        kind, j = layer % N_MIXERS, layer // N_MIXERS
        g = norm_mix[layer]
        if kind == 0:
            gq = a_q_norm[j] * scale
            q, k, vt = _qkv_proj(x, g, a_w_in[j][:, :A_QKV], gq, a_k_norm[j])
            qp, kp, wi = _idx_proj(x, g, a_w_in[j][:, A_QKV:], a_kidx_norm[j], tabs_idx)
            tiles = _dsa_bias_tiles(t5_bias)
            fixed, shift = _softmax_shift(gq, a_k_norm[j], HEAD_DIM, jnp.max(tiles), jnp.min(tiles))
            att = lax.cond(fixed, functools.partial(_dsa_attn, fixed=True), functools.partial(_dsa_attn, fixed=False),
                           q, k, vt, qp, kp, wi, shift, tiles)
            w_out = a_w_out[j]
        elif kind == 1:
            fixed, shift = _softmax_shift(b_q_norm[j] * (B_QK ** -0.5 * LOG2E), b_k_norm[j], B_QK)
            q, k, vt = _mla_proj(x, g, b_w_down[j], b_q_a_norm[j], b_kv_a_norm[j], b_w_uq[j], b_w_ukv[j],
                                 b_q_norm[j], b_k_norm[j], shift, tabs_mla)
            att = lax.cond(fixed, functools.partial(_mla_attn, fixed=True), functools.partial(_mla_attn, fixed=False),
                           q, k, vt)
            w_out = b_w_out[j]
        else:
            gq = c_q_norm[j] * scale
            q, k, vt = _qkv_proj(x, g, c_w_in[j], gq, c_k_norm[j])
            rel = c_rel_bias[j] * LOG2E
            fixed, shift = _softmax_shift(gq, c_k_norm[j], HEAD_DIM, jnp.max(rel), jnp.min(rel))
            att = lax.cond(fixed, functools.partial(_band_attn, fixed=True, tq=tq_c),
                           functools.partial(_band_attn, fixed=False, tq=tq_c),
                           q, k, vt, _band_bias(rel, shift, tq_c))
            w_out = c_w_out[j]
        x = _out_proj(x, att, w_out)
        x = _ffn(x, norm_ffn[layer], f_w_up[layer], f_conv_w[layer], f_conv_b[layer], f_w_down[layer])
    return x
```

```python
import functools
import math

import numpy as np
import jax
import jax.numpy as jnp
from jax import lax
from jax.experimental import pallas as pl
from jax.experimental.pallas import tpu as pltpu

F32 = jnp.float32
BF16 = jnp.bfloat16

D_MODEL = 1024
CHUNK = 64
EPS = 1e-6
ROPE_THETA = 10000.0
NEG = -1e30
NEG_BITS = int(np.float32(NEG).view(np.int32))
M_INIT = -1e38
LOG2E = math.log2(math.e)
SAFE_SPREAD = 100.0

N_MIXERS = 3
N_HEADS = 16
HEAD_DIM = 64
N_PAIRS = N_HEADS // 2
LANES = 128
VT_BLK = 256

H_IDX = 8
D_IDX = 64
ROPE_DIM = 32
TOPK_MAX = 256
T5_BUCKETS = 32
T5_MAX_DIST = 128
A_QKV = 3 * N_HEADS * HEAD_DIM
Q_LORA = 256
KV_LORA = 128
QK_NOPE = 64
QK_ROPE = 32
V_DIM = 64
B_QK = QK_NOPE + QK_ROPE
LEFT_CHUNKS = 8
REL_MAX_PAST = 128
D_FF = 2816
CONV_W = 3
FF_CHUNK = 256
HALO = 8

VMEM_LIMIT = 56 * 1024 * 1024
VMEM_LIMIT_DSA = 60 * 1024 * 1024


def _cparams(sem, vmem_limit=VMEM_LIMIT):
    return pltpu.CompilerParams(dimension_semantics=sem, vmem_limit_bytes=vmem_limit)


def _rms(x, g):
    return x * lax.rsqrt(jnp.mean(x * x, axis=-1, keepdims=True) + EPS) * g


def _dot(a, b):
    return jnp.dot(a, b, preferred_element_type=F32)


def _split_bf16(x):
    hi = x.astype(BF16)
    lo = (x - hi.astype(F32)).astype(BF16)
    return hi, lo


def _group_sumsq(y, bd):
    hi, lo = _split_bf16(y * y)
    return _dot(hi, bd) + _dot(lo, bd)


def _rope(y, ct, s1, s2):
    return y * ct + pltpu.roll(y, LANES - 16, 1) * s1 + pltpu.roll(y, 16, 1) * s2


def _rope_wide(y, ct, s1, s2):
    w = y.shape[-1]
    parts = [_rope(y[:, c:c + LANES], ct, s1, s2) for c in range(0, w, LANES)]
    return parts[0] if len(parts) == 1 else jnp.concatenate(parts, axis=-1)


def _store_vt(vt_ref, p, y):
    yt = y.T.astype(BF16)
    for r in range(y.shape[0] // VT_BLK):
        vt_ref[p, r] = yt[:, r * VT_BLK:(r + 1) * VT_BLK]


def _qkv_proj_kernel(x_ref, g_ref, w_ref, gq_ref, gk_ref, bd_ref, q_ref, k_ref, vt_ref, xb_ref, y_ref):
    xb_ref[...] = _rms(x_ref[...], g_ref[...]).astype(BF16)
    bd = bd_ref[...]
    per = N_HEADS * HEAD_DIM // 256
    outs = ((q_ref, gq_ref), (k_ref, gk_ref), (vt_ref, None))

    def project(n):
        y_ref[n % 2] = _dot(xb_ref[...], w_ref[:, n * 256:(n + 1) * 256])

    def finish(n):
        (o_ref, gain_ref), c = outs[n // per], n % per
        y = y_ref[n % 2]
        if gain_ref is None:
            _store_vt(o_ref, 2 * c, y[:, :LANES])
            _store_vt(o_ref, 2 * c + 1, y[:, LANES:])
        else:
            y = y * lax.rsqrt(_group_sumsq(y, bd) * (1.0 / HEAD_DIM) + EPS) * gain_ref[...]
            yb = y.astype(BF16)
            o_ref[2 * c] = yb[:, :LANES]
            o_ref[2 * c + 1] = yb[:, LANES:]

    project(0)
    for n in range(3 * per):
        if n + 1 < 3 * per:
            project(n + 1)
        finish(n)


def _vt_spec(tm):
    return pl.BlockSpec((None, N_PAIRS, tm // VT_BLK, LANES, VT_BLK), lambda b, i: (b, 0, i, 0, 0))


def _vt_shape(B, S):
    return jax.ShapeDtypeStruct((B, N_PAIRS, S // VT_BLK, LANES, VT_BLK), BF16)


def _qkv_proj(x, g, w, gq, gk, tm=512):
    B, S, D = x.shape
    hd = N_HEADS * HEAD_DIM
    bd = jnp.kron(jnp.eye(256 // HEAD_DIM, dtype=F32), jnp.ones((HEAD_DIM, HEAD_DIM), F32)).astype(BF16)
    out = jax.ShapeDtypeStruct((B, N_PAIRS, S, LANES), BF16)
    const = lambda b, i: (0, 0)
    ospec = pl.BlockSpec((None, N_PAIRS, tm, LANES), lambda b, i: (b, 0, i, 0))
    return pl.pallas_call(
        _qkv_proj_kernel,
        grid=(B, S // tm),
        in_specs=[
            pl.BlockSpec((None, tm, D), lambda b, i: (b, i, 0)),
            pl.BlockSpec((1, D), const),
            pl.BlockSpec((D, 3 * hd), const),
            pl.BlockSpec((1, 256), const),
            pl.BlockSpec((1, 256), const),
            pl.BlockSpec((256, 256), const),
        ],
        out_specs=[ospec, ospec, _vt_spec(tm)],
        out_shape=[out, out, _vt_shape(B, S)],
        scratch_shapes=[pltpu.VMEM((tm, D), BF16), pltpu.VMEM((2, tm, 256), F32)],
        compiler_params=_cparams(("parallel", "parallel")),
        name="qkv_proj",
    )(x, g.reshape(1, D), w.astype(BF16), jnp.tile(gq, 4).reshape(1, 256), jnp.tile(gk, 4).reshape(1, 256), bd)


def _idx_proj_kernel(x_ref, g_ref, wh_ref, wl_ref, gki_ref, ct_ref, s1_ref, s2_ref, qp_ref, kp_ref, wi_ref,
                     *, w_scale):
    xn = _rms(x_ref[...], g_ref[...])
    xh, xl = _split_bf16(xn)

    def mm(c0, c1):
        wh = wh_ref[:, c0:c1]
        return _dot(xh, wh) + (_dot(xl, wh) + _dot(xh, wl_ref[:, c0:c1]))

    ct, s1, s2 = ct_ref[...], s1_ref[...], s2_ref[...]
    tm = xn.shape[0]
    low = lax.broadcasted_iota(jnp.int32, (tm, LANES), 1) < D_IDX
    zero = jnp.zeros((tm, LANES), F32)

    for c in range(H_IDX // 2):
        y = _rope(mm(c * LANES, (c + 1) * LANES), ct, s1, s2)
        hi = y.astype(BF16).astype(F32)
        lo = y - hi
        rhi = pltpu.roll(hi, D_IDX, 1)
        rlo = pltpu.roll(lo, D_IDX, 1)
        qp_ref[2 * c, :LANES, :] = jnp.where(low, hi, rlo).T.astype(BF16)
        qp_ref[2 * c, LANES:, :] = jnp.where(low, hi, zero).T.astype(BF16)
        qp_ref[2 * c + 1, :LANES, :] = jnp.where(low, rhi, lo).T.astype(BF16)
        qp_ref[2 * c + 1, LANES:, :] = jnp.where(low, rhi, zero).T.astype(BF16)

    c0 = H_IDX * D_IDX
    y = mm(c0, c0 + LANES)
    y = y * lax.rsqrt(jnp.sum(y * y, axis=-1, keepdims=True) * (1.0 / D_IDX) + EPS) * gki_ref[...]
    y = _rope(y, ct, s1, s2)
    hi = y.astype(BF16).astype(F32)
    lo = y - hi
    kp_ref[:, :LANES] = jnp.where(low, hi, pltpu.roll(hi, D_IDX, 1)).astype(BF16)
    kp_ref[:, LANES:] = jnp.where(low, lo, zero).astype(BF16)

    wi_ref[...] = (mm(c0 + LANES, c0 + 2 * LANES) * w_scale).T[:H_IDX]


def _idx_proj(x, g, w_idx, gki, tabs, tm=512):
    B, S, D = x.shape
    nq = H_IDX * D_IDX
    wq, wk, ww = w_idx[:, :nq], w_idx[:, nq:nq + D_IDX], w_idx[:, nq + D_IDX:]
    wpad = jnp.concatenate([
        wq, jnp.pad(wk, ((0, 0), (0, LANES - D_IDX))), jnp.pad(ww, ((0, 0), (0, LANES - H_IDX)))], axis=1)
    wh = wpad.astype(BF16)
    wl = (wpad - wh.astype(F32)).astype(BF16)
    ncol = nq + 2 * LANES
    gpad = jnp.pad(gki, (0, LANES - D_IDX)).reshape(1, LANES)
    const = lambda b, i: (0, 0)
    tspec = pl.BlockSpec((tm, LANES), lambda b, i: (i, 0))
    return pl.pallas_call(
        functools.partial(_idx_proj_kernel, w_scale=(D_IDX ** -0.5) * (H_IDX ** -0.5)),
        grid=(B, S // tm),
        in_specs=[
            pl.BlockSpec((None, tm, D), lambda b, i: (b, i, 0)),
            pl.BlockSpec((1, D), const),
            pl.BlockSpec((D, ncol), const),
            pl.BlockSpec((D, ncol), const),
            pl.BlockSpec((1, LANES), const),
            tspec, tspec, tspec,
        ],
        out_specs=[
            pl.BlockSpec((None, H_IDX, 256, tm), lambda b, i: (b, 0, 0, i)),
            pl.BlockSpec((None, tm, 256), lambda b, i: (b, i, 0)),
            pl.BlockSpec((None, H_IDX, tm), lambda b, i: (b, 0, i)),
        ],
        out_shape=[
            jax.ShapeDtypeStruct((B, H_IDX, 256, S), BF16),
            jax.ShapeDtypeStruct((B, S, 256), BF16),
            jax.ShapeDtypeStruct((B, H_IDX, S), F32),
        ],
        compiler_params=_cparams(("parallel", "parallel")),
        name="idx_proj",
    )(x, g.reshape(1, D), wh, wl, gpad, *tabs)


def _mla_proj_kernel(x_ref, g_ref, wd_ref, gqa_ref, gkva_ref, wuq_ref, wuk_ref, wuv_ref, gq_ref, gk_ref, bd_ref,
                     qfix_ref, kfix_ref, ct_ref, s1_ref, s2_ref, q_ref, k_ref, vt_ref):
    xb = _rms(x_ref[...], g_ref[...]).astype(BF16)
    d = _dot(xb, wd_ref[...])
    cq = _rms(d[:, :Q_LORA], gqa_ref[...]).astype(BF16)
    ckv = _rms(d[:, Q_LORA:Q_LORA + KV_LORA], gkva_ref[...]).astype(BF16)
    krz = pltpu.roll(d[:, Q_LORA + KV_LORA:], QK_NOPE, 1)
    krz2 = jnp.concatenate([krz, krz], axis=-1)
    bd = bd_ref[...]
    ct, s1, s2 = ct_ref[...], s1_ref[...], s2_ref[...]
    inv = 1.0 / B_QK
    for p in range(N_PAIRS):
        cols = slice(p * 256, (p + 1) * 256)
        y = _dot(cq, wuq_ref[:, cols])
        y = y * lax.rsqrt(_group_sumsq(y, bd) * inv + EPS) * gq_ref[...]
        q_ref[p] = (_rope_wide(y, ct, s1, s2) + qfix_ref[...]).astype(BF16)
        y = _dot(ckv, wuk_ref[:, cols]) + krz2
        y = y * lax.rsqrt(_group_sumsq(y, bd) * inv + EPS) * gk_ref[...]
        k_ref[p] = (_rope_wide(y, ct, s1, s2) + kfix_ref[...]).astype(BF16)
        _store_vt(vt_ref, p, _dot(ckv, wuv_ref[:, p * LANES:(p + 1) * LANES]))


def _mla_proj(x, g, w_down, gqa, gkva, w_uq, w_ukv, gq, gk, shift, tabs, tm=512):
    B, S, D = x.shape
    nd = Q_LORA + KV_LORA + QK_ROPE
    wd = jnp.pad(w_down, ((0, 0), (0, 512 - nd))).astype(BF16)
    padh = LANES - B_QK
    wuq = jnp.pad(w_uq.reshape(Q_LORA, N_HEADS, B_QK), ((0, 0), (0, 0), (0, padh))).reshape(Q_LORA, N_HEADS * LANES)
    wukv = w_ukv.reshape(KV_LORA, N_HEADS, QK_NOPE + V_DIM)
    wuk = jnp.pad(wukv[:, :, :QK_NOPE], ((0, 0), (0, 0), (0, LANES - QK_NOPE))).reshape(KV_LORA, N_HEADS * LANES)
    wuv = wukv[:, :, QK_NOPE:].reshape(KV_LORA, N_HEADS * V_DIM)
    gq2 = jnp.tile(jnp.pad(gq * (B_QK ** -0.5 * LOG2E), (0, padh)), 2).reshape(1, 256)
    gk2 = jnp.tile(jnp.pad(gk, (0, padh)), 2).reshape(1, 256)
    bd = jnp.kron(jnp.eye(2, dtype=F32), jnp.ones((LANES, LANES), F32)).astype(BF16)
    kfix = jnp.asarray((np.arange(256) % LANES == B_QK).astype(np.float32)).reshape(1, 256)
    qfix = kfix * shift
    const = lambda b, i: (0, 0)
    tspec = pl.BlockSpec((tm, LANES), lambda b, i: (i, 0))
    qk_shape = jax.ShapeDtypeStruct((B, N_PAIRS, S, 256), BF16)
    qk_spec = pl.BlockSpec((None, N_PAIRS, tm, 256), lambda b, i: (b, 0, i, 0))
    return pl.pallas_call(
        _mla_proj_kernel,
        grid=(B, S // tm),
        in_specs=[
            pl.BlockSpec((None, tm, D), lambda b, i: (b, i, 0)),
            pl.BlockSpec((1, D), const),
            pl.BlockSpec((D, 512), const),
            pl.BlockSpec((1, Q_LORA), const),
            pl.BlockSpec((1, KV_LORA), const),
            pl.BlockSpec((Q_LORA, N_HEADS * LANES), const),
            pl.BlockSpec((KV_LORA, N_HEADS * LANES), const),
            pl.BlockSpec((KV_LORA, N_HEADS * V_DIM), const),
            pl.BlockSpec((1, 256), const),
            pl.BlockSpec((1, 256), const),
            pl.BlockSpec((256, 256), const),
            pl.BlockSpec((1, 256), const),
            pl.BlockSpec((1, 256), const),
            tspec, tspec, tspec,
        ],
        out_specs=[qk_spec, qk_spec, _vt_spec(tm)],
        out_shape=[qk_shape, qk_shape, _vt_shape(B, S)],
        compiler_params=_cparams(("parallel", "parallel")),
        name="mla_proj",
    )(x, g.reshape(1, D), wd, gqa.reshape(1, Q_LORA), gkva.reshape(1, KV_LORA), wuq.astype(BF16),
      wuk.astype(BF16), wuv.astype(BF16), gq2, gk2, bd, qfix, kfix, *tabs)


def _stack_heads(q2, half):
    qf = q2.astype(F32)
    low = lax.broadcasted_iota(jnp.int32, qf.shape, 1) < half
    zero = jnp.zeros_like(qf)
    return jnp.concatenate([jnp.where(low, qf, zero).T, jnp.where(low, zero, qf).T], axis=1).astype(BF16)


def _flash_init(acc_ref):
    r = acc_ref.shape[1]
    acc_ref[...] = jnp.zeros(acc_ref.shape, F32)
    return jnp.full((1, r), M_INIT, F32), jnp.zeros((1, r), F32)


def _consume(s, vt, m, l, acc_ref, fixed):
    if fixed:
        p = jnp.exp2(s)
        acc_ref[...] += _dot(vt, p.astype(BF16))
        return m, l + jnp.sum(p, axis=0, keepdims=True)
    m_new = jnp.maximum(m, jnp.max(s, axis=0, keepdims=True))
    alpha = jnp.exp2(m - m_new)
    p = jnp.exp2(s - m_new)
    l_new = alpha * l + jnp.sum(p, axis=0, keepdims=True)
    acc_ref[...] = alpha * acc_ref[...] + _dot(vt, p.astype(BF16))
    return m_new, l_new


def _sweep(first, n, s_ref, scores, values, ml, acc_ref, fixed, peel=True, consume=None):
    def one(j, s, ml):
        if consume is not None:
            return consume(j, s, ml)
        return _consume(s, values(j), *ml, acc_ref, fixed)

    if isinstance(n, int):
        tiles = [scores(first + j) for j in range(n)]
        for j in range(n):
            ml = one(first + j, tiles[j], ml)
        return ml

    odd = n % 2 if peel else 0
    if peel:
        ml = lax.cond(odd == 1, lambda ml: one(first, scores(first), ml), lambda ml: ml, ml)
    j0 = first + odd
    last = first + n - 1

    @pl.when(n >= 2)
    def _():
        s_ref[0] = scores(j0)

    def body(t, ml):
        j = j0 + 2 * t
        s1 = scores(j + 1)
        ml = one(j, s_ref[0], ml)
        s_ref[1] = s1
        s0 = scores(jnp.minimum(j + 2, last))
        ml = one(j + 1, s_ref[1], ml)
        s_ref[0] = s0
        return ml

    return lax.fori_loop(0, n // 2, body, ml)


def _flash_out(l, acc_ref, tq):
    o = acc_ref[...] / l
    ot = jnp.concatenate([o[:HEAD_DIM, :tq], o[HEAD_DIM:, tq:]], axis=0)
    return ot.T.astype(BF16)


def _load_vt(vt_ref, kb, tk):
    n = tk // VT_BLK
    parts = [vt_ref[kb * n + r] for r in range(n)]
    return parts[0] if n == 1 else jnp.concatenate(parts, axis=1)


def _band_attn_kernel(q_ref, k_ref, vt_ref, bias_ref, o_ref, acc_ref, s_ref, *, tq, nblk, fixed):
    i = pl.program_id(2)
    qs = _stack_heads(q_ref[...], HEAD_DIM)
    lead = i - (nblk - 1)

    def scores(kb):
        ks = pl.multiple_of(kb * tq, tq)
        return _dot(k_ref[pl.ds(ks, tq), :], qs) + bias_ref[kb - lead]

    values = lambda kb: _load_vt(vt_ref, kb, tq)
    ml0 = _flash_init(acc_ref)
    _, l = lax.cond(
        lead >= 0,
        lambda ml: _sweep(lead, nblk, s_ref, scores, values, ml, acc_ref, fixed),
        lambda ml: _sweep(0, i + 1, s_ref, scores, values, ml, acc_ref, fixed),
        ml0)
    o_ref[...] = _flash_out(l, acc_ref, tq)


def _band_attn(q, k, vt, bias, fixed, tq=256):
    B, _, S, _ = q.shape
    nblk = bias.shape[1]
    return pl.pallas_call(
        functools.partial(_band_attn_kernel, tq=tq, nblk=nblk, fixed=fixed),
        grid=(B, N_PAIRS, S // tq),
        in_specs=[
            pl.BlockSpec((None, None, tq, LANES), lambda b, p, i: (b, p, i, 0)),
            pl.BlockSpec((None, None, S, LANES), lambda b, p, i: (b, p, 0, 0)),
            pl.BlockSpec((None, None, S // VT_BLK, LANES, VT_BLK), lambda b, p, i: (b, p, 0, 0, 0)),
            pl.BlockSpec((None, nblk, tq, 2 * tq), lambda b, p, i: (p, 0, 0, 0)),
        ],
        out_specs=pl.BlockSpec((None, tq, LANES), lambda b, p, i: (b, i, p)),
        out_shape=jax.ShapeDtypeStruct((B, S, N_HEADS * HEAD_DIM), BF16),
        scratch_shapes=[pltpu.VMEM((LANES, 2 * tq), F32), pltpu.VMEM((2, tq, 2 * tq), F32)],
        compiler_params=_cparams(("parallel", "parallel", "arbitrary")),
        name="band_attn",
    )(q, k, vt, bias)


def _mla_attn_kernel(q_ref, k_ref, vt_ref, o_ref, acc_ref, s_ref, *, tq, tk, fixed):
    i = pl.program_id(2)
    qs = _stack_heads(q_ref[...], LANES)
    t0 = i * tq
    assert tq % (2 * tk) == 0
    n_full = t0 // tk
    n_edge = tq // tk
    lim = ((t0 + lax.broadcasted_iota(jnp.int32, (1, tq), 1)) // CHUNK + 1) * CHUNK

    def scores(kb):
        ks = pl.multiple_of(kb * tk, tk)
        return _dot(k_ref[pl.ds(ks, tk), :], qs)

    def edge_scores(kb):
        row = kb * tk + lax.broadcasted_iota(jnp.int32, (tk, tq), 0)
        add = jnp.where(row < lim, 0.0, NEG).astype(F32)
        return scores(kb) + jnp.concatenate([add, add], axis=1)

    values = lambda kb: _load_vt(vt_ref, kb, tk)
    ml = _sweep(0, n_full, s_ref, scores, values, _flash_init(acc_ref), acc_ref, fixed, peel=False)
    if not fixed:
        _, l = _sweep(n_full, n_edge, s_ref, edge_scores, values, ml, acc_ref, fixed, peel=False)
    else:
        kr = lax.broadcasted_iota(jnp.int32, (tk, tk), 0)
        qc = lax.broadcasted_iota(jnp.int32, (tk, tk), 1)
        diag = jnp.where(kr < (qc // CHUNK + 1) * CHUNK, 0.0, NEG).astype(F32)
        l = ml[1]
        for e in range(n_edge):
            lo, w = e * tk, tq - e * tk
            ks = pl.multiple_of((n_full + e) * tk, tk)
            qe = jnp.concatenate([qs[:, lo:tq], qs[:, tq + lo:]], axis=1)
            head_add = diag if w == tk else jnp.concatenate([diag, jnp.zeros((tk, w - tk), F32)], axis=1)
            p = jnp.exp2(_dot(k_ref[pl.ds(ks, tk), :], qe) + jnp.concatenate([head_add, head_add], axis=1))
            pv = _dot(values(n_full + e), p.astype(BF16))
            acc_ref[:, lo:tq] += pv[:, :w]
            acc_ref[:, tq + lo:] += pv[:, w:]
            ps = jnp.sum(p, axis=0, keepdims=True)
            parts = [l[:, lo:tq] + ps[:, :w], l[:, tq + lo:] + ps[:, w:]]
            if lo:
                parts = [l[:, :lo], parts[0], l[:, tq:tq + lo], parts[1]]
            l = jnp.concatenate(parts, axis=1)
    o_ref[...] = _flash_out(l, acc_ref, tq)


def _mla_attn(q, k, vt, fixed, tq=1024, tk=256):
    B, _, S, _ = q.shape
    tk = min(tk, S)
    return pl.pallas_call(
        functools.partial(_mla_attn_kernel, tq=tq, tk=tk, fixed=fixed),
        grid=(B, N_PAIRS, S // tq),
        in_specs=[
            pl.BlockSpec((None, None, tq, 256), lambda b, p, i: (b, p, i, 0)),
            pl.BlockSpec((None, None, S, 256), lambda b, p, i: (b, p, 0, 0)),
            pl.BlockSpec((None, None, S // VT_BLK, LANES, VT_BLK), lambda b, p, i: (b, p, 0, 0, 0)),
        ],
        out_specs=pl.BlockSpec((None, tq, LANES), lambda b, p, i: (b, i, p)),
        out_shape=jax.ShapeDtypeStruct((B, S, N_HEADS * V_DIM), BF16),
        scratch_shapes=[pltpu.VMEM((LANES, 2 * tq), F32), pltpu.VMEM((2, tk, 2 * tq), F32)],
        compiler_params=_cparams(("parallel", "parallel", "arbitrary")),
        name="mla_attn",
    )(q, k, vt)


def _dsa_attn_kernel(q_ref, qp_ref, wi_ref, shift_ref, bias_ref, k_ref, vt_ref, kp_ref, o_ref,
                     sc_ref, h_ref, acc_ref, stage_ref, s_ref, qs_ref, m_ref, l_ref, *, tq, tk, topk, fixed):
    i = pl.program_id(1)
    t0 = i * tq
    n_blk = (t0 + tq + tk - 1) // tk
    n_far = jnp.maximum((t0 // LANES - 1) // (tk // LANES), 0)
    n_far_even = n_far - n_far % 2
    lim = ((t0 + lax.broadcasted_iota(jnp.int32, (1, tq), 1)) // CHUNK + 1) * CHUNK

    wv = wi_ref[...]

    def raw_scores(kb):
        ks = pl.multiple_of(kb * tk, tk)
        kp = kp_ref[pl.ds(ks, tk), :]
        acc = jnp.zeros((tk, tq), F32)
        for h in range(H_IDX):
            acc = acc + jnp.maximum(_dot(kp, qp_ref[h]), 0.0) * wv[h:h + 1, :]
        return acc

    def store_keys(kb, acc, carry):
        row = kb * tk + lax.broadcasted_iota(jnp.int32, (tk, tq), 0)
        acc = jnp.where(row < lim, acc, -jnp.inf)
        bits = pltpu.bitcast(acc, jnp.int32)
        key = bits ^ ((bits >> 31) & 0x7FFFFFFF)
        sc_ref[kb] = key
        h_ref[kb] = (key >> 16).astype(jnp.int16)
        return carry

    _sweep(0, n_blk, s_ref.at[:, :, pl.ds(0, tq)], raw_scores, None, 0, None, fixed, consume=store_keys)

    def count16(pred):
        def count_body(kb, part):
            hit = jnp.where(pred(h_ref[kb]), jnp.int16(1), jnp.int16(0))
            lanes = [hit[r:r + 16] for r in range(0, tk, 16)]
            while len(lanes) > 1:
                lanes = [a + b for a, b in zip(lanes[0::2], lanes[1::2])]
            return part + lanes[0].astype(jnp.int32)

        part = lax.fori_loop(0, n_blk, count_body, jnp.zeros((16, tq), jnp.int32))
        return jnp.sum(part, axis=0, keepdims=True)

    def kth16(need):
        def bit_body(t, ans):
            cand = ans + jnp.left_shift(jnp.int32(1), 15 - t)
            cand16 = cand.astype(jnp.int16)
            return jnp.where(count16(lambda h: h >= cand16) >= need, cand, ans)

        return lax.fori_loop(0, 16, bit_body, jnp.full((1, tq), -(2 ** 15), jnp.int32))

    top = kth16(topk)
    top16 = top.astype(jnp.int16)
    need = topk - count16(lambda h: h > top16)

    def low_body(kb, carry):
        low = ((sc_ref[kb] & 0xFFFF) - 2 ** 15).astype(jnp.int16)
        h_ref[kb] = jnp.where(h_ref[kb] == top16, low, jnp.int16(-(2 ** 15)))
        return carry

    lax.fori_loop(0, n_blk, low_body, 0)
    low = kth16(need)
    thr = jnp.left_shift(top, 16) + (low + 2 ** 15)

    low16 = low.astype(jnp.int16)
    ties = (need - count16(lambda h: h > low16)).astype(F32)
    keep_bits = pltpu.bitcast(shift_ref[...], jnp.int32)
    below = (lax.broadcasted_iota(jnp.int32, (tk, tk), 0) > lax.broadcasted_iota(jnp.int32, (tk, tk), 1))
    below = jnp.where(below, 1.0, 0.0).astype(BF16)

    def mask_body(kb, seen):
        key = sc_ref[kb]
        row = kb * tk + lax.broadcasted_iota(jnp.int32, (tk, tq), 0)
        tie = jnp.where(key == thr, 1.0, 0.0)
        before = seen + _dot(below, tie.astype(BF16))
        thr_row = thr + jnp.where(before >= ties, 1, 0)
        sc_ref[kb] = jnp.where((key >= thr_row) & (row < lim), keep_bits, NEG_BITS)
        return seen + jnp.sum(tie, axis=0, keepdims=True)

    lax.fori_loop(0, n_blk, mask_body, jnp.zeros((1, tq), F32))

    def prep(p, carry):
        qs_ref[p] = _stack_heads(q_ref[p], HEAD_DIM)
        m_ref[p] = jnp.full((1, 2 * tq), M_INIT, F32)
        l_ref[p] = jnp.zeros((1, 2 * tq), F32)
        acc_ref[p] = jnp.zeros((LANES, 2 * tq), F32)
        return carry

    lax.fori_loop(0, N_PAIRS, prep, 0)

    def scores(t):
        kb, p = t // N_PAIRS, t % N_PAIRS
        ks = pl.multiple_of(kb * tk, tk)
        am = pltpu.bitcast(sc_ref[kb], F32)
        cols = []
        for hh in range(2):
            for a in range(tq // LANES):
                tiles = []
                for c in range(tk // LANES):
                    d = (kb * (tk // LANES) + c) - (i * (tq // LANES) + a)
                    kind = jnp.where(d == 0, 2, jnp.where(d == -1, 1, 0))
                    tiles.append(bias_ref[p, hh, kind])
                cols.append(jnp.concatenate(tiles, axis=0))
        return (_dot(k_ref[p, pl.ds(ks, tk), :], qs_ref[p]) + jnp.concatenate([am, am], axis=1)
                + jnp.concatenate(cols, axis=1))

    def consume(t, s, carry):
        kb, p = t // N_PAIRS, t % N_PAIRS
        m, l = _consume(s, _load_vt(vt_ref.at[p], kb, tk), m_ref[p], l_ref[p], acc_ref.at[p], fixed)
        l_ref[p] = l
        if not fixed:
            m_ref[p] = m
        return carry

    _sweep(0, n_blk * N_PAIRS, s_ref, scores, None, 0, None, fixed, peel=False, consume=consume)

    def finish(p, carry):
        stage_ref[p] = _flash_out(l_ref[p], acc_ref.at[p], tq)
        return carry

    lax.fori_loop(0, N_PAIRS, finish, 0)
    for p in range(N_PAIRS):
        o_ref[:, p * LANES:(p + 1) * LANES] = stage_ref[p]


def _dsa_attn(q, k, vt, qp, kp, wi, shift, bias, fixed, tq=256, tk=512):
    B, _, S, _ = q.shape
    topk = min(TOPK_MAX, S // 4)
    once = pl.Buffered(1)
    return pl.pallas_call(
        functools.partial(_dsa_attn_kernel, tq=tq, tk=tk, topk=topk, fixed=fixed),
        grid=(B, S // tq),
        in_specs=[
            pl.BlockSpec((None, N_PAIRS, tq, LANES), lambda b, i: (b, 0, i, 0)),
            pl.BlockSpec((None, H_IDX, 256, tq), lambda b, i: (b, 0, 0, i)),
            pl.BlockSpec((None, H_IDX, tq), lambda b, i: (b, 0, i)),
            pl.BlockSpec((1, tq), lambda b, i: (0, 0)),
            pl.BlockSpec((N_PAIRS, 2, 3, LANES, LANES), lambda b, i: (0, 0, 0, 0, 0), pipeline_mode=once),
            pl.BlockSpec((None, N_PAIRS, S, LANES), lambda b, i: (b, 0, 0, 0), pipeline_mode=once),
            pl.BlockSpec((None, N_PAIRS, S // VT_BLK, LANES, VT_BLK), lambda b, i: (b, 0, 0, 0, 0),
                         pipeline_mode=once),
            pl.BlockSpec((None, S, 256), lambda b, i: (b, 0, 0), pipeline_mode=once),
        ],
        out_specs=pl.BlockSpec((None, tq, N_HEADS * HEAD_DIM), lambda b, i: (b, i, 0)),
        out_shape=jax.ShapeDtypeStruct((B, S, N_HEADS * HEAD_DIM), BF16),
        scratch_shapes=[
            pltpu.VMEM((S // tk, tk, tq), jnp.int32),
            pltpu.VMEM((S // tk, tk, tq), jnp.int16),
            pltpu.VMEM((N_PAIRS, LANES, 2 * tq), F32),
            pltpu.VMEM((N_PAIRS, tq, LANES), BF16),
            pltpu.VMEM((2, tk, 2 * tq), F32),
            pltpu.VMEM((N_PAIRS, LANES, 2 * tq), BF16),
            pltpu.VMEM((N_PAIRS, 1, 2 * tq), F32),
            pltpu.VMEM((N_PAIRS, 1, 2 * tq), F32),
        ],
        compiler_params=_cparams(("parallel", "arbitrary"), VMEM_LIMIT_DSA),
        name="dsa_attn",
    )(q, qp, wi, jnp.full((1, tq), shift, F32), bias, k, vt, kp)


def _out_proj_kernel(x_ref, a_ref, w_ref, o_ref):
    o_ref[...] = x_ref[...] + _dot(a_ref[...], w_ref[...])


def _out_proj(x, a, w, tm=512):
    B, S, D = x.shape
    return pl.pallas_call(
        _out_proj_kernel,
        grid=(B, S // tm),
        in_specs=[
            pl.BlockSpec((None, tm, D), lambda b, i: (b, i, 0)),
            pl.BlockSpec((None, tm, a.shape[-1]), lambda b, i: (b, i, 0)),
            pl.BlockSpec(w.shape, lambda b, i: (0, 0)),
        ],
        out_specs=pl.BlockSpec((None, tm, D), lambda b, i: (b, i, 0)),
        out_shape=jax.ShapeDtypeStruct((B, S, D), F32),
        compiler_params=_cparams(("parallel", "parallel")),
        name="out_proj",
    )(x, a, w.astype(BF16))


def _ffn_kernel(x_ref, xp_ref, g_ref, wg_ref, wv_ref, cw_ref, cb_ref, wd_ref, o_ref, acc_ref, he_ref, u_ref,
                *, tm, n_chunks):
    i = pl.program_id(1)
    g = g_ref[...]
    he_ref[:HALO] = (_rms(xp_ref[...], g) * jnp.where(i > 0, 1.0, 0.0)).astype(BF16)
    he_ref[HALO:] = _rms(x_ref[...], g).astype(BF16)
    acc_ref[...] = jnp.zeros(acc_ref.shape, F32)

    def up(c, slot):
        he = he_ref[...]
        u_ref[slot, 0] = _dot(he, wg_ref[c])
        u_ref[slot, 1] = _dot(he, wv_ref[c])

    def conv(u, w, b):
        return (b + w[0:1] * u[HALO - 2:HALO - 2 + tm] + w[1:2] * u[HALO - 1:HALO - 1 + tm]
                + w[2:3] * u[HALO:HALO + tm])

    def down(c, slot):
        cw = cw_ref[c]
        cb = cb_ref[c]
        gate = conv(u_ref[slot, 0], cw[0], cb[0])
        val = conv(u_ref[slot, 1], cw[1], cb[1])
        act = gate * jax.nn.sigmoid(gate) * val
        acc_ref[...] += _dot(act.astype(BF16), wd_ref[c])

    up(0, 0)

    def pair_body(t, carry):
        c = 2 * t
        up(c + 1, 1)
        down(c, 0)
        up(c + 2, 0)
        down(c + 1, 1)
        return carry

    assert n_chunks % 2 == 1
    lax.fori_loop(0, n_chunks // 2, pair_body, 0)
    down(n_chunks - 1, 0)
    o_ref[...] = x_ref[...] + acc_ref[...]


def _ffn(x, g, w_up, conv_w, conv_b, w_down, tm=512):
    B, S, D = x.shape
    nc = D_FF // FF_CHUNK
    wup = w_up.astype(BF16).reshape(D, 2, nc, FF_CHUNK).transpose(1, 2, 0, 3)
    cw = conv_w.reshape(CONV_W, 2, nc, FF_CHUNK).transpose(2, 1, 0, 3)
    cb = conv_b.reshape(2, nc, 1, FF_CHUNK).transpose(1, 0, 2, 3)
    wd = w_down.astype(BF16).reshape(nc, FF_CHUNK, D)
    c3 = lambda b, i: (0, 0, 0)
    c4 = lambda b, i: (0, 0, 0, 0)
    return pl.pallas_call(
        functools.partial(_ffn_kernel, tm=tm, n_chunks=nc),
        grid=(B, S // tm),
        in_specs=[
            pl.BlockSpec((None, tm, D), lambda b, i: (b, i, 0)),
            pl.BlockSpec((None, HALO, D), lambda b, i: (b, jnp.maximum(i * (tm // HALO) - 1, 0), 0)),
            pl.BlockSpec((1, D), lambda b, i: (0, 0)),
            pl.BlockSpec((nc, D, FF_CHUNK), c3),
            pl.BlockSpec((nc, D, FF_CHUNK), c3),
            pl.BlockSpec((nc, 2, CONV_W, FF_CHUNK), c4),
            pl.BlockSpec((nc, 2, 1, FF_CHUNK), c4),
            pl.BlockSpec((nc, FF_CHUNK, D), c3),
        ],
        out_specs=pl.BlockSpec((None, tm, D), lambda b, i: (b, i, 0)),
        out_shape=jax.ShapeDtypeStruct((B, S, D), F32),
        scratch_shapes=[pltpu.VMEM((tm, D), F32), pltpu.VMEM((HALO + tm, D), BF16),
                        pltpu.VMEM((2, 2, HALO + tm, FF_CHUNK), F32)],
        compiler_params=_cparams(("parallel", "parallel")),
        name="conv_ffn",
    )(x, x, g.reshape(1, D), wup[0], wup[1], cw, cb, wd)


def _rope_tables(S, period, off):
    inv = 1.0 / (ROPE_THETA ** (jnp.arange(0, ROPE_DIM, 2, dtype=F32) / ROPE_DIM))
    ang = jnp.arange(S, dtype=F32)[:, None] * inv[None, :]
    cos, sin = jnp.cos(ang), jnp.sin(ang)
    one = jnp.ones((S, 1), F32)
    zero = jnp.zeros((S, 1), F32)

    def lay(x1, x2, fill):
        group = jnp.concatenate([jnp.tile(fill, (1, off)), x1, x2,
                                 jnp.tile(fill, (1, period - off - ROPE_DIM))], axis=1)
        return jnp.tile(group, (1, LANES // period))

    return lay(cos, cos, one), lay(-sin, zero * sin, zero), lay(zero * sin, sin, zero)


def _t5_bucket(rel):
    nb = T5_BUCKETS // 2
    max_exact = nb // 2
    n = jnp.abs(rel)
    large = max_exact + (jnp.log(jnp.maximum(n, 1).astype(F32) / max_exact)
                         / math.log(T5_MAX_DIST / max_exact) * (nb - max_exact)).astype(jnp.int32)
    large = jnp.minimum(large, nb - 1)
    return jnp.where(rel > 0, nb, 0) + jnp.where(n < max_exact, n, large)


def _dsa_bias_tiles(t5_bias):
    kj = jnp.arange(LANES, dtype=jnp.int32)[:, None]
    qi = jnp.arange(LANES, dtype=jnp.int32)[None, :]
    far = t5_bias[_t5_bucket(jnp.int32(-4 * T5_MAX_DIST))]

    def tile(rel):
        pick = jax.nn.one_hot(_t5_bucket(rel), T5_BUCKETS, dtype=F32)
        vals = jnp.einsum("kqb,bh->hkq", pick, t5_bias, precision=lax.Precision.HIGHEST)
        return (vals - far[:, None, None]) * LOG2E

    same, prev = tile(kj - qi), tile(kj - LANES - qi)
    return jnp.stack([jnp.zeros_like(same), prev, same], axis=1).reshape(N_PAIRS, 2, 3, LANES, LANES)


def _softmax_shift(gq, gk, dim, bias_hi=0.0, bias_lo=0.0):
    qk = 1.02 * dim * jnp.max(jnp.abs(gq)) * jnp.max(jnp.abs(gk))
    top = qk + bias_hi
    fixed = (top + qk - bias_lo) <= SAFE_SPREAD
    return fixed, jnp.where(fixed, -top, 0.0)


def _toeplitz_kernel(v_ref, o_ref, *, cols):
    rows, width = o_ref.shape[0], v_ref.shape[1]
    x = jnp.broadcast_to(v_ref[...], (rows, width))
    o_ref[...] = pltpu.roll(x, 0, 1, stride=1, stride_axis=0)[:, :cols]


def _toeplitz(v, rows, cols):
    H, L = v.shape
    return pl.pallas_call(
        functools.partial(_toeplitz_kernel, cols=cols),
        grid=(H,),
        in_specs=[pl.BlockSpec((None, 1, L), lambda h: (h, 0, 0))],
        out_specs=pl.BlockSpec((None, rows, cols), lambda h: (h, 0, 0)),
        out_shape=jax.ShapeDtypeStruct((H, rows, cols), F32),
        compiler_params=_cparams(("parallel",)),
        name="toeplitz",
    )(v.reshape(H, 1, L))


def _band_bias(rel_bias, shift, tq):
    pad = LEFT_CHUNKS * CHUNK
    nblk = -(-pad // tq) + 1
    lead = (nblk - 1) * tq
    nk = nblk * tq
    kk = jnp.arange(nk, dtype=jnp.int32)[:, None] - lead
    i = jnp.arange(tq, dtype=jnp.int32)[None, :]
    first = (i // CHUNK) * CHUNK - pad
    ok = (kk >= first) & (kk < (i // CHUNK + 1) * CHUNK)
    L = nk + tq
    dist = jnp.arange(L, dtype=jnp.int32)
    dist = lead + jnp.where(dist < tq, dist, dist - L)
    v = rel_bias[:, jnp.clip(dist, -(CHUNK - 1), REL_MAX_PAST) + CHUNK - 1]
    b = _toeplitz(v, nk, tq)
    b = jnp.where(ok[None], b + shift, NEG).reshape(N_PAIRS, 2, nblk, tq, tq)
    return jnp.transpose(b, (0, 2, 3, 1, 4)).reshape(N_PAIRS, nblk, tq, 2 * tq)


def kernel(x, norm_mix, norm_ffn, t5_bias, a_w_in, a_q_norm, a_k_norm, a_kidx_norm, a_w_out, b_w_down, b_q_a_norm, b_kv_a_norm, b_w_uq, b_w_ukv, b_q_norm, b_k_norm, b_w_out, c_w_in, c_q_norm, c_k_norm, c_rel_bias, c_w_out, f_w_up, f_conv_w, f_conv_b, f_w_down):
    B, S, D = x.shape
    depth = norm_mix.shape[0]

    tq_c = 256
    tabs_idx = _rope_tables(S, D_IDX, D_IDX - ROPE_DIM)
    tabs_mla = _rope_tables(S, LANES, QK_NOPE)
    scale = HEAD_DIM ** -0.5 * LOG2E
    for layer in range(depth):
        kind, j = layer % N_MIXERS, layer // N_MIXERS
        g = norm_mix[layer]
        if kind == 0:
            gq = a_q_norm[j] * scale
            q, k, vt = _qkv_proj(x, g, a_w_in[j][:, :A_QKV], gq, a_k_norm[j])
            qp, kp, wi = _idx_proj(x, g, a_w_in[j][:, A_QKV:], a_kidx_norm[j], tabs_idx)
            tiles = _dsa_bias_tiles(t5_bias)
            fixed, shift = _softmax_shift(gq, a_k_norm[j], HEAD_DIM, jnp.max(tiles), jnp.min(tiles))
            att = lax.cond(fixed, functools.partial(_dsa_attn, fixed=True), functools.partial(_dsa_attn, fixed=False),
                           q, k, vt, qp, kp, wi, shift, tiles)
            w_out = a_w_out[j]
        elif kind == 1:
            fixed, shift = _softmax_shift(b_q_norm[j] * (B_QK ** -0.5 * LOG2E), b_k_norm[j], B_QK)
            q, k, vt = _mla_proj(x, g, b_w_down[j], b_q_a_norm[j], b_kv_a_norm[j], b_w_uq[j], b_w_ukv[j],
                                 b_q_norm[j], b_k_norm[j], shift, tabs_mla)
            att = lax.cond(fixed, functools.partial(_mla_attn, fixed=True), functools.partial(_mla_attn, fixed=False),
                           q, k, vt)
            w_out = b_w_out[j]
        else:
            gq = c_q_norm[j] * scale
            q, k, vt = _qkv_proj(x, g, c_w_in[j], gq, c_k_norm[j])
            rel = c_rel_bias[j] * LOG2E
            fixed, shift = _softmax_shift(gq, c_k_norm[j], HEAD_DIM, jnp.max(rel), jnp.min(rel))
            att = lax.cond(fixed, functools.partial(_band_attn, fixed=True, tq=tq_c),
                           functools.partial(_band_attn, fixed=False, tq=tq_c),
                           q, k, vt, _band_bias(rel, shift, tq_c))
            w_out = c_w_out[j]
        x = _out_proj(x, att, w_out)
        x = _ffn(x, norm_ffn[layer], f_w_up[layer], f_conv_w[layer], f_conv_b[layer], f_w_down[layer])
    return x
```

```python
import functools
import math

import numpy as np
import jax
import jax.numpy as jnp
from jax import lax
from jax.experimental import pallas as pl
from jax.experimental.pallas import tpu as pltpu

F32 = jnp.float32
BF16 = jnp.bfloat16

D_MODEL = 1024
CHUNK = 64
EPS = 1e-6
ROPE_THETA = 10000.0
NEG = -1e30
NEG_BITS = int(np.float32(NEG).view(np.int32))
M_INIT = -1e38
LOG2E = math.log2(math.e)
SAFE_SPREAD = 100.0

N_MIXERS = 3
N_HEADS = 16
HEAD_DIM = 64
N_PAIRS = N_HEADS // 2
LANES = 128
VT_BLK = 256

H_IDX = 8
D_IDX = 64
ROPE_DIM = 32
TOPK_MAX = 256
T5_BUCKETS = 32
T5_MAX_DIST = 128
A_QKV = 3 * N_HEADS * HEAD_DIM
Q_LORA = 256
KV_LORA = 128
QK_NOPE = 64
QK_ROPE = 32
V_DIM = 64
B_QK = QK_NOPE + QK_ROPE
LEFT_CHUNKS = 8
REL_MAX_PAST = 128
D_FF = 2816
CONV_W = 3
FF_CHUNK = 256
HALO = 8

VMEM_LIMIT = 56 * 1024 * 1024
VMEM_LIMIT_DSA = 60 * 1024 * 1024


def _cparams(sem, vmem_limit=VMEM_LIMIT):
    return pltpu.CompilerParams(dimension_semantics=sem, vmem_limit_bytes=vmem_limit)


def _rms(x, g):
    return x * lax.rsqrt(jnp.mean(x * x, axis=-1, keepdims=True) + EPS) * g


def _dot(a, b):
    return jnp.dot(a, b, preferred_element_type=F32)


def _split_bf16(x):
    hi = x.astype(BF16)
    lo = (x - hi.astype(F32)).astype(BF16)
    return hi, lo


def _group_sumsq(y, bd):
    hi, lo = _split_bf16(y * y)
    return _dot(hi, bd) + _dot(lo, bd)


def _rope(y, ct, s1, s2):
    return y * ct + pltpu.roll(y, LANES - 16, 1) * s1 + pltpu.roll(y, 16, 1) * s2


def _rope_wide(y, ct, s1, s2):
    w = y.shape[-1]
    parts = [_rope(y[:, c:c + LANES], ct, s1, s2) for c in range(0, w, LANES)]
    return parts[0] if len(parts) == 1 else jnp.concatenate(parts, axis=-1)


def _store_vt(vt_ref, p, y):
    yt = y.T.astype(BF16)
    for r in range(y.shape[0] // VT_BLK):
        vt_ref[p, r] = yt[:, r * VT_BLK:(r + 1) * VT_BLK]


def _qkv_proj_kernel(x_ref, g_ref, w_ref, gq_ref, gk_ref, bd_ref, q_ref, k_ref, vt_ref, xb_ref, y_ref):
    xb_ref[...] = _rms(x_ref[...], g_ref[...]).astype(BF16)
    bd = bd_ref[...]
    per = N_HEADS * HEAD_DIM // 256
    outs = ((q_ref, gq_ref), (k_ref, gk_ref), (vt_ref, None))

    def project(n):
        y_ref[n % 2] = _dot(xb_ref[...], w_ref[:, n * 256:(n + 1) * 256])

    def finish(n):
        (o_ref, gain_ref), c = outs[n // per], n % per
        y = y_ref[n % 2]
        if gain_ref is None:
            _store_vt(o_ref, 2 * c, y[:, :LANES])
            _store_vt(o_ref, 2 * c + 1, y[:, LANES:])
        else:
            y = y * lax.rsqrt(_group_sumsq(y, bd) * (1.0 / HEAD_DIM) + EPS) * gain_ref[...]
            yb = y.astype(BF16)
            o_ref[2 * c] = yb[:, :LANES]
            o_ref[2 * c + 1] = yb[:, LANES:]

    project(0)
    for n in range(3 * per):
        if n + 1 < 3 * per:
            project(n + 1)
        finish(n)


def _vt_spec(tm):
    return pl.BlockSpec((None, N_PAIRS, tm // VT_BLK, LANES, VT_BLK), lambda b, i: (b, 0, i, 0, 0))


def _vt_shape(B, S):
    return jax.ShapeDtypeStruct((B, N_PAIRS, S // VT_BLK, LANES, VT_BLK), BF16)


def _qkv_proj(x, g, w, gq, gk, tm=512):
    B, S, D = x.shape
    hd = N_HEADS * HEAD_DIM
    bd = jnp.kron(jnp.eye(256 // HEAD_DIM, dtype=F32), jnp.ones((HEAD_DIM, HEAD_DIM), F32)).astype(BF16)
    out = jax.ShapeDtypeStruct((B, N_PAIRS, S, LANES), BF16)
    const = lambda b, i: (0, 0)
    ospec = pl.BlockSpec((None, N_PAIRS, tm, LANES), lambda b, i: (b, 0, i, 0))
    return pl.pallas_call(
        _qkv_proj_kernel,
        grid=(B, S // tm),
        in_specs=[
            pl.BlockSpec((None, tm, D), lambda b, i: (b, i, 0)),
            pl.BlockSpec((1, D), const),
            pl.BlockSpec((D, 3 * hd), const),
            pl.BlockSpec((1, 256), const),
            pl.BlockSpec((1, 256), const),
            pl.BlockSpec((256, 256), const),
        ],
        out_specs=[ospec, ospec, _vt_spec(tm)],
        out_shape=[out, out, _vt_shape(B, S)],
        scratch_shapes=[pltpu.VMEM((tm, D), BF16), pltpu.VMEM((2, tm, 256), F32)],
        compiler_params=_cparams(("parallel", "parallel")),
        name="qkv_proj",
    )(x, g.reshape(1, D), w.astype(BF16), jnp.tile(gq, 4).reshape(1, 256), jnp.tile(gk, 4).reshape(1, 256), bd)


def _idx_proj_kernel(x_ref, g_ref, wh_ref, wl_ref, gki_ref, ct_ref, s1_ref, s2_ref, qp_ref, kp_ref, wi_ref,
                     *, w_scale):
    xn = _rms(x_ref[...], g_ref[...])
    xh, xl = _split_bf16(xn)

    def mm(c0, c1):
        wh = wh_ref[:, c0:c1]
        return _dot(xh, wh) + (_dot(xl, wh) + _dot(xh, wl_ref[:, c0:c1]))

    ct, s1, s2 = ct_ref[...], s1_ref[...], s2_ref[...]
    tm = xn.shape[0]
    low = lax.broadcasted_iota(jnp.int32, (tm, LANES), 1) < D_IDX
    zero = jnp.zeros((tm, LANES), F32)

    for c in range(H_IDX // 2):
        y = _rope(mm(c * LANES, (c + 1) * LANES), ct, s1, s2)
        hi = y.astype(BF16).astype(F32)
        lo = y - hi
        rhi = pltpu.roll(hi, D_IDX, 1)
        rlo = pltpu.roll(lo, D_IDX, 1)
        qp_ref[2 * c, :LANES, :] = jnp.where(low, hi, rlo).T.astype(BF16)
        qp_ref[2 * c, LANES:, :] = jnp.where(low, hi, zero).T.astype(BF16)
        qp_ref[2 * c + 1, :LANES, :] = jnp.where(low, rhi, lo).T.astype(BF16)
        qp_ref[2 * c + 1, LANES:, :] = jnp.where(low, rhi, zero).T.astype(BF16)

    c0 = H_IDX * D_IDX
    y = mm(c0, c0 + LANES)
    y = y * lax.rsqrt(jnp.sum(y * y, axis=-1, keepdims=True) * (1.0 / D_IDX) + EPS) * gki_ref[...]
    y = _rope(y, ct, s1, s2)
    hi = y.astype(BF16).astype(F32)
    lo = y - hi
    kp_ref[:, :LANES] = jnp.where(low, hi, pltpu.roll(hi, D_IDX, 1)).astype(BF16)
    kp_ref[:, LANES:] = jnp.where(low, lo, zero).astype(BF16)

    wi_ref[...] = (mm(c0 + LANES, c0 + 2 * LANES) * w_scale).T[:H_IDX]


def _idx_proj(x, g, w_idx, gki, tabs, tm=512):
    B, S, D = x.shape
    nq = H_IDX * D_IDX
    wq, wk, ww = w_idx[:, :nq], w_idx[:, nq:nq + D_IDX], w_idx[:, nq + D_IDX:]
    wpad = jnp.concatenate([
        wq, jnp.pad(wk, ((0, 0), (0, LANES - D_IDX))), jnp.pad(ww, ((0, 0), (0, LANES - H_IDX)))], axis=1)
    wh = wpad.astype(BF16)
    wl = (wpad - wh.astype(F32)).astype(BF16)
    ncol = nq + 2 * LANES
    gpad = jnp.pad(gki, (0, LANES - D_IDX)).reshape(1, LANES)
    const = lambda b, i: (0, 0)
    tspec = pl.BlockSpec((tm, LANES), lambda b, i: (i, 0))
    return pl.pallas_call(
        functools.partial(_idx_proj_kernel, w_scale=(D_IDX ** -0.5) * (H_IDX ** -0.5)),
        grid=(B, S // tm),
        in_specs=[
            pl.BlockSpec((None, tm, D), lambda b, i: (b, i, 0)),
            pl.BlockSpec((1, D), const),
            pl.BlockSpec((D, ncol), const),
            pl.BlockSpec((D, ncol), const),
            pl.BlockSpec((1, LANES), const),
            tspec, tspec, tspec,
        ],
        out_specs=[
            pl.BlockSpec((None, H_IDX, 256, tm), lambda b, i: (b, 0, 0, i)),
            pl.BlockSpec((None, tm, 256), lambda b, i: (b, i, 0)),
            pl.BlockSpec((None, H_IDX, tm), lambda b, i: (b, 0, i)),
        ],
        out_shape=[
            jax.ShapeDtypeStruct((B, H_IDX, 256, S), BF16),
            jax.ShapeDtypeStruct((B, S, 256), BF16),
            jax.ShapeDtypeStruct((B, H_IDX, S), F32),
        ],
        compiler_params=_cparams(("parallel", "parallel")),
        name="idx_proj",
    )(x, g.reshape(1, D), wh, wl, gpad, *tabs)


def _mla_proj_kernel(x_ref, g_ref, wd_ref, gqa_ref, gkva_ref, wuq_ref, wuk_ref, wuv_ref, gq_ref, gk_ref, bd_ref,
                     qfix_ref, kfix_ref, ct_ref, s1_ref, s2_ref, q_ref, k_ref, vt_ref):
    xb = _rms(x_ref[...], g_ref[...]).astype(BF16)
    d = _dot(xb, wd_ref[...])
    cq = _rms(d[:, :Q_LORA], gqa_ref[...]).astype(BF16)
    ckv = _rms(d[:, Q_LORA:Q_LORA + KV_LORA], gkva_ref[...]).astype(BF16)
    krz = pltpu.roll(d[:, Q_LORA + KV_LORA:], QK_NOPE, 1)
    krz2 = jnp.concatenate([krz, krz], axis=-1)
    bd = bd_ref[...]
    ct, s1, s2 = ct_ref[...], s1_ref[...], s2_ref[...]
    inv = 1.0 / B_QK
    for p in range(N_PAIRS):
        cols = slice(p * 256, (p + 1) * 256)
        y = _dot(cq, wuq_ref[:, cols])
        y = y * lax.rsqrt(_group_sumsq(y, bd) * inv + EPS) * gq_ref[...]
        q_ref[p] = (_rope_wide(y, ct, s1, s2) + qfix_ref[...]).astype(BF16)
        y = _dot(ckv, wuk_ref[:, cols]) + krz2
        y = y * lax.rsqrt(_group_sumsq(y, bd) * inv + EPS) * gk_ref[...]
        k_ref[p] = (_rope_wide(y, ct, s1, s2) + kfix_ref[...]).astype(BF16)
        _store_vt(vt_ref, p, _dot(ckv, wuv_ref[:, p * LANES:(p + 1) * LANES]))


def _mla_proj(x, g, w_down, gqa, gkva, w_uq, w_ukv, gq, gk, shift, tabs, tm=512):
    B, S, D = x.shape
    nd = Q_LORA + KV_LORA + QK_ROPE
    wd = jnp.pad(w_down, ((0, 0), (0, 512 - nd))).astype(BF16)
    padh = LANES - B_QK
    wuq = jnp.pad(w_uq.reshape(Q_LORA, N_HEADS, B_QK), ((0, 0), (0, 0), (0, padh))).reshape(Q_LORA, N_HEADS * LANES)
    wukv = w_ukv.reshape(KV_LORA, N_HEADS, QK_NOPE + V_DIM)
    wuk = jnp.pad(wukv[:, :, :QK_NOPE], ((0, 0), (0, 0), (0, LANES - QK_NOPE))).reshape(KV_LORA, N_HEADS * LANES)
    wuv = wukv[:, :, QK_NOPE:].reshape(KV_LORA, N_HEADS * V_DIM)
    gq2 = jnp.tile(jnp.pad(gq * (B_QK ** -0.5 * LOG2E), (0, padh)), 2).reshape(1, 256)
    gk2 = jnp.tile(jnp.pad(gk, (0, padh)), 2).reshape(1, 256)
    bd = jnp.kron(jnp.eye(2, dtype=F32), jnp.ones((LANES, LANES), F32)).astype(BF16)
    kfix = jnp.asarray((np.arange(256) % LANES == B_QK).astype(np.float32)).reshape(1, 256)
    qfix = kfix * shift
    const = lambda b, i: (0, 0)
    tspec = pl.BlockSpec((tm, LANES), lambda b, i: (i, 0))
    qk_shape = jax.ShapeDtypeStruct((B, N_PAIRS, S, 256), BF16)
    qk_spec = pl.BlockSpec((None, N_PAIRS, tm, 256), lambda b, i: (b, 0, i, 0))
    return pl.pallas_call(
        _mla_proj_kernel,
        grid=(B, S // tm),
        in_specs=[
            pl.BlockSpec((None, tm, D), lambda b, i: (b, i, 0)),
            pl.BlockSpec((1, D), const),
            pl.BlockSpec((D, 512), const),
            pl.BlockSpec((1, Q_LORA), const),
            pl.BlockSpec((1, KV_LORA), const),
            pl.BlockSpec((Q_LORA, N_HEADS * LANES), const),
            pl.BlockSpec((KV_LORA, N_HEADS * LANES), const),
            pl.BlockSpec((KV_LORA, N_HEADS * V_DIM), const),
            pl.BlockSpec((1, 256), const),
            pl.BlockSpec((1, 256), const),
            pl.BlockSpec((256, 256), const),
            pl.BlockSpec((1, 256), const),
            pl.BlockSpec((1, 256), const),
            tspec, tspec, tspec,
        ],
        out_specs=[qk_spec, qk_spec, _vt_spec(tm)],
        out_shape=[qk_shape, qk_shape, _vt_shape(B, S)],
        compiler_params=_cparams(("parallel", "parallel")),
        name="mla_proj",
    )(x, g.reshape(1, D), wd, gqa.reshape(1, Q_LORA), gkva.reshape(1, KV_LORA), wuq.astype(BF16),
      wuk.astype(BF16), wuv.astype(BF16), gq2, gk2, bd, qfix, kfix, *tabs)


def _stack_heads(q2, half):
    qf = q2.astype(F32)
    low = lax.broadcasted_iota(jnp.int32, qf.shape, 1) < half
    zero = jnp.zeros_like(qf)
    return jnp.concatenate([jnp.where(low, qf, zero).T, jnp.where(low, zero, qf).T], axis=1).astype(BF16)


def _flash_init(acc_ref):
    r = acc_ref.shape[1]
    acc_ref[...] = jnp.zeros(acc_ref.shape, F32)
    return jnp.full((1, r), M_INIT, F32), jnp.zeros((1, r), F32)


def _consume(s, vt, m, l, acc_ref, fixed):
    if fixed:
        p = jnp.exp2(s)
        acc_ref[...] += _dot(vt, p.astype(BF16))
        return m, l + jnp.sum(p, axis=0, keepdims=True)
    m_new = jnp.maximum(m, jnp.max(s, axis=0, keepdims=True))
    alpha = jnp.exp2(m - m_new)
    p = jnp.exp2(s - m_new)
    l_new = alpha * l + jnp.sum(p, axis=0, keepdims=True)
    acc_ref[...] = alpha * acc_ref[...] + _dot(vt, p.astype(BF16))
    return m_new, l_new


def _sweep(first, n, s_ref, scores, values, ml, acc_ref, fixed, peel=True, consume=None, per_trip=2):
    def one(j, s, ml):
        if consume is not None:
            return consume(j, s, ml)
        return _consume(s, values(j), *ml, acc_ref, fixed)

    if isinstance(n, int):
        tiles = [scores(first + j) for j in range(n)]
        for j in range(n):
            ml = one(first + j, tiles[j], ml)
        return ml

    odd = n % 2 if peel else 0
    if peel:
        ml = lax.cond(odd == 1, lambda ml: one(first, scores(first), ml), lambda ml: ml, ml)
    j0 = first + odd
    last = first + n - 1

    @pl.when(n >= 2)
    def _():
        s_ref[0] = scores(j0)

    assert per_trip % 2 == 0 and (per_trip == 2 or not peel)

    def body(t, ml):
        for r in range(0, per_trip, 2):
            j = j0 + per_trip * t + r
            s1 = scores(j + 1)
            ml = one(j, s_ref[0], ml)
            s_ref[1] = s1
            s0 = scores(jnp.minimum(j + 2, last))
            ml = one(j + 1, s_ref[1], ml)
            s_ref[0] = s0
        return ml

    return lax.fori_loop(0, n // per_trip, body, ml)


def _flash_out(l, acc_ref, tq):
    o = acc_ref[...] / l
    ot = jnp.concatenate([o[:HEAD_DIM, :tq], o[HEAD_DIM:, tq:]], axis=0)
    return ot.T.astype(BF16)


def _load_vt(vt_ref, kb, tk):
    n = tk // VT_BLK
    parts = [vt_ref[kb * n + r] for r in range(n)]
    return parts[0] if n == 1 else jnp.concatenate(parts, axis=1)


def _band_attn_kernel(q_ref, k_ref, vt_ref, bias_ref, o_ref, acc_ref, s_ref, *, tq, nblk, fixed):
    i = pl.program_id(2)
    qs = _stack_heads(q_ref[...], HEAD_DIM)
    lead = i - (nblk - 1)

    def scores(kb):
        ks = pl.multiple_of(kb * tq, tq)
        return _dot(k_ref[pl.ds(ks, tq), :], qs) + bias_ref[kb - lead]

    values = lambda kb: _load_vt(vt_ref, kb, tq)
    ml0 = _flash_init(acc_ref)
    _, l = lax.cond(
        lead >= 0,
        lambda ml: _sweep(lead, nblk, s_ref, scores, values, ml, acc_ref, fixed),
        lambda ml: _sweep(0, i + 1, s_ref, scores, values, ml, acc_ref, fixed),
        ml0)
    o_ref[...] = _flash_out(l, acc_ref, tq)


def _band_attn(q, k, vt, bias, fixed, tq=256):
    B, _, S, _ = q.shape
    nblk = bias.shape[1]
    return pl.pallas_call(
        functools.partial(_band_attn_kernel, tq=tq, nblk=nblk, fixed=fixed),
        grid=(B, N_PAIRS, S // tq),
        in_specs=[
            pl.BlockSpec((None, None, tq, LANES), lambda b, p, i: (b, p, i, 0)),
            pl.BlockSpec((None, None, S, LANES), lambda b, p, i: (b, p, 0, 0)),
            pl.BlockSpec((None, None, S // VT_BLK, LANES, VT_BLK), lambda b, p, i: (b, p, 0, 0, 0)),
            pl.BlockSpec((None, nblk, tq, 2 * tq), lambda b, p, i: (p, 0, 0, 0)),
        ],
        out_specs=pl.BlockSpec((None, tq, LANES), lambda b, p, i: (b, i, p)),
        out_shape=jax.ShapeDtypeStruct((B, S, N_HEADS * HEAD_DIM), BF16),
        scratch_shapes=[pltpu.VMEM((LANES, 2 * tq), F32), pltpu.VMEM((2, tq, 2 * tq), F32)],
        compiler_params=_cparams(("parallel", "parallel", "arbitrary")),
        name="band_attn",
    )(q, k, vt, bias)


def _mla_attn_kernel(q_ref, k_ref, vt_ref, o_ref, acc_ref, s_ref, *, tq, tk, fixed):
    i = pl.program_id(2)
    qs = _stack_heads(q_ref[...], LANES)
    t0 = i * tq
    assert tq % (4 * tk) == 0
    n_full = t0 // tk
    n_edge = tq // tk
    lim = ((t0 + lax.broadcasted_iota(jnp.int32, (1, tq), 1)) // CHUNK + 1) * CHUNK

    def scores(kb):
        ks = pl.multiple_of(kb * tk, tk)
        return _dot(k_ref[pl.ds(ks, tk), :], qs)

    def edge_scores(kb):
        row = kb * tk + lax.broadcasted_iota(jnp.int32, (tk, tq), 0)
        add = jnp.where(row < lim, 0.0, NEG).astype(F32)
        return scores(kb) + jnp.concatenate([add, add], axis=1)

    values = lambda kb: _load_vt(vt_ref, kb, tk)
    ml = _sweep(0, n_full, s_ref, scores, values, _flash_init(acc_ref), acc_ref, fixed, peel=False, per_trip=4)
    if not fixed:
        _, l = _sweep(n_full, n_edge, s_ref, edge_scores, values, ml, acc_ref, fixed, peel=False)
    else:
        kr = lax.broadcasted_iota(jnp.int32, (tk, tk), 0)
        qc = lax.broadcasted_iota(jnp.int32, (tk, tk), 1)
        diag = jnp.where(kr < (qc // CHUNK + 1) * CHUNK, 0.0, NEG).astype(F32)
        l = ml[1]
        for e in range(n_edge):
            lo, w = e * tk, tq - e * tk
            ks = pl.multiple_of((n_full + e) * tk, tk)
            qe = jnp.concatenate([qs[:, lo:tq], qs[:, tq + lo:]], axis=1)
            head_add = diag if w == tk else jnp.concatenate([diag, jnp.zeros((tk, w - tk), F32)], axis=1)
            p = jnp.exp2(_dot(k_ref[pl.ds(ks, tk), :], qe) + jnp.concatenate([head_add, head_add], axis=1))
            pv = _dot(values(n_full + e), p.astype(BF16))
            acc_ref[:, lo:tq] += pv[:, :w]
            acc_ref[:, tq + lo:] += pv[:, w:]
            ps = jnp.sum(p, axis=0, keepdims=True)
            parts = [l[:, lo:tq] + ps[:, :w], l[:, tq + lo:] + ps[:, w:]]
            if lo:
                parts = [l[:, :lo], parts[0], l[:, tq:tq + lo], parts[1]]
            l = jnp.concatenate(parts, axis=1)
    o_ref[...] = _flash_out(l, acc_ref, tq)


def _mla_attn(q, k, vt, fixed, tq=1024, tk=256):
    B, _, S, _ = q.shape
    tk = min(tk, S)
    return pl.pallas_call(
        functools.partial(_mla_attn_kernel, tq=tq, tk=tk, fixed=fixed),
        grid=(B, N_PAIRS, S // tq),
        in_specs=[
            pl.BlockSpec((None, None, tq, 256), lambda b, p, i: (b, p, i, 0)),
            pl.BlockSpec((None, None, S, 256), lambda b, p, i: (b, p, 0, 0)),
            pl.BlockSpec((None, None, S // VT_BLK, LANES, VT_BLK), lambda b, p, i: (b, p, 0, 0, 0)),
        ],
        out_specs=pl.BlockSpec((None, tq, LANES), lambda b, p, i: (b, i, p)),
        out_shape=jax.ShapeDtypeStruct((B, S, N_HEADS * V_DIM), BF16),
        scratch_shapes=[pltpu.VMEM((LANES, 2 * tq), F32), pltpu.VMEM((2, tk, 2 * tq), F32)],
        compiler_params=_cparams(("parallel", "parallel", "arbitrary")),
        name="mla_attn",
    )(q, k, vt)


def _dsa_attn_kernel(q_ref, qp_ref, wi_ref, shift_ref, bias_ref, k_ref, vt_ref, kp_ref, o_ref,
                     sc_ref, h_ref, acc_ref, stage_ref, s_ref, qs_ref, m_ref, l_ref, *, tq, tk, topk, fixed):
    i = pl.program_id(1)
    t0 = i * tq
    n_blk = (t0 + tq + tk - 1) // tk
    n_far = jnp.maximum((t0 // LANES - 1) // (tk // LANES), 0)
    n_far_even = n_far - n_far % 2
    lim = ((t0 + lax.broadcasted_iota(jnp.int32, (1, tq), 1)) // CHUNK + 1) * CHUNK

    wv = wi_ref[...]

    def raw_scores(kb):
        ks = pl.multiple_of(kb * tk, tk)
        kp = kp_ref[pl.ds(ks, tk), :]
        acc = jnp.zeros((tk, tq), F32)
        for h in range(H_IDX):
            acc = acc + jnp.maximum(_dot(kp, qp_ref[h]), 0.0) * wv[h:h + 1, :]
        return acc

    def store_keys(kb, acc, carry):
        row = kb * tk + lax.broadcasted_iota(jnp.int32, (tk, tq), 0)
        acc = jnp.where(row < lim, acc, -jnp.inf)
        bits = pltpu.bitcast(acc, jnp.int32)
        key = bits ^ ((bits >> 31) & 0x7FFFFFFF)
        sc_ref[kb] = key
        h_ref[kb] = (key >> 16).astype(jnp.int16)
        return carry

    _sweep(0, n_blk, s_ref.at[:, :, pl.ds(0, tq)], raw_scores, None, 0, None, fixed, consume=store_keys)

    def count16(pred):
        def count_body(kb, part):
            hit = jnp.where(pred(h_ref[kb]), jnp.int16(1), jnp.int16(0))
            lanes = [hit[r:r + 16] for r in range(0, tk, 16)]
            while len(lanes) > 1:
                lanes = [a + b for a, b in zip(lanes[0::2], lanes[1::2])]
            return part + lanes[0].astype(jnp.int32)

        part = lax.fori_loop(0, n_blk, count_body, jnp.zeros((16, tq), jnp.int32))
        return jnp.sum(part, axis=0, keepdims=True)

    def kth16(need):
        def bit_body(t, ans):
            cand = ans + jnp.left_shift(jnp.int32(1), 15 - t)
            cand16 = cand.astype(jnp.int16)
            return jnp.where(count16(lambda h: h >= cand16) >= need, cand, ans)

        return lax.fori_loop(0, 16, bit_body, jnp.full((1, tq), -(2 ** 15), jnp.int32))

    top = kth16(topk)
    top16 = top.astype(jnp.int16)
    need = topk - count16(lambda h: h > top16)

    def low_body(kb, carry):
        low = ((sc_ref[kb] & 0xFFFF) - 2 ** 15).astype(jnp.int16)
        h_ref[kb] = jnp.where(h_ref[kb] == top16, low, jnp.int16(-(2 ** 15)))
        return carry

    lax.fori_loop(0, n_blk, low_body, 0)
    low = kth16(need)
    thr = jnp.left_shift(top, 16) + (low + 2 ** 15)

    low16 = low.astype(jnp.int16)
    ties = (need - count16(lambda h: h > low16)).astype(F32)
    keep_bits = pltpu.bitcast(shift_ref[...], jnp.int32)
    below = (lax.broadcasted_iota(jnp.int32, (tk, tk), 0) > lax.broadcasted_iota(jnp.int32, (tk, tk), 1))
    below = jnp.where(below, 1.0, 0.0).astype(BF16)

    def mask_body(kb, seen):
        key = sc_ref[kb]
        row = kb * tk + lax.broadcasted_iota(jnp.int32, (tk, tq), 0)
        tie = jnp.where(key == thr, 1.0, 0.0)
        before = seen + _dot(below, tie.astype(BF16))
        thr_row = thr + jnp.where(before >= ties, 1, 0)
        sc_ref[kb] = jnp.where((key >= thr_row) & (row < lim), keep_bits, NEG_BITS)
        return seen + jnp.sum(tie, axis=0, keepdims=True)

    lax.fori_loop(0, n_blk, mask_body, jnp.zeros((1, tq), F32))

    def prep(p, carry):
        qs_ref[p] = _stack_heads(q_ref[p], HEAD_DIM)
        m_ref[p] = jnp.full((1, 2 * tq), M_INIT, F32)
        l_ref[p] = jnp.zeros((1, 2 * tq), F32)
        acc_ref[p] = jnp.zeros((LANES, 2 * tq), F32)
        return carry

    lax.fori_loop(0, N_PAIRS, prep, 0)

    def scores(t):
        kb, p = t // N_PAIRS, t % N_PAIRS
        ks = pl.multiple_of(kb * tk, tk)
        am = pltpu.bitcast(sc_ref[kb], F32)
        cols = []
        for hh in range(2):
            for a in range(tq // LANES):
                tiles = []
                for c in range(tk // LANES):
                    d = (kb * (tk // LANES) + c) - (i * (tq // LANES) + a)
                    kind = jnp.where(d == 0, 2, jnp.where(d == -1, 1, 0))
                    tiles.append(bias_ref[p, hh, kind])
                cols.append(jnp.concatenate(tiles, axis=0))
        return (_dot(k_ref[p, pl.ds(ks, tk), :], qs_ref[p]) + jnp.concatenate([am, am], axis=1)
                + jnp.concatenate(cols, axis=1))

    def consume(t, s, carry):
        kb, p = t // N_PAIRS, t % N_PAIRS
        m, l = _consume(s, _load_vt(vt_ref.at[p], kb, tk), m_ref[p], l_ref[p], acc_ref.at[p], fixed)
        l_ref[p] = l
        if not fixed:
            m_ref[p] = m
        return carry

    _sweep(0, n_blk * N_PAIRS, s_ref, scores, None, 0, None, fixed, peel=False, consume=consume, per_trip=4)

    def finish(p, carry):
        stage_ref[p] = _flash_out(l_ref[p], acc_ref.at[p], tq)
        return carry

    lax.fori_loop(0, N_PAIRS, finish, 0)
    for p in range(N_PAIRS):
        o_ref[:, p * LANES:(p + 1) * LANES] = stage_ref[p]


def _dsa_attn(q, k, vt, qp, kp, wi, shift, bias, fixed, tq=256, tk=512):
    B, _, S, _ = q.shape
    topk = min(TOPK_MAX, S // 4)
    once = pl.Buffered(1)
    return pl.pallas_call(
        functools.partial(_dsa_attn_kernel, tq=tq, tk=tk, topk=topk, fixed=fixed),
        grid=(B, S // tq),
        in_specs=[
            pl.BlockSpec((None, N_PAIRS, tq, LANES), lambda b, i: (b, 0, i, 0)),
            pl.BlockSpec((None, H_IDX, 256, tq), lambda b, i: (b, 0, 0, i)),
            pl.BlockSpec((None, H_IDX, tq), lambda b, i: (b, 0, i)),
            pl.BlockSpec((1, tq), lambda b, i: (0, 0)),
            pl.BlockSpec((N_PAIRS, 2, 3, LANES, LANES), lambda b, i: (0, 0, 0, 0, 0), pipeline_mode=once),
            pl.BlockSpec((None, N_PAIRS, S, LANES), lambda b, i: (b, 0, 0, 0), pipeline_mode=once),
            pl.BlockSpec((None, N_PAIRS, S // VT_BLK, LANES, VT_BLK), lambda b, i: (b, 0, 0, 0, 0),
                         pipeline_mode=once),
            pl.BlockSpec((None, S, 256), lambda b, i: (b, 0, 0), pipeline_mode=once),
        ],
        out_specs=pl.BlockSpec((None, tq, N_HEADS * HEAD_DIM), lambda b, i: (b, i, 0)),
        out_shape=jax.ShapeDtypeStruct((B, S, N_HEADS * HEAD_DIM), BF16),
        scratch_shapes=[
            pltpu.VMEM((S // tk, tk, tq), jnp.int32),
            pltpu.VMEM((S // tk, tk, tq), jnp.int16),
            pltpu.VMEM((N_PAIRS, LANES, 2 * tq), F32),
            pltpu.VMEM((N_PAIRS, tq, LANES), BF16),
            pltpu.VMEM((2, tk, 2 * tq), F32),
            pltpu.VMEM((N_PAIRS, LANES, 2 * tq), BF16),
            pltpu.VMEM((N_PAIRS, 1, 2 * tq), F32),
            pltpu.VMEM((N_PAIRS, 1, 2 * tq), F32),
        ],
        compiler_params=_cparams(("parallel", "arbitrary"), VMEM_LIMIT_DSA),
        name="dsa_attn",
    )(q, qp, wi, jnp.full((1, tq), shift, F32), bias, k, vt, kp)


def _out_proj_kernel(x_ref, a_ref, w_ref, o_ref):
    o_ref[...] = x_ref[...] + _dot(a_ref[...], w_ref[...])


def _out_proj(x, a, w, tm=512):
    B, S, D = x.shape
    return pl.pallas_call(
        _out_proj_kernel,
        grid=(B, S // tm),
        in_specs=[
            pl.BlockSpec((None, tm, D), lambda b, i: (b, i, 0)),
            pl.BlockSpec((None, tm, a.shape[-1]), lambda b, i: (b, i, 0)),
            pl.BlockSpec(w.shape, lambda b, i: (0, 0)),
        ],
        out_specs=pl.BlockSpec((None, tm, D), lambda b, i: (b, i, 0)),
        out_shape=jax.ShapeDtypeStruct((B, S, D), F32),
        compiler_params=_cparams(("parallel", "parallel")),
        name="out_proj",
    )(x, a, w.astype(BF16))


def _ffn_kernel(x_ref, xp_ref, g_ref, wg_ref, wv_ref, cw_ref, cb_ref, wd_ref, o_ref, acc_ref, he_ref, u_ref,
                *, tm, n_chunks):
    i = pl.program_id(1)
    g = g_ref[...]
    he_ref[:HALO] = (_rms(xp_ref[...], g) * jnp.where(i > 0, 1.0, 0.0)).astype(BF16)
    he_ref[HALO:] = _rms(x_ref[...], g).astype(BF16)
    acc_ref[...] = jnp.zeros(acc_ref.shape, F32)

    def up(c, slot):
        he = he_ref[...]
        u_ref[slot, 0] = _dot(he, wg_ref[c])
        u_ref[slot, 1] = _dot(he, wv_ref[c])

    def conv(u, w, b):
        return (b + w[0:1] * u[HALO - 2:HALO - 2 + tm] + w[1:2] * u[HALO - 1:HALO - 1 + tm]
                + w[2:3] * u[HALO:HALO + tm])

    def down(c, slot):
        cw = cw_ref[c]
        cb = cb_ref[c]
        gate = conv(u_ref[slot, 0], cw[0], cb[0])
        val = conv(u_ref[slot, 1], cw[1], cb[1])
        act = gate * jax.nn.sigmoid(gate) * val
        acc_ref[...] += _dot(act.astype(BF16), wd_ref[c])

    up(0, 0)

    def pair_body(t, carry):
        c = 2 * t
        up(c + 1, 1)
        down(c, 0)
        up(c + 2, 0)
        down(c + 1, 1)
        return carry

    assert n_chunks % 2 == 1
    lax.fori_loop(0, n_chunks // 2, pair_body, 0)
    down(n_chunks - 1, 0)
    o_ref[...] = x_ref[...] + acc_ref[...]


def _ffn(x, g, w_up, conv_w, conv_b, w_down, tm=512):
    B, S, D = x.shape
    nc = D_FF // FF_CHUNK
    wup = w_up.astype(BF16).reshape(D, 2, nc, FF_CHUNK).transpose(1, 2, 0, 3)
    cw = conv_w.reshape(CONV_W, 2, nc, FF_CHUNK).transpose(2, 1, 0, 3)
    cb = conv_b.reshape(2, nc, 1, FF_CHUNK).transpose(1, 0, 2, 3)
    wd = w_down.astype(BF16).reshape(nc, FF_CHUNK, D)
    c3 = lambda b, i: (0, 0, 0)
    c4 = lambda b, i: (0, 0, 0, 0)
    return pl.pallas_call(
        functools.partial(_ffn_kernel, tm=tm, n_chunks=nc),
        grid=(B, S // tm),
        in_specs=[
            pl.BlockSpec((None, tm, D), lambda b, i: (b, i, 0)),
            pl.BlockSpec((None, HALO, D), lambda b, i: (b, jnp.maximum(i * (tm // HALO) - 1, 0), 0)),
            pl.BlockSpec((1, D), lambda b, i: (0, 0)),
            pl.BlockSpec((nc, D, FF_CHUNK), c3),
            pl.BlockSpec((nc, D, FF_CHUNK), c3),
            pl.BlockSpec((nc, 2, CONV_W, FF_CHUNK), c4),
            pl.BlockSpec((nc, 2, 1, FF_CHUNK), c4),
            pl.BlockSpec((nc, FF_CHUNK, D), c3),
        ],
        out_specs=pl.BlockSpec((None, tm, D), lambda b, i: (b, i, 0)),
        out_shape=jax.ShapeDtypeStruct((B, S, D), F32),
        scratch_shapes=[pltpu.VMEM((tm, D), F32), pltpu.VMEM((HALO + tm, D), BF16),
                        pltpu.VMEM((2, 2, HALO + tm, FF_CHUNK), F32)],
        compiler_params=_cparams(("parallel", "parallel")),
        name="conv_ffn",
    )(x, x, g.reshape(1, D), wup[0], wup[1], cw, cb, wd)


def _rope_tables(S, period, off):
    inv = 1.0 / (ROPE_THETA ** (jnp.arange(0, ROPE_DIM, 2, dtype=F32) / ROPE_DIM))
    ang = jnp.arange(S, dtype=F32)[:, None] * inv[None, :]
    cos, sin = jnp.cos(ang), jnp.sin(ang)
    one = jnp.ones((S, 1), F32)
    zero = jnp.zeros((S, 1), F32)

    def lay(x1, x2, fill):
        group = jnp.concatenate([jnp.tile(fill, (1, off)), x1, x2,
                                 jnp.tile(fill, (1, period - off - ROPE_DIM))], axis=1)
        return jnp.tile(group, (1, LANES // period))

    return lay(cos, cos, one), lay(-sin, zero * sin, zero), lay(zero * sin, sin, zero)


def _t5_bucket(rel):
    nb = T5_BUCKETS // 2
    max_exact = nb // 2
    n = jnp.abs(rel)
    large = max_exact + (jnp.log(jnp.maximum(n, 1).astype(F32) / max_exact)
                         / math.log(T5_MAX_DIST / max_exact) * (nb - max_exact)).astype(jnp.int32)
    large = jnp.minimum(large, nb - 1)
    return jnp.where(rel > 0, nb, 0) + jnp.where(n < max_exact, n, large)


def _dsa_bias_tiles(t5_bias):
    kj = jnp.arange(LANES, dtype=jnp.int32)[:, None]
    qi = jnp.arange(LANES, dtype=jnp.int32)[None, :]
    far = t5_bias[_t5_bucket(jnp.int32(-4 * T5_MAX_DIST))]

    def tile(rel):
        pick = jax.nn.one_hot(_t5_bucket(rel), T5_BUCKETS, dtype=F32)
        vals = jnp.einsum("kqb,bh->hkq", pick, t5_bias, precision=lax.Precision.HIGHEST)
        return (vals - far[:, None, None]) * LOG2E

    same, prev = tile(kj - qi), tile(kj - LANES - qi)
    return jnp.stack([jnp.zeros_like(same), prev, same], axis=1).reshape(N_PAIRS, 2, 3, LANES, LANES)


def _softmax_shift(gq, gk, dim, bias_hi=0.0, bias_lo=0.0):
    qk = 1.02 * dim * jnp.max(jnp.abs(gq)) * jnp.max(jnp.abs(gk))
    top = qk + bias_hi
    fixed = (top + qk - bias_lo) <= SAFE_SPREAD
    return fixed, jnp.where(fixed, -top, 0.0)


def _toeplitz_kernel(v_ref, o_ref, *, cols):
    rows, width = o_ref.shape[0], v_ref.shape[1]
    x = jnp.broadcast_to(v_ref[...], (rows, width))
    o_ref[...] = pltpu.roll(x, 0, 1, stride=1, stride_axis=0)[:, :cols]


def _toeplitz(v, rows, cols):
    H, L = v.shape
    return pl.pallas_call(
        functools.partial(_toeplitz_kernel, cols=cols),
        grid=(H,),
        in_specs=[pl.BlockSpec((None, 1, L), lambda h: (h, 0, 0))],
        out_specs=pl.BlockSpec((None, rows, cols), lambda h: (h, 0, 0)),
        out_shape=jax.ShapeDtypeStruct((H, rows, cols), F32),
        compiler_params=_cparams(("parallel",)),
        name="toeplitz",
    )(v.reshape(H, 1, L))


def _band_bias(rel_bias, shift, tq):
    pad = LEFT_CHUNKS * CHUNK
    nblk = -(-pad // tq) + 1
    lead = (nblk - 1) * tq
    nk = nblk * tq
    kk = jnp.arange(nk, dtype=jnp.int32)[:, None] - lead
    i = jnp.arange(tq, dtype=jnp.int32)[None, :]
    first = (i // CHUNK) * CHUNK - pad
    ok = (kk >= first) & (kk < (i // CHUNK + 1) * CHUNK)
    L = nk + tq
    dist = jnp.arange(L, dtype=jnp.int32)
    dist = lead + jnp.where(dist < tq, dist, dist - L)
    v = rel_bias[:, jnp.clip(dist, -(CHUNK - 1), REL_MAX_PAST) + CHUNK - 1]
    b = _toeplitz(v, nk, tq)
    b = jnp.where(ok[None], b + shift, NEG).reshape(N_PAIRS, 2, nblk, tq, tq)
    return jnp.transpose(b, (0, 2, 3, 1, 4)).reshape(N_PAIRS, nblk, tq, 2 * tq)


def kernel(x, norm_mix, norm_ffn, t5_bias, a_w_in, a_q_norm, a_k_norm, a_kidx_norm, a_w_out, b_w_down, b_q_a_norm, b_kv_a_norm, b_w_uq, b_w_ukv, b_q_norm, b_k_norm, b_w_out, c_w_in, c_q_norm, c_k_norm, c_rel_bias, c_w_out, f_w_up, f_conv_w, f_conv_b, f_w_down):
    B, S, D = x.shape
    depth = norm_mix.shape[0]

    tq_c = 256
    tabs_idx = _rope_tables(S, D_IDX, D_IDX - ROPE_DIM)
    tabs_mla = _rope_tables(S, LANES, QK_NOPE)
    scale = HEAD_DIM ** -0.5 * LOG2E
    for layer in range(depth):
        kind, j = layer % N_MIXERS, layer // N_MIXERS
        g = norm_mix[layer]
        if kind == 0:
            gq = a_q_norm[j] * scale
            q, k, vt = _qkv_proj(x, g, a_w_in[j][:, :A_QKV], gq, a_k_norm[j])
            qp, kp, wi = _idx_proj(x, g, a_w_in[j][:, A_QKV:], a_kidx_norm[j], tabs_idx)
            tiles = _dsa_bias_tiles(t5_bias)
            fixed, shift = _softmax_shift(gq, a_k_norm[j], HEAD_DIM, jnp.max(tiles), jnp.min(tiles))
            att = lax.cond(fixed, functools.partial(_dsa_attn, fixed=True), functools.partial(_dsa_attn, fixed=False),
                           q, k, vt, qp, kp, wi, shift, tiles)
            w_out = a_w_out[j]
        elif kind == 1:
            fixed, shift = _softmax_shift(b_q_norm[j] * (B_QK ** -0.5 * LOG2E), b_k_norm[j], B_QK)
            q, k, vt = _mla_proj(x, g, b_w_down[j], b_q_a_norm[j], b_kv_a_norm[j], b_w_uq[j], b_w_ukv[j],
                                 b_q_norm[j], b_k_norm[j], shift, tabs_mla)
            att = lax.cond(fixed, functools.partial(_mla_attn, fixed=True), functools.partial(_mla_attn, fixed=False),
                           q, k, vt)
            w_out = b_w_out[j]
        else:
            gq = c_q_norm[j] * scale
            q, k, vt = _qkv_proj(x, g, c_w_in[j], gq, c_k_norm[j])
            rel = c_rel_bias[j] * LOG2E
            fixed, shift = _softmax_shift(gq, c_k_norm[j], HEAD_DIM, jnp.max(rel), jnp.min(rel))
            att = lax.cond(fixed, functools.partial(_band_attn, fixed=True, tq=tq_c),
                           functools.partial(_band_attn, fixed=False, tq=tq_c),
                           q, k, vt, _band_bias(rel, shift, tq_c))
            w_out = c_w_out[j]
        x = _out_proj(x, att, w_out)
        x = _ffn(x, norm_ffn[layer], f_w_up[layer], f_conv_w[layer], f_conv_b[layer], f_w_down[layer])
    return x
```

```python
import functools
import math

import numpy as np
import jax
import jax.numpy as jnp
from jax import lax
from jax.experimental import pallas as pl
from jax.experimental.pallas import tpu as pltpu

F32 = jnp.float32
BF16 = jnp.bfloat16

D_MODEL = 1024
CHUNK = 64
EPS = 1e-6
ROPE_THETA = 10000.0
NEG = -1e30
NEG_BITS = int(np.float32(NEG).view(np.int32))
M_INIT = -1e38
LOG2E = math.log2(math.e)
SAFE_SPREAD = 100.0

N_MIXERS = 3
N_HEADS = 16
HEAD_DIM = 64
N_PAIRS = N_HEADS // 2
LANES = 128
VT_BLK = 256

H_IDX = 8
D_IDX = 64
ROPE_DIM = 32
TOPK_MAX = 256
T5_BUCKETS = 32
T5_MAX_DIST = 128
A_QKV = 3 * N_HEADS * HEAD_DIM
Q_LORA = 256
KV_LORA = 128
QK_NOPE = 64
QK_ROPE = 32
V_DIM = 64
B_QK = QK_NOPE + QK_ROPE
LEFT_CHUNKS = 8
REL_MAX_PAST = 128
D_FF = 2816
CONV_W = 3
FF_CHUNK = 256
HALO = 8

VMEM_LIMIT = 56 * 1024 * 1024
VMEM_LIMIT_DSA = 60 * 1024 * 1024


def _cparams(sem, vmem_limit=VMEM_LIMIT):
    return pltpu.CompilerParams(dimension_semantics=sem, vmem_limit_bytes=vmem_limit)


def _rms(x, g):
    return x * lax.rsqrt(jnp.mean(x * x, axis=-1, keepdims=True) + EPS) * g


def _dot(a, b):
    return jnp.dot(a, b, preferred_element_type=F32)


def _split_bf16(x):
    hi = x.astype(BF16)
    lo = (x - hi.astype(F32)).astype(BF16)
    return hi, lo


def _group_sumsq(y, bd):
    hi, lo = _split_bf16(y * y)
    return _dot(hi, bd) + _dot(lo, bd)


def _rope(y, ct, s1, s2):
    return y * ct + pltpu.roll(y, LANES - 16, 1) * s1 + pltpu.roll(y, 16, 1) * s2


def _rope_wide(y, ct, s1, s2):
    w = y.shape[-1]
    parts = [_rope(y[:, c:c + LANES], ct, s1, s2) for c in range(0, w, LANES)]
    return parts[0] if len(parts) == 1 else jnp.concatenate(parts, axis=-1)


def _store_vt(vt_ref, p, y):
    yt = y.T.astype(BF16)
    for r in range(y.shape[0] // VT_BLK):
        vt_ref[p, r] = yt[:, r * VT_BLK:(r + 1) * VT_BLK]


def _qkv_proj_kernel(x_ref, g_ref, w_ref, gq_ref, gk_ref, bd_ref, q_ref, k_ref, vt_ref, xb_ref, y_ref):
    xb_ref[...] = _rms(x_ref[...], g_ref[...]).astype(BF16)
    bd = bd_ref[...]
    per = N_HEADS * HEAD_DIM // 256
    outs = ((q_ref, gq_ref), (k_ref, gk_ref), (vt_ref, None))

    def project(n):
        y_ref[n % 2] = _dot(xb_ref[...], w_ref[:, n * 256:(n + 1) * 256])

    def finish(n):
        (o_ref, gain_ref), c = outs[n // per], n % per
        y = y_ref[n % 2]
        if gain_ref is None:
            _store_vt(o_ref, 2 * c, y[:, :LANES])
            _store_vt(o_ref, 2 * c + 1, y[:, LANES:])
        else:
            y = y * lax.rsqrt(_group_sumsq(y, bd) * (1.0 / HEAD_DIM) + EPS) * gain_ref[...]
            yb = y.astype(BF16)
            o_ref[2 * c] = yb[:, :LANES]
            o_ref[2 * c + 1] = yb[:, LANES:]

    project(0)
    for n in range(3 * per):
        if n + 1 < 3 * per:
            project(n + 1)
        finish(n)


def _vt_spec(tm):
    return pl.BlockSpec((None, N_PAIRS, tm // VT_BLK, LANES, VT_BLK), lambda b, i: (b, 0, i, 0, 0))


def _vt_shape(B, S):
    return jax.ShapeDtypeStruct((B, N_PAIRS, S // VT_BLK, LANES, VT_BLK), BF16)


def _qkv_proj(x, g, w, gq, gk, tm=512):
    B, S, D = x.shape
    hd = N_HEADS * HEAD_DIM
    bd = jnp.kron(jnp.eye(256 // HEAD_DIM, dtype=F32), jnp.ones((HEAD_DIM, HEAD_DIM), F32)).astype(BF16)
    out = jax.ShapeDtypeStruct((B, N_PAIRS, S, LANES), BF16)
    const = lambda b, i: (0, 0)
    ospec = pl.BlockSpec((None, N_PAIRS, tm, LANES), lambda b, i: (b, 0, i, 0))
    return pl.pallas_call(
        _qkv_proj_kernel,
        grid=(B, S // tm),
        in_specs=[
            pl.BlockSpec((None, tm, D), lambda b, i: (b, i, 0)),
            pl.BlockSpec((1, D), const),
            pl.BlockSpec((D, 3 * hd), const),
            pl.BlockSpec((1, 256), const),
            pl.BlockSpec((1, 256), const),
            pl.BlockSpec((256, 256), const),
        ],
        out_specs=[ospec, ospec, _vt_spec(tm)],
        out_shape=[out, out, _vt_shape(B, S)],
        scratch_shapes=[pltpu.VMEM((tm, D), BF16), pltpu.VMEM((2, tm, 256), F32)],
        compiler_params=_cparams(("parallel", "parallel")),
        name="qkv_proj",
    )(x, g.reshape(1, D), w.astype(BF16), jnp.tile(gq, 4).reshape(1, 256), jnp.tile(gk, 4).reshape(1, 256), bd)


def _idx_proj_kernel(x_ref, g_ref, wh_ref, wl_ref, gki_ref, ct_ref, s1_ref, s2_ref, qp_ref, kp_ref, wi_ref,
                     *, w_scale):
    xn = _rms(x_ref[...], g_ref[...])
    xh, xl = _split_bf16(xn)

    def mm(c0, c1):
        wh = wh_ref[:, c0:c1]
        return _dot(xh, wh) + (_dot(xl, wh) + _dot(xh, wl_ref[:, c0:c1]))

    ct, s1, s2 = ct_ref[...], s1_ref[...], s2_ref[...]
    tm = xn.shape[0]
    low = lax.broadcasted_iota(jnp.int32, (tm, LANES), 1) < D_IDX
    zero = jnp.zeros((tm, LANES), F32)

    for c in range(H_IDX // 2):
        y = _rope(mm(c * LANES, (c + 1) * LANES), ct, s1, s2)
        hi = y.astype(BF16).astype(F32)
        lo = y - hi
        rhi = pltpu.roll(hi, D_IDX, 1)
        rlo = pltpu.roll(lo, D_IDX, 1)
        qp_ref[2 * c, :LANES, :] = jnp.where(low, hi, rlo).T.astype(BF16)
        qp_ref[2 * c, LANES:, :] = jnp.where(low, hi, zero).T.astype(BF16)
        qp_ref[2 * c + 1, :LANES, :] = jnp.where(low, rhi, lo).T.astype(BF16)
        qp_ref[2 * c + 1, LANES:, :] = jnp.where(low, rhi, zero).T.astype(BF16)

    c0 = H_IDX * D_IDX
    y = mm(c0, c0 + LANES)
    y = y * lax.rsqrt(jnp.sum(y * y, axis=-1, keepdims=True) * (1.0 / D_IDX) + EPS) * gki_ref[...]
    y = _rope(y, ct, s1, s2)
    hi = y.astype(BF16).astype(F32)
    lo = y - hi
    kp_ref[:, :LANES] = jnp.where(low, hi, pltpu.roll(hi, D_IDX, 1)).astype(BF16)
    kp_ref[:, LANES:] = jnp.where(low, lo, zero).astype(BF16)

    wi_ref[...] = (mm(c0 + LANES, c0 + 2 * LANES) * w_scale).T[:H_IDX]


def _idx_proj(x, g, w_idx, gki, tabs, tm=512):
    B, S, D = x.shape
    nq = H_IDX * D_IDX
    wq, wk, ww = w_idx[:, :nq], w_idx[:, nq:nq + D_IDX], w_idx[:, nq + D_IDX:]
    wpad = jnp.concatenate([
        wq, jnp.pad(wk, ((0, 0), (0, LANES - D_IDX))), jnp.pad(ww, ((0, 0), (0, LANES - H_IDX)))], axis=1)
    wh = wpad.astype(BF16)
    wl = (wpad - wh.astype(F32)).astype(BF16)
    ncol = nq + 2 * LANES
    gpad = jnp.pad(gki, (0, LANES - D_IDX)).reshape(1, LANES)
    const = lambda b, i: (0, 0)
    tspec = pl.BlockSpec((tm, LANES), lambda b, i: (i, 0))
    return pl.pallas_call(
        functools.partial(_idx_proj_kernel, w_scale=(D_IDX ** -0.5) * (H_IDX ** -0.5)),
        grid=(B, S // tm),
        in_specs=[
            pl.BlockSpec((None, tm, D), lambda b, i: (b, i, 0)),
            pl.BlockSpec((1, D), const),
            pl.BlockSpec((D, ncol), const),
            pl.BlockSpec((D, ncol), const),
            pl.BlockSpec((1, LANES), const),
            tspec, tspec, tspec,
        ],
        out_specs=[
            pl.BlockSpec((None, H_IDX, 256, tm), lambda b, i: (b, 0, 0, i)),
            pl.BlockSpec((None, tm, 256), lambda b, i: (b, i, 0)),
            pl.BlockSpec((None, H_IDX, tm), lambda b, i: (b, 0, i)),
        ],
        out_shape=[
            jax.ShapeDtypeStruct((B, H_IDX, 256, S), BF16),
            jax.ShapeDtypeStruct((B, S, 256), BF16),
            jax.ShapeDtypeStruct((B, H_IDX, S), F32),
        ],
        compiler_params=_cparams(("parallel", "parallel")),
        name="idx_proj",
    )(x, g.reshape(1, D), wh, wl, gpad, *tabs)


def _mla_proj_kernel(x_ref, g_ref, wd_ref, gqa_ref, gkva_ref, wuq_ref, wuk_ref, wuv_ref, gq_ref, gk_ref, bd_ref,
                     qfix_ref, kfix_ref, ct_ref, s1_ref, s2_ref, q_ref, k_ref, vt_ref):
    xb = _rms(x_ref[...], g_ref[...]).astype(BF16)
    d = _dot(xb, wd_ref[...])
    cq = _rms(d[:, :Q_LORA], gqa_ref[...]).astype(BF16)
    ckv = _rms(d[:, Q_LORA:Q_LORA + KV_LORA], gkva_ref[...]).astype(BF16)
    krz = pltpu.roll(d[:, Q_LORA + KV_LORA:], QK_NOPE, 1)
    krz2 = jnp.concatenate([krz, krz], axis=-1)
    bd = bd_ref[...]
    ct, s1, s2 = ct_ref[...], s1_ref[...], s2_ref[...]
    inv = 1.0 / B_QK
    for p in range(N_PAIRS):
        cols = slice(p * 256, (p + 1) * 256)
        y = _dot(cq, wuq_ref[:, cols])
        y = y * lax.rsqrt(_group_sumsq(y, bd) * inv + EPS) * gq_ref[...]
        q_ref[p] = (_rope_wide(y, ct, s1, s2) + qfix_ref[...]).astype(BF16)
        y = _dot(ckv, wuk_ref[:, cols]) + krz2
        y = y * lax.rsqrt(_group_sumsq(y, bd) * inv + EPS) * gk_ref[...]
        k_ref[p] = (_rope_wide(y, ct, s1, s2) + kfix_ref[...]).astype(BF16)
        _store_vt(vt_ref, p, _dot(ckv, wuv_ref[:, p * LANES:(p + 1) * LANES]))


def _mla_proj(x, g, w_down, gqa, gkva, w_uq, w_ukv, gq, gk, shift, tabs, tm=512):
    B, S, D = x.shape
    nd = Q_LORA + KV_LORA + QK_ROPE
    wd = jnp.pad(w_down, ((0, 0), (0, 512 - nd))).astype(BF16)
    padh = LANES - B_QK
    wuq = jnp.pad(w_uq.reshape(Q_LORA, N_HEADS, B_QK), ((0, 0), (0, 0), (0, padh))).reshape(Q_LORA, N_HEADS * LANES)
    wukv = w_ukv.reshape(KV_LORA, N_HEADS, QK_NOPE + V_DIM)
    wuk = jnp.pad(wukv[:, :, :QK_NOPE], ((0, 0), (0, 0), (0, LANES - QK_NOPE))).reshape(KV_LORA, N_HEADS * LANES)
    wuv = wukv[:, :, QK_NOPE:].reshape(KV_LORA, N_HEADS * V_DIM)
    gq2 = jnp.tile(jnp.pad(gq * (B_QK ** -0.5 * LOG2E), (0, padh)), 2).reshape(1, 256)
    gk2 = jnp.tile(jnp.pad(gk, (0, padh)), 2).reshape(1, 256)
    bd = jnp.kron(jnp.eye(2, dtype=F32), jnp.ones((LANES, LANES), F32)).astype(BF16)
    kfix = jnp.asarray((np.arange(256) % LANES == B_QK).astype(np.float32)).reshape(1, 256)
    qfix = kfix * shift
    const = lambda b, i: (0, 0)
    tspec = pl.BlockSpec((tm, LANES), lambda b, i: (i, 0))
    qk_shape = jax.ShapeDtypeStruct((B, N_PAIRS, S, 256), BF16)
    qk_spec = pl.BlockSpec((None, N_PAIRS, tm, 256), lambda b, i: (b, 0, i, 0))
    return pl.pallas_call(
        _mla_proj_kernel,
        grid=(B, S // tm),
        in_specs=[
            pl.BlockSpec((None, tm, D), lambda b, i: (b, i, 0)),
            pl.BlockSpec((1, D), const),
            pl.BlockSpec((D, 512), const),
            pl.BlockSpec((1, Q_LORA), const),
            pl.BlockSpec((1, KV_LORA), const),
            pl.BlockSpec((Q_LORA, N_HEADS * LANES), const),
            pl.BlockSpec((KV_LORA, N_HEADS * LANES), const),
            pl.BlockSpec((KV_LORA, N_HEADS * V_DIM), const),
            pl.BlockSpec((1, 256), const),
            pl.BlockSpec((1, 256), const),
            pl.BlockSpec((256, 256), const),
            pl.BlockSpec((1, 256), const),
            pl.BlockSpec((1, 256), const),
            tspec, tspec, tspec,
        ],
        out_specs=[qk_spec, qk_spec, _vt_spec(tm)],
        out_shape=[qk_shape, qk_shape, _vt_shape(B, S)],
        compiler_params=_cparams(("parallel", "parallel")),
        name="mla_proj",
    )(x, g.reshape(1, D), wd, gqa.reshape(1, Q_LORA), gkva.reshape(1, KV_LORA), wuq.astype(BF16),
      wuk.astype(BF16), wuv.astype(BF16), gq2, gk2, bd, qfix, kfix, *tabs)


def _stack_heads(q2, half):
    qf = q2.astype(F32)
    low = lax.broadcasted_iota(jnp.int32, qf.shape, 1) < half
    zero = jnp.zeros_like(qf)
    return jnp.concatenate([jnp.where(low, qf, zero).T, jnp.where(low, zero, qf).T], axis=1).astype(BF16)


def _flash_init(acc_ref):
    r = acc_ref.shape[1]
    acc_ref[...] = jnp.zeros(acc_ref.shape, F32)
    return jnp.full((1, r), M_INIT, F32), jnp.zeros((1, r), F32)


def _consume(s, vt, m, l, acc_ref, fixed):
    if fixed:
        p = jnp.exp2(s)
        acc_ref[...] += _dot(vt, p.astype(BF16))
        return m, l + jnp.sum(p, axis=0, keepdims=True)
    m_new = jnp.maximum(m, jnp.max(s, axis=0, keepdims=True))
    alpha = jnp.exp2(m - m_new)
    p = jnp.exp2(s - m_new)
    l_new = alpha * l + jnp.sum(p, axis=0, keepdims=True)
    acc_ref[...] = alpha * acc_ref[...] + _dot(vt, p.astype(BF16))
    return m_new, l_new


def _sweep(first, n, s_ref, scores, values, ml, acc_ref, fixed, peel=True, consume=None, per_trip=2):
    def one(j, s, ml):
        if consume is not None:
            return consume(j, s, ml)
        return _consume(s, values(j), *ml, acc_ref, fixed)

    if isinstance(n, int):
        tiles = [scores(first + j) for j in range(n)]
        for j in range(n):
            ml = one(first + j, tiles[j], ml)
        return ml

    odd = n % 2 if peel else 0
    if peel:
        ml = lax.cond(odd == 1, lambda ml: one(first, scores(first), ml), lambda ml: ml, ml)
    j0 = first + odd
    last = first + n - 1

    @pl.when(n >= 2)
    def _():
        s_ref[0] = scores(j0)

    assert per_trip % 2 == 0 and (per_trip == 2 or not peel)

    def body(t, ml):
        for r in range(0, per_trip, 2):
            j = j0 + per_trip * t + r
            s1 = scores(j + 1)
            ml = one(j, s_ref[0], ml)
            s_ref[1] = s1
            s0 = scores(jnp.minimum(j + 2, last))
            ml = one(j + 1, s_ref[1], ml)
            s_ref[0] = s0
        return ml

    return lax.fori_loop(0, n // per_trip, body, ml)


def _flash_out(l, acc_ref, tq):
    o = acc_ref[...] / l
    ot = jnp.concatenate([o[:HEAD_DIM, :tq], o[HEAD_DIM:, tq:]], axis=0)
    return ot.T.astype(BF16)


def _load_vt(vt_ref, kb, tk):
    n = tk // VT_BLK
    parts = [vt_ref[kb * n + r] for r in range(n)]
    return parts[0] if n == 1 else jnp.concatenate(parts, axis=1)


def _band_attn_kernel(q_ref, k_ref, vt_ref, bias_ref, o_ref, acc_ref, s_ref, *, tq, nblk, fixed):
    i = pl.program_id(2)
    qs = _stack_heads(q_ref[...], HEAD_DIM)
    lead = i - (nblk - 1)

    def scores(kb):
        ks = pl.multiple_of(kb * tq, tq)
        return _dot(k_ref[pl.ds(ks, tq), :], qs) + bias_ref[kb - lead]

    values = lambda kb: _load_vt(vt_ref, kb, tq)
    ml0 = _flash_init(acc_ref)
    _, l = lax.cond(
        lead >= 0,
        lambda ml: _sweep(lead, nblk, s_ref, scores, values, ml, acc_ref, fixed),
        lambda ml: _sweep(0, i + 1, s_ref, scores, values, ml, acc_ref, fixed),
        ml0)
    o_ref[...] = _flash_out(l, acc_ref, tq)


def _band_attn(q, k, vt, bias, fixed, tq=256):
    B, _, S, _ = q.shape
    nblk = bias.shape[1]
    return pl.pallas_call(
        functools.partial(_band_attn_kernel, tq=tq, nblk=nblk, fixed=fixed),
        grid=(B, N_PAIRS, S // tq),
        in_specs=[
            pl.BlockSpec((None, None, tq, LANES), lambda b, p, i: (b, p, i, 0)),
            pl.BlockSpec((None, None, S, LANES), lambda b, p, i: (b, p, 0, 0)),
            pl.BlockSpec((None, None, S // VT_BLK, LANES, VT_BLK), lambda b, p, i: (b, p, 0, 0, 0)),
            pl.BlockSpec((None, nblk, tq, 2 * tq), lambda b, p, i: (p, 0, 0, 0)),
        ],
        out_specs=pl.BlockSpec((None, tq, LANES), lambda b, p, i: (b, i, p)),
        out_shape=jax.ShapeDtypeStruct((B, S, N_HEADS * HEAD_DIM), BF16),
        scratch_shapes=[pltpu.VMEM((LANES, 2 * tq), F32), pltpu.VMEM((2, tq, 2 * tq), F32)],
        compiler_params=_cparams(("parallel", "parallel", "arbitrary")),
        name="band_attn",
    )(q, k, vt, bias)


def _mla_attn_kernel(q_ref, k_ref, vt_ref, o_ref, acc_ref, s_ref, *, tq, tk, fixed):
    i = pl.program_id(2)
    qs = _stack_heads(q_ref[...], LANES)
    t0 = i * tq
    assert tq % (4 * tk) == 0
    n_full = t0 // tk
    n_edge = tq // tk
    lim = ((t0 + lax.broadcasted_iota(jnp.int32, (1, tq), 1)) // CHUNK + 1) * CHUNK

    def scores(kb):
        ks = pl.multiple_of(kb * tk, tk)
        return _dot(k_ref[pl.ds(ks, tk), :], qs)

    def edge_scores(kb):
        row = kb * tk + lax.broadcasted_iota(jnp.int32, (tk, tq), 0)
        add = jnp.where(row < lim, 0.0, NEG).astype(F32)
        return scores(kb) + jnp.concatenate([add, add], axis=1)

    values = lambda kb: _load_vt(vt_ref, kb, tk)
    ml = _sweep(0, n_full, s_ref, scores, values, _flash_init(acc_ref), acc_ref, fixed, peel=False, per_trip=4)
    if not fixed:
        _, l = _sweep(n_full, n_edge, s_ref, edge_scores, values, ml, acc_ref, fixed, peel=False)
    else:
        kr = lax.broadcasted_iota(jnp.int32, (tk, tk), 0)
        qc = lax.broadcasted_iota(jnp.int32, (tk, tk), 1)
        diag = jnp.where(kr < (qc // CHUNK + 1) * CHUNK, 0.0, NEG).astype(F32)
        l = ml[1]
        for e in range(n_edge):
            lo, w = e * tk, tq - e * tk
            ks = pl.multiple_of((n_full + e) * tk, tk)
            qe = jnp.concatenate([qs[:, lo:tq], qs[:, tq + lo:]], axis=1)
            head_add = diag if w == tk else jnp.concatenate([diag, jnp.zeros((tk, w - tk), F32)], axis=1)
            p = jnp.exp2(_dot(k_ref[pl.ds(ks, tk), :], qe) + jnp.concatenate([head_add, head_add], axis=1))
            pv = _dot(values(n_full + e), p.astype(BF16))
            acc_ref[:, lo:tq] += pv[:, :w]
            acc_ref[:, tq + lo:] += pv[:, w:]
            ps = jnp.sum(p, axis=0, keepdims=True)
            parts = [l[:, lo:tq] + ps[:, :w], l[:, tq + lo:] + ps[:, w:]]
            if lo:
                parts = [l[:, :lo], parts[0], l[:, tq:tq + lo], parts[1]]
            l = jnp.concatenate(parts, axis=1)
    o_ref[...] = _flash_out(l, acc_ref, tq)


def _mla_attn(q, k, vt, fixed, tq=1024, tk=256):
    B, _, S, _ = q.shape
    tk = min(tk, S)
    return pl.pallas_call(
        functools.partial(_mla_attn_kernel, tq=tq, tk=tk, fixed=fixed),
        grid=(B, N_PAIRS, S // tq),
        in_specs=[
            pl.BlockSpec((None, None, tq, 256), lambda b, p, i: (b, p, i, 0)),
            pl.BlockSpec((None, None, S, 256), lambda b, p, i: (b, p, 0, 0)),
            pl.BlockSpec((None, None, S // VT_BLK, LANES, VT_BLK), lambda b, p, i: (b, p, 0, 0, 0)),
        ],
        out_specs=pl.BlockSpec((None, tq, LANES), lambda b, p, i: (b, i, p)),
        out_shape=jax.ShapeDtypeStruct((B, S, N_HEADS * V_DIM), BF16),
        scratch_shapes=[pltpu.VMEM((LANES, 2 * tq), F32), pltpu.VMEM((2, tk, 2 * tq), F32)],
        compiler_params=_cparams(("parallel", "parallel", "arbitrary")),
        name="mla_attn",
    )(q, k, vt)


def _dsa_attn_kernel(q_ref, qp_ref, wi_ref, shift_ref, bias_ref, k_ref, vt_ref, kp_ref, o_ref,
                     sc_ref, h_ref, acc_ref, stage_ref, s_ref, qs_ref, m_ref, l_ref, *, tq, tk, topk, fixed):
    i = pl.program_id(1)
    t0 = i * tq
    n_blk = (t0 + tq + tk - 1) // tk
    n_far = jnp.maximum((t0 // LANES - 1) // (tk // LANES), 0)
    n_far_even = n_far - n_far % 2
    lim = ((t0 + lax.broadcasted_iota(jnp.int32, (1, tq), 1)) // CHUNK + 1) * CHUNK

    wv = wi_ref[...]

    def raw_scores(kb):
        ks = pl.multiple_of(kb * tk, tk)
        kp = kp_ref[pl.ds(ks, tk), :]
        acc = jnp.zeros((tk, tq), F32)
        for h in range(H_IDX):
            acc = acc + jnp.maximum(_dot(kp, qp_ref[h]), 0.0) * wv[h:h + 1, :]
        return acc

    def store_keys(kb, acc, carry):
        row = kb * tk + lax.broadcasted_iota(jnp.int32, (tk, tq), 0)
        acc = jnp.where(row < lim, acc, -jnp.inf)
        bits = pltpu.bitcast(acc, jnp.int32)
        key = bits ^ ((bits >> 31) & 0x7FFFFFFF)
        sc_ref[kb] = key
        h_ref[kb] = (key >> 16).astype(jnp.int16)
        return carry

    _sweep(0, n_blk, s_ref.at[:, :, pl.ds(0, tq)], raw_scores, None, 0, None, fixed, consume=store_keys)

    def count16(pred):
        def count_body(kb, part):
            hit = jnp.where(pred(h_ref[kb]), jnp.int16(1), jnp.int16(0))
            lanes = [hit[r:r + 16] for r in range(0, tk, 16)]
            while len(lanes) > 1:
                lanes = [a + b for a, b in zip(lanes[0::2], lanes[1::2])]
            return part + lanes[0].astype(jnp.int32)

        part = lax.fori_loop(0, n_blk, count_body, jnp.zeros((16, tq), jnp.int32))
        return jnp.sum(part, axis=0, keepdims=True)

    def kth16(need):
        def bit_body(t, ans):
            cand = ans + jnp.left_shift(jnp.int32(1), 15 - t)
            cand16 = cand.astype(jnp.int16)
            return jnp.where(count16(lambda h: h >= cand16) >= need, cand, ans)

        return lax.fori_loop(0, 16, bit_body, jnp.full((1, tq), -(2 ** 15), jnp.int32))

    top = kth16(topk)
    top16 = top.astype(jnp.int16)
    need = topk - count16(lambda h: h > top16)

    def low_body(kb, carry):
        low = ((sc_ref[kb] & 0xFFFF) - 2 ** 15).astype(jnp.int16)
        h_ref[kb] = jnp.where(h_ref[kb] == top16, low, jnp.int16(-(2 ** 15)))
        return carry

    lax.fori_loop(0, n_blk, low_body, 0)
    low = kth16(need)
    thr = jnp.left_shift(top, 16) + (low + 2 ** 15)

    low16 = low.astype(jnp.int16)
    ties = (need - count16(lambda h: h > low16)).astype(F32)
    keep_bits = pltpu.bitcast(shift_ref[...], jnp.int32)
    below = (lax.broadcasted_iota(jnp.int32, (tk, tk), 0) > lax.broadcasted_iota(jnp.int32, (tk, tk), 1))
    below = jnp.where(below, 1.0, 0.0).astype(BF16)

    def mask_body(kb, seen):
        key = sc_ref[kb]
        row = kb * tk + lax.broadcasted_iota(jnp.int32, (tk, tq), 0)
        tie = jnp.where(key == thr, 1.0, 0.0)
        before = seen + _dot(below, tie.astype(BF16))
        thr_row = thr + jnp.where(before >= ties, 1, 0)
        sc_ref[kb] = jnp.where((key >= thr_row) & (row < lim), keep_bits, NEG_BITS)
        return seen + jnp.sum(tie, axis=0, keepdims=True)

    lax.fori_loop(0, n_blk, mask_body, jnp.zeros((1, tq), F32))

    def prep(p, carry):
        qs_ref[p] = _stack_heads(q_ref[p], HEAD_DIM)
        m_ref[p] = jnp.full((1, 2 * tq), M_INIT, F32)
        l_ref[p] = jnp.zeros((1, 2 * tq), F32)
        acc_ref[p] = jnp.zeros((LANES, 2 * tq), F32)
        return carry

    lax.fori_loop(0, N_PAIRS, prep, 0)

    def scores(t):
        kb, p = t // N_PAIRS, t % N_PAIRS
        ks = pl.multiple_of(kb * tk, tk)
        am = pltpu.bitcast(sc_ref[kb], F32)
        cols = []
        for hh in range(2):
            for a in range(tq // LANES):
                tiles = []
                for c in range(tk // LANES):
                    d = (kb * (tk // LANES) + c) - (i * (tq // LANES) + a)
                    kind = jnp.where(d == 0, 2, jnp.where(d == -1, 1, 0))
                    tiles.append(bias_ref[p, hh, kind])
                cols.append(jnp.concatenate(tiles, axis=0))
        return (_dot(k_ref[p, pl.ds(ks, tk), :], qs_ref[p]) + jnp.concatenate([am, am], axis=1)
                + jnp.concatenate(cols, axis=1))

    def consume(t, s, carry):
        kb, p = t // N_PAIRS, t % N_PAIRS
        m, l = _consume(s, _load_vt(vt_ref.at[p], kb, tk), m_ref[p], l_ref[p], acc_ref.at[p], fixed)
        l_ref[p] = l
        if not fixed:
            m_ref[p] = m
        return carry

    _sweep(0, n_blk * N_PAIRS, s_ref, scores, None, 0, None, fixed, peel=False, consume=consume,
           per_trip=N_PAIRS)

    def finish(p, carry):
        stage_ref[p] = _flash_out(l_ref[p], acc_ref.at[p], tq)
        return carry

    lax.fori_loop(0, N_PAIRS, finish, 0)
    for p in range(N_PAIRS):
        o_ref[:, p * LANES:(p + 1) * LANES] = stage_ref[p]


def _dsa_attn(q, k, vt, qp, kp, wi, shift, bias, fixed, tq=256, tk=512):
    B, _, S, _ = q.shape
    topk = min(TOPK_MAX, S // 4)
    once = pl.Buffered(1)
    return pl.pallas_call(
        functools.partial(_dsa_attn_kernel, tq=tq, tk=tk, topk=topk, fixed=fixed),
        grid=(B, S // tq),
        in_specs=[
            pl.BlockSpec((None, N_PAIRS, tq, LANES), lambda b, i: (b, 0, i, 0)),
            pl.BlockSpec((None, H_IDX, 256, tq), lambda b, i: (b, 0, 0, i)),
            pl.BlockSpec((None, H_IDX, tq), lambda b, i: (b, 0, i)),
            pl.BlockSpec((1, tq), lambda b, i: (0, 0)),
            pl.BlockSpec((N_PAIRS, 2, 3, LANES, LANES), lambda b, i: (0, 0, 0, 0, 0), pipeline_mode=once),
            pl.BlockSpec((None, N_PAIRS, S, LANES), lambda b, i: (b, 0, 0, 0), pipeline_mode=once),
            pl.BlockSpec((None, N_PAIRS, S // VT_BLK, LANES, VT_BLK), lambda b, i: (b, 0, 0, 0, 0),
                         pipeline_mode=once),
            pl.BlockSpec((None, S, 256), lambda b, i: (b, 0, 0), pipeline_mode=once),
        ],
        out_specs=pl.BlockSpec((None, tq, N_HEADS * HEAD_DIM), lambda b, i: (b, i, 0)),
        out_shape=jax.ShapeDtypeStruct((B, S, N_HEADS * HEAD_DIM), BF16),
        scratch_shapes=[
            pltpu.VMEM((S // tk, tk, tq), jnp.int32),
            pltpu.VMEM((S // tk, tk, tq), jnp.int16),
            pltpu.VMEM((N_PAIRS, LANES, 2 * tq), F32),
            pltpu.VMEM((N_PAIRS, tq, LANES), BF16),
            pltpu.VMEM((2, tk, 2 * tq), F32),
            pltpu.VMEM((N_PAIRS, LANES, 2 * tq), BF16),
            pltpu.VMEM((N_PAIRS, 1, 2 * tq), F32),
            pltpu.VMEM((N_PAIRS, 1, 2 * tq), F32),
        ],
        compiler_params=_cparams(("parallel", "arbitrary"), VMEM_LIMIT_DSA),
        name="dsa_attn",
    )(q, qp, wi, jnp.full((1, tq), shift, F32), bias, k, vt, kp)


def _out_proj_kernel(x_ref, a_ref, w_ref, o_ref):
    o_ref[...] = x_ref[...] + _dot(a_ref[...], w_ref[...])


def _out_proj(x, a, w, tm=512):
    B, S, D = x.shape
    return pl.pallas_call(
        _out_proj_kernel,
        grid=(B, S // tm),
        in_specs=[
            pl.BlockSpec((None, tm, D), lambda b, i: (b, i, 0)),
            pl.BlockSpec((None, tm, a.shape[-1]), lambda b, i: (b, i, 0)),
            pl.BlockSpec(w.shape, lambda b, i: (0, 0)),
        ],
        out_specs=pl.BlockSpec((None, tm, D), lambda b, i: (b, i, 0)),
        out_shape=jax.ShapeDtypeStruct((B, S, D), F32),
        compiler_params=_cparams(("parallel", "parallel")),
        name="out_proj",
    )(x, a, w.astype(BF16))


def _ffn_kernel(x_ref, xp_ref, g_ref, wg_ref, wv_ref, cw_ref, cb_ref, wd_ref, o_ref, acc_ref, he_ref, u_ref,
                *, tm, n_chunks):
    i = pl.program_id(1)
    g = g_ref[...]
    he_ref[:HALO] = (_rms(xp_ref[...], g) * jnp.where(i > 0, 1.0, 0.0)).astype(BF16)
    he_ref[HALO:] = _rms(x_ref[...], g).astype(BF16)
    acc_ref[...] = jnp.zeros(acc_ref.shape, F32)

    def up(c, slot):
        he = he_ref[...]
        u_ref[slot, 0] = _dot(he, wg_ref[c])
        u_ref[slot, 1] = _dot(he, wv_ref[c])

    def conv(u, w, b):
        return (b + w[0:1] * u[HALO - 2:HALO - 2 + tm] + w[1:2] * u[HALO - 1:HALO - 1 + tm]
                + w[2:3] * u[HALO:HALO + tm])

    def down(c, slot):
        cw = cw_ref[c]
        cb = cb_ref[c]
        gate = conv(u_ref[slot, 0], cw[0], cb[0])
        val = conv(u_ref[slot, 1], cw[1], cb[1])
        act = gate * jax.nn.sigmoid(gate) * val
        acc_ref[...] += _dot(act.astype(BF16), wd_ref[c])

    up(0, 0)

    def pair_body(t, carry):
        c = 2 * t
        up(c + 1, 1)
        down(c, 0)
        up(c + 2, 0)
        down(c + 1, 1)
        return carry

    assert n_chunks % 2 == 1
    lax.fori_loop(0, n_chunks // 2, pair_body, 0)
    down(n_chunks - 1, 0)
    o_ref[...] = x_ref[...] + acc_ref[...]


def _ffn(x, g, w_up, conv_w, conv_b, w_down, tm=512):
    B, S, D = x.shape
    nc = D_FF // FF_CHUNK
    wup = w_up.astype(BF16).reshape(D, 2, nc, FF_CHUNK).transpose(1, 2, 0, 3)
    cw = conv_w.reshape(CONV_W, 2, nc, FF_CHUNK).transpose(2, 1, 0, 3)
    cb = conv_b.reshape(2, nc, 1, FF_CHUNK).transpose(1, 0, 2, 3)
    wd = w_down.astype(BF16).reshape(nc, FF_CHUNK, D)
    c3 = lambda b, i: (0, 0, 0)
    c4 = lambda b, i: (0, 0, 0, 0)
    return pl.pallas_call(
        functools.partial(_ffn_kernel, tm=tm, n_chunks=nc),
        grid=(B, S // tm),
        in_specs=[
            pl.BlockSpec((None, tm, D), lambda b, i: (b, i, 0)),
            pl.BlockSpec((None, HALO, D), lambda b, i: (b, jnp.maximum(i * (tm // HALO) - 1, 0), 0)),
            pl.BlockSpec((1, D), lambda b, i: (0, 0)),
            pl.BlockSpec((nc, D, FF_CHUNK), c3),
            pl.BlockSpec((nc, D, FF_CHUNK), c3),
            pl.BlockSpec((nc, 2, CONV_W, FF_CHUNK), c4),
            pl.BlockSpec((nc, 2, 1, FF_CHUNK), c4),
            pl.BlockSpec((nc, FF_CHUNK, D), c3),
        ],
        out_specs=pl.BlockSpec((None, tm, D), lambda b, i: (b, i, 0)),
        out_shape=jax.ShapeDtypeStruct((B, S, D), F32),
        scratch_shapes=[pltpu.VMEM((tm, D), F32), pltpu.VMEM((HALO + tm, D), BF16),
                        pltpu.VMEM((2, 2, HALO + tm, FF_CHUNK), F32)],
        compiler_params=_cparams(("parallel", "parallel")),
        name="conv_ffn",
    )(x, x, g.reshape(1, D), wup[0], wup[1], cw, cb, wd)


def _rope_tables(S, period, off):
    inv = 1.0 / (ROPE_THETA ** (jnp.arange(0, ROPE_DIM, 2, dtype=F32) / ROPE_DIM))
    ang = jnp.arange(S, dtype=F32)[:, None] * inv[None, :]
    cos, sin = jnp.cos(ang), jnp.sin(ang)
    one = jnp.ones((S, 1), F32)
    zero = jnp.zeros((S, 1), F32)

    def lay(x1, x2, fill):
        group = jnp.concatenate([jnp.tile(fill, (1, off)), x1, x2,
                                 jnp.tile(fill, (1, period - off - ROPE_DIM))], axis=1)
        return jnp.tile(group, (1, LANES // period))

    return lay(cos, cos, one), lay(-sin, zero * sin, zero), lay(zero * sin, sin, zero)


def _t5_bucket(rel):
    nb = T5_BUCKETS // 2
    max_exact = nb // 2
    n = jnp.abs(rel)
    large = max_exact + (jnp.log(jnp.maximum(n, 1).astype(F32) / max_exact)
                         / math.log(T5_MAX_DIST / max_exact) * (nb - max_exact)).astype(jnp.int32)
    large = jnp.minimum(large, nb - 1)
    return jnp.where(rel > 0, nb, 0) + jnp.where(n < max_exact, n, large)


def _dsa_bias_tiles(t5_bias):
    kj = jnp.arange(LANES, dtype=jnp.int32)[:, None]
    qi = jnp.arange(LANES, dtype=jnp.int32)[None, :]
    far = t5_bias[_t5_bucket(jnp.int32(-4 * T5_MAX_DIST))]

    def tile(rel):
        pick = jax.nn.one_hot(_t5_bucket(rel), T5_BUCKETS, dtype=F32)
        vals = jnp.einsum("kqb,bh->hkq", pick, t5_bias, precision=lax.Precision.HIGHEST)
        return (vals - far[:, None, None]) * LOG2E

    same, prev = tile(kj - qi), tile(kj - LANES - qi)
    return jnp.stack([jnp.zeros_like(same), prev, same], axis=1).reshape(N_PAIRS, 2, 3, LANES, LANES)


def _softmax_shift(gq, gk, dim, bias_hi=0.0, bias_lo=0.0):
    qk = 1.02 * dim * jnp.max(jnp.abs(gq)) * jnp.max(jnp.abs(gk))
    top = qk + bias_hi
    fixed = (top + qk - bias_lo) <= SAFE_SPREAD
    return fixed, jnp.where(fixed, -top, 0.0)


def _toeplitz_kernel(v_ref, o_ref, *, cols):
    rows, width = o_ref.shape[0], v_ref.shape[1]
    x = jnp.broadcast_to(v_ref[...], (rows, width))
    o_ref[...] = pltpu.roll(x, 0, 1, stride=1, stride_axis=0)[:, :cols]


def _toeplitz(v, rows, cols):
    H, L = v.shape
    return pl.pallas_call(
        functools.partial(_toeplitz_kernel, cols=cols),
        grid=(H,),
        in_specs=[pl.BlockSpec((None, 1, L), lambda h: (h, 0, 0))],
        out_specs=pl.BlockSpec((None, rows, cols), lambda h: (h, 0, 0)),
        out_shape=jax.ShapeDtypeStruct((H, rows, cols), F32),
        compiler_params=_cparams(("parallel",)),
        name="toeplitz",
    )(v.reshape(H, 1, L))


def _band_bias(rel_bias, shift, tq):
    pad = LEFT_CHUNKS * CHUNK
    nblk = -(-pad // tq) + 1
    lead = (nblk - 1) * tq
    nk = nblk * tq
    kk = jnp.arange(nk, dtype=jnp.int32)[:, None] - lead
    i = jnp.arange(tq, dtype=jnp.int32)[None, :]
    first = (i // CHUNK) * CHUNK - pad
    ok = (kk >= first) & (kk < (i // CHUNK + 1) * CHUNK)
    L = nk + tq
    dist = jnp.arange(L, dtype=jnp.int32)
    dist = lead + jnp.where(dist < tq, dist, dist - L)
    v = rel_bias[:, jnp.clip(dist, -(CHUNK - 1), REL_MAX_PAST) + CHUNK - 1]
    b = _toeplitz(v, nk, tq)
    b = jnp.where(ok[None], b + shift, NEG).reshape(N_PAIRS, 2, nblk, tq, tq)
    return jnp.transpose(b, (0, 2, 3, 1, 4)).reshape(N_PAIRS, nblk, tq, 2 * tq)


def kernel(x, norm_mix, norm_ffn, t5_bias, a_w_in, a_q_norm, a_k_norm, a_kidx_norm, a_w_out, b_w_down, b_q_a_norm, b_kv_a_norm, b_w_uq, b_w_ukv, b_q_norm, b_k_norm, b_w_out, c_w_in, c_q_norm, c_k_norm, c_rel_bias, c_w_out, f_w_up, f_conv_w, f_conv_b, f_w_down):
    B, S, D = x.shape
    depth = norm_mix.shape[0]

    tq_c = 256
    tabs_idx = _rope_tables(S, D_IDX, D_IDX - ROPE_DIM)
    tabs_mla = _rope_tables(S, LANES, QK_NOPE)
    scale = HEAD_DIM ** -0.5 * LOG2E
    for layer in range(depth):
        kind, j = layer % N_MIXERS, layer // N_MIXERS
        g = norm_mix[layer]
        if kind == 0:
            gq = a_q_norm[j] * scale
            q, k, vt = _qkv_proj(x, g, a_w_in[j][:, :A_QKV], gq, a_k_norm[j])
            qp, kp, wi = _idx_proj(x, g, a_w_in[j][:, A_QKV:], a_kidx_norm[j], tabs_idx)
            tiles = _dsa_bias_tiles(t5_bias)
            fixed, shift = _softmax_shift(gq, a_k_norm[j], HEAD_DIM, jnp.max(tiles), jnp.min(tiles))
            att = lax.cond(fixed, functools.partial(_dsa_attn, fixed=True), functools.partial(_dsa_attn, fixed=False),
                           q, k, vt, qp, kp, wi, shift, tiles)
            w_out = a_w_out[j]
        elif kind == 1:
            fixed, shift = _softmax_shift(b_q_norm[j] * (B_QK ** -0.5 * LOG2E), b_k_norm[j], B_QK)
            q, k, vt = _mla_proj(x, g, b_w_down[j], b_q_a_norm[j], b_kv_a_norm[j], b_w_uq[j], b_w_ukv[j],
                                 b_q_norm[j], b_k_norm[j], shift, tabs_mla)
            att = lax.cond(fixed, functools.partial(_mla_attn, fixed=True), functools.partial(_mla_attn, fixed=False),
                           q, k, vt)
            w_out = b_w_out[j]
        else:
            gq = c_q_norm[j] * scale
            q, k, vt = _qkv_proj(x, g, c_w_in[j], gq, c_k_norm[j])
            rel = c_rel_bias[j] * LOG2E
            fixed, shift = _softmax_shift(gq, c_k_norm[j], HEAD_DIM, jnp.max(rel), jnp.min(rel))
            att = lax.cond(fixed, functools.partial(_band_attn, fixed=True, tq=tq_c),
                           functools.partial(_band_attn, fixed=False, tq=tq_c),
                           q, k, vt, _band_bias(rel, shift, tq_c))
            w_out = c_w_out[j]
        x = _out_proj(x, att, w_out)
        x = _ffn(x, norm_ffn[layer], f_w_up[layer], f_conv_w[layer], f_conv_b[layer], f_w_down[layer])
    return x
```
